```python
import math
import jax, jax.numpy as jnp
from jax import lax
import numpy as np

D_MODEL = 1024
BATCH = 8
SEQ = 16384
DEPTH = 2

CHUNK = 64
N_MIXERS = 2
EPS = 1e-6
MLA_HEADS = 8
QK_NOPE = 128
QK_ROPE = 64
V_HEAD = 128
Q_LORA = 384
KV_LORA = 256
ROPE_THETA = 10000.0
Q_BLOCK = 128
SSM_WIDTH = D_MODEL
SSM_GROUP = 16
SSM_GROUPS = SSM_WIDTH // SSM_GROUP
SSM_STATE = 64
DT_MIN = 1e-3
DT_MAX = 1e-1
D_FF = 2816
CONV_W = 3

N_MLA_LAYERS = (DEPTH + 1) // 2
N_SSM_LAYERS = DEPTH // 2

kernel_name = "hybrid_mla_s5_convffn_stream_encoder"


def rmsnorm(x, g):
    xf = x.astype(jnp.float32)
    y = xf * lax.rsqrt(jnp.mean(xf * xf, axis=-1, keepdims=True) + EPS)
    return (y * g.astype(jnp.float32)).astype(x.dtype)


def rope_tables(positions):
    inv = 1.0 / (ROPE_THETA ** (jnp.arange(0, QK_ROPE, 2, dtype=jnp.float32) / QK_ROPE))
    ang = positions.astype(jnp.float32)[..., None] * inv
    return jnp.cos(ang), jnp.sin(ang)


def apply_rope(x, cos, sin):
    xf = x.astype(jnp.float32)
    x1, x2 = jnp.split(xf, 2, axis=-1)
    return jnp.concatenate([x1 * cos - x2 * sin, x1 * sin + x2 * cos], axis=-1).astype(x.dtype)


def mla_mixer(h, cos, sin, w_a, g_q, g_kv, w_uq, w_ukv, w_o):
    B, S, _ = h.shape
    H = MLA_HEADS
    a = h @ w_a
    c_q, c_kv, k_rope = jnp.split(a, [Q_LORA, Q_LORA + KV_LORA], axis=-1)
    q = (rmsnorm(c_q, g_q) @ w_uq).reshape(B, S, H, QK_NOPE + QK_ROPE)
    q_nope = q[..., :QK_NOPE]
    q_rope = apply_rope(q[..., QK_NOPE:], cos[:, :, None, :], sin[:, :, None, :])
    k_rope = apply_rope(k_rope, cos, sin)
    kv = (rmsnorm(c_kv, g_kv) @ w_ukv).reshape(B, S, H, QK_NOPE + V_HEAD)
    k_nope = kv[..., :QK_NOPE]
    v = kv[..., QK_NOPE:]
    scale = (QK_NOPE + QK_ROPE) ** -0.5
    nb = S // Q_BLOCK
    key_chunk = jnp.arange(S) // CHUNK

    def block(args):
        qn, qr, i = args
        s = (jnp.einsum('bqhd,bkhd->bhqk', qn, k_nope)
             + jnp.einsum('bqhr,bkr->bhqk', qr, k_rope)).astype(jnp.float32) * scale
        q_chunk = (i * Q_BLOCK + jnp.arange(Q_BLOCK)) // CHUNK
        mask = key_chunk[None, :] <= q_chunk[:, None]
        s = jnp.where(mask[None, None], s, -jnp.inf)
        p = jax.nn.softmax(s, axis=-1).astype(v.dtype)
        return jnp.einsum('bhqk,bkhd->bqhd', p, v)

    qn_b = q_nope.reshape(B, nb, Q_BLOCK, H, QK_NOPE).transpose(1, 0, 2, 3, 4)
    qr_b = q_rope.reshape(B, nb, Q_BLOCK, H, QK_ROPE).transpose(1, 0, 2, 3, 4)
    o = lax.map(block, (qn_b, qr_b, jnp.arange(nb)))
    o = o.transpose(1, 0, 2, 3, 4).reshape(B, S, H * V_HEAD)
    return o @ w_o


def s5_mixer(h, w_in, lam_re, lam_im, log_dt, b_re, b_im, c_re, c_im, d_skip, w_glu):
    B, S, _ = h.shape
    G, P, C = SSM_GROUPS, SSM_STATE, SSM_GROUP
    u = (h @ w_in).astype(jnp.float32).reshape(B, S, G, C)
    dt = jnp.exp(log_dt.astype(jnp.float32))[:, None]
    lr = lam_re.astype(jnp.float32)
    li = lam_im.astype(jnp.float32)
    mag = jnp.exp(lr * dt)
    ar = mag * jnp.cos(li * dt)
    ai = mag * jnp.sin(li * dt)
    den = lr * lr + li * li
    nr = ar - 1.0
    coef_r = (nr * lr + ai * li) / den
    coef_i = (ai * lr - nr * li) / den
    br = b_re.astype(jnp.float32)
    bi = b_im.astype(jnp.float32)
    bbar_r = coef_r[..., None] * br - coef_i[..., None] * bi
    bbar_i = coef_r[..., None] * bi + coef_i[..., None] * br
    bu_r = jnp.einsum('bsgc,gpc->bsgp', u, bbar_r)
    bu_i = jnp.einsum('bsgc,gpc->bsgp', u, bbar_i)
    a_r = jnp.broadcast_to(ar, bu_r.shape)
    a_i = jnp.broadcast_to(ai, bu_i.shape)

    def combine(e1, e2):
        a1r, a1i, b1r, b1i = e1
        a2r, a2i, b2r, b2i = e2
        return (a2r * a1r - a2i * a1i,
                a2r * a1i + a2i * a1r,
                a2r * b1r - a2i * b1i + b2r,
                a2r * b1i + a2i * b1r + b2i)

    _, _, xr, xi = lax.associative_scan(combine, (a_r, a_i, bu_r, bu_i), axis=1)
    y = (jnp.einsum('bsgp,gcp->bsgc', xr, c_re.astype(jnp.float32))
         - jnp.einsum('bsgp,gcp->bsgc', xi, c_im.astype(jnp.float32)))
    y = y.reshape(B, S, SSM_WIDTH) + d_skip.astype(jnp.float32) * u.reshape(B, S, SSM_WIDTH)
    y = jax.nn.gelu(y).astype(h.dtype)
    val, gate = jnp.split(y @ w_glu, 2, axis=-1)
    return val * jax.nn.sigmoid(gate)


def conv_ffn(h, w_up, conv_w, conv_b, w_down):
    up = h @ w_up
    up = lax.conv_general_dilated(up, conv_w[:, None, :], window_strides=(1,),
                                  padding=[(CONV_W - 1, 0)],
                                  dimension_numbers=('NWC', 'WIO', 'NWC'),
                                  feature_group_count=2 * D_FF) + conv_b
    val, gate = jnp.split(up, 2, axis=-1)
    return (jax.nn.silu(gate) * val) @ w_down


def _fwd_setup_inputs(seed: int = 0) -> dict:
    key = jax.random.key(seed)
    ks = iter(jax.random.split(key, 32))
    f32 = jnp.float32

    def nrm(shape, fan_in):
        return jax.random.normal(next(ks), shape, f32) * (fan_in ** -0.5)

    def gain(shape):
        return 1.0 + 0.02 * jax.random.normal(next(ks), shape, f32)

    Lm, Ls = N_MLA_LAYERS, N_SSM_LAYERS
    G, P, C = SSM_GROUPS, SSM_STATE, SSM_GROUP
    x = jax.random.normal(next(ks), (BATCH, SEQ, D_MODEL), f32)
    offset = jax.random.randint(next(ks), (BATCH, 1), 0, 4096, dtype=jnp.int32)
    positions = (offset + jnp.arange(SEQ, dtype=jnp.int32)[None, :]).astype(jnp.int32)

    mla_w_a = nrm((Lm, D_MODEL, Q_LORA + KV_LORA + QK_ROPE), D_MODEL)
    mla_g_q = gain((Lm, Q_LORA))
    mla_g_kv = gain((Lm, KV_LORA))
    mla_w_uq = nrm((Lm, Q_LORA, MLA_HEADS * (QK_NOPE + QK_ROPE)), Q_LORA)
    mla_w_ukv = nrm((Lm, KV_LORA, MLA_HEADS * (QK_NOPE + V_HEAD)), KV_LORA)
    mla_w_o = nrm((Lm, MLA_HEADS * V_HEAD, D_MODEL), MLA_HEADS * V_HEAD)

    ssm_w_in = nrm((Ls, D_MODEL, SSM_WIDTH), D_MODEL)
    ssm_lambda_re = -0.5 + 0.02 * jax.random.normal(next(ks), (Ls, G, P), f32)
    ssm_lambda_im = math.pi * jnp.arange(P, dtype=f32)[None, None, :] + 0.02 * jax.random.normal(next(ks), (Ls, G, P), f32)
    ssm_log_dt = jax.random.uniform(next(ks), (Ls, G), f32, math.log(DT_MIN), math.log(DT_MAX))
    ssm_b_re = nrm((Ls, G, P, C), 2 * C)
    ssm_b_im = nrm((Ls, G, P, C), 2 * C)
    ssm_c_re = nrm((Ls, G, C, P), 2 * P)
    ssm_c_im = nrm((Ls, G, C, P), 2 * P)
    ssm_d = jax.random.normal(next(ks), (Ls, SSM_WIDTH), f32)
    ssm_w_glu = nrm((Ls, SSM_WIDTH, 2 * D_MODEL), SSM_WIDTH)

    ffn_w_up = nrm((DEPTH, D_MODEL, 2 * D_FF), D_MODEL)
    ffn_conv_w = nrm((DEPTH, CONV_W, 2 * D_FF), CONV_W)
    ffn_conv_b = 0.02 * jax.random.normal(next(ks), (DEPTH, 2 * D_FF), f32)
    ffn_w_down = nrm((DEPTH, D_FF, D_MODEL), D_FF)

    g_mix = gain((DEPTH, D_MODEL))
    g_ffn = gain((DEPTH, D_MODEL))
    g_final = gain((D_MODEL,))
    return {"x": x, "positions": positions,
            "mla_w_a": mla_w_a, "mla_g_q": mla_g_q, "mla_g_kv": mla_g_kv,
            "mla_w_uq": mla_w_uq, "mla_w_ukv": mla_w_ukv, "mla_w_o": mla_w_o,
            "ssm_w_in": ssm_w_in, "ssm_lambda_re": ssm_lambda_re, "ssm_lambda_im": ssm_lambda_im,
            "ssm_log_dt": ssm_log_dt, "ssm_b_re": ssm_b_re, "ssm_b_im": ssm_b_im,
            "ssm_c_re": ssm_c_re, "ssm_c_im": ssm_c_im, "ssm_d": ssm_d, "ssm_w_glu": ssm_w_glu,
            "ffn_w_up": ffn_w_up, "ffn_conv_w": ffn_conv_w, "ffn_conv_b": ffn_conv_b,
            "ffn_w_down": ffn_w_down, "g_mix": g_mix, "g_ffn": g_ffn, "g_final": g_final}


def _fwd_reference(x, positions, mla_w_a, mla_g_q, mla_g_kv, mla_w_uq, mla_w_ukv, mla_w_o,
              ssm_w_in, ssm_lambda_re, ssm_lambda_im, ssm_log_dt, ssm_b_re, ssm_b_im,
              ssm_c_re, ssm_c_im, ssm_d, ssm_w_glu, ffn_w_up, ffn_conv_w, ffn_conv_b,
              ffn_w_down, g_mix, g_ffn, g_final):
    cos, sin = rope_tables(positions)
    h = x
    for i in range(DEPTH):
        hn = rmsnorm(h, g_mix[i])
        j = i // N_MIXERS
        if i % N_MIXERS == 0:
            mix = mla_mixer(hn, cos, sin, mla_w_a[j], mla_g_q[j], mla_g_kv[j],
                            mla_w_uq[j], mla_w_ukv[j], mla_w_o[j])
        else:
            mix = s5_mixer(hn, ssm_w_in[j], ssm_lambda_re[j], ssm_lambda_im[j], ssm_log_dt[j],
                           ssm_b_re[j], ssm_b_im[j], ssm_c_re[j], ssm_c_im[j], ssm_d[j], ssm_w_glu[j])
        h = h + mix
        h = h + conv_ffn(rmsnorm(h, g_ffn[i]), ffn_w_up[i], ffn_conv_w[i], ffn_conv_b[i], ffn_w_down[i])
    return rmsnorm(h, g_final)


import jax as _jax
import jax.numpy as _jnp

TWIN_FORMAT = 'train_step'
FWD_PARAMS = ['x', 'positions', 'mla_w_a', 'mla_g_q', 'mla_g_kv', 'mla_w_uq', 'mla_w_ukv', 'mla_w_o', 'ssm_w_in', 'ssm_lambda_re', 'ssm_lambda_im', 'ssm_log_dt', 'ssm_b_re', 'ssm_b_im', 'ssm_c_re', 'ssm_c_im', 'ssm_d', 'ssm_w_glu', 'ffn_w_up', 'ffn_conv_w', 'ffn_conv_b', 'ffn_w_down', 'g_mix', 'g_ffn', 'g_final']
TWIN_WEIGHTS = ['mla_w_a', 'mla_g_q', 'mla_g_kv', 'mla_w_uq', 'mla_w_ukv', 'mla_w_o', 'ssm_w_in', 'ssm_lambda_re', 'ssm_lambda_im', 'ssm_log_dt', 'ssm_b_re', 'ssm_b_im', 'ssm_c_re', 'ssm_c_im', 'ssm_d', 'ssm_w_glu', 'ffn_w_up', 'ffn_conv_w', 'ffn_conv_b', 'ffn_w_down', 'g_mix', 'g_ffn', 'g_final']
TWIN_DIFF_INPUT = 'x'
TWIN_INPUTS = ['x', 'positions', 'mla_w_a', 'mla_g_q', 'mla_g_kv', 'mla_w_uq', 'mla_w_ukv', 'mla_w_o', 'ssm_w_in', 'ssm_lambda_re', 'ssm_lambda_im', 'ssm_log_dt', 'ssm_b_re', 'ssm_b_im', 'ssm_c_re', 'ssm_c_im', 'ssm_d', 'ssm_w_glu', 'ffn_w_up', 'ffn_conv_w', 'ffn_conv_b', 'ffn_w_down', 'g_mix', 'g_ffn', 'g_final', 'loss_target', 'm_mla_w_a', 'm_mla_g_q', 'm_mla_g_kv', 'm_mla_w_uq', 'm_mla_w_ukv', 'm_mla_w_o', 'm_ssm_w_in', 'm_ssm_lambda_re', 'm_ssm_lambda_im', 'm_ssm_log_dt', 'm_ssm_b_re', 'm_ssm_b_im', 'm_ssm_c_re', 'm_ssm_c_im', 'm_ssm_d', 'm_ssm_w_glu', 'm_ffn_w_up', 'm_ffn_conv_w', 'm_ffn_conv_b', 'm_ffn_w_down', 'm_g_mix', 'm_g_ffn', 'm_g_final', 'v_mla_w_a', 'v_mla_g_q', 'v_mla_g_kv', 'v_mla_w_uq', 'v_mla_w_ukv', 'v_mla_w_o', 'v_ssm_w_in', 'v_ssm_lambda_re', 'v_ssm_lambda_im', 'v_ssm_log_dt', 'v_ssm_b_re', 'v_ssm_b_im', 'v_ssm_c_re', 'v_ssm_c_im', 'v_ssm_d', 'v_ssm_w_glu', 'v_ffn_w_up', 'v_ffn_conv_w', 'v_ffn_conv_b', 'v_ffn_w_down', 'v_g_mix', 'v_g_ffn', 'v_g_final']
TWIN_OUTPUTS = ['loss', 'grad_x', 'grad_mla_w_a', 'grad_mla_g_q', 'grad_mla_g_kv', 'grad_mla_w_uq', 'grad_mla_w_ukv', 'grad_mla_w_o', 'grad_ssm_w_in', 'grad_ssm_lambda_re', 'grad_ssm_lambda_im', 'grad_ssm_log_dt', 'grad_ssm_b_re', 'grad_ssm_b_im', 'grad_ssm_c_re', 'grad_ssm_c_im', 'grad_ssm_d', 'grad_ssm_w_glu', 'grad_ffn_w_up', 'grad_ffn_conv_w', 'grad_ffn_conv_b', 'grad_ffn_w_down', 'grad_g_mix', 'grad_g_ffn', 'grad_g_final', 'delta_mla_w_a', 'delta_mla_g_q', 'delta_mla_g_kv', 'delta_mla_w_uq', 'delta_mla_w_ukv', 'delta_mla_w_o', 'delta_ssm_w_in', 'delta_ssm_lambda_re', 'delta_ssm_lambda_im', 'delta_ssm_log_dt', 'delta_ssm_b_re', 'delta_ssm_b_im', 'delta_ssm_c_re', 'delta_ssm_c_im', 'delta_ssm_d', 'delta_ssm_w_glu', 'delta_ffn_w_up', 'delta_ffn_conv_w', 'delta_ffn_conv_b', 'delta_ffn_w_down', 'delta_g_mix', 'delta_g_ffn', 'delta_g_final', 'new_m_mla_w_a', 'new_m_mla_g_q', 'new_m_mla_g_kv', 'new_m_mla_w_uq', 'new_m_mla_w_ukv', 'new_m_mla_w_o', 'new_m_ssm_w_in', 'new_m_ssm_lambda_re', 'new_m_ssm_lambda_im', 'new_m_ssm_log_dt', 'new_m_ssm_b_re', 'new_m_ssm_b_im', 'new_m_ssm_c_re', 'new_m_ssm_c_im', 'new_m_ssm_d', 'new_m_ssm_w_glu', 'new_m_ffn_w_up', 'new_m_ffn_conv_w', 'new_m_ffn_conv_b', 'new_m_ffn_w_down', 'new_m_g_mix', 'new_m_g_ffn', 'new_m_g_final', 'new_v_mla_w_a', 'new_v_mla_g_q', 'new_v_mla_g_kv', 'new_v_mla_w_uq', 'new_v_mla_w_ukv', 'new_v_mla_w_o', 'new_v_ssm_w_in', 'new_v_ssm_lambda_re', 'new_v_ssm_lambda_im', 'new_v_ssm_log_dt', 'new_v_ssm_b_re', 'new_v_ssm_b_im', 'new_v_ssm_c_re', 'new_v_ssm_c_im', 'new_v_ssm_d', 'new_v_ssm_w_glu', 'new_v_ffn_w_up', 'new_v_ffn_conv_w', 'new_v_ffn_conv_b', 'new_v_ffn_w_down', 'new_v_g_mix', 'new_v_g_ffn', 'new_v_g_final']
TWIN_LEAF_KINDS = {'loss': 'loss', 'grad_x': 'grad_x', 'grad_mla_w_a': 'grad_w', 'grad_mla_g_q': 'grad_w', 'grad_mla_g_kv': 'grad_w', 'grad_mla_w_uq': 'grad_w', 'grad_mla_w_ukv': 'grad_w', 'grad_mla_w_o': 'grad_w', 'grad_ssm_w_in': 'grad_w', 'grad_ssm_lambda_re': 'grad_w', 'grad_ssm_lambda_im': 'grad_w', 'grad_ssm_log_dt': 'grad_w', 'grad_ssm_b_re': 'grad_w', 'grad_ssm_b_im': 'grad_w', 'grad_ssm_c_re': 'grad_w', 'grad_ssm_c_im': 'grad_w', 'grad_ssm_d': 'grad_w', 'grad_ssm_w_glu': 'grad_w', 'grad_ffn_w_up': 'grad_w', 'grad_ffn_conv_w': 'grad_w', 'grad_ffn_conv_b': 'grad_w', 'grad_ffn_w_down': 'grad_w', 'grad_g_mix': 'grad_w', 'grad_g_ffn': 'grad_w', 'grad_g_final': 'grad_w', 'delta_mla_w_a': 'delta_w', 'delta_mla_g_q': 'delta_w', 'delta_mla_g_kv': 'delta_w', 'delta_mla_w_uq': 'delta_w', 'delta_mla_w_ukv': 'delta_w', 'delta_mla_w_o': 'delta_w', 'delta_ssm_w_in': 'delta_w', 'delta_ssm_lambda_re': 'delta_w', 'delta_ssm_lambda_im': 'delta_w', 'delta_ssm_log_dt': 'delta_w', 'delta_ssm_b_re': 'delta_w', 'delta_ssm_b_im': 'delta_w', 'delta_ssm_c_re': 'delta_w', 'delta_ssm_c_im': 'delta_w', 'delta_ssm_d': 'delta_w', 'delta_ssm_w_glu': 'delta_w', 'delta_ffn_w_up': 'delta_w', 'delta_ffn_conv_w': 'delta_w', 'delta_ffn_conv_b': 'delta_w', 'delta_ffn_w_down': 'delta_w', 'delta_g_mix': 'delta_w', 'delta_g_ffn': 'delta_w', 'delta_g_final': 'delta_w', 'new_m_mla_w_a': 'new_m', 'new_m_mla_g_q': 'new_m', 'new_m_mla_g_kv': 'new_m', 'new_m_mla_w_uq': 'new_m', 'new_m_mla_w_ukv': 'new_m', 'new_m_mla_w_o': 'new_m', 'new_m_ssm_w_in': 'new_m', 'new_m_ssm_lambda_re': 'new_m', 'new_m_ssm_lambda_im': 'new_m', 'new_m_ssm_log_dt': 'new_m', 'new_m_ssm_b_re': 'new_m', 'new_m_ssm_b_im': 'new_m', 'new_m_ssm_c_re': 'new_m', 'new_m_ssm_c_im': 'new_m', 'new_m_ssm_d': 'new_m', 'new_m_ssm_w_glu': 'new_m', 'new_m_ffn_w_up': 'new_m', 'new_m_ffn_conv_w': 'new_m', 'new_m_ffn_conv_b': 'new_m', 'new_m_ffn_w_down': 'new_m', 'new_m_g_mix': 'new_m', 'new_m_g_ffn': 'new_m', 'new_m_g_final': 'new_m', 'new_v_mla_w_a': 'new_v', 'new_v_mla_g_q': 'new_v', 'new_v_mla_g_kv': 'new_v', 'new_v_mla_w_uq': 'new_v', 'new_v_mla_w_ukv': 'new_v', 'new_v_mla_w_o': 'new_v', 'new_v_ssm_w_in': 'new_v', 'new_v_ssm_lambda_re': 'new_v', 'new_v_ssm_lambda_im': 'new_v', 'new_v_ssm_log_dt': 'new_v', 'new_v_ssm_b_re': 'new_v', 'new_v_ssm_b_im': 'new_v', 'new_v_ssm_c_re': 'new_v', 'new_v_ssm_c_im': 'new_v', 'new_v_ssm_d': 'new_v', 'new_v_ssm_w_glu': 'new_v', 'new_v_ffn_w_up': 'new_v', 'new_v_ffn_conv_w': 'new_v', 'new_v_ffn_conv_b': 'new_v', 'new_v_ffn_w_down': 'new_v', 'new_v_g_mix': 'new_v', 'new_v_g_ffn': 'new_v', 'new_v_g_final': 'new_v'}


def _forward(args):
    return _fwd_reference(*[args[k] for k in FWD_PARAMS])


def _output_shape():
    def fwd():
        inp = _fwd_setup_inputs(0)
        return _fwd_reference(*[inp[k] for k in FWD_PARAMS])
    out = _jax.eval_shape(fwd)
    return out.shape, out.dtype

N_MICROBATCH = 1
ADAM_LR = 0.001
ADAM_B1 = 0.9
ADAM_B2 = 0.999
ADAM_EPS = 1e-08
ADAM_WD = 0.01
ADAM_STEP = 10
PER_EXAMPLE_BATCH_AXIS = {'x': 0, 'positions': 0, 'loss_target': 0}
SHARED_INPUTS = []
_WEIGHT_DTYPES = {'mla_w_a': _jnp.float32, 'mla_g_q': _jnp.float32, 'mla_g_kv': _jnp.float32, 'mla_w_uq': _jnp.float32, 'mla_w_ukv': _jnp.float32, 'mla_w_o': _jnp.float32, 'ssm_w_in': _jnp.float32, 'ssm_lambda_re': _jnp.float32, 'ssm_lambda_im': _jnp.float32, 'ssm_log_dt': _jnp.float32, 'ssm_b_re': _jnp.float32, 'ssm_b_im': _jnp.float32, 'ssm_c_re': _jnp.float32, 'ssm_c_im': _jnp.float32, 'ssm_d': _jnp.float32, 'ssm_w_glu': _jnp.float32, 'ffn_w_up': _jnp.float32, 'ffn_conv_w': _jnp.float32, 'ffn_conv_b': _jnp.float32, 'ffn_w_down': _jnp.float32, 'g_mix': _jnp.float32, 'g_ffn': _jnp.float32, 'g_final': _jnp.float32}
MOMENT_SCALE = {'mla_w_a': 1.314905e-01, 'mla_g_q': 1.014452e-01, 'mla_g_kv': 1.762059e-01, 'mla_w_uq': 5.251986e-02, 'mla_w_ukv': 6.109367e-02, 'mla_w_o': 6.733961e-02, 'ssm_w_in': 1.190573e-01, 'ssm_lambda_re': 8.427036e-03, 'ssm_lambda_im': 7.029107e-03, 'ssm_log_dt': 1.025204e+01, 'ssm_b_re': 4.092480e-03, 'ssm_b_im': 4.052586e-03, 'ssm_c_re': 8.354982e-03, 'ssm_c_im': 7.990508e-03, 'ssm_d': 1.249440e-01, 'ssm_w_glu': 8.677381e-02, 'ffn_w_up': 1.126579e-01, 'ffn_conv_w': 1.126997e-01, 'ffn_conv_b': 1.080008e-01, 'ffn_w_down': 1.845941e-01, 'g_mix': 1.103401e-01, 'g_ffn': 2.605712e-01, 'g_final': 1.281594e+02}


def _to_microbatches(a, axis):
    t = _jnp.moveaxis(a, axis, 0)
    t = t.reshape((N_MICROBATCH, t.shape[0] // N_MICROBATCH) + t.shape[1:])
    return _jnp.moveaxis(t, 1, axis + 1)


def setup_inputs(seed: int = 0) -> dict:
    inp = _fwd_setup_inputs(seed)
    key = _jax.random.fold_in(_jax.random.key(seed), 7919)
    shape, _ = _output_shape()
    out = dict(inp)
    out["loss_target"] = _jax.random.normal(_jax.random.fold_in(key, 0), shape, _jnp.float32)
    for i, name in enumerate(TWIN_WEIGHTS):
        w = inp[name].astype(_jnp.float32)
        if MOMENT_SCALE is None:
            s = _jnp.sqrt(_jnp.mean(_jnp.square(w)) + 1e-30)
        else:
            s = MOMENT_SCALE[name]
        km, kv = _jax.random.split(_jax.random.fold_in(key, i + 1))
        out[name] = w
        out["m_" + name] = s * _jax.random.normal(km, w.shape, _jnp.float32)
        out["v_" + name] = (s * s) * _jax.random.uniform(kv, w.shape, _jnp.float32, 0.5, 1.5)
    if N_MICROBATCH > 1:
        for name, axis in PER_EXAMPLE_BATCH_AXIS.items():
            out[name] = _to_microbatches(out[name], axis)
    return {'x': out['x'], 'positions': out['positions'], 'mla_w_a': out['mla_w_a'], 'mla_g_q': out['mla_g_q'], 'mla_g_kv': out['mla_g_kv'], 'mla_w_uq': out['mla_w_uq'], 'mla_w_ukv': out['mla_w_ukv'], 'mla_w_o': out['mla_w_o'], 'ssm_w_in': out['ssm_w_in'], 'ssm_lambda_re': out['ssm_lambda_re'], 'ssm_lambda_im': out['ssm_lambda_im'], 'ssm_log_dt': out['ssm_log_dt'], 'ssm_b_re': out['ssm_b_re'], 'ssm_b_im': out['ssm_b_im'], 'ssm_c_re': out['ssm_c_re'], 'ssm_c_im': out['ssm_c_im'], 'ssm_d': out['ssm_d'], 'ssm_w_glu': out['ssm_w_glu'], 'ffn_w_up': out['ffn_w_up'], 'ffn_conv_w': out['ffn_conv_w'], 'ffn_conv_b': out['ffn_conv_b'], 'ffn_w_down': out['ffn_w_down'], 'g_mix': out['g_mix'], 'g_ffn': out['g_ffn'], 'g_final': out['g_final'], 'loss_target': out['loss_target'], 'm_mla_w_a': out['m_mla_w_a'], 'm_mla_g_q': out['m_mla_g_q'], 'm_mla_g_kv': out['m_mla_g_kv'], 'm_mla_w_uq': out['m_mla_w_uq'], 'm_mla_w_ukv': out['m_mla_w_ukv'], 'm_mla_w_o': out['m_mla_w_o'], 'm_ssm_w_in': out['m_ssm_w_in'], 'm_ssm_lambda_re': out['m_ssm_lambda_re'], 'm_ssm_lambda_im': out['m_ssm_lambda_im'], 'm_ssm_log_dt': out['m_ssm_log_dt'], 'm_ssm_b_re': out['m_ssm_b_re'], 'm_ssm_b_im': out['m_ssm_b_im'], 'm_ssm_c_re': out['m_ssm_c_re'], 'm_ssm_c_im': out['m_ssm_c_im'], 'm_ssm_d': out['m_ssm_d'], 'm_ssm_w_glu': out['m_ssm_w_glu'], 'm_ffn_w_up': out['m_ffn_w_up'], 'm_ffn_conv_w': out['m_ffn_conv_w'], 'm_ffn_conv_b': out['m_ffn_conv_b'], 'm_ffn_w_down': out['m_ffn_w_down'], 'm_g_mix': out['m_g_mix'], 'm_g_ffn': out['m_g_ffn'], 'm_g_final': out['m_g_final'], 'v_mla_w_a': out['v_mla_w_a'], 'v_mla_g_q': out['v_mla_g_q'], 'v_mla_g_kv': out['v_mla_g_kv'], 'v_mla_w_uq': out['v_mla_w_uq'], 'v_mla_w_ukv': out['v_mla_w_ukv'], 'v_mla_w_o': out['v_mla_w_o'], 'v_ssm_w_in': out['v_ssm_w_in'], 'v_ssm_lambda_re': out['v_ssm_lambda_re'], 'v_ssm_lambda_im': out['v_ssm_lambda_im'], 'v_ssm_log_dt': out['v_ssm_log_dt'], 'v_ssm_b_re': out['v_ssm_b_re'], 'v_ssm_b_im': out['v_ssm_b_im'], 'v_ssm_c_re': out['v_ssm_c_re'], 'v_ssm_c_im': out['v_ssm_c_im'], 'v_ssm_d': out['v_ssm_d'], 'v_ssm_w_glu': out['v_ssm_w_glu'], 'v_ffn_w_up': out['v_ffn_w_up'], 'v_ffn_conv_w': out['v_ffn_conv_w'], 'v_ffn_conv_b': out['v_ffn_conv_b'], 'v_ffn_w_down': out['v_ffn_w_down'], 'v_g_mix': out['v_g_mix'], 'v_g_ffn': out['v_g_ffn'], 'v_g_final': out['v_g_final']}


def _loss(weights, diff, rest, loss_target):
    with _jax.named_scope("forward"):
        args = {**rest, TWIN_DIFF_INPUT: diff, **{k: w.astype(_WEIGHT_DTYPES[k]) for k, w in weights.items()}}
        y = _forward(args)
    with _jax.named_scope("loss_head"):
        err = _jnp.square(y.astype(_jnp.float32) - loss_target)
        return 0.5 * _jnp.sum(_jnp.mean(err, axis=-1)) if err.ndim else 0.5 * err


def _adamw(w, g, m, v):
    m = ADAM_B1 * m + (1.0 - ADAM_B1) * g
    v = ADAM_B2 * v + (1.0 - ADAM_B2) * _jnp.square(g)
    m_hat = m / (1.0 - ADAM_B1 ** ADAM_STEP)
    v_hat = v / (1.0 - ADAM_B2 ** ADAM_STEP)
    delta = -ADAM_LR * (m_hat / (_jnp.sqrt(v_hat) + ADAM_EPS) + ADAM_WD * w)
    return delta, m, v


def reference(x, positions, mla_w_a, mla_g_q, mla_g_kv, mla_w_uq, mla_w_ukv, mla_w_o, ssm_w_in, ssm_lambda_re, ssm_lambda_im, ssm_log_dt, ssm_b_re, ssm_b_im, ssm_c_re, ssm_c_im, ssm_d, ssm_w_glu, ffn_w_up, ffn_conv_w, ffn_conv_b, ffn_w_down, g_mix, g_ffn, g_final, loss_target, m_mla_w_a, m_mla_g_q, m_mla_g_kv, m_mla_w_uq, m_mla_w_ukv, m_mla_w_o, m_ssm_w_in, m_ssm_lambda_re, m_ssm_lambda_im, m_ssm_log_dt, m_ssm_b_re, m_ssm_b_im, m_ssm_c_re, m_ssm_c_im, m_ssm_d, m_ssm_w_glu, m_ffn_w_up, m_ffn_conv_w, m_ffn_conv_b, m_ffn_w_down, m_g_mix, m_g_ffn, m_g_final, v_mla_w_a, v_mla_g_q, v_mla_g_kv, v_mla_w_uq, v_mla_w_ukv, v_mla_w_o, v_ssm_w_in, v_ssm_lambda_re, v_ssm_lambda_im, v_ssm_log_dt, v_ssm_b_re, v_ssm_b_im, v_ssm_c_re, v_ssm_c_im, v_ssm_d, v_ssm_w_glu, v_ffn_w_up, v_ffn_conv_w, v_ffn_conv_b, v_ffn_w_down, v_g_mix, v_g_ffn, v_g_final):
    given = dict(x=x, positions=positions, mla_w_a=mla_w_a, mla_g_q=mla_g_q, mla_g_kv=mla_g_kv, mla_w_uq=mla_w_uq, mla_w_ukv=mla_w_ukv, mla_w_o=mla_w_o, ssm_w_in=ssm_w_in, ssm_lambda_re=ssm_lambda_re, ssm_lambda_im=ssm_lambda_im, ssm_log_dt=ssm_log_dt, ssm_b_re=ssm_b_re, ssm_b_im=ssm_b_im, ssm_c_re=ssm_c_re, ssm_c_im=ssm_c_im, ssm_d=ssm_d, ssm_w_glu=ssm_w_glu, ffn_w_up=ffn_w_up, ffn_conv_w=ffn_conv_w, ffn_conv_b=ffn_conv_b, ffn_w_down=ffn_w_down, g_mix=g_mix, g_ffn=g_ffn, g_final=g_final, loss_target=loss_target, m_mla_w_a=m_mla_w_a, m_mla_g_q=m_mla_g_q, m_mla_g_kv=m_mla_g_kv, m_mla_w_uq=m_mla_w_uq, m_mla_w_ukv=m_mla_w_ukv, m_mla_w_o=m_mla_w_o, m_ssm_w_in=m_ssm_w_in, m_ssm_lambda_re=m_ssm_lambda_re, m_ssm_lambda_im=m_ssm_lambda_im, m_ssm_log_dt=m_ssm_log_dt, m_ssm_b_re=m_ssm_b_re, m_ssm_b_im=m_ssm_b_im, m_ssm_c_re=m_ssm_c_re, m_ssm_c_im=m_ssm_c_im, m_ssm_d=m_ssm_d, m_ssm_w_glu=m_ssm_w_glu, m_ffn_w_up=m_ffn_w_up, m_ffn_conv_w=m_ffn_conv_w, m_ffn_conv_b=m_ffn_conv_b, m_ffn_w_down=m_ffn_w_down, m_g_mix=m_g_mix, m_g_ffn=m_g_ffn, m_g_final=m_g_final, v_mla_w_a=v_mla_w_a, v_mla_g_q=v_mla_g_q, v_mla_g_kv=v_mla_g_kv, v_mla_w_uq=v_mla_w_uq, v_mla_w_ukv=v_mla_w_ukv, v_mla_w_o=v_mla_w_o, v_ssm_w_in=v_ssm_w_in, v_ssm_lambda_re=v_ssm_lambda_re, v_ssm_lambda_im=v_ssm_lambda_im, v_ssm_log_dt=v_ssm_log_dt, v_ssm_b_re=v_ssm_b_re, v_ssm_b_im=v_ssm_b_im, v_ssm_c_re=v_ssm_c_re, v_ssm_c_im=v_ssm_c_im, v_ssm_d=v_ssm_d, v_ssm_w_glu=v_ssm_w_glu, v_ffn_w_up=v_ffn_w_up, v_ffn_conv_w=v_ffn_conv_w, v_ffn_conv_b=v_ffn_conv_b, v_ffn_w_down=v_ffn_w_down, v_g_mix=v_g_mix, v_g_ffn=v_g_ffn, v_g_final=v_g_final)
    weights = {n: given[n] for n in TWIN_WEIGHTS}
    shared = {n: given[n] for n in SHARED_INPUTS}
    per_example = {n: given[n] for n in ['x', 'positions']}
    grad_fn = _jax.value_and_grad(_loss, argnums=(0, 1))

    def one_microbatch(ex, loss_target):
        ex = dict(ex)
        diff = ex.pop(TWIN_DIFF_INPUT)
        return grad_fn(weights, diff, {**shared, **ex}, loss_target)

    if N_MICROBATCH == 1:
        loss, (grad_w, grad_x) = one_microbatch(per_example, given["loss_target"])
    else:
        def body(carry, xs):
            loss_sum, grad_sum = carry
            l_k, (gw_k, gx_k) = one_microbatch(xs[0], xs[1])
            with _jax.named_scope("update"):
                return (loss_sum + l_k, _jax.tree.map(_jnp.add, grad_sum, gw_k)), gx_k

        init = (_jnp.zeros((), _jnp.float32), _jax.tree.map(_jnp.zeros_like, weights))
        (loss, grad_w), grad_x = _jax.lax.scan(body, init, (per_example, given["loss_target"]))
    with _jax.named_scope("update"):
        delta_w, new_m, new_v = {}, {}, {}
        for n in TWIN_WEIGHTS:
            delta_w[n], new_m[n], new_v[n] = _adamw(weights[n], grad_w[n], given["m_" + n], given["v_" + n])
    return (loss, grad_x, *[grad_w[n] for n in TWIN_WEIGHTS], *[delta_w[n] for n in TWIN_WEIGHTS],
            *[new_m[n] for n in TWIN_WEIGHTS], *[new_v[n] for n in TWIN_WEIGHTS])
```

```python
import functools
import math

import jax
import jax.numpy as jnp
from jax import lax
from jax.experimental import pallas as pl
from jax.experimental.pallas import tpu as pltpu

F32 = jnp.float32
MXU = jnp.bfloat16

D = 1024
HEADS = 8
NOPE = 128
ROPE = 64
VH = 128
QL = 384
KVL = 256
CHUNK = 64
ROPE_THETA = 10000.0
EPS = 1e-6
G, P, C = 64, 64, 16
DFF = 2816
ADAM_LR, ADAM_B1, ADAM_B2, ADAM_EPS, ADAM_WD, ADAM_STEP = 0.001, 0.9, 0.999, 1e-08, 0.01, 10

LANES = 128
AW = 768
HC = 256
GB = 8
SL = (G // GB) * P
NCOL = 2 * SL // LANES
TC = 256
HALO = 16
NDEV = 8
VMEM_LIMIT = 56 * 1024 * 1024


def _mm(a, b):
    return jnp.dot(a.astype(MXU), b.astype(MXU), preferred_element_type=F32)


def _mm_tn(a, b):
    return lax.dot_general(a.astype(MXU), b.astype(MXU), (((0,), (0,)), ((), ())), preferred_element_type=F32)


def _mm_nt(a, b):
    return lax.dot_general(a.astype(MXU), b.astype(MXU), (((1,), (1,)), ((), ())), preferred_element_type=F32)


def _rms_fwd(x, g):
    r = lax.rsqrt(jnp.mean(x * x, axis=-1, keepdims=True) + EPS)
    xh = x * r
    return xh * g, xh, r


def _rms_bwd(dy, xh, r, g):
    dxh = dy * g
    dx = r * (dxh - xh * jnp.mean(dxh * xh, axis=-1, keepdims=True))
    return dx, jnp.sum(dy * xh, axis=0, keepdims=True)


def _rot_partner(b):
    lane = lax.broadcasted_iota(jnp.int32, b.shape, 1)
    return jnp.where(lane < ROPE // 2, -pltpu.roll(b, LANES - ROPE // 2, 1), pltpu.roll(b, ROPE // 2, 1))


def _rope_blk(b, cos2, sin2):
    return b * cos2 + _rot_partner(b) * sin2


def _unrope_blk(db, cos2, sin2):
    return db * cos2 - _rot_partner(db * sin2)


def _cparams(n_axes, vmem=VMEM_LIMIT):
    return pltpu.CompilerParams(dimension_semantics=("arbitrary",) * n_axes, vmem_limit_bytes=vmem)


def _rowcall(name, fn, tm, row_ins, consts, row_outs, acc_outs):
    n = row_ins[0].shape[0]
    n_in = len(row_ins) + len(consts)
    n_ro = len(row_outs)

    def body(*refs):
        ins, ro_refs, acc_refs = refs[:n_in], refs[n_in:n_in + n_ro], refs[n_in + n_ro:]
        ro, ao = fn(*[r[...] for r in ins])

        @pl.when(pl.program_id(0) == 0)
        def _():
            for r in acc_refs:
                r[...] = jnp.zeros(r.shape, r.dtype)

        for r, val in zip(ro_refs, ro):
            r[...] = val.astype(r.dtype)
        for r, val in zip(acc_refs, ao):
            r[...] += val

    in_specs = [pl.BlockSpec((tm, a.shape[1]), lambda i: (i, 0)) for a in row_ins]
    in_specs += [pl.BlockSpec(c.shape, lambda i, nd=c.ndim: (0,) * nd) for c in consts]
    out_specs = [pl.BlockSpec((tm, w), lambda i: (i, 0)) for w, _ in row_outs]
    out_specs += [pl.BlockSpec(s, lambda i, nd=len(s): (0,) * nd) for s in acc_outs]
    out_shape = [jax.ShapeDtypeStruct((n, w), dt) for w, dt in row_outs]
    out_shape += [jax.ShapeDtypeStruct(s, F32) for s in acc_outs]
    out = pl.pallas_call(body, grid=(n // tm,), in_specs=in_specs, out_specs=out_specs, out_shape=out_shape,
                         compiler_params=_cparams(1), name=name)(*row_ins, *consts)
    return list(out[:n_ro]), list(out[n_ro:])


def _mla_front_tile(x, pos, gmix, wa, gq, gkv, wuq, wukv, invf):
    hn, xh, r = _rms_fwd(x, gmix)
    a = _mm(hn, wa)
    cq, ckv, krb = a[:, :QL], a[:, QL:QL + KVL], a[:, QL + KVL:]
    cqn, cqh, rq = _rms_fwd(cq, gq)
    ckvn, ckvh, rkv = _rms_fwd(ckv, gkv)
    q = _mm(cqn, wuq)
    kv = _mm(ckvn, wukv)
    ang = pos * invf
    cos2, sin2 = jnp.cos(ang), jnp.sin(ang)
    krr = _rope_blk(krb, cos2, sin2)
    qp, kp, vp = [], [], []
    for h in range(HEADS):
        qp += [q[:, h * HC:h * HC + NOPE], _rope_blk(q[:, h * HC + NOPE:(h + 1) * HC], cos2, sin2)]
        kp += [kv[:, h * HC:h * HC + NOPE], krr]
        vp += [kv[:, h * HC + NOPE:(h + 1) * HC]]
    res = (hn, xh, r, cqn, cqh, rq, ckvn, ckvh, rkv, cos2, sin2)
    return jnp.concatenate(qp, axis=1), jnp.concatenate(kp, axis=1), jnp.concatenate(vp, axis=1), res


def _mla_front_fwd(x, pos, gmix, wa, gq, gkv, wuq, wukv, invf, tm):
    def fn(*args):
        qc, kc, v, _ = _mla_front_tile(*args)
        return (qc, kc, v), ()

    return _rowcall("mla_front_fwd", fn, tm, [x, pos], [gmix, wa, gq, gkv, wuq, wukv, invf],
                    [(HEADS * HC, MXU), (HEADS * HC, MXU), (HEADS * VH, MXU)], [])[0]


def _mla_front_bwd(x, pos, dqc, dkc, dv, dh, gmix, wa, gq, gkv, wuq, wukv, invf, wa_t, wuq_t, wukv_t, tm):
    def fn(x, pos, dqc, dkc, dv, dh, gmix, wa, gq, gkv, wuq, wukv, invf, wa_t, wuq_t, wukv_t):
        _, _, _, (hn, xh, r, cqn, cqh, rq, ckvn, ckvh, rkv, cos2, sin2) = _mla_front_tile(
            x, pos, gmix, wa, gq, gkv, wuq, wukv, invf)
        dqc, dkc, dv = dqc.astype(F32), dkc.astype(F32), dv.astype(F32)
        dqp, dkvp = [], []
        dkr = jnp.zeros((x.shape[0], LANES), F32)
        for h in range(HEADS):
            dqp += [dqc[:, h * HC:h * HC + NOPE], _unrope_blk(dqc[:, h * HC + NOPE:(h + 1) * HC], cos2, sin2)]
            dkvp += [dkc[:, h * HC:h * HC + NOPE], dv[:, h * VH:(h + 1) * VH]]
            dkr = dkr + dkc[:, h * HC + NOPE:(h + 1) * HC]
        dq = jnp.concatenate(dqp, axis=1)
        dkv = jnp.concatenate(dkvp, axis=1)
        dkrb = _unrope_blk(dkr, cos2, sin2)
        dcqn = _mm(dq, wuq_t)
        dckvn = _mm(dkv, wukv_t)
        d_wuq = _mm_tn(cqn, dq)
        d_wukv = _mm_tn(ckvn, dkv)
        dcq, d_gq = _rms_bwd(dcqn, cqh, rq, gq)
        dckv, d_gkv = _rms_bwd(dckvn, ckvh, rkv, gkv)
        da = jnp.concatenate([dcq, dckv, dkrb], axis=1)
        d_wa = _mm_tn(hn, da)
        dhn = _mm(da, wa_t)
        dx, d_gmix = _rms_bwd(dhn, xh, r, gmix)
        return (dh + dx,), (d_wa, d_wuq, d_wukv, d_gq, d_gkv, d_gmix)

    return _rowcall("mla_front_bwd", fn, tm, [x, pos, dqc, dkc, dv, dh],
                    [gmix, wa, gq, gkv, wuq, wukv, invf, wa_t, wuq_t, wukv_t], [(D, F32)],
                    [(D, AW), (QL, HEADS * HC), (KVL, HEADS * HC), (1, QL), (1, KVL), (1, D)])


def _pair_tables(n, q_major):
    if q_major:
        pairs = [(qi, ki) for qi in range(n) for ki in range(qi + 1)]
    else:
        pairs = [(qi, ki) for ki in range(n) for qi in range(ki, n)]
    return (jnp.asarray([p[0] for p in pairs], jnp.int32), jnp.asarray([p[1] for p in pairs], jnp.int32))


def _tile_scores(q, k, qi, ki, t):
    s = _mm_nt(q, k) * ((NOPE + ROPE) ** -0.5)
    row = lax.broadcasted_iota(jnp.int32, (t, t), 0) // CHUNK
    col = lax.broadcasted_iota(jnp.int32, (t, t), 1) // CHUNK
    keep = jnp.logical_or(ki < qi, col <= row)
    return s, keep


def _flash_fwd(qc, kc, v, t):
    s_len = qc.shape[0]
    n = s_len // t
    qt, kt = _pair_tables(n, True)

    def body(qt_ref, kt_ref, q_ref, k_ref, v_ref, o_ref, lse_ref, m_sc, l_sc, acc_sc):
        p_id = pl.program_id(1)
        qi, ki = qt_ref[p_id], kt_ref[p_id]

        @pl.when(ki == 0)
        def _():
            m_sc[...] = jnp.full(m_sc.shape, -jnp.inf, F32)
            l_sc[...] = jnp.zeros(l_sc.shape, F32)
            acc_sc[...] = jnp.zeros(acc_sc.shape, F32)

        s, keep = _tile_scores(q_ref[...], k_ref[...], qi, ki, t)
        s = jnp.where(keep, s, -jnp.inf)
        m_old = m_sc[...]
        m_new = jnp.maximum(m_old, jnp.max(s, axis=1, keepdims=True))
        alpha = jnp.exp(m_old - m_new)
        p = jnp.exp(s - m_new)
        l_sc[...] = alpha * l_sc[...] + jnp.sum(p, axis=1, keepdims=True)
        acc_sc[...] = alpha * acc_sc[...] + _mm(p, v_ref[...])
        m_sc[...] = m_new

        @pl.when(ki == qi)
        def _():
            l = l_sc[...]
            o_ref[...] = (acc_sc[...] / l).astype(o_ref.dtype)
            lse_ref[...] = jnp.broadcast_to(m_sc[...] + jnp.log(l), lse_ref.shape)

    grid_spec = pltpu.PrefetchScalarGridSpec(
        num_scalar_prefetch=2, grid=(HEADS, qt.shape[0]),
        in_specs=[pl.BlockSpec((t, HC), lambda h, p, qt, kt: (qt[p], h)),
                  pl.BlockSpec((t, HC), lambda h, p, qt, kt: (kt[p], h)),
                  pl.BlockSpec((t, VH), lambda h, p, qt, kt: (kt[p], h))],
        out_specs=[pl.BlockSpec((t, VH), lambda h, p, qt, kt: (qt[p], h)),
                   pl.BlockSpec((t, LANES), lambda h, p, qt, kt: (qt[p], h))],
        scratch_shapes=[pltpu.VMEM((t, 1), F32), pltpu.VMEM((t, 1), F32), pltpu.VMEM((t, VH), F32)])
    return pl.pallas_call(body, grid_spec=grid_spec,
                          out_shape=[jax.ShapeDtypeStruct((s_len, HEADS * VH), MXU),
                                     jax.ShapeDtypeStruct((s_len, HEADS * LANES), F32)],
                          compiler_params=_cparams(2), name="flash_fwd")(qt, kt, qc, kc, v)


def _tile_dscores(q, k, v, do, lse, delta, qi, ki, t):
    s, keep = _tile_scores(q, k, qi, ki, t)
    p = jnp.where(keep, jnp.exp(s - lse[:, :1]), 0.0)
    dp = _mm_nt(do, v)
    ds = p * (dp - delta[:, :1]) * ((NOPE + ROPE) ** -0.5)
    return p, ds


def _flash_bwd_kv(qc, kc, v, do, lse, delta, t):
    s_len = qc.shape[0]
    n = s_len // t
    qt, kt = _pair_tables(n, False)

    def body(qt_ref, kt_ref, q_ref, k_ref, v_ref, do_ref, lse_ref, dl_ref, dk_ref, dv_ref, dk_sc, dv_sc):
        p_id = pl.program_id(1)
        qi, ki = qt_ref[p_id], kt_ref[p_id]

        @pl.when(qi == ki)
        def _():
            dk_sc[...] = jnp.zeros(dk_sc.shape, F32)
            dv_sc[...] = jnp.zeros(dv_sc.shape, F32)

        q, do = q_ref[...], do_ref[...]
        p, ds = _tile_dscores(q, k_ref[...], v_ref[...], do, lse_ref[...], dl_ref[...], qi, ki, t)
        dv_sc[...] += _mm_tn(p, do)
        dk_sc[...] += _mm_tn(ds, q)

        @pl.when(qi == n - 1)
        def _():
            dk_ref[...] = dk_sc[...].astype(dk_ref.dtype)
            dv_ref[...] = dv_sc[...].astype(dv_ref.dtype)

    qmap = lambda h, p, qt, kt: (qt[p], h)
    kmap = lambda h, p, qt, kt: (kt[p], h)
    grid_spec = pltpu.PrefetchScalarGridSpec(
        num_scalar_prefetch=2, grid=(HEADS, qt.shape[0]),
        in_specs=[pl.BlockSpec((t, HC), qmap), pl.BlockSpec((t, HC), kmap), pl.BlockSpec((t, VH), kmap),
                  pl.BlockSpec((t, VH), qmap), pl.BlockSpec((t, LANES), qmap), pl.BlockSpec((t, LANES), qmap)],
        out_specs=[pl.BlockSpec((t, HC), kmap), pl.BlockSpec((t, VH), kmap)],
        scratch_shapes=[pltpu.VMEM((t, HC), F32), pltpu.VMEM((t, VH), F32)])
    return pl.pallas_call(body, grid_spec=grid_spec,
                          out_shape=[jax.ShapeDtypeStruct((s_len, HEADS * HC), MXU),
                                     jax.ShapeDtypeStruct((s_len, HEADS * VH), MXU)],
                          compiler_params=_cparams(2), name="flash_bwd_kv")(qt, kt, qc, kc, v, do, lse, delta)


def _flash_bwd_q(qc, kc, v, do, lse, delta, t):
    s_len = qc.shape[0]
    n = s_len // t
    qt, kt = _pair_tables(n, True)

    def body(qt_ref, kt_ref, q_ref, k_ref, v_ref, do_ref, lse_ref, dl_ref, dq_ref, dq_sc):
        p_id = pl.program_id(1)
        qi, ki = qt_ref[p_id], kt_ref[p_id]

        @pl.when(ki == 0)
        def _():
            dq_sc[...] = jnp.zeros(dq_sc.shape, F32)

        k = k_ref[...]
        _, ds = _tile_dscores(q_ref[...], k, v_ref[...], do_ref[...], lse_ref[...], dl_ref[...], qi, ki, t)
        dq_sc[...] += _mm(ds, k)

        @pl.when(ki == qi)
        def _():
            dq_ref[...] = dq_sc[...].astype(dq_ref.dtype)

    qmap = lambda h, p, qt, kt: (qt[p], h)
    kmap = lambda h, p, qt, kt: (kt[p], h)
    grid_spec = pltpu.PrefetchScalarGridSpec(
        num_scalar_prefetch=2, grid=(HEADS, qt.shape[0]),
        in_specs=[pl.BlockSpec((t, HC), qmap), pl.BlockSpec((t, HC), kmap), pl.BlockSpec((t, VH), kmap),
                  pl.BlockSpec((t, VH), qmap), pl.BlockSpec((t, LANES), qmap), pl.BlockSpec((t, LANES), qmap)],
        out_specs=[pl.BlockSpec((t, HC), qmap)],
        scratch_shapes=[pltpu.VMEM((t, HC), F32)])
    return pl.pallas_call(body, grid_spec=grid_spec,
                          out_shape=[jax.ShapeDtypeStruct((s_len, HEADS * HC), MXU)],
                          compiler_params=_cparams(2), name="flash_bwd_q")(qt, kt, qc, kc, v, do, lse, delta)[0]


def _attn_out_fwd(x, o, wo, tm):
    def fn(x, o, wo):
        return (x + _mm(o, wo),), ()

    return _rowcall("attn_out_fwd", fn, tm, [x, o], [wo], [(D, F32)], [])[0][0]


def _attn_out_bwd(dh, o, wo_t, tm):
    def fn(dh, o, wo_t):
        do = _mm(dh, wo_t)
        of = o.astype(F32)
        dl = [jnp.broadcast_to(jnp.sum(do[:, h * VH:(h + 1) * VH] * of[:, h * VH:(h + 1) * VH], axis=1, keepdims=True),
                               (dh.shape[0], LANES)) for h in range(HEADS)]
        return (do, jnp.concatenate(dl, axis=1)), (_mm_tn(o, dh),)

    return _rowcall("attn_out_bwd", fn, tm, [dh, o], [wo_t], [(HEADS * VH, MXU), (HEADS * LANES, F32)], [(HEADS * VH, D)])


def _shift_rows(a, k):
    return a if k == 0 else pltpu.roll(a, k % a.shape[0], 0)


def _stack_rows(rows):
    idx = lax.broadcasted_iota(jnp.int32, (8, rows[0].shape[1]), 0)
    out = jnp.zeros((8, rows[0].shape[1]), F32)
    for k, r in enumerate(rows):
        out = jnp.where(idx == k, r, out)
    return out


def _ffn_fwd(h, g, wup, cw, cb, wdown, tm):
    s_len = h.shape[0]
    nj = DFF // TC
    hb = tm // HALO

    def body(h_ref, hp_ref, g_ref, wv_ref, wg_ref, cwv_ref, cwg_ref, cbv_ref, cbg_ref, wd_ref, out_ref, act_ref,
             hn_sc, acc_sc):
        i, j = pl.program_id(0), pl.program_id(1)

        @pl.when(j == 0)
        def _():
            gg = g_ref[...]
            hp = _rms_fwd(hp_ref[...], gg)[0]
            hn_sc[:HALO, :] = jnp.where(i > 0, hp, 0.0).astype(MXU)
            hn_sc[HALO:, :] = _rms_fwd(h_ref[...], gg)[0].astype(MXU)
            acc_sc[...] = jnp.zeros(acc_sc.shape, F32)

        hn = hn_sc[...]

        def conv(w_ref, cw_ref, cb_ref):
            up = jnp.dot(hn, w_ref[...], preferred_element_type=F32)
            cwv = cw_ref[...]
            c = cwv[2:3] * up + cwv[1:2] * _shift_rows(up, 1) + cwv[0:1] * _shift_rows(up, 2)
            return c[HALO:] + cb_ref[...]

        cv = conv(wv_ref, cwv_ref, cbv_ref)
        cg = conv(wg_ref, cwg_ref, cbg_ref)
        act = cg * jax.nn.sigmoid(cg) * cv
        act_ref[...] = act.astype(act_ref.dtype)
        acc_sc[...] += _mm(act, wd_ref[...])

        @pl.when(j == nj - 1)
        def _():
            out_ref[...] = h_ref[...] + acc_sc[...]

    in_specs = [pl.BlockSpec((tm, D), lambda i, j: (i, 0)),
                pl.BlockSpec((HALO, D), lambda i, j: (jnp.maximum(i * hb - 1, 0), 0)),
                pl.BlockSpec((1, D), lambda i, j: (0, 0)),
                pl.BlockSpec((D, TC), lambda i, j: (0, j)), pl.BlockSpec((D, TC), lambda i, j: (0, j + nj)),
                pl.BlockSpec((3, TC), lambda i, j: (0, j)), pl.BlockSpec((3, TC), lambda i, j: (0, j + nj)),
                pl.BlockSpec((1, TC), lambda i, j: (0, j)), pl.BlockSpec((1, TC), lambda i, j: (0, j + nj)),
                pl.BlockSpec((TC, D), lambda i, j: (j, 0))]
    out_specs = [pl.BlockSpec((tm, D), lambda i, j: (i, 0)), pl.BlockSpec((tm, TC), lambda i, j: (i, j))]
    return pl.pallas_call(body, grid=(s_len // tm, nj), in_specs=in_specs, out_specs=out_specs,
                          out_shape=[jax.ShapeDtypeStruct((s_len, D), F32), jax.ShapeDtypeStruct((s_len, DFF), MXU)],
                          scratch_shapes=[pltpu.VMEM((tm + HALO, D), MXU), pltpu.VMEM((tm, D), F32)],
                          compiler_params=_cparams(2), name="ffn_fwd")(h, h, g, wup, wup, cw, cw, cb, cb, wdown)


def _ffn_bwd(h, dout, g, wup, cw, cb, wdown_t, wup_t, tm):
    s_len = h.shape[0]
    nj = DFF // TC
    ni = s_len // tm
    hb = tm // HALO
    rows = tm + 2 * HALO

    def body(h_ref, hp_ref, hx_ref, d_ref, dx_ref, g_ref, wv_ref, wg_ref, cwv_ref, cwg_ref, cbv_ref, cbg_ref,
             wdt_ref, wutv_ref, wutg_ref, din_ref, dupv_ref, dupg_ref, hn_ref, cacc_ref, dg_ref, hn_sc, d_sc, acc_sc):
        i, j = pl.program_id(0), pl.program_id(1)

        @pl.when(j == 0)
        def _():
            gg = g_ref[...]
            hn_sc[:HALO, :] = jnp.where(i > 0, _rms_fwd(hp_ref[...], gg)[0], 0.0).astype(MXU)
            hn = _rms_fwd(h_ref[...], gg)[0].astype(MXU)
            hn_sc[HALO:HALO + tm, :] = hn
            hn_ref[...] = hn
            hn_sc[HALO + tm:, :] = _rms_fwd(hx_ref[...], gg)[0].astype(MXU)
            d_sc[:tm, :] = d_ref[...].astype(MXU)
            d_sc[tm:, :] = jnp.where(i < ni - 1, dx_ref[...], 0.0).astype(MXU)
            acc_sc[...] = jnp.zeros(acc_sc.shape, F32)

        @pl.when(jnp.logical_and(i == 0, j == 0))
        def _():
            cacc_ref[...] = jnp.zeros(cacc_ref.shape, F32)
            dg_ref[...] = jnp.zeros(dg_ref.shape, F32)

        hn = hn_sc[...]
        dd = d_sc[...]

        def half(w_ref, cw_ref, cb_ref):
            up = jnp.dot(hn, w_ref[...], preferred_element_type=F32)
            cwv = cw_ref[...]
            u1, u2 = _shift_rows(up, 1), _shift_rows(up, 2)
            c = (cwv[2:3] * up + cwv[1:2] * u1 + cwv[0:1] * u2)[HALO:] + cb_ref[...]
            return c, (up[HALO:HALO + tm], u1[HALO:HALO + tm], u2[HALO:HALO + tm]), cwv

        cv, upv, cwv = half(wv_ref, cwv_ref, cbv_ref)
        cg, upg, cwg = half(wg_ref, cwg_ref, cbg_ref)
        dact = jnp.dot(dd, wdt_ref[...], preferred_element_type=F32)
        sg = jax.nn.sigmoid(cg)
        dcv = dact * (cg * sg)
        dcg = dact * cv * (sg * (1.0 + cg * (1.0 - sg)))

        def back(dc, ups, cwx, slot, dup_ref, wut_ref):
            dup = (cwx[2:3] * dc + cwx[1:2] * _shift_rows(dc, -1) + cwx[0:1] * _shift_rows(dc, -2))[:tm]
            dct = dc[:tm]
            cacc_ref[slot] += _stack_rows([jnp.sum(dct * ups[2], axis=0, keepdims=True),
                                           jnp.sum(dct * ups[1], axis=0, keepdims=True),
                                           jnp.sum(dct * ups[0], axis=0, keepdims=True),
                                           jnp.sum(dct, axis=0, keepdims=True)])
            dup_ref[...] = dup.astype(dup_ref.dtype)
            return _mm(dup, wut_ref[...])

        acc_sc[...] += back(dcv, upv, cwv, j, dupv_ref, wutv_ref) + back(dcg, upg, cwg, j + nj, dupg_ref, wutg_ref)

        @pl.when(j == nj - 1)
        def _():
            gg = g_ref[...]
            _, xh, r = _rms_fwd(h_ref[...], gg)
            dx, dgp = _rms_bwd(acc_sc[...], xh, r, gg)
            din_ref[...] = d_ref[...] + dx
            dg_ref[...] += dgp

    last_blk = s_len // HALO - 1
    in_specs = [pl.BlockSpec((tm, D), lambda i, j: (i, 0)),
                pl.BlockSpec((HALO, D), lambda i, j: (jnp.maximum(i * hb - 1, 0), 0)),
                pl.BlockSpec((HALO, D), lambda i, j: (jnp.minimum((i + 1) * hb, last_blk), 0)),
                pl.BlockSpec((tm, D), lambda i, j: (i, 0)),
                pl.BlockSpec((HALO, D), lambda i, j: (jnp.minimum((i + 1) * hb, last_blk), 0)),
                pl.BlockSpec((1, D), lambda i, j: (0, 0)),
                pl.BlockSpec((D, TC), lambda i, j: (0, j)), pl.BlockSpec((D, TC), lambda i, j: (0, j + nj)),
                pl.BlockSpec((3, TC), lambda i, j: (0, j)), pl.BlockSpec((3, TC), lambda i, j: (0, j + nj)),
                pl.BlockSpec((1, TC), lambda i, j: (0, j)), pl.BlockSpec((1, TC), lambda i, j: (0, j + nj)),
                pl.BlockSpec((D, TC), lambda i, j: (0, j)),
                pl.BlockSpec((TC, D), lambda i, j: (j, 0)), pl.BlockSpec((TC, D), lambda i, j: (j + nj, 0))]
    out_specs = [pl.BlockSpec((tm, D), lambda i, j: (i, 0)),
                 pl.BlockSpec((tm, TC), lambda i, j: (i, j)), pl.BlockSpec((tm, TC), lambda i, j: (i, j)),
                 pl.BlockSpec((tm, D), lambda i, j: (i, 0)),
                 pl.BlockSpec((2 * nj, 8, TC), lambda i, j: (0, 0, 0)),
                 pl.BlockSpec((1, D), lambda i, j: (0, 0))]
    out_shape = [jax.ShapeDtypeStruct((s_len, D), F32),
                 jax.ShapeDtypeStruct((s_len, DFF), MXU), jax.ShapeDtypeStruct((s_len, DFF), MXU),
                 jax.ShapeDtypeStruct((s_len, D), MXU),
                 jax.ShapeDtypeStruct((2 * nj, 8, TC), F32), jax.ShapeDtypeStruct((1, D), F32)]
    return pl.pallas_call(body, grid=(ni, nj), in_specs=in_specs, out_specs=out_specs, out_shape=out_shape,
                          scratch_shapes=[pltpu.VMEM((rows, D), MXU), pltpu.VMEM((tm + HALO, D), MXU),
                                          pltpu.VMEM((tm, D), F32)],
                          compiler_params=_cparams(2), name="ffn_bwd")(
        h, h, h, dout, dout, g, wup, wup, cw, cw, cb, cb, wdown_t, wup_t, wup_t)


def _matmul_tn(a, b, tn, ts, name):
    s_len, m = a.shape
    n = b.shape[1]

    def body(a_ref, b_ref, o_ref):
        @pl.when(pl.program_id(1) == 0)
        def _():
            o_ref[...] = jnp.zeros(o_ref.shape, F32)

        o_ref[...] += _mm_tn(a_ref[...], b_ref[...])

    return pl.pallas_call(body, grid=(n // tn, s_len // ts),
                          in_specs=[pl.BlockSpec((ts, m), lambda jn, k: (k, 0)), pl.BlockSpec((ts, tn), lambda jn, k: (k, jn))],
                          out_specs=pl.BlockSpec((m, tn), lambda jn, k: (0, jn)),
                          out_shape=jax.ShapeDtypeStruct((m, n), F32), compiler_params=_cparams(2), name=name)(a, b)


def _s5_coefs(lr, li, ldt):
    dt = jnp.exp(ldt)
    mag = jnp.exp(lr * dt)
    th = li * dt
    ar, ai = mag * jnp.cos(th), mag * jnp.sin(th)
    den = lr * lr + li * li
    nr = ar - 1.0
    cr = (nr * lr + ai * li) / den
    ci = (ai * lr - nr * li) / den
    return dt, mag, th, ar, ai, den, nr, cr, ci


def _s5_prep(lr, li, ldt, braw, seg):
    half = NCOL // 2

    def body(lr_ref, li_ref, ldt_ref, b_ref, bb_ref, ap_ref):
        lr_, li_, ldt_ = lr_ref[0], li_ref[0], ldt_ref[0]
        dt, mag, th, ar, ai, den, nr, cr, ci = _s5_coefs(lr_, li_, ldt_)
        br, bi = b_ref[0, :, :SL], b_ref[0, :, SL:]
        bb_ref[0, :, :SL] = (cr * br - ci * bi).astype(bb_ref.dtype)
        bb_ref[0, :, SL:] = (cr * bi + ci * br).astype(bb_ref.dtype)
        for i in range(seg):
            m = jnp.exp((i + 1.0) * (lr_ * dt))
            pr = jnp.broadcast_to(m * jnp.cos((i + 1.0) * th), (8, SL))
            pi = jnp.broadcast_to(m * jnp.sin((i + 1.0) * th), (8, SL))
            for c in range(half):
                ap_ref[0, c, i * 8:(i + 1) * 8, :] = pr[:, c * LANES:(c + 1) * LANES]
                ap_ref[0, half + c, i * 8:(i + 1) * 8, :] = pi[:, c * LANES:(c + 1) * LANES]

    vec = pl.BlockSpec((1, 1, SL), lambda k: (k, 0, 0))
    return pl.pallas_call(
        body, grid=(GB,), in_specs=[vec, vec, vec, pl.BlockSpec((1, LANES, 2 * SL), lambda k: (k, 0, 0))],
        out_specs=[pl.BlockSpec((1, LANES, 2 * SL), lambda k: (k, 0, 0)),
                   pl.BlockSpec((1, NCOL, 8 * seg, LANES), lambda k: (k, 0, 0, 0))],
        out_shape=[jax.ShapeDtypeStruct((GB, LANES, 2 * SL), MXU), jax.ShapeDtypeStruct((GB, NCOL, 8 * seg, LANES), F32)],
        compiler_params=_cparams(1), name="s5_prep")(lr, li, ldt, braw)


def _s5_prep_bwd(lr, li, ldt, braw, dbb, da):
    def body(lr_ref, li_ref, ldt_ref, b_ref, dbb_ref, da_ref, dbraw_ref, dlr_ref, dli_ref, dldt_ref):
        lr_, li_, ldt_ = lr_ref[0], li_ref[0], ldt_ref[0]
        dt, mag, th, ar, ai, den, nr, cr, ci = _s5_coefs(lr_, li_, ldt_)
        br, bi = b_ref[0, :, :SL], b_ref[0, :, SL:]
        gbr, gbi = dbb_ref[0, :, :SL], dbb_ref[0, :, SL:]
        dbraw_ref[0, :, :SL] = cr * gbr + ci * gbi
        dbraw_ref[0, :, SL:] = cr * gbi - ci * gbr
        dcr = jnp.sum(gbr * br + gbi * bi, axis=0, keepdims=True)
        dci = jnp.sum(gbi * br - gbr * bi, axis=0, keepdims=True)
        dar = jnp.sum(da_ref[0, :, :SL], axis=0, keepdims=True)
        dai = jnp.sum(da_ref[0, :, SL:], axis=0, keepdims=True)
        g1, g2 = dcr / den, dci / den
        gden = -(dcr * cr + dci * ci) / den
        gar = dar + g1 * lr_ - g2 * li_
        gai = dai + g1 * li_ + g2 * lr_
        glr = g1 * nr + g2 * ai + 2.0 * lr_ * gden
        gli = g1 * ai - g2 * nr + 2.0 * li_ * gden
        gmag = gar * jnp.cos(th) + gai * jnp.sin(th)
        gth = gai * ar - gar * ai
        dlr_ref[0] = glr + gmag * mag * dt
        dli_ref[0] = gli + gth * dt
        dldt_ref[0] = (gmag * mag * lr_ + gth * li_) * dt

    vec = pl.BlockSpec((1, 1, SL), lambda k: (k, 0, 0))
    mat = pl.BlockSpec((1, LANES, 2 * SL), lambda k: (k, 0, 0))
    return pl.pallas_call(
        body, grid=(GB,), in_specs=[vec, vec, vec, mat, mat, pl.BlockSpec((1, 8, 2 * SL), lambda k: (k, 0, 0))],
        out_specs=[mat, vec, vec, vec],
        out_shape=[jax.ShapeDtypeStruct((GB, LANES, 2 * SL), F32)] + [jax.ShapeDtypeStruct((GB, 1, SL), F32)] * 3,
        compiler_params=_cparams(1), name="s5_prep_bwd")(lr, li, ldt, braw, dbb, da)


def _bcast_row(tile, j):
    return jnp.broadcast_to(tile[j:j + 1, :], tile.shape)


def _segment_scan(src, dst, a_r, a_i, seg, reverse):
    half = NCOL // 2
    sign = -1.0 if reverse else 1.0

    def step(n, carry):
        i = seg - 1 - n if reverse else n
        rows = pl.ds(i, 8, stride=seg)
        out_r, out_i = [], []
        for c in range(half):
            xr, xi = carry[c], carry[half + c]
            nr = a_r[c] * xr - sign * a_i[c] * xi + src.at[c][rows, :]
            ni = a_r[c] * xi + sign * a_i[c] * xr + src.at[half + c][rows, :]
            dst.at[c][rows, :] = nr
            dst.at[half + c][rows, :] = ni
            out_r.append(nr)
            out_i.append(ni)
        return tuple(out_r + out_i)

    zero = jnp.zeros((8, LANES), F32)
    return lax.fori_loop(0, seg, step, (zero,) * NCOL)


def _segment_entries(ends, cin, al_r, al_i, reverse):
    half = NCOL // 2
    sign = -1.0 if reverse else 1.0
    row = lax.broadcasted_iota(jnp.int32, (8, LANES), 0)
    ent, out = [None] * NCOL, [None] * NCOL
    for c in range(half):
        zr, zi = cin[c], cin[half + c]
        er, ei = jnp.zeros((8, LANES), F32), jnp.zeros((8, LANES), F32)
        for j in (range(7, -1, -1) if reverse else range(8)):
            er, ei = jnp.where(row == j, zr, er), jnp.where(row == j, zi, ei)
            fr, fi = _bcast_row(ends[c], j), _bcast_row(ends[half + c], j)
            zr, zi = (al_r[c] * zr - sign * al_i[c] * zi + fr, al_r[c] * zi + sign * al_i[c] * zr + fi)
        ent[c], ent[half + c] = er, ei
        out[c], out[half + c] = zr, zi
    return ent, out


def _chunk_states(u, bb_ref, ap_ref, cin, bu_sc, x_sc, seg):
    half = NCOL // 2
    res = _mm(u, bb_ref[0])
    for c in range(NCOL):
        bu_sc[c] = res[:, c * LANES:(c + 1) * LANES]
    a_r = [ap_ref[0, c, 0:8, :] for c in range(half)]
    a_i = [ap_ref[0, half + c, 0:8, :] for c in range(half)]
    al_r = [ap_ref[0, c, (seg - 1) * 8:seg * 8, :] for c in range(half)]
    al_i = [ap_ref[0, half + c, (seg - 1) * 8:seg * 8, :] for c in range(half)]
    ends = _segment_scan(bu_sc, x_sc, a_r, a_i, seg, False)
    ent, out = _segment_entries(ends, cin, al_r, al_i, False)

    def fix(i, _):
        rows = pl.ds(i, 8, stride=seg)
        tab = pl.ds(pl.multiple_of(i * 8, 8), 8)
        for c in range(half):
            pr, pi = ap_ref.at[0, c][tab, :], ap_ref.at[0, half + c][tab, :]
            x_sc.at[c][rows, :] += pr * ent[c] - pi * ent[half + c]
            x_sc.at[half + c][rows, :] += pr * ent[half + c] + pi * ent[c]
        return 0

    lax.fori_loop(0, seg, fix, 0)
    return out


def _s5_scan_fwd(u, bb, apow, cblk, dskip, t_chunk):
    s_len = u.shape[0]
    nc = s_len // t_chunk
    seg = t_chunk // 8

    def body(u_ref, bb_ref, ap_ref, c_ref, d_ref, y_ref, xin_ref, bu_sc, x_sc, carry_sc):
        @pl.when(pl.program_id(1) == 0)
        def _():
            carry_sc[...] = jnp.zeros(carry_sc.shape, F32)

        uu = u_ref[...]
        cin = [carry_sc[c] for c in range(NCOL)]
        for c in range(NCOL):
            xin_ref[0, 0, :, c * LANES:(c + 1) * LANES] = cin[c]
        out = _chunk_states(uu, bb_ref, ap_ref, cin, bu_sc, x_sc, seg)
        for c in range(NCOL):
            carry_sc[c] = out[c]
        xf = jnp.concatenate([x_sc[c] for c in range(NCOL)], axis=1)
        y_ref[...] = _mm(xf, c_ref[0]) + d_ref[...] * uu

    in_specs = [pl.BlockSpec((t_chunk, LANES), lambda k, c: (c, k)),
                pl.BlockSpec((1, LANES, 2 * SL), lambda k, c: (k, 0, 0)),
                pl.BlockSpec((1, NCOL, 8 * seg, LANES), lambda k, c: (k, 0, 0, 0)),
                pl.BlockSpec((1, 2 * SL, LANES), lambda k, c: (k, 0, 0)),
                pl.BlockSpec((1, LANES), lambda k, c: (0, k))]
    out_specs = [pl.BlockSpec((t_chunk, LANES), lambda k, c: (c, k)),
                 pl.BlockSpec((1, 1, 8, 2 * SL), lambda k, c: (k, c, 0, 0))]
    return pl.pallas_call(body, grid=(GB, nc), in_specs=in_specs, out_specs=out_specs,
                          out_shape=[jax.ShapeDtypeStruct((s_len, D), F32), jax.ShapeDtypeStruct((GB, nc, 8, 2 * SL), F32)],
                          scratch_shapes=[pltpu.VMEM((NCOL, t_chunk, LANES), F32), pltpu.VMEM((NCOL, t_chunk, LANES), F32),
                                          pltpu.VMEM((NCOL, 8, LANES), F32)],
                          compiler_params=_cparams(2), name="s5_scan_fwd")(u, bb, apow, cblk, dskip)


def _s5_scan_bwd(u, dy, xin, bb, bb_t, apow, cblk_t, dskip, t_chunk):
    s_len = u.shape[0]
    nc = s_len // t_chunk
    seg = t_chunk // 8
    half = NCOL // 2

    def body(u_ref, dy_ref, xin_ref, bb_ref, bbt_ref, ap_ref, ct_ref, d_ref, du_ref, dbb_ref, dc_ref, da_ref, dd_ref,
             bu_sc, x_sc, g_sc, carry_sc):
        @pl.when(pl.program_id(1) == 0)
        def _():
            carry_sc[...] = jnp.zeros(carry_sc.shape, F32)
            dbb_ref[...] = jnp.zeros(dbb_ref.shape, F32)
            dc_ref[...] = jnp.zeros(dc_ref.shape, F32)
            da_ref[...] = jnp.zeros(da_ref.shape, F32)
            dd_ref[...] = jnp.zeros(dd_ref.shape, F32)

        uu, dyy = u_ref[...], dy_ref[...]
        cin = [xin_ref[0, 0, :, c * LANES:(c + 1) * LANES] for c in range(NCOL)]
        _chunk_states(uu, bb_ref, ap_ref, cin, bu_sc, x_sc, seg)
        gy = _mm(dyy, ct_ref[0])
        for c in range(NCOL):
            bu_sc[c] = gy[:, c * LANES:(c + 1) * LANES]
        a_r = [ap_ref[0, c, 0:8, :] for c in range(half)]
        a_i = [ap_ref[0, half + c, 0:8, :] for c in range(half)]
        al_r = [ap_ref[0, c, (seg - 1) * 8:seg * 8, :] for c in range(half)]
        al_i = [ap_ref[0, half + c, (seg - 1) * 8:seg * 8, :] for c in range(half)]
        ends = _segment_scan(bu_sc, g_sc, a_r, a_i, seg, True)
        lam_in = [carry_sc[c] for c in range(NCOL)]
        ent, out = _segment_entries(ends, lam_in, al_r, al_i, True)
        for c in range(NCOL):
            carry_sc[c] = out[c]
        row = lax.broadcasted_iota(jnp.int32, (8, LANES), 0)
        last = pl.ds(seg - 1, 8, stride=seg)
        xp0 = [jnp.where(row == 0, cin[c], pltpu.roll(x_sc.at[c][last, :], 1, 0)) for c in range(NCOL)]

        def fix(i, acc):
            rows = pl.ds(i, 8, stride=seg)
            prev = pl.ds(jnp.maximum(i - 1, 0), 8, stride=seg)
            tab = pl.ds(pl.multiple_of((seg - 1 - i) * 8, 8), 8)
            new = list(acc)
            for c in range(half):
                pr, pi = ap_ref.at[0, c][tab, :], ap_ref.at[0, half + c][tab, :]
                lr_ = g_sc.at[c][rows, :] + pr * ent[c] + pi * ent[half + c]
                li_ = g_sc.at[half + c][rows, :] + pr * ent[half + c] - pi * ent[c]
                g_sc.at[c][rows, :] = lr_
                g_sc.at[half + c][rows, :] = li_
                xr = jnp.where(i == 0, xp0[c], x_sc.at[c][prev, :])
                xi = jnp.where(i == 0, xp0[half + c], x_sc.at[half + c][prev, :])
                new[c] = acc[c] + lr_ * xr + li_ * xi
                new[half + c] = acc[half + c] + li_ * xr - lr_ * xi
            return tuple(new)

        zero = jnp.zeros((8, LANES), F32)
        dacc = lax.fori_loop(0, seg, fix, (zero,) * NCOL)
        for c in range(NCOL):
            da_ref[0, :, c * LANES:(c + 1) * LANES] += dacc[c]
        lam = jnp.concatenate([g_sc[c] for c in range(NCOL)], axis=1)
        xf = jnp.concatenate([x_sc[c] for c in range(NCOL)], axis=1)
        dsk = d_ref[...]
        du_ref[...] = _mm(lam, bbt_ref[0]) + dsk * dyy
        dbb_ref[0] += _mm_tn(uu, lam)
        dc_ref[0] += _mm_tn(xf, dyy)
        dd_ref[0] += _stack_rows([jnp.sum(dyy * uu, axis=0, keepdims=True)])

    rev = lambda k, c: (nc - 1 - c, k)
    in_specs = [pl.BlockSpec((t_chunk, LANES), rev), pl.BlockSpec((t_chunk, LANES), rev),
                pl.BlockSpec((1, 1, 8, 2 * SL), lambda k, c: (k, nc - 1 - c, 0, 0)),
                pl.BlockSpec((1, LANES, 2 * SL), lambda k, c: (k, 0, 0)),
                pl.BlockSpec((1, 2 * SL, LANES), lambda k, c: (k, 0, 0)),
                pl.BlockSpec((1, NCOL, 8 * seg, LANES), lambda k, c: (k, 0, 0, 0)),
                pl.BlockSpec((1, LANES, 2 * SL), lambda k, c: (k, 0, 0)),
                pl.BlockSpec((1, LANES), lambda k, c: (0, k))]
    out_specs = [pl.BlockSpec((t_chunk, LANES), rev),
                 pl.BlockSpec((1, LANES, 2 * SL), lambda k, c: (k, 0, 0)),
                 pl.BlockSpec((1, 2 * SL, LANES), lambda k, c: (k, 0, 0)),
                 pl.BlockSpec((1, 8, 2 * SL), lambda k, c: (k, 0, 0)),
                 pl.BlockSpec((1, 8, LANES), lambda k, c: (k, 0, 0))]
    out_shape = [jax.ShapeDtypeStruct((s_len, D), F32), jax.ShapeDtypeStruct((GB, LANES, 2 * SL), F32),
                 jax.ShapeDtypeStruct((GB, 2 * SL, LANES), F32), jax.ShapeDtypeStruct((GB, 8, 2 * SL), F32),
                 jax.ShapeDtypeStruct((GB, 8, LANES), F32)]
    return pl.pallas_call(body, grid=(GB, nc), in_specs=in_specs, out_specs=out_specs, out_shape=out_shape,
                          scratch_shapes=[pltpu.VMEM((NCOL, t_chunk, LANES), F32)] * 3 + [pltpu.VMEM((NCOL, 8, LANES), F32)],
                          compiler_params=_cparams(2), name="s5_scan_bwd")(u, dy, xin, bb, bb_t, apow, cblk_t, dskip)


_GELU_K = math.sqrt(2.0 / math.pi)


def _gelu(y):
    t = jnp.tanh(_GELU_K * (y + 0.044715 * (y * y * y)))
    return 0.5 * y * (1.0 + t), 0.5 * (1.0 + t) + 0.5 * y * (1.0 - t * t) * (_GELU_K * (1.0 + 3 * 0.044715 * (y * y)))


def _s5_in_fwd(h, gmix, win, tm):
    def fn(h, gmix, win):
        return (_mm(_rms_fwd(h, gmix)[0], win),), ()

    return _rowcall("s5_in_fwd", fn, tm, [h], [gmix, win], [(D, F32)], [])[0][0]


def _s5_in_bwd(h, du, dh, gmix, win_t, tm):
    def fn(h, du, dh, gmix, win_t):
        hn, xh, r = _rms_fwd(h, gmix)
        dx, dg = _rms_bwd(_mm(du, win_t), xh, r, gmix)
        return (dh + dx,), (_mm_tn(hn, du), dg)

    return _rowcall("s5_in_bwd", fn, tm, [h, du, dh], [gmix, win_t], [(D, F32)], [(D, D), (1, D)])


def _s5_out_fwd(h, y, wglu, tm):
    def fn(h, y, wglu):
        z = _mm(_gelu(y)[0], wglu)
        return (h + z[:, :D] * jax.nn.sigmoid(z[:, D:]),), ()

    return _rowcall("s5_out_fwd", fn, tm, [h, y], [wglu], [(D, F32)], [])[0][0]


def _s5_out_bwd(dh, y, wglu, wglu_t, tm):
    def fn(dh, y, wglu, wglu_t):
        yg, dgelu = _gelu(y)
        z = _mm(yg, wglu)
        val, sg = z[:, :D], jax.nn.sigmoid(z[:, D:])
        dz = jnp.concatenate([dh * sg, dh * val * sg * (1.0 - sg)], axis=1)
        return (_mm(dz, wglu_t) * dgelu,), (_mm_tn(yg, dz),)

    return _rowcall("s5_out_bwd", fn, tm, [dh, y], [wglu, wglu_t], [(D, F32)], [(D, 2 * D)])


def _final_loss(h, tgt, gfin, tm):
    def fn(h, tgt, gfin):
        y, xh, r = _rms_fwd(h, gfin)
        err = y - tgt
        dx, dg = _rms_bwd(err * (1.0 / D), xh, r, gfin)
        return (dx,), (jnp.sum(err * err, axis=0, keepdims=True), dg)

    return _rowcall("final_loss", fn, tm, [h, tgt], [gfin], [(D, F32)], [(1, D), (1, D)])


def _bf(a):
    return a.astype(MXU)


def _s5_block_mats(b_re, b_im, c_re, c_im):
    gl = G // GB
    eye = jnp.eye(gl, dtype=F32)

    def b_blk(b):
        bt = b.reshape(GB, gl, P, C).transpose(0, 1, 3, 2)
        return (bt[:, :, :, None, :] * eye[None, :, None, :, None]).reshape(GB, gl * C, gl * P)

    def c_blk(cm):
        ct = cm.reshape(GB, gl, C, P).transpose(0, 1, 3, 2)
        return (ct[:, :, :, None, :] * eye[None, :, None, :, None]).reshape(GB, gl * P, gl * C)

    braw = jnp.concatenate([b_blk(b_re), b_blk(b_im)], axis=2)
    cblk = jnp.concatenate([c_blk(c_re), -c_blk(c_im)], axis=1)
    return braw, cblk


def _s5_unblock_b(d):
    gl = G // GB
    eye = jnp.eye(gl, dtype=F32)
    d5 = d.reshape(GB, gl, C, gl, P)
    return jnp.sum(d5 * eye[None, :, None, :, None], axis=3).transpose(0, 1, 3, 2).reshape(G, P, C)


def _s5_unblock_c(d):
    gl = G // GB
    eye = jnp.eye(gl, dtype=F32)
    d5 = d.reshape(GB, gl, P, gl, C)
    return jnp.sum(d5 * eye[None, :, None, :, None], axis=3).transpose(0, 1, 3, 2).reshape(G, C, P)


def _tiles(s_len):
    return min(256, s_len), min(512, s_len), min(512, s_len), min(256, s_len)


def _sequence_step(x, pos, tgt, w):
    s_len = x.shape[0]
    tm_w, tm, t_att, t_chunk = _tiles(s_len)
    seg = t_chunk // 8
    row = lambda v: v.reshape(1, -1)

    wa = _bf(jnp.pad(w["mla_w_a"][0], ((0, 0), (0, AW - (QL + KVL + ROPE)))))
    wuq = _bf(jnp.pad(w["mla_w_uq"][0].reshape(QL, HEADS, NOPE + ROPE), ((0, 0), (0, 0), (0, HC - NOPE - ROPE))).reshape(QL, HEADS * HC))
    wukv = _bf(w["mla_w_ukv"][0])
    wo = _bf(w["mla_w_o"][0])
    inv = 1.0 / (ROPE_THETA ** (jnp.arange(0, ROPE, 2, dtype=F32) / ROPE))
    invf = jnp.concatenate([inv, inv, jnp.zeros((LANES - ROPE,), F32)]).reshape(1, LANES)
    gmix0, gmix1 = row(w["g_mix"][0]), row(w["g_mix"][1])
    gq, gkv = row(w["mla_g_q"][0]), row(w["mla_g_kv"][0])
    win, wglu = _bf(w["ssm_w_in"][0]), _bf(w["ssm_w_glu"][0])
    dskip = row(w["ssm_d"][0])
    lr = w["ssm_lambda_re"][0].reshape(GB, 1, SL)
    li = w["ssm_lambda_im"][0].reshape(GB, 1, SL)
    ldt = jnp.broadcast_to(w["ssm_log_dt"][0][:, None], (G, P)).reshape(GB, 1, SL)
    braw, cblk = _s5_block_mats(w["ssm_b_re"][0], w["ssm_b_im"][0], w["ssm_c_re"][0], w["ssm_c_im"][0])
    cblk = _bf(cblk)
    ffn = []
    for l in range(2):
        wup, wdown = _bf(w["ffn_w_up"][l]), _bf(w["ffn_w_down"][l])
        ffn.append(dict(g=row(w["g_ffn"][l]), wup=wup, wup_t=wup.T, wdown=wdown, wdown_t=wdown.T,
                        cw=w["ffn_conv_w"][l], cb=row(w["ffn_conv_b"][l])))

    qc, kc, v = _mla_front_fwd(x, pos, gmix0, wa, gq, gkv, wuq, wukv, invf, tm_w)
    o, lse = _flash_fwd(qc, kc, v, t_att)
    h1 = _attn_out_fwd(x, o, wo, tm)
    f0 = ffn[0]
    h2, act0 = _ffn_fwd(h1, f0["g"], f0["wup"], f0["cw"], f0["cb"], f0["wdown"], tm)
    bb, apow = _s5_prep(lr, li, ldt, braw, seg)
    u = _s5_in_fwd(h2, gmix1, win, tm)
    y, xin = _s5_scan_fwd(u, bb, apow, cblk, dskip, t_chunk)
    h3 = _s5_out_fwd(h2, y, wglu, tm)
    f1 = ffn[1]
    h4, act1 = _ffn_fwd(h3, f1["g"], f1["wup"], f1["cw"], f1["cb"], f1["wdown"], tm)
    (dh4,), (sq, d_gfinal) = _final_loss(h4, tgt, row(w["g_final"]), tm)
    loss = 0.5 * jnp.sum(sq) / D

    grads = {}

    def ffn_back(hin, dout, act, f):
        din, dupv, dupg, hn, cacc, dg = _ffn_bwd(hin, dout, f["g"], f["wup"], f["cw"], f["cb"], f["wdown_t"], f["wup_t"], tm)
        d_wup = jnp.concatenate([_matmul_tn(hn, dupv, TC, min(1024, s_len), "ffn_dwup_v"),
                                 _matmul_tn(hn, dupg, TC, min(1024, s_len), "ffn_dwup_g")], axis=1)
        d_wdown = _matmul_tn(act, dout, TC, min(512, s_len), "ffn_dwdown")
        cflat = cacc.transpose(1, 0, 2).reshape(8, 2 * DFF)
        return din, d_wup, d_wdown, cflat[:3], cflat[3], dg[0]

    dh3, d_wup1, d_wdown1, d_cw1, d_cb1, d_gffn1 = ffn_back(h3, dh4, act1, f1)
    (dy,), (d_wglu,) = _s5_out_bwd(dh3, y, wglu, wglu.T, tm)
    du, d_bb, d_cblk, d_a, d_dsk = _s5_scan_bwd(u, dy, xin, bb, bb.transpose(0, 2, 1), apow, cblk.transpose(0, 2, 1), dskip, t_chunk)
    d_braw, d_lr, d_li, d_ldt = _s5_prep_bwd(lr, li, ldt, braw, d_bb, d_a)
    (dh2,), (d_win, d_gmix1) = _s5_in_bwd(h2, du, dh3, gmix1, win.T, tm)
    dh1, d_wup0, d_wdown0, d_cw0, d_cb0, d_gffn0 = ffn_back(h1, dh2, act0, f0)
    (do, delta), (d_wo,) = _attn_out_bwd(dh1, o, wo.T, tm)
    dkc, dv = _flash_bwd_kv(qc, kc, v, do, lse, delta, t_att)
    dqc = _flash_bwd_q(qc, kc, v, do, lse, delta, t_att)
    (dx,), (d_wa, d_wuq, d_wukv, d_gq, d_gkv, d_gmix0) = _mla_front_bwd(
        x, pos, dqc, dkc, dv, dh1, gmix0, wa, gq, gkv, wuq, wukv, invf, wa.T, wuq.T, wukv.T, tm_w)

    grads["mla_w_a"] = d_wa[None, :, :QL + KVL + ROPE]
    grads["mla_g_q"] = d_gq
    grads["mla_g_kv"] = d_gkv
    grads["mla_w_uq"] = d_wuq.reshape(QL, HEADS, HC)[:, :, :NOPE + ROPE].reshape(1, QL, HEADS * (NOPE + ROPE))
    grads["mla_w_ukv"] = d_wukv[None]
    grads["mla_w_o"] = d_wo[None]
    grads["ssm_w_in"] = d_win[None]
    grads["ssm_lambda_re"] = d_lr.reshape(1, G, P)
    grads["ssm_lambda_im"] = d_li.reshape(1, G, P)
    grads["ssm_log_dt"] = jnp.sum(d_ldt.reshape(G, P), axis=1)[None]
    grads["ssm_b_re"] = _s5_unblock_b(d_braw[:, :, :SL])[None]
    grads["ssm_b_im"] = _s5_unblock_b(d_braw[:, :, SL:])[None]
    grads["ssm_c_re"] = _s5_unblock_c(d_cblk[:, :SL, :])[None]
    grads["ssm_c_im"] = -_s5_unblock_c(d_cblk[:, SL:, :])[None]
    grads["ssm_d"] = jnp.sum(d_dsk, axis=1).reshape(1, D)
    grads["ssm_w_glu"] = d_wglu[None]
    grads["ffn_w_up"] = jnp.stack([d_wup0, d_wup1])
    grads["ffn_conv_w"] = jnp.stack([d_cw0, d_cw1])
    grads["ffn_conv_b"] = jnp.stack([d_cb0, d_cb1])
    grads["ffn_w_down"] = jnp.stack([d_wdown0, d_wdown1])
    grads["g_mix"] = jnp.concatenate([d_gmix0, d_gmix1], axis=0)
    grads["g_ffn"] = jnp.stack([d_gffn0, d_gffn1])
    grads["g_final"] = d_gfinal[0]
    return loss, dx, grads


MESH = pl.DeviceIdType.MESH
ANY = pl.BlockSpec(memory_space=pl.ANY)


def _all_gather(blk, name):
    r, lanes = blk.shape

    def body(x_ref, out_ref, send_sems, recv_sems, local_sem):
        x, y, c = lax.axis_index("x"), lax.axis_index("y"), lax.axis_index("c")
        me, sibling = (x, y, c), (x, y, 1 - c)
        chips = [(1 - x, y), (x, 1 - y), (1 - x, 1 - y)]

        def slot(px, py, pc):
            return out_ref.at[4 * px + 2 * py + pc]

        def copy(k, block, to, src=None):
            return pltpu.make_async_remote_copy(src_ref=slot(*block) if src is None else src, dst_ref=slot(*block),
                                                send_sem=send_sems.at[k], recv_sem=recv_sems.at[k],
                                                device_id=to, device_id_type=MESH)

        mine = pltpu.make_async_copy(x_ref, slot(*me), local_sem)
        mine.start()
        first = [copy(0, me, sibling, src=x_ref)]
        first += [copy(1 + j, me, (*chip, c), src=x_ref) for j, chip in enumerate(chips)]
        for cp in first:
            cp.start()
        passed = [copy(4 + j, (*chip, c), sibling) for j, chip in enumerate(chips)]
        for j, chip in enumerate(chips):
            copy(1 + j, (*chip, c), me).wait_recv()
            passed[j].start()
        copy(0, sibling, me).wait_recv()
        for j, chip in enumerate(chips):
            copy(4 + j, (*chip, 1 - c), me).wait_recv()
        for cp in first + passed:
            cp.wait_send()
        mine.wait()

    return pl.pallas_call(body, out_shape=jax.ShapeDtypeStruct((NDEV, r, lanes), blk.dtype), in_specs=[ANY], out_specs=ANY,
                          scratch_shapes=[pltpu.SemaphoreType.DMA((7,)), pltpu.SemaphoreType.DMA((7,)), pltpu.SemaphoreType.DMA],
                          name=name)(blk)


def _all_to_all(send, name):
    _, r, lanes = send.shape

    def body(s_ref, land_ref, send_sems, recv_sems, local_sem):
        x, y, c = lax.axis_index("x"), lax.axis_index("y"), lax.axis_index("c")
        me = 4 * x + 2 * y + c
        local = pltpu.make_async_copy(s_ref.at[me], land_ref.at[me], local_sem)
        local.start()
        copies = []
        for m in range(1, NDEV):
            px = 1 - x if m & 4 else x
            py = 1 - y if m & 2 else y
            pc = 1 - c if m & 1 else c
            peer = 4 * px + 2 * py + pc
            out = pltpu.make_async_remote_copy(src_ref=s_ref.at[peer], dst_ref=land_ref.at[me],
                                               send_sem=send_sems.at[m - 1], recv_sem=recv_sems.at[m - 1],
                                               device_id=(px, py, pc), device_id_type=MESH)
            out.start()
            arrival = pltpu.make_async_remote_copy(src_ref=s_ref.at[me], dst_ref=land_ref.at[peer],
                                                   send_sem=send_sems.at[m - 1], recv_sem=recv_sems.at[m - 1],
                                                   device_id=(px, py, pc), device_id_type=MESH)
            copies.append((out, arrival))
        for out, arrival in copies:
            arrival.wait_recv()
        for out, arrival in copies:
            out.wait_send()
        local.wait()

    return pl.pallas_call(body, out_shape=jax.ShapeDtypeStruct(send.shape, send.dtype), in_specs=[ANY], out_specs=ANY,
                          scratch_shapes=[pltpu.SemaphoreType.DMA((7,)), pltpu.SemaphoreType.DMA((7,)), pltpu.SemaphoreType.DMA],
                          name=name)(send)


ADAM_ROWS = 512
PACK = ADAM_ROWS * LANES


def _adamw(w, parts, m, v, name):
    r = w.shape[0]

    def body(w_ref, p_ref, m_ref, v_ref, g_ref, d_ref, m2_ref, v2_ref):
        g = p_ref[0]
        for k in range(1, NDEV):
            g = g + p_ref[k]
        m2 = ADAM_B1 * m_ref[...] + (1.0 - ADAM_B1) * g
        v2 = ADAM_B2 * v_ref[...] + (1.0 - ADAM_B2) * jnp.square(g)
        m_hat = m2 / (1.0 - ADAM_B1 ** ADAM_STEP)
        v_hat = v2 / (1.0 - ADAM_B2 ** ADAM_STEP)
        g_ref[...] = g
        d_ref[...] = -ADAM_LR * (m_hat / (jnp.sqrt(v_hat) + ADAM_EPS) + ADAM_WD * w_ref[...])
        m2_ref[...] = m2
        v2_ref[...] = v2

    flat = pl.BlockSpec((ADAM_ROWS, LANES), lambda i: (i, 0))
    return pl.pallas_call(body, grid=(r // ADAM_ROWS,),
                          in_specs=[flat, pl.BlockSpec((NDEV, ADAM_ROWS, LANES), lambda i: (0, i, 0)), flat, flat],
                          out_specs=[flat] * 4, out_shape=[jax.ShapeDtypeStruct((r, LANES), F32)] * 4,
                          compiler_params=_cparams(1), name=name)(w, parts, m, v)


SHARDED = (("mla_w_a", 1), ("mla_w_uq", 2), ("mla_w_ukv", 2), ("mla_w_o", 1), ("ssm_w_in", 1), ("ssm_d", 1),
           ("ssm_w_glu", 2), ("ffn_w_up", 2), ("ffn_conv_w", 2), ("ffn_w_down", 1))
REPLICATED = ("mla_g_q", "mla_g_kv", "ssm_lambda_re", "ssm_lambda_im", "ssm_log_dt", "ssm_b_re", "ssm_b_im",
              "ssm_c_re", "ssm_c_im", "ffn_conv_b", "g_mix", "g_ffn", "g_final")
WEIGHTS = ("mla_w_a", "mla_g_q", "mla_g_kv", "mla_w_uq", "mla_w_ukv", "mla_w_o", "ssm_w_in", "ssm_lambda_re",
           "ssm_lambda_im", "ssm_log_dt", "ssm_b_re", "ssm_b_im", "ssm_c_re", "ssm_c_im", "ssm_d", "ssm_w_glu",
           "ffn_w_up", "ffn_conv_w", "ffn_conv_b", "ffn_w_down", "g_mix", "g_ffn", "g_final")


def _pack(arrays):
    lead = arrays[0].shape[:-1]
    flat = jnp.concatenate(arrays, axis=-1)
    n = flat.shape[-1]
    padded = -(-n // PACK) * PACK
    flat = jnp.pad(flat, [(0, 0)] * len(lead) + [(0, padded - n)])
    return flat.reshape(lead + (padded // LANES, LANES))


def _unpack(flat, shapes):
    flat = flat.reshape(-1)
    out, off = [], 0
    for s in shapes:
        n = math.prod(s)
        out.append(flat[off:off + n].reshape(s))
        off += n
    return out


def kernel(x, positions, mla_w_a, mla_g_q, mla_g_kv, mla_w_uq, mla_w_ukv, mla_w_o, ssm_w_in, ssm_lambda_re, ssm_lambda_im, ssm_log_dt, ssm_b_re, ssm_b_im, ssm_c_re, ssm_c_im, ssm_d, ssm_w_glu, ffn_w_up, ffn_conv_w, ffn_conv_b, ffn_w_down, g_mix, g_ffn, g_final, loss_target, m_mla_w_a, m_mla_g_q, m_mla_g_kv, m_mla_w_uq, m_mla_w_ukv, m_mla_w_o, m_ssm_w_in, m_ssm_lambda_re, m_ssm_lambda_im, m_ssm_log_dt, m_ssm_b_re, m_ssm_b_im, m_ssm_c_re, m_ssm_c_im, m_ssm_d, m_ssm_w_glu, m_ffn_w_up, m_ffn_conv_w, m_ffn_conv_b, m_ffn_w_down, m_g_mix, m_g_ffn, m_g_final, v_mla_w_a, v_mla_g_q, v_mla_g_kv, v_mla_w_uq, v_mla_w_ukv, v_mla_w_o, v_ssm_w_in, v_ssm_lambda_re, v_ssm_lambda_im, v_ssm_log_dt, v_ssm_b_re, v_ssm_b_im, v_ssm_c_re, v_ssm_c_im, v_ssm_d, v_ssm_w_glu, v_ffn_w_up, v_ffn_conv_w, v_ffn_conv_b, v_ffn_w_down, v_g_mix, v_g_ffn, v_g_final):
    a = dict(locals())
    s_len = x.shape[1]
    sh_names = [n for n, _ in SHARDED]
    sh_shapes = [a[n].shape for n in sh_names]

    gathered = _all_gather(_pack([a[n].reshape(-1) for n in sh_names]), "gather_weights")
    w = {n: a[n] for n in REPLICATED}
    for (n, axis), blocks in zip(SHARDED, zip(*[_unpack(gathered[d], sh_shapes) for d in range(NDEV)])):
        w[n] = jnp.concatenate(blocks, axis=axis)

    loss, dx, grads = _sequence_step(x[0], positions.reshape(s_len, 1).astype(F32), loss_target[0], w)
    loss = lax.psum(loss, ("x", "y", "c"))

    send = _pack([jnp.stack(jnp.split(grads[n], NDEV, axis=axis)).reshape(NDEV, -1) for n, axis in SHARDED])
    landed = _all_to_all(send, "exchange_grads")
    rep_parts = _all_gather(_pack([grads[n].reshape(-1) for n in REPLICATED]), "gather_small_grads")

    out = {}
    for names, shapes, parts, tag in ((sh_names, sh_shapes, landed, "adamw_sharded"),
                                     (list(REPLICATED), [a[n].shape for n in REPLICATED], rep_parts, "adamw_replicated")):
        packed = [_pack([a[pre + n].reshape(-1) for n in names]) for pre in ("", "m_", "v_")]
        res = _adamw(packed[0], parts, packed[1], packed[2], tag)
        for kind, flat in zip(("grad_", "delta_", "new_m_", "new_v_"), res):
            for n, val in zip(names, _unpack(flat, shapes)):
                out[kind + n] = val
    return (loss, dx[None], *[out[kind + n] for kind in ("grad_", "delta_", "new_m_", "new_v_") for n in WEIGHTS])
```

```python
import functools
import math

import jax
import jax.numpy as jnp
from jax import lax
from jax.experimental import pallas as pl
from jax.experimental.pallas import tpu as pltpu

F32 = jnp.float32
MXU = jnp.bfloat16

D = 1024
HEADS = 8
NOPE = 128
ROPE = 64
VH = 128
QL = 384
KVL = 256
CHUNK = 64
ROPE_THETA = 10000.0
EPS = 1e-6
G, P, C = 64, 64, 16
DFF = 2816
ADAM_LR, ADAM_B1, ADAM_B2, ADAM_EPS, ADAM_WD, ADAM_STEP = 0.001, 0.9, 0.999, 1e-08, 0.01, 10

LANES = 128
AW = 768
HC = 256
GB = 8
SL = (G // GB) * P
NCOL = 2 * SL // LANES
TC = 256
HALO = 16
NDEV = 8
VMEM_LIMIT = 56 * 1024 * 1024


def _mm(a, b):
    return jnp.dot(a.astype(MXU), b.astype(MXU), preferred_element_type=F32)


def _mm_tn(a, b):
    return lax.dot_general(a.astype(MXU), b.astype(MXU), (((0,), (0,)), ((), ())), preferred_element_type=F32)


def _mm_nt(a, b):
    return lax.dot_general(a.astype(MXU), b.astype(MXU), (((1,), (1,)), ((), ())), preferred_element_type=F32)


def _rms_fwd(x, g):
    r = lax.rsqrt(jnp.mean(x * x, axis=-1, keepdims=True) + EPS)
    xh = x * r
    return xh * g, xh, r


def _rms_bwd(dy, xh, r, g):
    dxh = dy * g
    dx = r * (dxh - xh * jnp.mean(dxh * xh, axis=-1, keepdims=True))
    return dx, jnp.sum(dy * xh, axis=0, keepdims=True)


def _rot_partner(b):
    lane = lax.broadcasted_iota(jnp.int32, b.shape, 1)
    return jnp.where(lane < ROPE // 2, -pltpu.roll(b, LANES - ROPE // 2, 1), pltpu.roll(b, ROPE // 2, 1))


def _rope_blk(b, cos2, sin2):
    return b * cos2 + _rot_partner(b) * sin2


def _unrope_blk(db, cos2, sin2):
    return db * cos2 - _rot_partner(db * sin2)


def _cparams(n_axes, vmem=VMEM_LIMIT):
    return pltpu.CompilerParams(dimension_semantics=("arbitrary",) * n_axes, vmem_limit_bytes=vmem)


def _rowcall(name, fn, tm, row_ins, consts, row_outs, acc_outs):
    n = row_ins[0].shape[0]
    n_in = len(row_ins) + len(consts)
    n_ro = len(row_outs)

    def body(*refs):
        ins, ro_refs, acc_refs = refs[:n_in], refs[n_in:n_in + n_ro], refs[n_in + n_ro:]
        ro, ao = fn(*[r[...] for r in ins])

        @pl.when(pl.program_id(0) == 0)
        def _():
            for r in acc_refs:
                r[...] = jnp.zeros(r.shape, r.dtype)

        for r, val in zip(ro_refs, ro):
            r[...] = val.astype(r.dtype)
        for r, val in zip(acc_refs, ao):
            r[...] += val

    in_specs = [pl.BlockSpec((tm, a.shape[1]), lambda i: (i, 0)) for a in row_ins]
    in_specs += [pl.BlockSpec(c.shape, lambda i, nd=c.ndim: (0,) * nd) for c in consts]
    out_specs = [pl.BlockSpec((tm, w), lambda i: (i, 0)) for w, _ in row_outs]
    out_specs += [pl.BlockSpec(s, lambda i, nd=len(s): (0,) * nd) for s in acc_outs]
    out_shape = [jax.ShapeDtypeStruct((n, w), dt) for w, dt in row_outs]
    out_shape += [jax.ShapeDtypeStruct(s, F32) for s in acc_outs]
    out = pl.pallas_call(body, grid=(n // tm,), in_specs=in_specs, out_specs=out_specs, out_shape=out_shape,
                         compiler_params=_cparams(1), name=name)(*row_ins, *consts)
    return list(out[:n_ro]), list(out[n_ro:])


def _mla_front_tile(x, pos, gmix, wa, gq, gkv, wuq, wukv, invf):
    hn, xh, r = _rms_fwd(x, gmix)
    a = _mm(hn, wa)
    cq, ckv, krb = a[:, :QL], a[:, QL:QL + KVL], a[:, QL + KVL:]
    cqn, cqh, rq = _rms_fwd(cq, gq)
    ckvn, ckvh, rkv = _rms_fwd(ckv, gkv)
    q = _mm(cqn, wuq)
    kv = _mm(ckvn, wukv)
    ang = pos * invf
    cos2, sin2 = jnp.cos(ang), jnp.sin(ang)
    krr = _rope_blk(krb, cos2, sin2)
    qp, kp, vp = [], [], []
    for h in range(HEADS):
        qp += [q[:, h * HC:h * HC + NOPE], _rope_blk(q[:, h * HC + NOPE:(h + 1) * HC], cos2, sin2)]
        kp += [kv[:, h * HC:h * HC + NOPE], krr]
        vp += [kv[:, h * HC + NOPE:(h + 1) * HC]]
    res = (hn, xh, r, cqn, cqh, rq, ckvn, ckvh, rkv, cos2, sin2)
    return jnp.concatenate(qp, axis=1), jnp.concatenate(kp, axis=1), jnp.concatenate(vp, axis=1), res


def _mla_front_fwd(x, pos, gmix, wa, gq, gkv, wuq, wukv, invf, tm):
    def fn(*args):
        qc, kc, v, _ = _mla_front_tile(*args)
        return (qc * Q_PRESCALE, kc, v), ()

    return _rowcall("mla_front_fwd", fn, tm, [x, pos], [gmix, wa, gq, gkv, wuq, wukv, invf],
                    [(HEADS * HC, MXU), (HEADS * HC, MXU), (HEADS * VH, MXU)], [])[0]


def _mla_front_bwd(x, pos, dqc, dkc, dv, dh, gmix, wa, gq, gkv, wuq, wukv, invf, wa_t, wuq_t, wukv_t, tm):
    def fn(x, pos, dqc, dkc, dv, dh, gmix, wa, gq, gkv, wuq, wukv, invf, wa_t, wuq_t, wukv_t):
        _, _, _, (hn, xh, r, cqn, cqh, rq, ckvn, ckvh, rkv, cos2, sin2) = _mla_front_tile(
            x, pos, gmix, wa, gq, gkv, wuq, wukv, invf)
        dqc, dkc, dv = dqc.astype(F32), dkc.astype(F32), dv.astype(F32)
        dqp, dkvp = [], []
        dkr = jnp.zeros((x.shape[0], LANES), F32)
        for h in range(HEADS):
            dqp += [dqc[:, h * HC:h * HC + NOPE], _unrope_blk(dqc[:, h * HC + NOPE:(h + 1) * HC], cos2, sin2)]
            dkvp += [dkc[:, h * HC:h * HC + NOPE], dv[:, h * VH:(h + 1) * VH]]
            dkr = dkr + dkc[:, h * HC + NOPE:(h + 1) * HC]
        dq = jnp.concatenate(dqp, axis=1)
        dkv = jnp.concatenate(dkvp, axis=1)
        dkrb = _unrope_blk(dkr, cos2, sin2)
        dcqn = _mm(dq, wuq_t)
        dckvn = _mm(dkv, wukv_t)
        d_wuq = _mm_tn(cqn, dq)
        d_wukv = _mm_tn(ckvn, dkv)
        dcq, d_gq = _rms_bwd(dcqn, cqh, rq, gq)
        dckv, d_gkv = _rms_bwd(dckvn, ckvh, rkv, gkv)
        da = jnp.concatenate([dcq, dckv, dkrb], axis=1)
        d_wa = _mm_tn(hn, da)
        dhn = _mm(da, wa_t)
        dx, d_gmix = _rms_bwd(dhn, xh, r, gmix)
        return (dh + dx,), (d_wa, d_wuq, d_wukv, d_gq, d_gkv, d_gmix)

    return _rowcall("mla_front_bwd", fn, tm, [x, pos, dqc, dkc, dv, dh],
                    [gmix, wa, gq, gkv, wuq, wukv, invf, wa_t, wuq_t, wukv_t], [(D, F32)],
                    [(D, AW), (QL, HEADS * HC), (KVL, HEADS * HC), (1, QL), (1, KVL), (1, D)])


SM_SCALE = (NOPE + ROPE) ** -0.5
LOG2E = 1.0 / math.log(2.0)
Q_PRESCALE = SM_SCALE * LOG2E


def _pair_tables(s_len, tq, tk, q_major):
    pairs = [(qi, ki) for qi in range(s_len // tq) for ki in range(s_len // tk) if ki * tk < (qi + 1) * tq]
    if not q_major:
        pairs.sort(key=lambda p: (p[1], p[0]))
    return (jnp.asarray([p[0] for p in pairs], jnp.int32), jnp.asarray([p[1] for p in pairs], jnp.int32))


def _last_key_tile(qi, tq, tk):
    return ((qi + 1) * tq - 1) // tk


def _visible(qi, ki, tq, tk):
    row = qi * (tq // CHUNK) + lax.broadcasted_iota(jnp.int32, (tq, tk), 0) // CHUNK
    col = ki * (tk // CHUNK) + lax.broadcasted_iota(jnp.int32, (tq, tk), 1) // CHUNK
    return col <= row


def _masked_and_not(qi, ki, tq, tk, fn):
    needs_mask = (ki + 1) * tk > qi * tq
    pl.when(needs_mask)(lambda: fn(True))
    pl.when(jnp.logical_not(needs_mask))(lambda: fn(False))


def _flash_fwd(qc, kc, v, tq, tk):
    s_len = qc.shape[0]
    qt, kt = _pair_tables(s_len, tq, tk, True)

    def body(qt_ref, kt_ref, q_ref, k_ref, v_ref, o_ref, lse_ref, m_sc, l_sc, acc_sc):
        p_id = pl.program_id(1)
        qi, ki = qt_ref[p_id], kt_ref[p_id]

        @pl.when(ki == 0)
        def _():
            m_sc[...] = jnp.full(m_sc.shape, -jnp.inf, F32)
            l_sc[...] = jnp.zeros(l_sc.shape, F32)
            acc_sc[...] = jnp.zeros(acc_sc.shape, F32)

        def update(masked):
            s = _mm_nt(q_ref[...], k_ref[...])
            if masked:
                s = jnp.where(_visible(qi, ki, tq, tk), s, -jnp.inf)
            m_old = m_sc[...]
            m_new = jnp.maximum(m_old, jnp.max(s, axis=1, keepdims=True))
            alpha = jnp.exp2(m_old - m_new)
            p = jnp.exp2(s - m_new)
            l_sc[...] = alpha * l_sc[...] + jnp.sum(p, axis=1, keepdims=True)
            acc_sc[...] = alpha * acc_sc[...] + _mm(p, v_ref[...])
            m_sc[...] = m_new

        _masked_and_not(qi, ki, tq, tk, update)

        @pl.when(ki == _last_key_tile(qi, tq, tk))
        def _():
            l = l_sc[...]
            o_ref[...] = (acc_sc[...] / l).astype(o_ref.dtype)
            lse_ref[...] = jnp.broadcast_to(m_sc[...] + jnp.log2(l), lse_ref.shape)

    qmap = lambda h, p, qt, kt: (qt[p], h)
    kmap = lambda h, p, qt, kt: (kt[p], h)
    grid_spec = pltpu.PrefetchScalarGridSpec(
        num_scalar_prefetch=2, grid=(HEADS, qt.shape[0]),
        in_specs=[pl.BlockSpec((tq, HC), qmap), pl.BlockSpec((tk, HC), kmap), pl.BlockSpec((tk, VH), kmap)],
        out_specs=[pl.BlockSpec((tq, VH), qmap), pl.BlockSpec((tq, LANES), qmap)],
        scratch_shapes=[pltpu.VMEM((tq, 1), F32), pltpu.VMEM((tq, 1), F32), pltpu.VMEM((tq, VH), F32)])
    return pl.pallas_call(body, grid_spec=grid_spec,
                          out_shape=[jax.ShapeDtypeStruct((s_len, HEADS * VH), MXU),
                                     jax.ShapeDtypeStruct((s_len, HEADS * LANES), F32)],
                          compiler_params=_cparams(2), name="flash_fwd")(qt, kt, qc, kc, v)


def _tile_dscores(q, k, v, do, lse, delta, qi, ki, tq, tk, masked):
    p = jnp.exp2(_mm_nt(q, k) - lse[:, :1])
    if masked:
        p = jnp.where(_visible(qi, ki, tq, tk), p, 0.0)
    return p, p * (_mm_nt(do, v) - delta[:, :1])


def _flash_bwd_kv(qc, kc, v, do, lse, delta, tq, tk):
    s_len = qc.shape[0]
    qt, kt = _pair_tables(s_len, tq, tk, False)
    nq = s_len // tq

    def body(qt_ref, kt_ref, q_ref, k_ref, v_ref, do_ref, lse_ref, dl_ref, dk_ref, dv_ref, dk_sc, dv_sc):
        p_id = pl.program_id(1)
        qi, ki = qt_ref[p_id], kt_ref[p_id]

        @pl.when(qi == (ki * tk) // tq)
        def _():
            dk_sc[...] = jnp.zeros(dk_sc.shape, F32)
            dv_sc[...] = jnp.zeros(dv_sc.shape, F32)

        def update(masked):
            q, do = q_ref[...], do_ref[...]
            p, ds = _tile_dscores(q, k_ref[...], v_ref[...], do, lse_ref[...], dl_ref[...], qi, ki, tq, tk, masked)
            dv_sc[...] += _mm_tn(p, do)
            dk_sc[...] += _mm_tn(ds, q)

        _masked_and_not(qi, ki, tq, tk, update)

        @pl.when(qi == nq - 1)
        def _():
            dk_ref[...] = (dk_sc[...] * (1.0 / LOG2E)).astype(dk_ref.dtype)
            dv_ref[...] = dv_sc[...].astype(dv_ref.dtype)

    qmap = lambda h, p, qt, kt: (qt[p], h)
    kmap = lambda h, p, qt, kt: (kt[p], h)
    grid_spec = pltpu.PrefetchScalarGridSpec(
        num_scalar_prefetch=2, grid=(HEADS, qt.shape[0]),
        in_specs=[pl.BlockSpec((tq, HC), qmap), pl.BlockSpec((tk, HC), kmap), pl.BlockSpec((tk, VH), kmap),
                  pl.BlockSpec((tq, VH), qmap), pl.BlockSpec((tq, LANES), qmap), pl.BlockSpec((tq, LANES), qmap)],
        out_specs=[pl.BlockSpec((tk, HC), kmap), pl.BlockSpec((tk, VH), kmap)],
        scratch_shapes=[pltpu.VMEM((tk, HC), F32), pltpu.VMEM((tk, VH), F32)])
    return pl.pallas_call(body, grid_spec=grid_spec,
                          out_shape=[jax.ShapeDtypeStruct((s_len, HEADS * HC), MXU),
                                     jax.ShapeDtypeStruct((s_len, HEADS * VH), MXU)],
                          compiler_params=_cparams(2), name="flash_bwd_kv")(qt, kt, qc, kc, v, do, lse, delta)


def _flash_bwd_q(qc, kc, v, do, lse, delta, tq, tk):
    s_len = qc.shape[0]
    qt, kt = _pair_tables(s_len, tq, tk, True)

    def body(qt_ref, kt_ref, q_ref, k_ref, v_ref, do_ref, lse_ref, dl_ref, dq_ref, dq_sc):
        p_id = pl.program_id(1)
        qi, ki = qt_ref[p_id], kt_ref[p_id]

        @pl.when(ki == 0)
        def _():
            dq_sc[...] = jnp.zeros(dq_sc.shape, F32)

        def update(masked):
            k = k_ref[...]
            _, ds = _tile_dscores(q_ref[...], k, v_ref[...], do_ref[...], lse_ref[...], dl_ref[...], qi, ki, tq, tk, masked)
            dq_sc[...] += _mm(ds, k)

        _masked_and_not(qi, ki, tq, tk, update)

        @pl.when(ki == _last_key_tile(qi, tq, tk))
        def _():
            dq_ref[...] = (dq_sc[...] * SM_SCALE).astype(dq_ref.dtype)

    qmap = lambda h, p, qt, kt: (qt[p], h)
    kmap = lambda h, p, qt, kt: (kt[p], h)
    grid_spec = pltpu.PrefetchScalarGridSpec(
        num_scalar_prefetch=2, grid=(HEADS, qt.shape[0]),
        in_specs=[pl.BlockSpec((tq, HC), qmap), pl.BlockSpec((tk, HC), kmap), pl.BlockSpec((tk, VH), kmap),
                  pl.BlockSpec((tq, VH), qmap), pl.BlockSpec((tq, LANES), qmap), pl.BlockSpec((tq, LANES), qmap)],
        out_specs=[pl.BlockSpec((tq, HC), qmap)],
        scratch_shapes=[pltpu.VMEM((tq, HC), F32)])
    return pl.pallas_call(body, grid_spec=grid_spec,
                          out_shape=[jax.ShapeDtypeStruct((s_len, HEADS * HC), MXU)],
                          compiler_params=_cparams(2), name="flash_bwd_q")(qt, kt, qc, kc, v, do, lse, delta)[0]


def _attn_out_fwd(x, o, wo, tm):
    def fn(x, o, wo):
        return (x + _mm(o, wo),), ()

    return _rowcall("attn_out_fwd", fn, tm, [x, o], [wo], [(D, F32)], [])[0][0]


def _attn_out_bwd(dh, o, wo_t, tm):
    def fn(dh, o, wo_t):
        do = _mm(dh, wo_t)
        of = o.astype(F32)
        dl = [jnp.broadcast_to(jnp.sum(do[:, h * VH:(h + 1) * VH] * of[:, h * VH:(h + 1) * VH], axis=1, keepdims=True),
                               (dh.shape[0], LANES)) for h in range(HEADS)]
        return (do, jnp.concatenate(dl, axis=1)), (_mm_tn(o, dh),)

    return _rowcall("attn_out_bwd", fn, tm, [dh, o], [wo_t], [(HEADS * VH, MXU), (HEADS * LANES, F32)], [(HEADS * VH, D)])


def _shift_rows(a, k):
    return a if k == 0 else pltpu.roll(a, k % a.shape[0], 0)


def _stack_rows(rows):
    idx = lax.broadcasted_iota(jnp.int32, (8, rows[0].shape[1]), 0)
    out = jnp.zeros((8, rows[0].shape[1]), F32)
    for k, r in enumerate(rows):
        out = jnp.where(idx == k, r, out)
    return out


def _ffn_fwd(h, g, wup, cw, cb, wdown, tm):
    s_len = h.shape[0]
    nj = DFF // TC
    hb = tm // HALO

    def body(h_ref, hp_ref, g_ref, wv_ref, wg_ref, cwv_ref, cwg_ref, cbv_ref, cbg_ref, wd_ref, out_ref, act_ref,
             hn_sc, acc_sc):
        i, j = pl.program_id(0), pl.program_id(1)

        @pl.when(j == 0)
        def _():
            gg = g_ref[...]
            hp = _rms_fwd(hp_ref[...], gg)[0]
            hn_sc[:HALO, :] = jnp.where(i > 0, hp, 0.0).astype(MXU)
            hn_sc[HALO:, :] = _rms_fwd(h_ref[...], gg)[0].astype(MXU)
            acc_sc[...] = jnp.zeros(acc_sc.shape, F32)

        hn = hn_sc[...]

        def conv(w_ref, cw_ref, cb_ref):
            up = jnp.dot(hn, w_ref[...], preferred_element_type=F32)
            cwv = cw_ref[...]
            c = cwv[2:3] * up + cwv[1:2] * _shift_rows(up, 1) + cwv[0:1] * _shift_rows(up, 2)
            return c[HALO:] + cb_ref[...]

        cv = conv(wv_ref, cwv_ref, cbv_ref)
        cg = conv(wg_ref, cwg_ref, cbg_ref)
        act = cg * jax.nn.sigmoid(cg) * cv
        act_ref[...] = act.astype(act_ref.dtype)
        acc_sc[...] += _mm(act, wd_ref[...])

        @pl.when(j == nj - 1)
        def _():
            out_ref[...] = h_ref[...] + acc_sc[...]

    in_specs = [pl.BlockSpec((tm, D), lambda i, j: (i, 0)),
                pl.BlockSpec((HALO, D), lambda i, j: (jnp.maximum(i * hb - 1, 0), 0)),
                pl.BlockSpec((1, D), lambda i, j: (0, 0)),
                pl.BlockSpec((D, TC), lambda i, j: (0, j)), pl.BlockSpec((D, TC), lambda i, j: (0, j + nj)),
                pl.BlockSpec((3, TC), lambda i, j: (0, j)), pl.BlockSpec((3, TC), lambda i, j: (0, j + nj)),
                pl.BlockSpec((1, TC), lambda i, j: (0, j)), pl.BlockSpec((1, TC), lambda i, j: (0, j + nj)),
                pl.BlockSpec((TC, D), lambda i, j: (j, 0))]
    out_specs = [pl.BlockSpec((tm, D), lambda i, j: (i, 0)), pl.BlockSpec((tm, TC), lambda i, j: (i, j))]
    return pl.pallas_call(body, grid=(s_len // tm, nj), in_specs=in_specs, out_specs=out_specs,
                          out_shape=[jax.ShapeDtypeStruct((s_len, D), F32), jax.ShapeDtypeStruct((s_len, DFF), MXU)],
                          scratch_shapes=[pltpu.VMEM((tm + HALO, D), MXU), pltpu.VMEM((tm, D), F32)],
                          compiler_params=_cparams(2), name="ffn_fwd")(h, h, g, wup, wup, cw, cw, cb, cb, wdown)


def _ffn_bwd(h, dout, g, wup, cw, cb, wdown_t, wup_t, tm):
    s_len = h.shape[0]
    nj = DFF // TC
    ni = s_len // tm
    hb = tm // HALO
    rows = tm + 2 * HALO

    def body(h_ref, hp_ref, hx_ref, d_ref, dx_ref, g_ref, wv_ref, wg_ref, cwv_ref, cwg_ref, cbv_ref, cbg_ref,
             wdt_ref, wutv_ref, wutg_ref, din_ref, dupv_ref, dupg_ref, hn_ref, cacc_ref, dg_ref, hn_sc, d_sc, acc_sc):
        i, j = pl.program_id(0), pl.program_id(1)

        @pl.when(j == 0)
        def _():
            gg = g_ref[...]
            hn_sc[:HALO, :] = jnp.where(i > 0, _rms_fwd(hp_ref[...], gg)[0], 0.0).astype(MXU)
            hn = _rms_fwd(h_ref[...], gg)[0].astype(MXU)
            hn_sc[HALO:HALO + tm, :] = hn
            hn_ref[...] = hn
            hn_sc[HALO + tm:, :] = _rms_fwd(hx_ref[...], gg)[0].astype(MXU)
            d_sc[:tm, :] = d_ref[...].astype(MXU)
            d_sc[tm:, :] = jnp.where(i < ni - 1, dx_ref[...], 0.0).astype(MXU)
            acc_sc[...] = jnp.zeros(acc_sc.shape, F32)

        @pl.when(jnp.logical_and(i == 0, j == 0))
        def _():
            cacc_ref[...] = jnp.zeros(cacc_ref.shape, F32)
            dg_ref[...] = jnp.zeros(dg_ref.shape, F32)

        hn = hn_sc[...]
        dd = d_sc[...]

        def half(w_ref, cw_ref, cb_ref):
            up = jnp.dot(hn, w_ref[...], preferred_element_type=F32)
            cwv = cw_ref[...]
            u1, u2 = _shift_rows(up, 1), _shift_rows(up, 2)
            c = (cwv[2:3] * up + cwv[1:2] * u1 + cwv[0:1] * u2)[HALO:] + cb_ref[...]
            return c, (up[HALO:HALO + tm], u1[HALO:HALO + tm], u2[HALO:HALO + tm]), cwv

        cv, upv, cwv = half(wv_ref, cwv_ref, cbv_ref)
        cg, upg, cwg = half(wg_ref, cwg_ref, cbg_ref)
        dact = jnp.dot(dd, wdt_ref[...], preferred_element_type=F32)
        sg = jax.nn.sigmoid(cg)
        dcv = dact * (cg * sg)
        dcg = dact * cv * (sg * (1.0 + cg * (1.0 - sg)))

        def back(dc, ups, cwx, slot, dup_ref, wut_ref):
            dup = (cwx[2:3] * dc + cwx[1:2] * _shift_rows(dc, -1) + cwx[0:1] * _shift_rows(dc, -2))[:tm]
            dct = dc[:tm]
            cacc_ref[slot] += _stack_rows([jnp.sum(dct * ups[2], axis=0, keepdims=True),
                                           jnp.sum(dct * ups[1], axis=0, keepdims=True),
                                           jnp.sum(dct * ups[0], axis=0, keepdims=True),
                                           jnp.sum(dct, axis=0, keepdims=True)])
            dup_ref[...] = dup.astype(dup_ref.dtype)
            return _mm(dup, wut_ref[...])

        acc_sc[...] += back(dcv, upv, cwv, j, dupv_ref, wutv_ref) + back(dcg, upg, cwg, j + nj, dupg_ref, wutg_ref)

        @pl.when(j == nj - 1)
        def _():
            gg = g_ref[...]
            _, xh, r = _rms_fwd(h_ref[...], gg)
            dx, dgp = _rms_bwd(acc_sc[...], xh, r, gg)
            din_ref[...] = d_ref[...] + dx
            dg_ref[...] += dgp

    last_blk = s_len // HALO - 1
    in_specs = [pl.BlockSpec((tm, D), lambda i, j: (i, 0)),
                pl.BlockSpec((HALO, D), lambda i, j: (jnp.maximum(i * hb - 1, 0), 0)),
                pl.BlockSpec((HALO, D), lambda i, j: (jnp.minimum((i + 1) * hb, last_blk), 0)),
                pl.BlockSpec((tm, D), lambda i, j: (i, 0)),
                pl.BlockSpec((HALO, D), lambda i, j: (jnp.minimum((i + 1) * hb, last_blk), 0)),
                pl.BlockSpec((1, D), lambda i, j: (0, 0)),
                pl.BlockSpec((D, TC), lambda i, j: (0, j)), pl.BlockSpec((D, TC), lambda i, j: (0, j + nj)),
                pl.BlockSpec((3, TC), lambda i, j: (0, j)), pl.BlockSpec((3, TC), lambda i, j: (0, j + nj)),
                pl.BlockSpec((1, TC), lambda i, j: (0, j)), pl.BlockSpec((1, TC), lambda i, j: (0, j + nj)),
                pl.BlockSpec((D, TC), lambda i, j: (0, j)),
                pl.BlockSpec((TC, D), lambda i, j: (j, 0)), pl.BlockSpec((TC, D), lambda i, j: (j + nj, 0))]
    out_specs = [pl.BlockSpec((tm, D), lambda i, j: (i, 0)),
                 pl.BlockSpec((tm, TC), lambda i, j: (i, j)), pl.BlockSpec((tm, TC), lambda i, j: (i, j)),
                 pl.BlockSpec((tm, D), lambda i, j: (i, 0)),
                 pl.BlockSpec((2 * nj, 8, TC), lambda i, j: (0, 0, 0)),
                 pl.BlockSpec((1, D), lambda i, j: (0, 0))]
    out_shape = [jax.ShapeDtypeStruct((s_len, D), F32),
                 jax.ShapeDtypeStruct((s_len, DFF), MXU), jax.ShapeDtypeStruct((s_len, DFF), MXU),
                 jax.ShapeDtypeStruct((s_len, D), MXU),
                 jax.ShapeDtypeStruct((2 * nj, 8, TC), F32), jax.ShapeDtypeStruct((1, D), F32)]
    return pl.pallas_call(body, grid=(ni, nj), in_specs=in_specs, out_specs=out_specs, out_shape=out_shape,
                          scratch_shapes=[pltpu.VMEM((rows, D), MXU), pltpu.VMEM((tm + HALO, D), MXU),
                                          pltpu.VMEM((tm, D), F32)],
                          compiler_params=_cparams(2), name="ffn_bwd")(
        h, h, h, dout, dout, g, wup, wup, cw, cw, cb, cb, wdown_t, wup_t, wup_t)


def _matmul_tn(a, b, tn, ts, name):
    s_len, m = a.shape
    n = b.shape[1]

    def body(a_ref, b_ref, o_ref):
        @pl.when(pl.program_id(1) == 0)
        def _():
            o_ref[...] = jnp.zeros(o_ref.shape, F32)

        o_ref[...] += _mm_tn(a_ref[...], b_ref[...])

    return pl.pallas_call(body, grid=(n // tn, s_len // ts),
                          in_specs=[pl.BlockSpec((ts, m), lambda jn, k: (k, 0)), pl.BlockSpec((ts, tn), lambda jn, k: (k, jn))],
                          out_specs=pl.BlockSpec((m, tn), lambda jn, k: (0, jn)),
                          out_shape=jax.ShapeDtypeStruct((m, n), F32), compiler_params=_cparams(2), name=name)(a, b)


def _s5_coefs(lr, li, ldt):
    dt = jnp.exp(ldt)
    mag = jnp.exp(lr * dt)
    th = li * dt
    ar, ai = mag * jnp.cos(th), mag * jnp.sin(th)
    den = lr * lr + li * li
    nr = ar - 1.0
    cr = (nr * lr + ai * li) / den
    ci = (ai * lr - nr * li) / den
    return dt, mag, th, ar, ai, den, nr, cr, ci


def _s5_prep(lr, li, ldt, braw, seg):
    def body(lr_ref, li_ref, ldt_ref, b_ref, bb_ref, ap_ref):
        lr_, li_, ldt_ = lr_ref[0], li_ref[0], ldt_ref[0]
        dt, mag, th, ar, ai, den, nr, cr, ci = _s5_coefs(lr_, li_, ldt_)
        br, bi = b_ref[0, :, :SL], b_ref[0, :, SL:]
        bb_ref[0, :, :SL] = (cr * br - ci * bi).astype(bb_ref.dtype)
        bb_ref[0, :, SL:] = (cr * bi + ci * br).astype(bb_ref.dtype)
        for i in range(seg):
            m = jnp.exp((i + 1.0) * (lr_ * dt))
            ap_ref[0, i * 8:(i + 1) * 8, :SL] = jnp.broadcast_to(m * jnp.cos((i + 1.0) * th), (8, SL))
            ap_ref[0, i * 8:(i + 1) * 8, SL:] = jnp.broadcast_to(m * jnp.sin((i + 1.0) * th), (8, SL))

    vec = pl.BlockSpec((1, 1, SL), lambda k: (k, 0, 0))
    return pl.pallas_call(
        body, grid=(GB,), in_specs=[vec, vec, vec, pl.BlockSpec((1, LANES, 2 * SL), lambda k: (k, 0, 0))],
        out_specs=[pl.BlockSpec((1, LANES, 2 * SL), lambda k: (k, 0, 0)),
                   pl.BlockSpec((1, 8 * seg, 2 * SL), lambda k: (k, 0, 0))],
        out_shape=[jax.ShapeDtypeStruct((GB, LANES, 2 * SL), MXU), jax.ShapeDtypeStruct((GB, 8 * seg, 2 * SL), F32)],
        compiler_params=_cparams(1), name="s5_prep")(lr, li, ldt, braw)


def _s5_prep_bwd(lr, li, ldt, braw, dbb, da):
    def body(lr_ref, li_ref, ldt_ref, b_ref, dbb_ref, da_ref, dbraw_ref, dlr_ref, dli_ref, dldt_ref):
        lr_, li_, ldt_ = lr_ref[0], li_ref[0], ldt_ref[0]
        dt, mag, th, ar, ai, den, nr, cr, ci = _s5_coefs(lr_, li_, ldt_)
        br, bi = b_ref[0, :, :SL], b_ref[0, :, SL:]
        gbr, gbi = dbb_ref[0, :, :SL], dbb_ref[0, :, SL:]
        dbraw_ref[0, :, :SL] = cr * gbr + ci * gbi
        dbraw_ref[0, :, SL:] = cr * gbi - ci * gbr
        dcr = jnp.sum(gbr * br + gbi * bi, axis=0, keepdims=True)
        dci = jnp.sum(gbi * br - gbr * bi, axis=0, keepdims=True)
        dar = jnp.sum(da_ref[0, :, :SL], axis=0, keepdims=True)
        dai = jnp.sum(da_ref[0, :, SL:], axis=0, keepdims=True)
        g1, g2 = dcr / den, dci / den
        gden = -(dcr * cr + dci * ci) / den
        gar = dar + g1 * lr_ - g2 * li_
        gai = dai + g1 * li_ + g2 * lr_
        glr = g1 * nr + g2 * ai + 2.0 * lr_ * gden
        gli = g1 * ai - g2 * nr + 2.0 * li_ * gden
        gmag = gar * jnp.cos(th) + gai * jnp.sin(th)
        gth = gai * ar - gar * ai
        dlr_ref[0] = glr + gmag * mag * dt
        dli_ref[0] = gli + gth * dt
        dldt_ref[0] = (gmag * mag * lr_ + gth * li_) * dt

    vec = pl.BlockSpec((1, 1, SL), lambda k: (k, 0, 0))
    mat = pl.BlockSpec((1, LANES, 2 * SL), lambda k: (k, 0, 0))
    return pl.pallas_call(
        body, grid=(GB,), in_specs=[vec, vec, vec, mat, mat, pl.BlockSpec((1, 8, 2 * SL), lambda k: (k, 0, 0))],
        out_specs=[mat, vec, vec, vec],
        out_shape=[jax.ShapeDtypeStruct((GB, LANES, 2 * SL), F32)] + [jax.ShapeDtypeStruct((GB, 1, SL), F32)] * 3,
        compiler_params=_cparams(1), name="s5_prep_bwd")(lr, li, ldt, braw, dbb, da)


def _bcast_row(tile, j):
    return jnp.broadcast_to(tile[j:j + 1, :], tile.shape)


def _tile_rows(i):
    return pl.ds(pl.multiple_of(i * 8, 8), 8)


def _col(c):
    return slice(c * LANES, (c + 1) * LANES)


def _permute_rows(ref, seg):
    return jnp.concatenate([ref[pl.ds(i, 8, stride=seg), :] for i in range(seg)], axis=0)


def _unpermute_rows(src, dst, seg):
    for j in range(8):
        dst[j * seg:(j + 1) * seg, :] = src[pl.ds(j, seg, stride=8), :]


SCAN_UNROLL = 2


def _segment_scan(src, dst, ap, seg, reverse):
    half = NCOL // 2
    sign = -1.0 if reverse else 1.0
    a_r = [ap[0:8, _col(c)] for c in range(half)]
    a_i = [ap[0:8, _col(half + c)] for c in range(half)]

    def step(n, carry):
        i = seg - 1 - n if reverse else n
        rows = _tile_rows(i)
        out_r, out_i = [], []
        for c in range(half):
            xr, xi = carry[c], carry[half + c]
            nr = a_r[c] * xr - sign * a_i[c] * xi + src[rows, _col(c)]
            ni = a_r[c] * xi + sign * a_i[c] * xr + src[rows, _col(half + c)]
            dst[rows, _col(c)] = nr
            dst[rows, _col(half + c)] = ni
            out_r.append(nr)
            out_i.append(ni)
        return tuple(out_r + out_i)

    zero = jnp.zeros((8, LANES), F32)
    return lax.fori_loop(0, seg, step, (zero,) * NCOL, unroll=SCAN_UNROLL)


def _segment_entries(ends, cin, ap, seg, reverse):
    half = NCOL // 2
    sign = -1.0 if reverse else 1.0
    al_r = [ap[(seg - 1) * 8:seg * 8, _col(c)] for c in range(half)]
    al_i = [ap[(seg - 1) * 8:seg * 8, _col(half + c)] for c in range(half)]
    row = lax.broadcasted_iota(jnp.int32, (8, LANES), 0)
    ent, out = [None] * NCOL, [None] * NCOL
    for c in range(half):
        zr, zi = cin[c], cin[half + c]
        er, ei = jnp.zeros((8, LANES), F32), jnp.zeros((8, LANES), F32)
        for j in (range(7, -1, -1) if reverse else range(8)):
            er, ei = jnp.where(row == j, zr, er), jnp.where(row == j, zi, ei)
            fr, fi = _bcast_row(ends[c], j), _bcast_row(ends[half + c], j)
            zr, zi = (al_r[c] * zr - sign * al_i[c] * zi + fr, al_r[c] * zi + sign * al_i[c] * zr + fi)
        ent[c], ent[half + c] = er, ei
        out[c], out[half + c] = zr, zi
    return ent, out


def _chunk_states(u, bb_ref, ap, cin, bu_sc, x_sc, seg):
    half = NCOL // 2
    bu_sc[...] = _mm(u, bb_ref[0])
    ends = _segment_scan(bu_sc, x_sc, ap, seg, False)
    ent, out = _segment_entries(ends, cin, ap, seg, False)

    def fix(i, _):
        rows = _tile_rows(i)
        for c in range(half):
            pr, pi = ap[rows, _col(c)], ap[rows, _col(half + c)]
            x_sc[rows, _col(c)] += pr * ent[c] - pi * ent[half + c]
            x_sc[rows, _col(half + c)] += pr * ent[half + c] + pi * ent[c]
        return 0

    lax.fori_loop(0, seg, fix, 0, unroll=SCAN_UNROLL)
    return out


def _s5_scan_fwd(u, bb, apow, cblk, dskip, t_chunk):
    s_len = u.shape[0]
    nc = s_len // t_chunk
    seg = t_chunk // 8

    def body(u_ref, bb_ref, ap_ref, c_ref, d_ref, y_ref, xin_ref, bu_sc, x_sc, y_sc, carry_sc):
        @pl.when(pl.program_id(1) == 0)
        def _():
            carry_sc[...] = jnp.zeros(carry_sc.shape, F32)

        uu = _permute_rows(u_ref, seg)
        cin = [carry_sc[:, _col(c)] for c in range(NCOL)]
        xin_ref[0, 0] = carry_sc[...]
        out = _chunk_states(uu, bb_ref, ap_ref.at[0], cin, bu_sc, x_sc, seg)
        for c in range(NCOL):
            carry_sc[:, _col(c)] = out[c]
        y_sc[...] = _mm(x_sc[...], c_ref[0]) + d_ref[...] * uu
        _unpermute_rows(y_sc, y_ref, seg)

    in_specs = [pl.BlockSpec((t_chunk, LANES), lambda k, c: (c, k)),
                pl.BlockSpec((1, LANES, 2 * SL), lambda k, c: (k, 0, 0)),
                pl.BlockSpec((1, 8 * seg, 2 * SL), lambda k, c: (k, 0, 0)),
                pl.BlockSpec((1, 2 * SL, LANES), lambda k, c: (k, 0, 0)),
                pl.BlockSpec((1, LANES), lambda k, c: (0, k))]
    out_specs = [pl.BlockSpec((t_chunk, LANES), lambda k, c: (c, k)),
                 pl.BlockSpec((1, 1, 8, 2 * SL), lambda k, c: (k, c, 0, 0))]
    return pl.pallas_call(body, grid=(GB, nc), in_specs=in_specs, out_specs=out_specs,
                          out_shape=[jax.ShapeDtypeStruct((s_len, D), F32), jax.ShapeDtypeStruct((GB, nc, 8, 2 * SL), F32)],
                          scratch_shapes=[pltpu.VMEM((t_chunk, 2 * SL), F32), pltpu.VMEM((t_chunk, 2 * SL), F32),
                                          pltpu.VMEM((t_chunk, LANES), F32), pltpu.VMEM((8, 2 * SL), F32)],
                          compiler_params=_cparams(2), name="s5_scan_fwd")(u, bb, apow, cblk, dskip)


def _s5_scan_bwd(u, dy, xin, bb, bb_t, apow, cblk_t, dskip, t_chunk):
    s_len = u.shape[0]
    nc = s_len // t_chunk
    seg = t_chunk // 8
    half = NCOL // 2

    def body(u_ref, dy_ref, xin_ref, bb_ref, bbt_ref, ap_ref, ct_ref, d_ref, du_ref, dbb_ref, dc_ref, da_ref, dd_ref,
             bu_sc, x_sc, g_sc, y_sc, carry_sc):
        @pl.when(pl.program_id(1) == 0)
        def _():
            carry_sc[...] = jnp.zeros(carry_sc.shape, F32)
            dbb_ref[...] = jnp.zeros(dbb_ref.shape, F32)
            dc_ref[...] = jnp.zeros(dc_ref.shape, F32)
            da_ref[...] = jnp.zeros(da_ref.shape, F32)
            dd_ref[...] = jnp.zeros(dd_ref.shape, F32)

        ap = ap_ref.at[0]
        uu, dyy = _permute_rows(u_ref, seg), _permute_rows(dy_ref, seg)
        cin = [xin_ref[0, 0, :, _col(c)] for c in range(NCOL)]
        _chunk_states(uu, bb_ref, ap, cin, bu_sc, x_sc, seg)
        bu_sc[...] = _mm(dyy, ct_ref[0])
        ends = _segment_scan(bu_sc, g_sc, ap, seg, True)
        lam_in = [carry_sc[:, _col(c)] for c in range(NCOL)]
        ent, out = _segment_entries(ends, lam_in, ap, seg, True)
        for c in range(NCOL):
            carry_sc[:, _col(c)] = out[c]
        row = lax.broadcasted_iota(jnp.int32, (8, LANES), 0)
        xp0 = [jnp.where(row == 0, cin[c], pltpu.roll(x_sc[(seg - 1) * 8:seg * 8, _col(c)], 1, 0)) for c in range(NCOL)]

        def fix(i, acc):
            rows, prev, tab = _tile_rows(i), _tile_rows(jnp.maximum(i - 1, 0)), _tile_rows(seg - 1 - i)
            new = list(acc)
            for c in range(half):
                pr, pi = ap[tab, _col(c)], ap[tab, _col(half + c)]
                lr_ = g_sc[rows, _col(c)] + pr * ent[c] + pi * ent[half + c]
                li_ = g_sc[rows, _col(half + c)] + pr * ent[half + c] - pi * ent[c]
                g_sc[rows, _col(c)] = lr_
                g_sc[rows, _col(half + c)] = li_
                xr = jnp.where(i == 0, xp0[c], x_sc[prev, _col(c)])
                xi = jnp.where(i == 0, xp0[half + c], x_sc[prev, _col(half + c)])
                new[c] = acc[c] + lr_ * xr + li_ * xi
                new[half + c] = acc[half + c] + li_ * xr - lr_ * xi
            return tuple(new)

        zero = jnp.zeros((8, LANES), F32)
        dacc = lax.fori_loop(0, seg, fix, (zero,) * NCOL, unroll=SCAN_UNROLL)
        for c in range(NCOL):
            da_ref[0, :, _col(c)] += dacc[c]
        lam = g_sc[...]
        y_sc[...] = _mm(lam, bbt_ref[0]) + d_ref[...] * dyy
        _unpermute_rows(y_sc, du_ref, seg)
        dbb_ref[0] += _mm_tn(uu, lam)
        dc_ref[0] += _mm_tn(x_sc[...], dyy)
        dd_ref[0] += _stack_rows([jnp.sum(dyy * uu, axis=0, keepdims=True)])

    rev = lambda k, c: (nc - 1 - c, k)
    in_specs = [pl.BlockSpec((t_chunk, LANES), rev), pl.BlockSpec((t_chunk, LANES), rev),
                pl.BlockSpec((1, 1, 8, 2 * SL), lambda k, c: (k, nc - 1 - c, 0, 0)),
                pl.BlockSpec((1, LANES, 2 * SL), lambda k, c: (k, 0, 0)),
                pl.BlockSpec((1, 2 * SL, LANES), lambda k, c: (k, 0, 0)),
                pl.BlockSpec((1, 8 * seg, 2 * SL), lambda k, c: (k, 0, 0)),
                pl.BlockSpec((1, LANES, 2 * SL), lambda k, c: (k, 0, 0)),
                pl.BlockSpec((1, LANES), lambda k, c: (0, k))]
    out_specs = [pl.BlockSpec((t_chunk, LANES), rev),
                 pl.BlockSpec((1, LANES, 2 * SL), lambda k, c: (k, 0, 0)),
                 pl.BlockSpec((1, 2 * SL, LANES), lambda k, c: (k, 0, 0)),
                 pl.BlockSpec((1, 8, 2 * SL), lambda k, c: (k, 0, 0)),
                 pl.BlockSpec((1, 8, LANES), lambda k, c: (k, 0, 0))]
    out_shape = [jax.ShapeDtypeStruct((s_len, D), F32), jax.ShapeDtypeStruct((GB, LANES, 2 * SL), F32),
                 jax.ShapeDtypeStruct((GB, 2 * SL, LANES), F32), jax.ShapeDtypeStruct((GB, 8, 2 * SL), F32),
                 jax.ShapeDtypeStruct((GB, 8, LANES), F32)]
    return pl.pallas_call(body, grid=(GB, nc), in_specs=in_specs, out_specs=out_specs, out_shape=out_shape,
                          scratch_shapes=[pltpu.VMEM((t_chunk, 2 * SL), F32)] * 3 + [pltpu.VMEM((t_chunk, LANES), F32),
                                                                                        pltpu.VMEM((8, 2 * SL), F32)],
                          compiler_params=_cparams(2), name="s5_scan_bwd")(u, dy, xin, bb, bb_t, apow, cblk_t, dskip)


_GELU_K = math.sqrt(2.0 / math.pi)


def _gelu(y):
    t = jnp.tanh(_GELU_K * (y + 0.044715 * (y * y * y)))
    return 0.5 * y * (1.0 + t), 0.5 * (1.0 + t) + 0.5 * y * (1.0 - t * t) * (_GELU_K * (1.0 + 3 * 0.044715 * (y * y)))


def _s5_in_fwd(h, gmix, win, tm):
    def fn(h, gmix, win):
        return (_mm(_rms_fwd(h, gmix)[0], win),), ()

    return _rowcall("s5_in_fwd", fn, tm, [h], [gmix, win], [(D, F32)], [])[0][0]


def _s5_in_bwd(h, du, dh, gmix, win_t, tm):
    def fn(h, du, dh, gmix, win_t):
        hn, xh, r = _rms_fwd(h, gmix)
        dx, dg = _rms_bwd(_mm(du, win_t), xh, r, gmix)
        return (dh + dx,), (_mm_tn(hn, du), dg)

    return _rowcall("s5_in_bwd", fn, tm, [h, du, dh], [gmix, win_t], [(D, F32)], [(D, D), (1, D)])


def _s5_out_fwd(h, y, wglu, tm):
    def fn(h, y, wglu):
        z = _mm(_gelu(y)[0], wglu)
        return (h + z[:, :D] * jax.nn.sigmoid(z[:, D:]),), ()

    return _rowcall("s5_out_fwd", fn, tm, [h, y], [wglu], [(D, F32)], [])[0][0]


def _s5_out_bwd(dh, y, wglu, wglu_t, tm):
    def fn(dh, y, wglu, wglu_t):
        yg, dgelu = _gelu(y)
        z = _mm(yg, wglu)
        val, sg = z[:, :D], jax.nn.sigmoid(z[:, D:])
        dz = jnp.concatenate([dh * sg, dh * val * sg * (1.0 - sg)], axis=1)
        return (_mm(dz, wglu_t) * dgelu,), (_mm_tn(yg, dz),)

    return _rowcall("s5_out_bwd", fn, tm, [dh, y], [wglu, wglu_t], [(D, F32)], [(D, 2 * D)])


def _final_loss(h, tgt, gfin, tm):
    def fn(h, tgt, gfin):
        y, xh, r = _rms_fwd(h, gfin)
        err = y - tgt
        dx, dg = _rms_bwd(err * (1.0 / D), xh, r, gfin)
        return (dx,), (jnp.sum(err * err, axis=0, keepdims=True), dg)

    return _rowcall("final_loss", fn, tm, [h, tgt], [gfin], [(D, F32)], [(1, D), (1, D)])


def _bf(a):
    return a.astype(MXU)


def _s5_block_mats(b_re, b_im, c_re, c_im):
    gl = G // GB
    eye = jnp.eye(gl, dtype=F32)

    def b_blk(b):
        bt = b.reshape(GB, gl, P, C).transpose(0, 1, 3, 2)
        return (bt[:, :, :, None, :] * eye[None, :, None, :, None]).reshape(GB, gl * C, gl * P)

    def c_blk(cm):
        ct = cm.reshape(GB, gl, C, P).transpose(0, 1, 3, 2)
        return (ct[:, :, :, None, :] * eye[None, :, None, :, None]).reshape(GB, gl * P, gl * C)

    braw = jnp.concatenate([b_blk(b_re), b_blk(b_im)], axis=2)
    cblk = jnp.concatenate([c_blk(c_re), -c_blk(c_im)], axis=1)
    return braw, cblk


def _s5_unblock_b(d):
    gl = G // GB
    eye = jnp.eye(gl, dtype=F32)
    d5 = d.reshape(GB, gl, C, gl, P)
    return jnp.sum(d5 * eye[None, :, None, :, None], axis=3).transpose(0, 1, 3, 2).reshape(G, P, C)


def _s5_unblock_c(d):
    gl = G // GB
    eye = jnp.eye(gl, dtype=F32)
    d5 = d.reshape(GB, gl, P, gl, C)
    return jnp.sum(d5 * eye[None, :, None, :, None], axis=3).transpose(0, 1, 3, 2).reshape(G, C, P)


def _tiles(s_len):
    return (min(256, s_len), min(512, s_len), min(1024, s_len),
            (min(512, s_len), min(2048, s_len)), (min(512, s_len), min(2048, s_len)), (min(2048, s_len), min(512, s_len)))


def _sequence_step(x, pos, tgt, w):
    s_len = x.shape[0]
    tm_w, tm, t_chunk, t_fwd, t_bkv, t_bq = _tiles(s_len)
    seg = t_chunk // 8
    row = lambda v: v.reshape(1, -1)

    wa = _bf(jnp.pad(w["mla_w_a"][0], ((0, 0), (0, AW - (QL + KVL + ROPE)))))
    wuq = _bf(jnp.pad(w["mla_w_uq"][0].reshape(QL, HEADS, NOPE + ROPE), ((0, 0), (0, 0), (0, HC - NOPE - ROPE))).reshape(QL, HEADS * HC))
    wukv = _bf(w["mla_w_ukv"][0])
    wo = _bf(w["mla_w_o"][0])
    inv = 1.0 / (ROPE_THETA ** (jnp.arange(0, ROPE, 2, dtype=F32) / ROPE))
    invf = jnp.concatenate([inv, inv, jnp.zeros((LANES - ROPE,), F32)]).reshape(1, LANES)
    gmix0, gmix1 = row(w["g_mix"][0]), row(w["g_mix"][1])
    gq, gkv = row(w["mla_g_q"][0]), row(w["mla_g_kv"][0])
    win, wglu = _bf(w["ssm_w_in"][0]), _bf(w["ssm_w_glu"][0])
    dskip = row(w["ssm_d"][0])
    lr = w["ssm_lambda_re"][0].reshape(GB, 1, SL)
    li = w["ssm_lambda_im"][0].reshape(GB, 1, SL)
    ldt = jnp.broadcast_to(w["ssm_log_dt"][0][:, None], (G, P)).reshape(GB, 1, SL)
    braw, cblk = _s5_block_mats(w["ssm_b_re"][0], w["ssm_b_im"][0], w["ssm_c_re"][0], w["ssm_c_im"][0])
    cblk = _bf(cblk)
    ffn = []
    for l in range(2):
        wup, wdown = _bf(w["ffn_w_up"][l]), _bf(w["ffn_w_down"][l])
        ffn.append(dict(g=row(w["g_ffn"][l]), wup=wup, wup_t=wup.T, wdown=wdown, wdown_t=wdown.T,
                        cw=w["ffn_conv_w"][l], cb=row(w["ffn_conv_b"][l])))

    qc, kc, v = _mla_front_fwd(x, pos, gmix0, wa, gq, gkv, wuq, wukv, invf, tm_w)
    o, lse = _flash_fwd(qc, kc, v, *t_fwd)
    h1 = _attn_out_fwd(x, o, wo, tm)
    f0 = ffn[0]
    h2, act0 = _ffn_fwd(h1, f0["g"], f0["wup"], f0["cw"], f0["cb"], f0["wdown"], tm)
    bb, apow = _s5_prep(lr, li, ldt, braw, seg)
    u = _s5_in_fwd(h2, gmix1, win, tm)
    y, xin = _s5_scan_fwd(u, bb, apow, cblk, dskip, t_chunk)
    h3 = _s5_out_fwd(h2, y, wglu, tm)
    f1 = ffn[1]
    h4, act1 = _ffn_fwd(h3, f1["g"], f1["wup"], f1["cw"], f1["cb"], f1["wdown"], tm)
    (dh4,), (sq, d_gfinal) = _final_loss(h4, tgt, row(w["g_final"]), tm)
    loss = 0.5 * jnp.sum(sq) / D

    grads = {}

    def ffn_back(hin, dout, act, f):
        din, dupv, dupg, hn, cacc, dg = _ffn_bwd(hin, dout, f["g"], f["wup"], f["cw"], f["cb"], f["wdown_t"], f["wup_t"], tm)
        d_wup = jnp.concatenate([_matmul_tn(hn, dupv, TC, min(1024, s_len), "ffn_dwup_v"),
                                 _matmul_tn(hn, dupg, TC, min(1024, s_len), "ffn_dwup_g")], axis=1)
        d_wdown = _matmul_tn(act, dout, TC, min(512, s_len), "ffn_dwdown")
        cflat = cacc.transpose(1, 0, 2).reshape(8, 2 * DFF)
        return din, d_wup, d_wdown, cflat[:3], cflat[3], dg[0]

    dh3, d_wup1, d_wdown1, d_cw1, d_cb1, d_gffn1 = ffn_back(h3, dh4, act1, f1)
    (dy,), (d_wglu,) = _s5_out_bwd(dh3, y, wglu, wglu.T, tm)
    du, d_bb, d_cblk, d_a, d_dsk = _s5_scan_bwd(u, dy, xin, bb, bb.transpose(0, 2, 1), apow, cblk.transpose(0, 2, 1), dskip, t_chunk)
    d_braw, d_lr, d_li, d_ldt = _s5_prep_bwd(lr, li, ldt, braw, d_bb, d_a)
    (dh2,), (d_win, d_gmix1) = _s5_in_bwd(h2, du, dh3, gmix1, win.T, tm)
    dh1, d_wup0, d_wdown0, d_cw0, d_cb0, d_gffn0 = ffn_back(h1, dh2, act0, f0)
    (do, delta), (d_wo,) = _attn_out_bwd(dh1, o, wo.T, tm)
    dkc, dv = _flash_bwd_kv(qc, kc, v, do, lse, delta, *t_bkv)
    dqc = _flash_bwd_q(qc, kc, v, do, lse, delta, *t_bq)
    (dx,), (d_wa, d_wuq, d_wukv, d_gq, d_gkv, d_gmix0) = _mla_front_bwd(
        x, pos, dqc, dkc, dv, dh1, gmix0, wa, gq, gkv, wuq, wukv, invf, wa.T, wuq.T, wukv.T, tm_w)

    grads["mla_w_a"] = d_wa[None, :, :QL + KVL + ROPE]
    grads["mla_g_q"] = d_gq
    grads["mla_g_kv"] = d_gkv
    grads["mla_w_uq"] = d_wuq.reshape(QL, HEADS, HC)[:, :, :NOPE + ROPE].reshape(1, QL, HEADS * (NOPE + ROPE))
    grads["mla_w_ukv"] = d_wukv[None]
    grads["mla_w_o"] = d_wo[None]
    grads["ssm_w_in"] = d_win[None]
    grads["ssm_lambda_re"] = d_lr.reshape(1, G, P)
    grads["ssm_lambda_im"] = d_li.reshape(1, G, P)
    grads["ssm_log_dt"] = jnp.sum(d_ldt.reshape(G, P), axis=1)[None]
    grads["ssm_b_re"] = _s5_unblock_b(d_braw[:, :, :SL])[None]
    grads["ssm_b_im"] = _s5_unblock_b(d_braw[:, :, SL:])[None]
    grads["ssm_c_re"] = _s5_unblock_c(d_cblk[:, :SL, :])[None]
    grads["ssm_c_im"] = -_s5_unblock_c(d_cblk[:, SL:, :])[None]
    grads["ssm_d"] = jnp.sum(d_dsk, axis=1).reshape(1, D)
    grads["ssm_w_glu"] = d_wglu[None]
    grads["ffn_w_up"] = jnp.stack([d_wup0, d_wup1])
    grads["ffn_conv_w"] = jnp.stack([d_cw0, d_cw1])
    grads["ffn_conv_b"] = jnp.stack([d_cb0, d_cb1])
    grads["ffn_w_down"] = jnp.stack([d_wdown0, d_wdown1])
    grads["g_mix"] = jnp.concatenate([d_gmix0, d_gmix1], axis=0)
    grads["g_ffn"] = jnp.stack([d_gffn0, d_gffn1])
    grads["g_final"] = d_gfinal[0]
    return loss, dx, grads


MESH = pl.DeviceIdType.MESH
ANY = pl.BlockSpec(memory_space=pl.ANY)


def _all_gather(blk, name):
    r, lanes = blk.shape

    def body(x_ref, out_ref, send_sems, recv_sems, local_sem):
        x, y, c = lax.axis_index("x"), lax.axis_index("y"), lax.axis_index("c")
        me, sibling = (x, y, c), (x, y, 1 - c)
        chips = [(1 - x, y), (x, 1 - y), (1 - x, 1 - y)]

        def slot(px, py, pc):
            return out_ref.at[4 * px + 2 * py + pc]

        def copy(k, block, to, src=None):
            return pltpu.make_async_remote_copy(src_ref=slot(*block) if src is None else src, dst_ref=slot(*block),
                                                send_sem=send_sems.at[k], recv_sem=recv_sems.at[k],
                                                device_id=to, device_id_type=MESH)

        mine = pltpu.make_async_copy(x_ref, slot(*me), local_sem)
        mine.start()
        first = [copy(0, me, sibling, src=x_ref)]
        first += [copy(1 + j, me, (*chip, c), src=x_ref) for j, chip in enumerate(chips)]
        for cp in first:
            cp.start()
        passed = [copy(4 + j, (*chip, c), sibling) for j, chip in enumerate(chips)]
        for j, chip in enumerate(chips):
            copy(1 + j, (*chip, c), me).wait_recv()
            passed[j].start()
        copy(0, sibling, me).wait_recv()
        for j, chip in enumerate(chips):
            copy(4 + j, (*chip, 1 - c), me).wait_recv()
        for cp in first + passed:
            cp.wait_send()
        mine.wait()

    return pl.pallas_call(body, out_shape=jax.ShapeDtypeStruct((NDEV, r, lanes), blk.dtype), in_specs=[ANY], out_specs=ANY,
                          scratch_shapes=[pltpu.SemaphoreType.DMA((7,)), pltpu.SemaphoreType.DMA((7,)), pltpu.SemaphoreType.DMA],
                          name=name)(blk)


def _all_to_all(send, name):
    _, r, lanes = send.shape

    def body(s_ref, land_ref, send_sems, recv_sems, local_sem):
        x, y, c = lax.axis_index("x"), lax.axis_index("y"), lax.axis_index("c")
        me = 4 * x + 2 * y + c
        local = pltpu.make_async_copy(s_ref.at[me], land_ref.at[me], local_sem)
        local.start()
        copies = []
        for m in range(1, NDEV):
            px = 1 - x if m & 4 else x
            py = 1 - y if m & 2 else y
            pc = 1 - c if m & 1 else c
            peer = 4 * px + 2 * py + pc
            out = pltpu.make_async_remote_copy(src_ref=s_ref.at[peer], dst_ref=land_ref.at[me],
                                               send_sem=send_sems.at[m - 1], recv_sem=recv_sems.at[m - 1],
                                               device_id=(px, py, pc), device_id_type=MESH)
            out.start()
            arrival = pltpu.make_async_remote_copy(src_ref=s_ref.at[me], dst_ref=land_ref.at[peer],
                                                   send_sem=send_sems.at[m - 1], recv_sem=recv_sems.at[m - 1],
                                                   device_id=(px, py, pc), device_id_type=MESH)
            copies.append((out, arrival))
        for out, arrival in copies:
            arrival.wait_recv()
        for out, arrival in copies:
            out.wait_send()
        local.wait()

    return pl.pallas_call(body, out_shape=jax.ShapeDtypeStruct(send.shape, send.dtype), in_specs=[ANY], out_specs=ANY,
                          scratch_shapes=[pltpu.SemaphoreType.DMA((7,)), pltpu.SemaphoreType.DMA((7,)), pltpu.SemaphoreType.DMA],
                          name=name)(send)


ADAM_ROWS = 512
PACK = ADAM_ROWS * LANES


def _adamw(w, parts, m, v, name):
    r = w.shape[0]

    def body(w_ref, p_ref, m_ref, v_ref, g_ref, d_ref, m2_ref, v2_ref):
        g = p_ref[0]
        for k in range(1, NDEV):
            g = g + p_ref[k]
        m2 = ADAM_B1 * m_ref[...] + (1.0 - ADAM_B1) * g
        v2 = ADAM_B2 * v_ref[...] + (1.0 - ADAM_B2) * jnp.square(g)
        m_hat = m2 / (1.0 - ADAM_B1 ** ADAM_STEP)
        v_hat = v2 / (1.0 - ADAM_B2 ** ADAM_STEP)
        g_ref[...] = g
        d_ref[...] = -ADAM_LR * (m_hat / (jnp.sqrt(v_hat) + ADAM_EPS) + ADAM_WD * w_ref[...])
        m2_ref[...] = m2
        v2_ref[...] = v2

    flat = pl.BlockSpec((ADAM_ROWS, LANES), lambda i: (i, 0))
    return pl.pallas_call(body, grid=(r // ADAM_ROWS,),
                          in_specs=[flat, pl.BlockSpec((NDEV, ADAM_ROWS, LANES), lambda i: (0, i, 0)), flat, flat],
                          out_specs=[flat] * 4, out_shape=[jax.ShapeDtypeStruct((r, LANES), F32)] * 4,
                          compiler_params=_cparams(1), name=name)(w, parts, m, v)


SHARDED = (("mla_w_a", 1), ("mla_w_uq", 2), ("mla_w_ukv", 2), ("mla_w_o", 1), ("ssm_w_in", 1), ("ssm_d", 1),
           ("ssm_w_glu", 2), ("ffn_w_up", 2), ("ffn_conv_w", 2), ("ffn_w_down", 1))
REPLICATED = ("mla_g_q", "mla_g_kv", "ssm_lambda_re", "ssm_lambda_im", "ssm_log_dt", "ssm_b_re", "ssm_b_im",
              "ssm_c_re", "ssm_c_im", "ffn_conv_b", "g_mix", "g_ffn", "g_final")
WEIGHTS = ("mla_w_a", "mla_g_q", "mla_g_kv", "mla_w_uq", "mla_w_ukv", "mla_w_o", "ssm_w_in", "ssm_lambda_re",
           "ssm_lambda_im", "ssm_log_dt", "ssm_b_re", "ssm_b_im", "ssm_c_re", "ssm_c_im", "ssm_d", "ssm_w_glu",
           "ffn_w_up", "ffn_conv_w", "ffn_conv_b", "ffn_w_down", "g_mix", "g_ffn", "g_final")


def _pack(arrays):
    lead = arrays[0].shape[:-1]
    flat = jnp.concatenate(arrays, axis=-1)
    n = flat.shape[-1]
    padded = -(-n // PACK) * PACK
    flat = jnp.pad(flat, [(0, 0)] * len(lead) + [(0, padded - n)])
    return flat.reshape(lead + (padded // LANES, LANES))


def _unpack(flat, shapes):
    flat = flat.reshape(-1)
    out, off = [], 0
    for s in shapes:
        n = math.prod(s)
        out.append(flat[off:off + n].reshape(s))
        off += n
    return out


def kernel(x, positions, mla_w_a, mla_g_q, mla_g_kv, mla_w_uq, mla_w_ukv, mla_w_o, ssm_w_in, ssm_lambda_re, ssm_lambda_im, ssm_log_dt, ssm_b_re, ssm_b_im, ssm_c_re, ssm_c_im, ssm_d, ssm_w_glu, ffn_w_up, ffn_conv_w, ffn_conv_b, ffn_w_down, g_mix, g_ffn, g_final, loss_target, m_mla_w_a, m_mla_g_q, m_mla_g_kv, m_mla_w_uq, m_mla_w_ukv, m_mla_w_o, m_ssm_w_in, m_ssm_lambda_re, m_ssm_lambda_im, m_ssm_log_dt, m_ssm_b_re, m_ssm_b_im, m_ssm_c_re, m_ssm_c_im, m_ssm_d, m_ssm_w_glu, m_ffn_w_up, m_ffn_conv_w, m_ffn_conv_b, m_ffn_w_down, m_g_mix, m_g_ffn, m_g_final, v_mla_w_a, v_mla_g_q, v_mla_g_kv, v_mla_w_uq, v_mla_w_ukv, v_mla_w_o, v_ssm_w_in, v_ssm_lambda_re, v_ssm_lambda_im, v_ssm_log_dt, v_ssm_b_re, v_ssm_b_im, v_ssm_c_re, v_ssm_c_im, v_ssm_d, v_ssm_w_glu, v_ffn_w_up, v_ffn_conv_w, v_ffn_conv_b, v_ffn_w_down, v_g_mix, v_g_ffn, v_g_final):
    a = dict(locals())
    s_len = x.shape[1]
    sh_names = [n for n, _ in SHARDED]
    sh_shapes = [a[n].shape for n in sh_names]

    gathered = _all_gather(_pack([a[n].reshape(-1) for n in sh_names]), "gather_weights")
    w = {n: a[n] for n in REPLICATED}
    for (n, axis), blocks in zip(SHARDED, zip(*[_unpack(gathered[d], sh_shapes) for d in range(NDEV)])):
        w[n] = jnp.concatenate(blocks, axis=axis)

    loss, dx, grads = _sequence_step(x[0], positions.reshape(s_len, 1).astype(F32), loss_target[0], w)
    loss = lax.psum(loss, ("x", "y", "c"))

    send = _pack([jnp.stack(jnp.split(grads[n], NDEV, axis=axis)).reshape(NDEV, -1) for n, axis in SHARDED])
    landed = _all_to_all(send, "exchange_grads")
    rep_parts = _all_gather(_pack([grads[n].reshape(-1) for n in REPLICATED]), "gather_small_grads")

    out = {}
    for names, shapes, parts, tag in ((sh_names, sh_shapes, landed, "adamw_sharded"),
                                     (list(REPLICATED), [a[n].shape for n in REPLICATED], rep_parts, "adamw_replicated")):
        packed = [_pack([a[pre + n].reshape(-1) for n in names]) for pre in ("", "m_", "v_")]
        res = _adamw(packed[0], parts, packed[1], packed[2], tag)
        for kind, flat in zip(("grad_", "delta_", "new_m_", "new_v_"), res):
            for n, val in zip(names, _unpack(flat, shapes)):
                out[kind + n] = val
    return (loss, dx[None], *[out[kind + n] for kind in ("grad_", "delta_", "new_m_", "new_v_") for n in WEIGHTS])
```

```python
import functools
import math

import jax
import jax.numpy as jnp
from jax import lax
from jax.experimental import pallas as pl
from jax.experimental.pallas import tpu as pltpu

F32 = jnp.float32
MXU = jnp.bfloat16

D = 1024
HEADS = 8
NOPE = 128
ROPE = 64
VH = 128
QL = 384
KVL = 256
CHUNK = 64
ROPE_THETA = 10000.0
EPS = 1e-6
G, P, C = 64, 64, 16
DFF = 2816
ADAM_LR, ADAM_B1, ADAM_B2, ADAM_EPS, ADAM_WD, ADAM_STEP = 0.001, 0.9, 0.999, 1e-08, 0.01, 10

LANES = 128
AW = 768
HC = 256
GB = 8
SL = (G // GB) * P
NCOL = 2 * SL // LANES
TC = DFF
HALO = 16
NDEV = 8
VMEM_LIMIT = 56 * 1024 * 1024


def _mm(a, b):
    return jnp.dot(a.astype(MXU), b.astype(MXU), preferred_element_type=F32)


def _mm_tn(a, b):
    return lax.dot_general(a.astype(MXU), b.astype(MXU), (((0,), (0,)), ((), ())), preferred_element_type=F32)


def _mm_nt(a, b):
    return lax.dot_general(a.astype(MXU), b.astype(MXU), (((1,), (1,)), ((), ())), preferred_element_type=F32)


def _rms_fwd(x, g):
    r = lax.rsqrt(jnp.mean(x * x, axis=-1, keepdims=True) + EPS)
    xh = x * r
    return xh * g, xh, r


def _rms_bwd(dy, xh, r, g):
    dxh = dy * g
    dx = r * (dxh - xh * jnp.mean(dxh * xh, axis=-1, keepdims=True))
    return dx, jnp.sum(dy * xh, axis=0, keepdims=True)


def _rot_partner(b):
    lane = lax.broadcasted_iota(jnp.int32, b.shape, 1)
    return jnp.where(lane < ROPE // 2, -pltpu.roll(b, LANES - ROPE // 2, 1), pltpu.roll(b, ROPE // 2, 1))


def _rope_blk(b, cos2, sin2):
    return b * cos2 + _rot_partner(b) * sin2


def _unrope_blk(db, cos2, sin2):
    return db * cos2 - _rot_partner(db * sin2)


def _cparams(n_axes, vmem=VMEM_LIMIT):
    return pltpu.CompilerParams(dimension_semantics=("arbitrary",) * n_axes, vmem_limit_bytes=vmem)


def _rowcall(name, fn, tm, row_ins, consts, row_outs, acc_outs):
    n = row_ins[0].shape[0]
    n_in = len(row_ins) + len(consts)
    n_ro = len(row_outs)

    def body(*refs):
        ins, ro_refs, acc_refs = refs[:n_in], refs[n_in:n_in + n_ro], refs[n_in + n_ro:]
        ro, ao = fn(*[r[...] for r in ins])

        @pl.when(pl.program_id(0) == 0)
        def _():
            for r in acc_refs:
                r[...] = jnp.zeros(r.shape, r.dtype)

        for r, val in zip(ro_refs, ro):
            r[...] = val.astype(r.dtype)
        for r, val in zip(acc_refs, ao):
            r[...] += val

    in_specs = [pl.BlockSpec((tm, a.shape[1]), lambda i: (i, 0)) for a in row_ins]
    in_specs += [pl.BlockSpec(c.shape, lambda i, nd=c.ndim: (0,) * nd) for c in consts]
    out_specs = [pl.BlockSpec((tm, w), lambda i: (i, 0)) for w, _ in row_outs]
    out_specs += [pl.BlockSpec(s, lambda i, nd=len(s): (0,) * nd) for s in acc_outs]
    out_shape = [jax.ShapeDtypeStruct((n, w), dt) for w, dt in row_outs]
    out_shape += [jax.ShapeDtypeStruct(s, F32) for s in acc_outs]
    out = pl.pallas_call(body, grid=(n // tm,), in_specs=in_specs, out_specs=out_specs, out_shape=out_shape,
                         compiler_params=_cparams(1), name=name)(*row_ins, *consts)
    return list(out[:n_ro]), list(out[n_ro:])


def _mla_front_tile(x, pos, gmix, wa, gq, gkv, wuq, wukv, invf):
    hn, xh, r = _rms_fwd(x, gmix)
    a = _mm(hn, wa)
    cq, ckv, krb = a[:, :QL], a[:, QL:QL + KVL], a[:, QL + KVL:]
    cqn, cqh, rq = _rms_fwd(cq, gq)
    ckvn, ckvh, rkv = _rms_fwd(ckv, gkv)
    q = _mm(cqn, wuq)
    kv = _mm(ckvn, wukv)
    ang = pos * invf
    cos2, sin2 = jnp.cos(ang), jnp.sin(ang)
    krr = _rope_blk(krb, cos2, sin2)
    qp, kp, vp = [], [], []
    for h in range(HEADS):
        qp += [q[:, h * HC:h * HC + NOPE], _rope_blk(q[:, h * HC + NOPE:(h + 1) * HC], cos2, sin2)]
        kp += [kv[:, h * HC:h * HC + NOPE], krr]
        vp += [kv[:, h * HC + NOPE:(h + 1) * HC]]
    res = (hn, xh, r, cqn, cqh, rq, ckvn, ckvh, rkv, cos2, sin2)
    return jnp.concatenate(qp, axis=1), jnp.concatenate(kp, axis=1), jnp.concatenate(vp, axis=1), res


def _mla_front_fwd(x, pos, gmix, wa, gq, gkv, wuq, wukv, invf, tm):
    def fn(*args):
        qc, kc, v, _ = _mla_front_tile(*args)
        return (qc * Q_PRESCALE, kc, v), ()

    return _rowcall("mla_front_fwd", fn, tm, [x, pos], [gmix, wa, gq, gkv, wuq, wukv, invf],
                    [(HEADS * HC, MXU), (HEADS * HC, MXU), (HEADS * VH, MXU)], [])[0]


def _mla_front_bwd(x, pos, dqc, dkc, dv, dh, gmix, wa, gq, gkv, wuq, wukv, invf, wa_t, wuq_t, wukv_t, tm):
    def fn(x, pos, dqc, dkc, dv, dh, gmix, wa, gq, gkv, wuq, wukv, invf, wa_t, wuq_t, wukv_t):
        _, _, _, (hn, xh, r, cqn, cqh, rq, ckvn, ckvh, rkv, cos2, sin2) = _mla_front_tile(
            x, pos, gmix, wa, gq, gkv, wuq, wukv, invf)
        dqc, dkc, dv = dqc * SM_SCALE, dkc.astype(F32), dv.astype(F32)
        dqp, dkvp = [], []
        dkr = jnp.zeros((x.shape[0], LANES), F32)
        for h in range(HEADS):
            dqp += [dqc[:, h * HC:h * HC + NOPE], _unrope_blk(dqc[:, h * HC + NOPE:(h + 1) * HC], cos2, sin2)]
            dkvp += [dkc[:, h * HC:h * HC + NOPE], dv[:, h * VH:(h + 1) * VH]]
            dkr = dkr + dkc[:, h * HC + NOPE:(h + 1) * HC]
        dq = jnp.concatenate(dqp, axis=1)
        dkv = jnp.concatenate(dkvp, axis=1)
        dkrb = _unrope_blk(dkr, cos2, sin2)
        dcqn = _mm(dq, wuq_t)
        dckvn = _mm(dkv, wukv_t)
        d_wuq = _mm_tn(cqn, dq)
        d_wukv = _mm_tn(ckvn, dkv)
        dcq, d_gq = _rms_bwd(dcqn, cqh, rq, gq)
        dckv, d_gkv = _rms_bwd(dckvn, ckvh, rkv, gkv)
        da = jnp.concatenate([dcq, dckv, dkrb], axis=1)
        d_wa = _mm_tn(hn, da)
        dhn = _mm(da, wa_t)
        dx, d_gmix = _rms_bwd(dhn, xh, r, gmix)
        return (dh + dx,), (d_wa, d_wuq, d_wukv, d_gq, d_gkv, d_gmix)

    return _rowcall("mla_front_bwd", fn, tm, [x, pos, dqc, dkc, dv, dh],
                    [gmix, wa, gq, gkv, wuq, wukv, invf, wa_t, wuq_t, wukv_t], [(D, F32)],
                    [(D, AW), (QL, HEADS * HC), (KVL, HEADS * HC), (1, QL), (1, KVL), (1, D)])


SM_SCALE = (NOPE + ROPE) ** -0.5
LOG2E = 1.0 / math.log(2.0)
Q_PRESCALE = SM_SCALE * LOG2E


def _pair_tables(s_len, tq, tk, q_major):
    pairs = [(qi, ki) for qi in range(s_len // tq) for ki in range(s_len // tk) if ki * tk < (qi + 1) * tq]
    if not q_major:
        pairs.sort(key=lambda p: (p[1], p[0]))
    return (jnp.asarray([p[0] for p in pairs], jnp.int32), jnp.asarray([p[1] for p in pairs], jnp.int32))


def _last_key_tile(qi, tq, tk):
    return ((qi + 1) * tq - 1) // tk


def _visible(qi, ki, tq, tk):
    row = qi * (tq // CHUNK) + lax.broadcasted_iota(jnp.int32, (tq, tk), 0) // CHUNK
    col = ki * (tk // CHUNK) + lax.broadcasted_iota(jnp.int32, (tq, tk), 1) // CHUNK
    return col <= row


def _masked_and_not(qi, ki, tq, tk, fn):
    needs_mask = (ki + 1) * tk > qi * tq
    pl.when(needs_mask)(lambda: fn(True))
    pl.when(jnp.logical_not(needs_mask))(lambda: fn(False))


def _flash_fwd(qc, kc, v, tq, tk):
    s_len = qc.shape[0]
    qt, kt = _pair_tables(s_len, tq, tk, True)

    def body(qt_ref, kt_ref, q_ref, k_ref, v_ref, o_ref, lse_ref, m_sc, l_sc, acc_sc):
        p_id = pl.program_id(1)
        qi, ki = qt_ref[p_id], kt_ref[p_id]

        @pl.when(ki == 0)
        def _():
            m_sc[...] = jnp.full(m_sc.shape, -jnp.inf, F32)
            l_sc[...] = jnp.zeros(l_sc.shape, F32)
            acc_sc[...] = jnp.zeros(acc_sc.shape, F32)

        def update(masked):
            s = _mm_nt(q_ref[...], k_ref[...])
            if masked:
                s = jnp.where(_visible(qi, ki, tq, tk), s, -jnp.inf)
            m_old = m_sc[...]
            m_new = jnp.maximum(m_old, jnp.max(s, axis=1, keepdims=True))
            alpha = jnp.exp2(m_old - m_new)
            p = jnp.exp2(s - m_new)
            l_sc[...] = alpha * l_sc[...] + jnp.sum(p, axis=1, keepdims=True)
            acc_sc[...] = alpha * acc_sc[...] + _mm(p, v_ref[...])
            m_sc[...] = m_new

        _masked_and_not(qi, ki, tq, tk, update)

        @pl.when(ki == _last_key_tile(qi, tq, tk))
        def _():
            l = l_sc[...]
            o_ref[...] = (acc_sc[...] / l).astype(o_ref.dtype)
            lse_ref[...] = jnp.broadcast_to(m_sc[...] + jnp.log2(l), lse_ref.shape)

    qmap = lambda h, p, qt, kt: (qt[p], h)
    kmap = lambda h, p, qt, kt: (kt[p], h)
    grid_spec = pltpu.PrefetchScalarGridSpec(
        num_scalar_prefetch=2, grid=(HEADS, qt.shape[0]),
        in_specs=[pl.BlockSpec((tq, HC), qmap), pl.BlockSpec((tk, HC), kmap), pl.BlockSpec((tk, VH), kmap)],
        out_specs=[pl.BlockSpec((tq, VH), qmap), pl.BlockSpec((tq, LANES), qmap)],
        scratch_shapes=[pltpu.VMEM((tq, 1), F32), pltpu.VMEM((tq, 1), F32), pltpu.VMEM((tq, VH), F32)])
    return pl.pallas_call(body, grid_spec=grid_spec,
                          out_shape=[jax.ShapeDtypeStruct((s_len, HEADS * VH), MXU),
                                     jax.ShapeDtypeStruct((s_len, HEADS * LANES), F32)],
                          compiler_params=_cparams(2), name="flash_fwd")(qt, kt, qc, kc, v)


def _tile_dscores(q, k, v, do, lse, delta, qi, ki, tq, tk, masked):
    p = jnp.exp2(_mm_nt(q, k) - lse[:, :1])
    if masked:
        p = jnp.where(_visible(qi, ki, tq, tk), p, 0.0)
    return p, p * (_mm_nt(do, v) - delta[:, :1])


def _flash_bwd(qc, kc, v, do, lse, delta, tq, tk):
    s_len = qc.shape[0]
    qt, kt = _pair_tables(s_len, tq, tk, False)
    nq = s_len // tq

    def body(qt_ref, kt_ref, q_ref, k_ref, v_ref, do_ref, lse_ref, dl_ref, dq_ref, dk_ref, dv_ref, dk_sc, dv_sc):
        p_id = pl.program_id(1)
        qi, ki = qt_ref[p_id], kt_ref[p_id]
        rows = pl.ds(pl.multiple_of(qi * tq, tq), tq)

        @pl.when(qi == (ki * tk) // tq)
        def _():
            dk_sc[...] = jnp.zeros(dk_sc.shape, F32)
            dv_sc[...] = jnp.zeros(dv_sc.shape, F32)

        @pl.when(ki == 0)
        def _():
            dq_ref[rows, :] = jnp.zeros((tq, HC), F32)

        def update(masked):
            q, k, do = q_ref[...], k_ref[...], do_ref[...]
            p, ds = _tile_dscores(q, k, v_ref[...], do, lse_ref[...], dl_ref[...], qi, ki, tq, tk, masked)
            ds = ds.astype(MXU)
            dv_sc[...] += _mm_tn(p, do)
            dk_sc[...] += _mm_tn(ds, q)
            dq_ref[rows, :] += _mm(ds, k)

        _masked_and_not(qi, ki, tq, tk, update)

        @pl.when(qi == nq - 1)
        def _():
            dk_ref[...] = (dk_sc[...] * (1.0 / LOG2E)).astype(dk_ref.dtype)
            dv_ref[...] = dv_sc[...].astype(dv_ref.dtype)

    qmap = lambda h, p, qt, kt: (qt[p], h)
    kmap = lambda h, p, qt, kt: (kt[p], h)
    grid_spec = pltpu.PrefetchScalarGridSpec(
        num_scalar_prefetch=2, grid=(HEADS, qt.shape[0]),
        in_specs=[pl.BlockSpec((tq, HC), qmap), pl.BlockSpec((tk, HC), kmap), pl.BlockSpec((tk, VH), kmap),
                  pl.BlockSpec((tq, VH), qmap), pl.BlockSpec((tq, LANES), qmap), pl.BlockSpec((tq, LANES), qmap)],
        out_specs=[pl.BlockSpec((s_len, HC), lambda h, p, qt, kt: (0, h), pipeline_mode=pl.Buffered(1)),
                   pl.BlockSpec((tk, HC), kmap), pl.BlockSpec((tk, VH), kmap)],
        scratch_shapes=[pltpu.VMEM((tk, HC), F32), pltpu.VMEM((tk, VH), F32)])
    return pl.pallas_call(body, grid_spec=grid_spec,
                          out_shape=[jax.ShapeDtypeStruct((s_len, HEADS * HC), F32),
                                     jax.ShapeDtypeStruct((s_len, HEADS * HC), MXU),
                                     jax.ShapeDtypeStruct((s_len, HEADS * VH), MXU)],
                          compiler_params=_cparams(2), name="flash_bwd")(qt, kt, qc, kc, v, do, lse, delta)


def _attn_out_fwd(x, o, wo, tm):
    def fn(x, o, wo):
        return (x + _mm(o, wo),), ()

    return _rowcall("attn_out_fwd", fn, tm, [x, o], [wo], [(D, F32)], [])[0][0]


def _attn_out_bwd(dh, o, wo_t, tm):
    def fn(dh, o, wo_t):
        do = _mm(dh, wo_t)
        of = o.astype(F32)
        dl = [jnp.broadcast_to(jnp.sum(do[:, h * VH:(h + 1) * VH] * of[:, h * VH:(h + 1) * VH], axis=1, keepdims=True),
                               (dh.shape[0], LANES)) for h in range(HEADS)]
        return (do, jnp.concatenate(dl, axis=1)), (_mm_tn(o, dh),)

    return _rowcall("attn_out_bwd", fn, tm, [dh, o], [wo_t], [(HEADS * VH, MXU), (HEADS * LANES, F32)], [(HEADS * VH, D)])


def _shift_rows(a, k):
    return a if k == 0 else pltpu.roll(a, k % a.shape[0], 0)


def _stack_rows(rows):
    idx = lax.broadcasted_iota(jnp.int32, (8, rows[0].shape[1]), 0)
    out = jnp.zeros((8, rows[0].shape[1]), F32)
    for k, r in enumerate(rows):
        out = jnp.where(idx == k, r, out)
    return out


def _ffn_fwd(h, g, wup, cw, cb, wdown, tm):
    s_len = h.shape[0]
    nj = DFF // TC
    hb = tm // HALO

    def body(h_ref, hp_ref, g_ref, wv_ref, wg_ref, cwv_ref, cwg_ref, cbv_ref, cbg_ref, wd_ref, out_ref, act_ref,
             hn_sc, acc_sc):
        i, j = pl.program_id(0), pl.program_id(1)

        @pl.when(j == 0)
        def _():
            gg = g_ref[...]
            hp = _rms_fwd(hp_ref[...], gg)[0]
            hn_sc[:HALO, :] = jnp.where(i > 0, hp, 0.0).astype(MXU)
            hn_sc[HALO:, :] = _rms_fwd(h_ref[...], gg)[0].astype(MXU)
            acc_sc[...] = jnp.zeros(acc_sc.shape, F32)

        hn = hn_sc[...]

        def conv(w_ref, cw_ref, cb_ref):
            up = jnp.dot(hn, w_ref[...], preferred_element_type=F32)
            cwv = cw_ref[...]
            c = cwv[2:3] * up + cwv[1:2] * _shift_rows(up, 1) + cwv[0:1] * _shift_rows(up, 2)
            return c[HALO:] + cb_ref[...]

        cv = conv(wv_ref, cwv_ref, cbv_ref)
        cg = conv(wg_ref, cwg_ref, cbg_ref)
        act = cg * jax.nn.sigmoid(cg) * cv
        act_ref[...] = act.astype(act_ref.dtype)
        acc_sc[...] += _mm(act, wd_ref[...])

        @pl.when(j == nj - 1)
        def _():
            out_ref[...] = h_ref[...] + acc_sc[...]

    in_specs = [pl.BlockSpec((tm, D), lambda i, j: (i, 0)),
                pl.BlockSpec((HALO, D), lambda i, j: (jnp.maximum(i * hb - 1, 0), 0)),
                pl.BlockSpec((1, D), lambda i, j: (0, 0)),
                pl.BlockSpec((D, TC), lambda i, j: (0, j)), pl.BlockSpec((D, TC), lambda i, j: (0, j + nj)),
                pl.BlockSpec((3, TC), lambda i, j: (0, j)), pl.BlockSpec((3, TC), lambda i, j: (0, j + nj)),
                pl.BlockSpec((1, TC), lambda i, j: (0, j)), pl.BlockSpec((1, TC), lambda i, j: (0, j + nj)),
                pl.BlockSpec((TC, D), lambda i, j: (j, 0))]
    out_specs = [pl.BlockSpec((tm, D), lambda i, j: (i, 0)), pl.BlockSpec((tm, TC), lambda i, j: (i, j))]
    return pl.pallas_call(body, grid=(s_len // tm, nj), in_specs=in_specs, out_specs=out_specs,
                          out_shape=[jax.ShapeDtypeStruct((s_len, D), F32), jax.ShapeDtypeStruct((s_len, DFF), MXU)],
                          scratch_shapes=[pltpu.VMEM((tm + HALO, D), MXU), pltpu.VMEM((tm, D), F32)],
                          compiler_params=_cparams(2), name="ffn_fwd")(h, h, g, wup, wup, cw, cw, cb, cb, wdown)


def _ffn_bwd(h, dout, g, wup, cw, cb, wdown_t, wup_t, tm):
    s_len = h.shape[0]
    nj = DFF // TC
    ni = s_len // tm
    hb = tm // HALO
    rows = tm + 2 * HALO

    def body(h_ref, hp_ref, hx_ref, d_ref, dx_ref, g_ref, wv_ref, wg_ref, cwv_ref, cwg_ref, cbv_ref, cbg_ref,
             wdt_ref, wutv_ref, wutg_ref, din_ref, dupv_ref, dupg_ref, hn_ref, cacc_ref, dg_ref, hn_sc, d_sc, acc_sc):
        i, j = pl.program_id(0), pl.program_id(1)

        @pl.when(j == 0)
        def _():
            gg = g_ref[...]
            hn_sc[:HALO, :] = jnp.where(i > 0, _rms_fwd(hp_ref[...], gg)[0], 0.0).astype(MXU)
            hn = _rms_fwd(h_ref[...], gg)[0].astype(MXU)
            hn_sc[HALO:HALO + tm, :] = hn
            hn_ref[...] = hn
            hn_sc[HALO + tm:, :] = _rms_fwd(hx_ref[...], gg)[0].astype(MXU)
            d_sc[:tm, :] = d_ref[...].astype(MXU)
            d_sc[tm:, :] = jnp.where(i < ni - 1, dx_ref[...], 0.0).astype(MXU)
            acc_sc[...] = jnp.zeros(acc_sc.shape, F32)

        @pl.when(jnp.logical_and(i == 0, j == 0))
        def _():
            cacc_ref[...] = jnp.zeros(cacc_ref.shape, F32)
            dg_ref[...] = jnp.zeros(dg_ref.shape, F32)

        hn = hn_sc[...]
        dd = d_sc[...]

        def half(w_ref, cw_ref, cb_ref):
            up = jnp.dot(hn, w_ref[...], preferred_element_type=F32)
            cwv = cw_ref[...]
            u1, u2 = _shift_rows(up, 1), _shift_rows(up, 2)
            c = (cwv[2:3] * up + cwv[1:2] * u1 + cwv[0:1] * u2)[HALO:] + cb_ref[...]
            return c, (up[HALO:HALO + tm], u1[HALO:HALO + tm], u2[HALO:HALO + tm]), cwv

        cv, upv, cwv = half(wv_ref, cwv_ref, cbv_ref)
        cg, upg, cwg = half(wg_ref, cwg_ref, cbg_ref)
        dact = jnp.dot(dd, wdt_ref[...], preferred_element_type=F32)
        sg = jax.nn.sigmoid(cg)
        dcv = dact * (cg * sg)
        dcg = dact * cv * (sg * (1.0 + cg * (1.0 - sg)))

        def back(dc, ups, cwx, slot, dup_ref, wut_ref):
            dup = (cwx[2:3] * dc + cwx[1:2] * _shift_rows(dc, -1) + cwx[0:1] * _shift_rows(dc, -2))[:tm]
            dct = dc[:tm]
            cacc_ref[slot] += _stack_rows([jnp.sum(dct * ups[2], axis=0, keepdims=True),
                                           jnp.sum(dct * ups[1], axis=0, keepdims=True),
                                           jnp.sum(dct * ups[0], axis=0, keepdims=True),
                                           jnp.sum(dct, axis=0, keepdims=True)])
            dup_ref[...] = dup.astype(dup_ref.dtype)
            return _mm(dup, wut_ref[...])

        acc_sc[...] += back(dcv, upv, cwv, j, dupv_ref, wutv_ref) + back(dcg, upg, cwg, j + nj, dupg_ref, wutg_ref)

        @pl.when(j == nj - 1)
        def _():
            gg = g_ref[...]
            _, xh, r = _rms_fwd(h_ref[...], gg)
            dx, dgp = _rms_bwd(acc_sc[...], xh, r, gg)
            din_ref[...] = d_ref[...] + dx
            dg_ref[...] += dgp

    last_blk = s_len // HALO - 1
    in_specs = [pl.BlockSpec((tm, D), lambda i, j: (i, 0)),
                pl.BlockSpec((HALO, D), lambda i, j: (jnp.maximum(i * hb - 1, 0), 0)),
                pl.BlockSpec((HALO, D), lambda i, j: (jnp.minimum((i + 1) * hb, last_blk), 0)),
                pl.BlockSpec((tm, D), lambda i, j: (i, 0)),
                pl.BlockSpec((HALO, D), lambda i, j: (jnp.minimum((i + 1) * hb, last_blk), 0)),
                pl.BlockSpec((1, D), lambda i, j: (0, 0)),
                pl.BlockSpec((D, TC), lambda i, j: (0, j)), pl.BlockSpec((D, TC), lambda i, j: (0, j + nj)),
                pl.BlockSpec((3, TC), lambda i, j: (0, j)), pl.BlockSpec((3, TC), lambda i, j: (0, j + nj)),
                pl.BlockSpec((1, TC), lambda i, j: (0, j)), pl.BlockSpec((1, TC), lambda i, j: (0, j + nj)),
                pl.BlockSpec((D, TC), lambda i, j: (0, j)),
                pl.BlockSpec((TC, D), lambda i, j: (j, 0)), pl.BlockSpec((TC, D), lambda i, j: (j + nj, 0))]
    out_specs = [pl.BlockSpec((tm, D), lambda i, j: (i, 0)),
                 pl.BlockSpec((tm, TC), lambda i, j: (i, j)), pl.BlockSpec((tm, TC), lambda i, j: (i, j)),
                 pl.BlockSpec((tm, D), lambda i, j: (i, 0)),
                 pl.BlockSpec((2 * nj, 8, TC), lambda i, j: (0, 0, 0)),
                 pl.BlockSpec((1, D), lambda i, j: (0, 0))]
    out_shape = [jax.ShapeDtypeStruct((s_len, D), F32),
                 jax.ShapeDtypeStruct((s_len, DFF), MXU), jax.ShapeDtypeStruct((s_len, DFF), MXU),
                 jax.ShapeDtypeStruct((s_len, D), MXU),
                 jax.ShapeDtypeStruct((2 * nj, 8, TC), F32), jax.ShapeDtypeStruct((1, D), F32)]
    return pl.pallas_call(body, grid=(ni, nj), in_specs=in_specs, out_specs=out_specs, out_shape=out_shape,
                          scratch_shapes=[pltpu.VMEM((rows, D), MXU), pltpu.VMEM((tm + HALO, D), MXU),
                                          pltpu.VMEM((tm, D), F32)],
                          compiler_params=_cparams(2), name="ffn_bwd")(
        h, h, h, dout, dout, g, wup, wup, cw, cw, cb, cb, wdown_t, wup_t, wup_t)


def _matmul_tn(a, b, tn, ts, name):
    s_len, m = a.shape
    n = b.shape[1]

    def body(a_ref, b_ref, o_ref):
        @pl.when(pl.program_id(1) == 0)
        def _():
            o_ref[...] = jnp.zeros(o_ref.shape, F32)

        o_ref[...] += _mm_tn(a_ref[...], b_ref[...])

    return pl.pallas_call(body, grid=(n // tn, s_len // ts),
                          in_specs=[pl.BlockSpec((ts, m), lambda jn, k: (k, 0)), pl.BlockSpec((ts, tn), lambda jn, k: (k, jn))],
                          out_specs=pl.BlockSpec((m, tn), lambda jn, k: (0, jn)),
                          out_shape=jax.ShapeDtypeStruct((m, n), F32), compiler_params=_cparams(2), name=name)(a, b)


def _s5_coefs(lr, li, ldt):
    dt = jnp.exp(ldt)
    mag = jnp.exp(lr * dt)
    th = li * dt
    ar, ai = mag * jnp.cos(th), mag * jnp.sin(th)
    den = lr * lr + li * li
    nr = ar - 1.0
    cr = (nr * lr + ai * li) / den
    ci = (ai * lr - nr * li) / den
    return dt, mag, th, ar, ai, den, nr, cr, ci


def _s5_prep(lr, li, ldt, braw, seg):
    def body(lr_ref, li_ref, ldt_ref, b_ref, bb_ref, ap_ref):
        lr_, li_, ldt_ = lr_ref[0], li_ref[0], ldt_ref[0]
        dt, mag, th, ar, ai, den, nr, cr, ci = _s5_coefs(lr_, li_, ldt_)
        br, bi = b_ref[0, :, :SL], b_ref[0, :, SL:]
        bb_ref[0, :, :SL] = (cr * br - ci * bi).astype(bb_ref.dtype)
        bb_ref[0, :, SL:] = (cr * bi + ci * br).astype(bb_ref.dtype)
        for i in range(seg):
            m = jnp.exp((i + 1.0) * (lr_ * dt))
            ap_ref[0, i * 8:(i + 1) * 8, :SL] = jnp.broadcast_to(m * jnp.cos((i + 1.0) * th), (8, SL))
            ap_ref[0, i * 8:(i + 1) * 8, SL:] = jnp.broadcast_to(m * jnp.sin((i + 1.0) * th), (8, SL))

    vec = pl.BlockSpec((1, 1, SL), lambda k: (k, 0, 0))
    return pl.pallas_call(
        body, grid=(GB,), in_specs=[vec, vec, vec, pl.BlockSpec((1, LANES, 2 * SL), lambda k: (k, 0, 0))],
        out_specs=[pl.BlockSpec((1, LANES, 2 * SL), lambda k: (k, 0, 0)),
                   pl.BlockSpec((1, 8 * seg, 2 * SL), lambda k: (k, 0, 0))],
        out_shape=[jax.ShapeDtypeStruct((GB, LANES, 2 * SL), MXU), jax.ShapeDtypeStruct((GB, 8 * seg, 2 * SL), F32)],
        compiler_params=_cparams(1), name="s5_prep")(lr, li, ldt, braw)


def _s5_prep_bwd(lr, li, ldt, braw, dbb, da):
    def body(lr_ref, li_ref, ldt_ref, b_ref, dbb_ref, da_ref, dbraw_ref, dlr_ref, dli_ref, dldt_ref):
        lr_, li_, ldt_ = lr_ref[0], li_ref[0], ldt_ref[0]
        dt, mag, th, ar, ai, den, nr, cr, ci = _s5_coefs(lr_, li_, ldt_)
        br, bi = b_ref[0, :, :SL], b_ref[0, :, SL:]
        gbr, gbi = dbb_ref[0, :, :SL], dbb_ref[0, :, SL:]
        dbraw_ref[0, :, :SL] = cr * gbr + ci * gbi
        dbraw_ref[0, :, SL:] = cr * gbi - ci * gbr
        dcr = jnp.sum(gbr * br + gbi * bi, axis=0, keepdims=True)
        dci = jnp.sum(gbi * br - gbr * bi, axis=0, keepdims=True)
        dar = jnp.sum(da_ref[0, :, :SL], axis=0, keepdims=True)
        dai = jnp.sum(da_ref[0, :, SL:], axis=0, keepdims=True)
        g1, g2 = dcr / den, dci / den
        gden = -(dcr * cr + dci * ci) / den
        gar = dar + g1 * lr_ - g2 * li_
        gai = dai + g1 * li_ + g2 * lr_
        glr = g1 * nr + g2 * ai + 2.0 * lr_ * gden
        gli = g1 * ai - g2 * nr + 2.0 * li_ * gden
        gmag = gar * jnp.cos(th) + gai * jnp.sin(th)
        gth = gai * ar - gar * ai
        dlr_ref[0] = glr + gmag * mag * dt
        dli_ref[0] = gli + gth * dt
        dldt_ref[0] = (gmag * mag * lr_ + gth * li_) * dt

    vec = pl.BlockSpec((1, 1, SL), lambda k: (k, 0, 0))
    mat = pl.BlockSpec((1, LANES, 2 * SL), lambda k: (k, 0, 0))
    return pl.pallas_call(
        body, grid=(GB,), in_specs=[vec, vec, vec, mat, mat, pl.BlockSpec((1, 8, 2 * SL), lambda k: (k, 0, 0))],
        out_specs=[mat, vec, vec, vec],
        out_shape=[jax.ShapeDtypeStruct((GB, LANES, 2 * SL), F32)] + [jax.ShapeDtypeStruct((GB, 1, SL), F32)] * 3,
        compiler_params=_cparams(1), name="s5_prep_bwd")(lr, li, ldt, braw, dbb, da)


def _bcast_row(tile, j):
    return jnp.broadcast_to(tile[j:j + 1, :], tile.shape)


def _tile_rows(i):
    return pl.ds(pl.multiple_of(i * 8, 8), 8)


def _col(c):
    return slice(c * LANES, (c + 1) * LANES)


def _permute_rows(ref, seg):
    return jnp.concatenate([ref[pl.ds(i, 8, stride=seg), :] for i in range(seg)], axis=0)


def _unpermute_rows(src, dst, seg):
    for j in range(8):
        dst[j * seg:(j + 1) * seg, :] = src[pl.ds(j, seg, stride=8), :]


SCAN_UNROLL = 2


def _segment_scan(src, dst, ap, seg, reverse):
    half = NCOL // 2
    sign = -1.0 if reverse else 1.0
    a_r = [ap[0:8, _col(c)] for c in range(half)]
    a_i = [ap[0:8, _col(half + c)] for c in range(half)]

    def step(n, carry):
        i = seg - 1 - n if reverse else n
        rows = _tile_rows(i)
        out_r, out_i = [], []
        for c in range(half):
            xr, xi = carry[c], carry[half + c]
            nr = a_r[c] * xr - sign * a_i[c] * xi + src[rows, _col(c)]
            ni = a_r[c] * xi + sign * a_i[c] * xr + src[rows, _col(half + c)]
            dst[rows, _col(c)] = nr
            dst[rows, _col(half + c)] = ni
            out_r.append(nr)
            out_i.append(ni)
        return tuple(out_r + out_i)

    zero = jnp.zeros((8, LANES), F32)
    return lax.fori_loop(0, seg, step, (zero,) * NCOL, unroll=SCAN_UNROLL)


def _segment_entries(ends, cin, ap, seg, reverse):
    half = NCOL // 2
    sign = -1.0 if reverse else 1.0
    al_r = [ap[(seg - 1) * 8:seg * 8, _col(c)] for c in range(half)]
    al_i = [ap[(seg - 1) * 8:seg * 8, _col(half + c)] for c in range(half)]
    row = lax.broadcasted_iota(jnp.int32, (8, LANES), 0)
    ent, out = [None] * NCOL, [None] * NCOL
    for c in range(half):
        zr, zi = cin[c], cin[half + c]
        er, ei = jnp.zeros((8, LANES), F32), jnp.zeros((8, LANES), F32)
        for j in (range(7, -1, -1) if reverse else range(8)):
            er, ei = jnp.where(row == j, zr, er), jnp.where(row == j, zi, ei)
            fr, fi = _bcast_row(ends[c], j), _bcast_row(ends[half + c], j)
            zr, zi = (al_r[c] * zr - sign * al_i[c] * zi + fr, al_r[c] * zi + sign * al_i[c] * zr + fi)
        ent[c], ent[half + c] = er, ei
        out[c], out[half + c] = zr, zi
    return ent, out


def _chunk_states(u, bb_ref, ap, cin, bu_sc, x_sc, seg):
    half = NCOL // 2
    bu_sc[...] = _mm(u, bb_ref[0])
    ends = _segment_scan(bu_sc, x_sc, ap, seg, False)
    ent, out = _segment_entries(ends, cin, ap, seg, False)

    def fix(i, _):
        rows = _tile_rows(i)
        for c in range(half):
            pr, pi = ap[rows, _col(c)], ap[rows, _col(half + c)]
            x_sc[rows, _col(c)] += pr * ent[c] - pi * ent[half + c]
            x_sc[rows, _col(half + c)] += pr * ent[half + c] + pi * ent[c]
        return 0

    lax.fori_loop(0, seg, fix, 0, unroll=SCAN_UNROLL)
    return out


def _s5_scan_fwd(u, bb, apow, cblk, dskip, t_chunk):
    s_len = u.shape[0]
    nc = s_len // t_chunk
    seg = t_chunk // 8

    def body(u_ref, bb_ref, ap_ref, c_ref, d_ref, y_ref, xin_ref, bu_sc, x_sc, y_sc, carry_sc):
        @pl.when(pl.program_id(1) == 0)
        def _():
            carry_sc[...] = jnp.zeros(carry_sc.shape, F32)

        uu = _permute_rows(u_ref, seg)
        cin = [carry_sc[:, _col(c)] for c in range(NCOL)]
        xin_ref[0, 0] = carry_sc[...]
        out = _chunk_states(uu, bb_ref, ap_ref.at[0], cin, bu_sc, x_sc, seg)
        for c in range(NCOL):
            carry_sc[:, _col(c)] = out[c]
        y_sc[...] = _mm(x_sc[...], c_ref[0]) + d_ref[...] * uu
        _unpermute_rows(y_sc, y_ref, seg)

    in_specs = [pl.BlockSpec((t_chunk, LANES), lambda k, c: (c, k)),
                pl.BlockSpec((1, LANES, 2 * SL), lambda k, c: (k, 0, 0)),
                pl.BlockSpec((1, 8 * seg, 2 * SL), lambda k, c: (k, 0, 0)),
                pl.BlockSpec((1, 2 * SL, LANES), lambda k, c: (k, 0, 0)),
                pl.BlockSpec((1, LANES), lambda k, c: (0, k))]
    out_specs = [pl.BlockSpec((t_chunk, LANES), lambda k, c: (c, k)),
                 pl.BlockSpec((1, 1, 8, 2 * SL), lambda k, c: (k, c, 0, 0))]
    return pl.pallas_call(body, grid=(GB, nc), in_specs=in_specs, out_specs=out_specs,
                          out_shape=[jax.ShapeDtypeStruct((s_len, D), F32), jax.ShapeDtypeStruct((GB, nc, 8, 2 * SL), F32)],
                          scratch_shapes=[pltpu.VMEM((t_chunk, 2 * SL), F32), pltpu.VMEM((t_chunk, 2 * SL), F32),
                                          pltpu.VMEM((t_chunk, LANES), F32), pltpu.VMEM((8, 2 * SL), F32)],
                          compiler_params=_cparams(2), name="s5_scan_fwd")(u, bb, apow, cblk, dskip)


def _s5_scan_bwd(u, dy, xin, bb, bb_t, apow, cblk_t, dskip, t_chunk):
    s_len = u.shape[0]
    nc = s_len // t_chunk
    seg = t_chunk // 8
    half = NCOL // 2

    def body(u_ref, dy_ref, xin_ref, bb_ref, bbt_ref, ap_ref, ct_ref, d_ref, du_ref, dbb_ref, dc_ref, da_ref, dd_ref,
             bu_sc, x_sc, g_sc, y_sc, carry_sc):
        @pl.when(pl.program_id(1) == 0)
        def _():
            carry_sc[...] = jnp.zeros(carry_sc.shape, F32)
            dbb_ref[...] = jnp.zeros(dbb_ref.shape, F32)
            dc_ref[...] = jnp.zeros(dc_ref.shape, F32)
            da_ref[...] = jnp.zeros(da_ref.shape, F32)
            dd_ref[...] = jnp.zeros(dd_ref.shape, F32)

        ap = ap_ref.at[0]
        uu, dyy = _permute_rows(u_ref, seg), _permute_rows(dy_ref, seg)
        cin = [xin_ref[0, 0, :, _col(c)] for c in range(NCOL)]
        _chunk_states(uu, bb_ref, ap, cin, bu_sc, x_sc, seg)
        bu_sc[...] = _mm(dyy, ct_ref[0])
        ends = _segment_scan(bu_sc, g_sc, ap, seg, True)
        lam_in = [carry_sc[:, _col(c)] for c in range(NCOL)]
        ent, out = _segment_entries(ends, lam_in, ap, seg, True)
        for c in range(NCOL):
            carry_sc[:, _col(c)] = out[c]
        row = lax.broadcasted_iota(jnp.int32, (8, LANES), 0)
        xp0 = [jnp.where(row == 0, cin[c], pltpu.roll(x_sc[(seg - 1) * 8:seg * 8, _col(c)], 1, 0)) for c in range(NCOL)]

        def fix(i, acc):
            rows, prev, tab = _tile_rows(i), _tile_rows(jnp.maximum(i - 1, 0)), _tile_rows(seg - 1 - i)
            new = list(acc)
            for c in range(half):
                pr, pi = ap[tab, _col(c)], ap[tab, _col(half + c)]
                lr_ = g_sc[rows, _col(c)] + pr * ent[c] + pi * ent[half + c]
                li_ = g_sc[rows, _col(half + c)] + pr * ent[half + c] - pi * ent[c]
                g_sc[rows, _col(c)] = lr_
                g_sc[rows, _col(half + c)] = li_
                xr = jnp.where(i == 0, xp0[c], x_sc[prev, _col(c)])
                xi = jnp.where(i == 0, xp0[half + c], x_sc[prev, _col(half + c)])
                new[c] = acc[c] + lr_ * xr + li_ * xi
                new[half + c] = acc[half + c] + li_ * xr - lr_ * xi
            return tuple(new)

        zero = jnp.zeros((8, LANES), F32)
        dacc = lax.fori_loop(0, seg, fix, (zero,) * NCOL, unroll=SCAN_UNROLL)
        for c in range(NCOL):
            da_ref[0, :, _col(c)] += dacc[c]
        lam = g_sc[...]
        y_sc[...] = _mm(lam, bbt_ref[0]) + d_ref[...] * dyy
        _unpermute_rows(y_sc, du_ref, seg)
        dbb_ref[0] += _mm_tn(uu, lam)
        dc_ref[0] += _mm_tn(x_sc[...], dyy)
        dd_ref[0] += _stack_rows([jnp.sum(dyy * uu, axis=0, keepdims=True)])

    rev = lambda k, c: (nc - 1 - c, k)
    in_specs = [pl.BlockSpec((t_chunk, LANES), rev), pl.BlockSpec((t_chunk, LANES), rev),
                pl.BlockSpec((1, 1, 8, 2 * SL), lambda k, c: (k, nc - 1 - c, 0, 0)),
                pl.BlockSpec((1, LANES, 2 * SL), lambda k, c: (k, 0, 0)),
                pl.BlockSpec((1, 2 * SL, LANES), lambda k, c: (k, 0, 0)),
                pl.BlockSpec((1, 8 * seg, 2 * SL), lambda k, c: (k, 0, 0)),
                pl.BlockSpec((1, LANES, 2 * SL), lambda k, c: (k, 0, 0)),
                pl.BlockSpec((1, LANES), lambda k, c: (0, k))]
    out_specs = [pl.BlockSpec((t_chunk, LANES), rev),
                 pl.BlockSpec((1, LANES, 2 * SL), lambda k, c: (k, 0, 0)),
                 pl.BlockSpec((1, 2 * SL, LANES), lambda k, c: (k, 0, 0)),
                 pl.BlockSpec((1, 8, 2 * SL), lambda k, c: (k, 0, 0)),
                 pl.BlockSpec((1, 8, LANES), lambda k, c: (k, 0, 0))]
    out_shape = [jax.ShapeDtypeStruct((s_len, D), F32), jax.ShapeDtypeStruct((GB, LANES, 2 * SL), F32),
                 jax.ShapeDtypeStruct((GB, 2 * SL, LANES), F32), jax.ShapeDtypeStruct((GB, 8, 2 * SL), F32),
                 jax.ShapeDtypeStruct((GB, 8, LANES), F32)]
    return pl.pallas_call(body, grid=(GB, nc), in_specs=in_specs, out_specs=out_specs, out_shape=out_shape,
                          scratch_shapes=[pltpu.VMEM((t_chunk, 2 * SL), F32)] * 3 + [pltpu.VMEM((t_chunk, LANES), F32),
                                                                                        pltpu.VMEM((8, 2 * SL), F32)],
                          compiler_params=_cparams(2), name="s5_scan_bwd")(u, dy, xin, bb, bb_t, apow, cblk_t, dskip)


_GELU_K = math.sqrt(2.0 / math.pi)


def _gelu(y):
    t = jnp.tanh(_GELU_K * (y + 0.044715 * (y * y * y)))
    return 0.5 * y * (1.0 + t), 0.5 * (1.0 + t) + 0.5 * y * (1.0 - t * t) * (_GELU_K * (1.0 + 3 * 0.044715 * (y * y)))


def _s5_in_fwd(h, gmix, win, tm):
    def fn(h, gmix, win):
        return (_mm(_rms_fwd(h, gmix)[0], win),), ()

    return _rowcall("s5_in_fwd", fn, tm, [h], [gmix, win], [(D, F32)], [])[0][0]


def _s5_in_bwd(h, du, dh, gmix, win_t, tm):
    def fn(h, du, dh, gmix, win_t):
        hn, xh, r = _rms_fwd(h, gmix)
        dx, dg = _rms_bwd(_mm(du, win_t), xh, r, gmix)
        return (dh + dx,), (_mm_tn(hn, du), dg)

    return _rowcall("s5_in_bwd", fn, tm, [h, du, dh], [gmix, win_t], [(D, F32)], [(D, D), (1, D)])


def _s5_out_fwd(h, y, wglu, tm):
    def fn(h, y, wglu):
        z = _mm(_gelu(y)[0], wglu)
        return (h + z[:, :D] * jax.nn.sigmoid(z[:, D:]),), ()

    return _rowcall("s5_out_fwd", fn, tm, [h, y], [wglu], [(D, F32)], [])[0][0]


def _s5_out_bwd(dh, y, wglu, wglu_t, tm):
    def fn(dh, y, wglu, wglu_t):
        yg, dgelu = _gelu(y)
        z = _mm(yg, wglu)
        val, sg = z[:, :D], jax.nn.sigmoid(z[:, D:])
        dz = jnp.concatenate([dh * sg, dh * val * sg * (1.0 - sg)], axis=1)
        return (_mm(dz, wglu_t) * dgelu,), (_mm_tn(yg, dz),)

    return _rowcall("s5_out_bwd", fn, tm, [dh, y], [wglu, wglu_t], [(D, F32)], [(D, 2 * D)])


def _final_loss(h, tgt, gfin, tm):
    def fn(h, tgt, gfin):
        y, xh, r = _rms_fwd(h, gfin)
        err = y - tgt
        dx, dg = _rms_bwd(err * (1.0 / D), xh, r, gfin)
        return (dx,), (jnp.sum(err * err, axis=0, keepdims=True), dg)

    return _rowcall("final_loss", fn, tm, [h, tgt], [gfin], [(D, F32)], [(1, D), (1, D)])


def _bf(a):
    return a.astype(MXU)


def _s5_block_mats(b_re, b_im, c_re, c_im):
    gl = G // GB
    eye = jnp.eye(gl, dtype=F32)

    def b_blk(b):
        bt = b.reshape(GB, gl, P, C).transpose(0, 1, 3, 2)
        return (bt[:, :, :, None, :] * eye[None, :, None, :, None]).reshape(GB, gl * C, gl * P)

    def c_blk(cm):
        ct = cm.reshape(GB, gl, C, P).transpose(0, 1, 3, 2)
        return (ct[:, :, :, None, :] * eye[None, :, None, :, None]).reshape(GB, gl * P, gl * C)

    braw = jnp.concatenate([b_blk(b_re), b_blk(b_im)], axis=2)
    cblk = jnp.concatenate([c_blk(c_re), -c_blk(c_im)], axis=1)
    return braw, cblk


def _s5_unblock_b(d):
    gl = G // GB
    eye = jnp.eye(gl, dtype=F32)
    d5 = d.reshape(GB, gl, C, gl, P)
    return jnp.sum(d5 * eye[None, :, None, :, None], axis=3).transpose(0, 1, 3, 2).reshape(G, P, C)


def _s5_unblock_c(d):
    gl = G // GB
    eye = jnp.eye(gl, dtype=F32)
    d5 = d.reshape(GB, gl, P, gl, C)
    return jnp.sum(d5 * eye[None, :, None, :, None], axis=3).transpose(0, 1, 3, 2).reshape(G, C, P)


def _tiles(s_len):
    return (min(256, s_len), min(512, s_len), min(1024, s_len),
            (min(512, s_len), min(2048, s_len)), (min(1024, s_len), min(1024, s_len)),
            min(256, s_len), min(128, s_len))


def _sequence_step(x, pos, tgt, w):
    s_len = x.shape[0]
    tm_w, tm, t_chunk, t_fwd, t_bwd, tm_ff, tm_fb = _tiles(s_len)
    seg = t_chunk // 8
    row = lambda v: v.reshape(1, -1)

    wa = _bf(jnp.pad(w["mla_w_a"][0], ((0, 0), (0, AW - (QL + KVL + ROPE)))))
    wuq = _bf(jnp.pad(w["mla_w_uq"][0].reshape(QL, HEADS, NOPE + ROPE), ((0, 0), (0, 0), (0, HC - NOPE - ROPE))).reshape(QL, HEADS * HC))
    wukv = _bf(w["mla_w_ukv"][0])
    wo = _bf(w["mla_w_o"][0])
    inv = 1.0 / (ROPE_THETA ** (jnp.arange(0, ROPE, 2, dtype=F32) / ROPE))
    invf = jnp.concatenate([inv, inv, jnp.zeros((LANES - ROPE,), F32)]).reshape(1, LANES)
    gmix0, gmix1 = row(w["g_mix"][0]), row(w["g_mix"][1])
    gq, gkv = row(w["mla_g_q"][0]), row(w["mla_g_kv"][0])
    win, wglu = _bf(w["ssm_w_in"][0]), _bf(w["ssm_w_glu"][0])
    dskip = row(w["ssm_d"][0])
    lr = w["ssm_lambda_re"][0].reshape(GB, 1, SL)
    li = w["ssm_lambda_im"][0].reshape(GB, 1, SL)
    ldt = jnp.broadcast_to(w["ssm_log_dt"][0][:, None], (G, P)).reshape(GB, 1, SL)
    braw, cblk = _s5_block_mats(w["ssm_b_re"][0], w["ssm_b_im"][0], w["ssm_c_re"][0], w["ssm_c_im"][0])
    cblk = _bf(cblk)
    ffn = []
    for l in range(2):
        wup, wdown = _bf(w["ffn_w_up"][l]), _bf(w["ffn_w_down"][l])
        ffn.append(dict(g=row(w["g_ffn"][l]), wup=wup, wup_t=wup.T, wdown=wdown, wdown_t=wdown.T,
                        cw=w["ffn_conv_w"][l], cb=row(w["ffn_conv_b"][l])))

    qc, kc, v = _mla_front_fwd(x, pos, gmix0, wa, gq, gkv, wuq, wukv, invf, tm_w)
    o, lse = _flash_fwd(qc, kc, v, *t_fwd)
    h1 = _attn_out_fwd(x, o, wo, tm)
    f0 = ffn[0]
    h2, act0 = _ffn_fwd(h1, f0["g"], f0["wup"], f0["cw"], f0["cb"], f0["wdown"], tm_ff)
    bb, apow = _s5_prep(lr, li, ldt, braw, seg)
    u = _s5_in_fwd(h2, gmix1, win, tm)
    y, xin = _s5_scan_fwd(u, bb, apow, cblk, dskip, t_chunk)
    h3 = _s5_out_fwd(h2, y, wglu, tm)
    f1 = ffn[1]
    h4, act1 = _ffn_fwd(h3, f1["g"], f1["wup"], f1["cw"], f1["cb"], f1["wdown"], tm_ff)
    (dh4,), (sq, d_gfinal) = _final_loss(h4, tgt, row(w["g_final"]), tm)
    loss = 0.5 * jnp.sum(sq) / D

    grads = {}

    def ffn_back(hin, dout, act, f):
        din, dupv, dupg, hn, cacc, dg = _ffn_bwd(hin, dout, f["g"], f["wup"], f["cw"], f["cb"], f["wdown_t"], f["wup_t"], tm_fb)
        d_wup = jnp.concatenate([_matmul_tn(hn, dupv, DFF, min(512, s_len), "ffn_dwup_v"),
                                 _matmul_tn(hn, dupg, DFF, min(512, s_len), "ffn_dwup_g")], axis=1)
        d_wdown = _matmul_tn(act, dout, D, min(512, s_len), "ffn_dwdown")
        cflat = cacc.transpose(1, 0, 2).reshape(8, 2 * DFF)
        return din, d_wup, d_wdown, cflat[:3], cflat[3], dg[0]

    dh3, d_wup1, d_wdown1, d_cw1, d_cb1, d_gffn1 = ffn_back(h3, dh4, act1, f1)
    (dy,), (d_wglu,) = _s5_out_bwd(dh3, y, wglu, wglu.T, tm)
    du, d_bb, d_cblk, d_a, d_dsk = _s5_scan_bwd(u, dy, xin, bb, bb.transpose(0, 2, 1), apow, cblk.transpose(0, 2, 1), dskip, t_chunk)
    d_braw, d_lr, d_li, d_ldt = _s5_prep_bwd(lr, li, ldt, braw, d_bb, d_a)
    (dh2,), (d_win, d_gmix1) = _s5_in_bwd(h2, du, dh3, gmix1, win.T, tm)
    dh1, d_wup0, d_wdown0, d_cw0, d_cb0, d_gffn0 = ffn_back(h1, dh2, act0, f0)
    (do, delta), (d_wo,) = _attn_out_bwd(dh1, o, wo.T, tm)
    dqc, dkc, dv = _flash_bwd(qc, kc, v, do, lse, delta, *t_bwd)
    (dx,), (d_wa, d_wuq, d_wukv, d_gq, d_gkv, d_gmix0) = _mla_front_bwd(
        x, pos, dqc, dkc, dv, dh1, gmix0, wa, gq, gkv, wuq, wukv, invf, wa.T, wuq.T, wukv.T, tm_w)

    grads["mla_w_a"] = d_wa[None, :, :QL + KVL + ROPE]
    grads["mla_g_q"] = d_gq
    grads["mla_g_kv"] = d_gkv
    grads["mla_w_uq"] = d_wuq.reshape(QL, HEADS, HC)[:, :, :NOPE + ROPE].reshape(1, QL, HEADS * (NOPE + ROPE))
    grads["mla_w_ukv"] = d_wukv[None]
    grads["mla_w_o"] = d_wo[None]
    grads["ssm_w_in"] = d_win[None]
    grads["ssm_lambda_re"] = d_lr.reshape(1, G, P)
    grads["ssm_lambda_im"] = d_li.reshape(1, G, P)
    grads["ssm_log_dt"] = jnp.sum(d_ldt.reshape(G, P), axis=1)[None]
    grads["ssm_b_re"] = _s5_unblock_b(d_braw[:, :, :SL])[None]
    grads["ssm_b_im"] = _s5_unblock_b(d_braw[:, :, SL:])[None]
    grads["ssm_c_re"] = _s5_unblock_c(d_cblk[:, :SL, :])[None]
    grads["ssm_c_im"] = -_s5_unblock_c(d_cblk[:, SL:, :])[None]
    grads["ssm_d"] = jnp.sum(d_dsk, axis=1).reshape(1, D)
    grads["ssm_w_glu"] = d_wglu[None]
    grads["ffn_w_up"] = jnp.stack([d_wup0, d_wup1])
    grads["ffn_conv_w"] = jnp.stack([d_cw0, d_cw1])
    grads["ffn_conv_b"] = jnp.stack([d_cb0, d_cb1])
    grads["ffn_w_down"] = jnp.stack([d_wdown0, d_wdown1])
    grads["g_mix"] = jnp.concatenate([d_gmix0, d_gmix1], axis=0)
    grads["g_ffn"] = jnp.stack([d_gffn0, d_gffn1])
    grads["g_final"] = d_gfinal[0]
    return loss, dx, grads


MESH = pl.DeviceIdType.MESH
ANY = pl.BlockSpec(memory_space=pl.ANY)


def _all_gather(blk, name):
    r, lanes = blk.shape

    def body(x_ref, out_ref, send_sems, recv_sems, local_sem):
        x, y, c = lax.axis_index("x"), lax.axis_index("y"), lax.axis_index("c")
        me, sibling = (x, y, c), (x, y, 1 - c)
        chips = [(1 - x, y), (x, 1 - y), (1 - x, 1 - y)]

        def slot(px, py, pc):
            return out_ref.at[4 * px + 2 * py + pc]

        def copy(k, block, to, src=None):
            return pltpu.make_async_remote_copy(src_ref=slot(*block) if src is None else src, dst_ref=slot(*block),
                                                send_sem=send_sems.at[k], recv_sem=recv_sems.at[k],
                                                device_id=to, device_id_type=MESH)

        mine = pltpu.make_async_copy(x_ref, slot(*me), local_sem)
        mine.start()
        first = [copy(0, me, sibling, src=x_ref)]
        first += [copy(1 + j, me, (*chip, c), src=x_ref) for j, chip in enumerate(chips)]
        for cp in first:
            cp.start()
        passed = [copy(4 + j, (*chip, c), sibling) for j, chip in enumerate(chips)]
        for j, chip in enumerate(chips):
            copy(1 + j, (*chip, c), me).wait_recv()
            passed[j].start()
        copy(0, sibling, me).wait_recv()
        for j, chip in enumerate(chips):
            copy(4 + j, (*chip, 1 - c), me).wait_recv()
        for cp in first + passed:
            cp.wait_send()
        mine.wait()

    return pl.pallas_call(body, out_shape=jax.ShapeDtypeStruct((NDEV, r, lanes), blk.dtype), in_specs=[ANY], out_specs=ANY,
                          scratch_shapes=[pltpu.SemaphoreType.DMA((7,)), pltpu.SemaphoreType.DMA((7,)), pltpu.SemaphoreType.DMA],
                          name=name)(blk)


def _all_to_all(send, name):
    _, r, lanes = send.shape

    def body(s_ref, land_ref, send_sems, recv_sems, local_sem):
        x, y, c = lax.axis_index("x"), lax.axis_index("y"), lax.axis_index("c")
        me = 4 * x + 2 * y + c
        local = pltpu.make_async_copy(s_ref.at[me], land_ref.at[me], local_sem)
        local.start()
        copies = []
        for m in range(1, NDEV):
            px = 1 - x if m & 4 else x
            py = 1 - y if m & 2 else y
            pc = 1 - c if m & 1 else c
            peer = 4 * px + 2 * py + pc
            out = pltpu.make_async_remote_copy(src_ref=s_ref.at[peer], dst_ref=land_ref.at[me],
                                               send_sem=send_sems.at[m - 1], recv_sem=recv_sems.at[m - 1],
                                               device_id=(px, py, pc), device_id_type=MESH)
            out.start()
            arrival = pltpu.make_async_remote_copy(src_ref=s_ref.at[me], dst_ref=land_ref.at[peer],
                                                   send_sem=send_sems.at[m - 1], recv_sem=recv_sems.at[m - 1],
                                                   device_id=(px, py, pc), device_id_type=MESH)
            copies.append((out, arrival))
        for out, arrival in copies:
            arrival.wait_recv()
        for out, arrival in copies:
            out.wait_send()
        local.wait()

    return pl.pallas_call(body, out_shape=jax.ShapeDtypeStruct(send.shape, send.dtype), in_specs=[ANY], out_specs=ANY,
                          scratch_shapes=[pltpu.SemaphoreType.DMA((7,)), pltpu.SemaphoreType.DMA((7,)), pltpu.SemaphoreType.DMA],
                          name=name)(send)


ADAM_ROWS = 512
PACK = ADAM_ROWS * LANES


def _adamw(w, parts, m, v, name):
    r = w.shape[0]

    def body(w_ref, p_ref, m_ref, v_ref, g_ref, d_ref, m2_ref, v2_ref):
        g = p_ref[0].astype(F32)
        for k in range(1, NDEV):
            g = g + p_ref[k].astype(F32)
        m2 = ADAM_B1 * m_ref[...] + (1.0 - ADAM_B1) * g
        v2 = ADAM_B2 * v_ref[...] + (1.0 - ADAM_B2) * jnp.square(g)
        m_hat = m2 / (1.0 - ADAM_B1 ** ADAM_STEP)
        v_hat = v2 / (1.0 - ADAM_B2 ** ADAM_STEP)
        g_ref[...] = g
        d_ref[...] = -ADAM_LR * (m_hat / (jnp.sqrt(v_hat) + ADAM_EPS) + ADAM_WD * w_ref[...])
        m2_ref[...] = m2
        v2_ref[...] = v2

    flat = pl.BlockSpec((ADAM_ROWS, LANES), lambda i: (i, 0))
    return pl.pallas_call(body, grid=(r // ADAM_ROWS,),
                          in_specs=[flat, pl.BlockSpec((NDEV, ADAM_ROWS, LANES), lambda i: (0, i, 0)), flat, flat],
                          out_specs=[flat] * 4, out_shape=[jax.ShapeDtypeStruct((r, LANES), F32)] * 4,
                          compiler_params=_cparams(1), name=name)(w, parts, m, v)


SHARDED = (("mla_w_a", 1), ("mla_w_uq", 2), ("mla_w_ukv", 2), ("mla_w_o", 1), ("ssm_w_in", 1), ("ssm_d", 1),
           ("ssm_w_glu", 2), ("ffn_w_up", 2), ("ffn_conv_w", 2), ("ffn_w_down", 1))
WIRE_EXACT = ("ssm_d", "ffn_conv_w")
REPLICATED = ("mla_g_q", "mla_g_kv", "ssm_lambda_re", "ssm_lambda_im", "ssm_log_dt", "ssm_b_re", "ssm_b_im",
              "ssm_c_re", "ssm_c_im", "ffn_conv_b", "g_mix", "g_ffn", "g_final")
WEIGHTS = ("mla_w_a", "mla_g_q", "mla_g_kv", "mla_w_uq", "mla_w_ukv", "mla_w_o", "ssm_w_in", "ssm_lambda_re",
           "ssm_lambda_im", "ssm_log_dt", "ssm_b_re", "ssm_b_im", "ssm_c_re", "ssm_c_im", "ssm_d", "ssm_w_glu",
           "ffn_w_up", "ffn_conv_w", "ffn_conv_b", "ffn_w_down", "g_mix", "g_ffn", "g_final")


def _pack(arrays):
    lead = arrays[0].shape[:-1]
    flat = jnp.concatenate(arrays, axis=-1)
    n = flat.shape[-1]
    padded = -(-n // PACK) * PACK
    flat = jnp.pad(flat, [(0, 0)] * len(lead) + [(0, padded - n)])
    return flat.reshape(lead + (padded // LANES, LANES))


def _unpack(flat, shapes):
    flat = flat.reshape(-1)
    out, off = [], 0
    for s in shapes:
        n = math.prod(s)
        out.append(flat[off:off + n].reshape(s))
        off += n
    return out


def kernel(x, positions, mla_w_a, mla_g_q, mla_g_kv, mla_w_uq, mla_w_ukv, mla_w_o, ssm_w_in, ssm_lambda_re, ssm_lambda_im, ssm_log_dt, ssm_b_re, ssm_b_im, ssm_c_re, ssm_c_im, ssm_d, ssm_w_glu, ffn_w_up, ffn_conv_w, ffn_conv_b, ffn_w_down, g_mix, g_ffn, g_final, loss_target, m_mla_w_a, m_mla_g_q, m_mla_g_kv, m_mla_w_uq, m_mla_w_ukv, m_mla_w_o, m_ssm_w_in, m_ssm_lambda_re, m_ssm_lambda_im, m_ssm_log_dt, m_ssm_b_re, m_ssm_b_im, m_ssm_c_re, m_ssm_c_im, m_ssm_d, m_ssm_w_glu, m_ffn_w_up, m_ffn_conv_w, m_ffn_conv_b, m_ffn_w_down, m_g_mix, m_g_ffn, m_g_final, v_mla_w_a, v_mla_g_q, v_mla_g_kv, v_mla_w_uq, v_mla_w_ukv, v_mla_w_o, v_ssm_w_in, v_ssm_lambda_re, v_ssm_lambda_im, v_ssm_log_dt, v_ssm_b_re, v_ssm_b_im, v_ssm_c_re, v_ssm_c_im, v_ssm_d, v_ssm_w_glu, v_ffn_w_up, v_ffn_conv_w, v_ffn_conv_b, v_ffn_w_down, v_g_mix, v_g_ffn, v_g_final):
    a = dict(locals())
    s_len = x.shape[1]
    sh_names = [n for n, _ in SHARDED]
    sh_shapes = [a[n].shape for n in sh_names]

    def to_wire(n):
        flat = a[n].reshape(-1)
        return lax.bitcast_convert_type(flat, MXU).reshape(-1) if n in WIRE_EXACT else flat.astype(MXU)

    wire_shapes = [s + (2,) if n in WIRE_EXACT else s for n, s in zip(sh_names, sh_shapes)]
    gathered = _all_gather(_pack([to_wire(n) for n in sh_names]), "gather_weights")
    w = {n: a[n] for n in REPLICATED}
    for (n, axis), blocks in zip(SHARDED, zip(*[_unpack(gathered[d], wire_shapes) for d in range(NDEV)])):
        if n in WIRE_EXACT:
            blocks = [lax.bitcast_convert_type(b, F32) for b in blocks]
        w[n] = jnp.concatenate(blocks, axis=axis)

    loss, dx, grads = _sequence_step(x[0], positions.reshape(s_len, 1).astype(F32), loss_target[0], w)
    loss = lax.psum(loss, ("x", "y", "c"))

    send = _pack([jnp.stack(jnp.split(grads[n], NDEV, axis=axis)).reshape(NDEV, -1).astype(MXU) for n, axis in SHARDED])
    landed = _all_to_all(send, "exchange_grads")
    rep_parts = _all_gather(_pack([grads[n].reshape(-1) for n in REPLICATED]), "gather_small_grads")

    out = {}
    for names, shapes, parts, tag in ((sh_names, sh_shapes, landed, "adamw_sharded"),
                                     (list(REPLICATED), [a[n].shape for n in REPLICATED], rep_parts, "adamw_replicated")):
        packed = [_pack([a[pre + n].reshape(-1) for n in names]) for pre in ("", "m_", "v_")]
        res = _adamw(packed[0], parts, packed[1], packed[2], tag)
        for kind, flat in zip(("grad_", "delta_", "new_m_", "new_v_"), res):
            for n, val in zip(names, _unpack(flat, shapes)):
                out[kind + n] = val
    return (loss, dx[None], *[out[kind + n] for kind in ("grad_", "delta_", "new_m_", "new_v_") for n in WEIGHTS])
```

```python
import functools
import math

import jax
import jax.numpy as jnp
from jax import lax
from jax.experimental import pallas as pl
from jax.experimental.pallas import tpu as pltpu

F32 = jnp.float32
MXU = jnp.bfloat16

D = 1024
HEADS = 8
NOPE = 128
ROPE = 64
VH = 128
QL = 384
KVL = 256
CHUNK = 64
ROPE_THETA = 10000.0
EPS = 1e-6
G, P, C = 64, 64, 16
DFF = 2816
ADAM_LR, ADAM_B1, ADAM_B2, ADAM_EPS, ADAM_WD, ADAM_STEP = 0.001, 0.9, 0.999, 1e-08, 0.01, 10

LANES = 128
AW = 768
HC = 256
GB = 8
SL = (G // GB) * P
NCOL = 2 * SL // LANES
TC = DFF
HALO = 16
NDEV = 8
VMEM_LIMIT = 56 * 1024 * 1024


def _mm(a, b):
    return jnp.dot(a.astype(MXU), b.astype(MXU), preferred_element_type=F32)


def _mm_tn(a, b):
    return lax.dot_general(a.astype(MXU), b.astype(MXU), (((0,), (0,)), ((), ())), preferred_element_type=F32)


def _mm_nt(a, b):
    return lax.dot_general(a.astype(MXU), b.astype(MXU), (((1,), (1,)), ((), ())), preferred_element_type=F32)


def _rms_fwd(x, g):
    r = lax.rsqrt(jnp.mean(x * x, axis=-1, keepdims=True) + EPS)
    xh = x * r
    return xh * g, xh, r


def _rms_bwd(dy, xh, r, g):
    dxh = dy * g
    dx = r * (dxh - xh * jnp.mean(dxh * xh, axis=-1, keepdims=True))
    return dx, jnp.sum(dy * xh, axis=0, keepdims=True)


def _rot_partner(b):
    lane = lax.broadcasted_iota(jnp.int32, b.shape, 1)
    return jnp.where(lane < ROPE // 2, -pltpu.roll(b, LANES - ROPE // 2, 1), pltpu.roll(b, ROPE // 2, 1))


def _rope_blk(b, cos2, sin2):
    return b * cos2 + _rot_partner(b) * sin2


def _unrope_blk(db, cos2, sin2):
    return db * cos2 - _rot_partner(db * sin2)


def _cparams(n_axes, vmem=VMEM_LIMIT):
    return pltpu.CompilerParams(dimension_semantics=("arbitrary",) * n_axes, vmem_limit_bytes=vmem)


def _rowcall(name, fn, tm, row_ins, consts, row_outs, acc_outs):
    n = row_ins[0].shape[0]
    n_in = len(row_ins) + len(consts)
    n_ro = len(row_outs)

    def body(*refs):
        ins, ro_refs, acc_refs = refs[:n_in], refs[n_in:n_in + n_ro], refs[n_in + n_ro:]
        ro, ao = fn(*[r[...] for r in ins])

        @pl.when(pl.program_id(0) == 0)
        def _():
            for r in acc_refs:
                r[...] = jnp.zeros(r.shape, r.dtype)

        for r, val in zip(ro_refs, ro):
            r[...] = val.astype(r.dtype)
        for r, val in zip(acc_refs, ao):
            r[...] += val

    in_specs = [pl.BlockSpec((tm, a.shape[1]), lambda i: (i, 0)) for a in row_ins]
    in_specs += [pl.BlockSpec(c.shape, lambda i, nd=c.ndim: (0,) * nd) for c in consts]
    out_specs = [pl.BlockSpec((tm, w), lambda i: (i, 0)) for w, _ in row_outs]
    out_specs += [pl.BlockSpec(s, lambda i, nd=len(s): (0,) * nd) for s in acc_outs]
    out_shape = [jax.ShapeDtypeStruct((n, w), dt) for w, dt in row_outs]
    out_shape += [jax.ShapeDtypeStruct(s, F32) for s in acc_outs]
    out = pl.pallas_call(body, grid=(n // tm,), in_specs=in_specs, out_specs=out_specs, out_shape=out_shape,
                         compiler_params=_cparams(1), name=name)(*row_ins, *consts)
    return list(out[:n_ro]), list(out[n_ro:])


def _mla_front_tile(x, pos, gmix, wa, gq, gkv, wuq, wukv, invf):
    hn, xh, r = _rms_fwd(x, gmix)
    a = _mm(hn, wa)
    cq, ckv, krb = a[:, :QL], a[:, QL:QL + KVL], a[:, QL + KVL:]
    cqn, cqh, rq = _rms_fwd(cq, gq)
    ckvn, ckvh, rkv = _rms_fwd(ckv, gkv)
    q = _mm(cqn, wuq)
    kv = _mm(ckvn, wukv)
    ang = pos * invf
    cos2, sin2 = jnp.cos(ang), jnp.sin(ang)
    krr = _rope_blk(krb, cos2, sin2)
    qp, kp, vp = [], [], []
    for h in range(HEADS):
        qp += [q[:, h * HC:h * HC + NOPE], _rope_blk(q[:, h * HC + NOPE:(h + 1) * HC], cos2, sin2)]
        kp += [kv[:, h * HC:h * HC + NOPE], krr]
        vp += [kv[:, h * HC + NOPE:(h + 1) * HC]]
    res = (hn, xh, r, cqn, cqh, rq, ckvn, ckvh, rkv, cos2, sin2)
    return jnp.concatenate(qp, axis=1), jnp.concatenate(kp, axis=1), jnp.concatenate(vp, axis=1), res


def _mla_front_fwd(x, pos, gmix, wa, gq, gkv, wuq, wukv, invf, tm):
    def fn(*args):
        qc, kc, v, _ = _mla_front_tile(*args)
        return (qc * Q_PRESCALE, kc, v), ()

    return _rowcall("mla_front_fwd", fn, tm, [x, pos], [gmix, wa, gq, gkv, wuq, wukv, invf],
                    [(HEADS * HC, MXU), (HEADS * HC, MXU), (HEADS * VH, MXU)], [])[0]


def _mla_front_bwd(x, pos, dqc, dkc, dv, dh, gmix, wa, gq, gkv, wuq, wukv, invf, wa_t, wuq_t, wukv_t, tm):
    def fn(x, pos, dqc, dkc, dv, dh, gmix, wa, gq, gkv, wuq, wukv, invf, wa_t, wuq_t, wukv_t):
        _, _, _, (hn, xh, r, cqn, cqh, rq, ckvn, ckvh, rkv, cos2, sin2) = _mla_front_tile(
            x, pos, gmix, wa, gq, gkv, wuq, wukv, invf)
        dqc, dkc, dv = dqc * SM_SCALE, dkc.astype(F32), dv.astype(F32)
        dqp, dkvp = [], []
        dkr = jnp.zeros((x.shape[0], LANES), F32)
        for h in range(HEADS):
            dqp += [dqc[:, h * HC:h * HC + NOPE], _unrope_blk(dqc[:, h * HC + NOPE:(h + 1) * HC], cos2, sin2)]
            dkvp += [dkc[:, h * HC:h * HC + NOPE], dv[:, h * VH:(h + 1) * VH]]
            dkr = dkr + dkc[:, h * HC + NOPE:(h + 1) * HC]
        dq = jnp.concatenate(dqp, axis=1)
        dkv = jnp.concatenate(dkvp, axis=1)
        dkrb = _unrope_blk(dkr, cos2, sin2)
        dcqn = _mm(dq, wuq_t)
        dckvn = _mm(dkv, wukv_t)
        d_wuq = _mm_tn(cqn, dq)
        d_wukv = _mm_tn(ckvn, dkv)
        dcq, d_gq = _rms_bwd(dcqn, cqh, rq, gq)
        dckv, d_gkv = _rms_bwd(dckvn, ckvh, rkv, gkv)
        da = jnp.concatenate([dcq, dckv, dkrb], axis=1)
        d_wa = _mm_tn(hn, da)
        dhn = _mm(da, wa_t)
        dx, d_gmix = _rms_bwd(dhn, xh, r, gmix)
        return (dh + dx,), (d_wa, d_wuq, d_wukv, d_gq, d_gkv, d_gmix)

    return _rowcall("mla_front_bwd", fn, tm, [x, pos, dqc, dkc, dv, dh],
                    [gmix, wa, gq, gkv, wuq, wukv, invf, wa_t, wuq_t, wukv_t], [(D, F32)],
                    [(D, AW), (QL, HEADS * HC), (KVL, HEADS * HC), (1, QL), (1, KVL), (1, D)])


SM_SCALE = (NOPE + ROPE) ** -0.5
LOG2E = 1.0 / math.log(2.0)
Q_PRESCALE = SM_SCALE * LOG2E


def _pair_tables(s_len, tq, tk, q_major):
    pairs = [(qi, ki) for qi in range(s_len // tq) for ki in range(s_len // tk) if ki * tk < (qi + 1) * tq]
    if not q_major:
        pairs.sort(key=lambda p: (p[1], p[0]))
    return (jnp.asarray([p[0] for p in pairs], jnp.int32), jnp.asarray([p[1] for p in pairs], jnp.int32))


def _last_key_tile(qi, tq, tk):
    return ((qi + 1) * tq - 1) // tk


def _visible(qi, ki, tq, tk):
    row = qi * (tq // CHUNK) + lax.broadcasted_iota(jnp.int32, (tq, tk), 0) // CHUNK
    col = ki * (tk // CHUNK) + lax.broadcasted_iota(jnp.int32, (tq, tk), 1) // CHUNK
    return col <= row


def _masked_and_not(qi, ki, tq, tk, fn):
    needs_mask = (ki + 1) * tk > qi * tq
    pl.when(needs_mask)(lambda: fn(True))
    pl.when(jnp.logical_not(needs_mask))(lambda: fn(False))


def _flash_fwd(qc, kc, v, tq, tk):
    s_len = qc.shape[0]
    qt, kt = _pair_tables(s_len, tq, tk, True)

    def body(qt_ref, kt_ref, q_ref, k_ref, v_ref, o_ref, lse_ref, m_sc, l_sc, acc_sc):
        p_id = pl.program_id(1)
        qi, ki = qt_ref[p_id], kt_ref[p_id]

        @pl.when(ki == 0)
        def _():
            m_sc[...] = jnp.full(m_sc.shape, -jnp.inf, F32)
            l_sc[...] = jnp.zeros(l_sc.shape, F32)
            acc_sc[...] = jnp.zeros(acc_sc.shape, F32)

        def update(masked):
            s = _mm_nt(q_ref[...], k_ref[...])
            if masked:
                s = jnp.where(_visible(qi, ki, tq, tk), s, -jnp.inf)
            m_old = m_sc[...]
            m_new = jnp.maximum(m_old, jnp.max(s, axis=1, keepdims=True))
            alpha = jnp.exp2(m_old - m_new)
            p = jnp.exp2(s - m_new)
            l_sc[...] = alpha * l_sc[...] + jnp.sum(p, axis=1, keepdims=True)
            acc_sc[...] = alpha * acc_sc[...] + _mm(p, v_ref[...])
            m_sc[...] = m_new

        _masked_and_not(qi, ki, tq, tk, update)

        @pl.when(ki == _last_key_tile(qi, tq, tk))
        def _():
            l = l_sc[...]
            o_ref[...] = (acc_sc[...] / l).astype(o_ref.dtype)
            lse_ref[...] = jnp.broadcast_to(m_sc[...] + jnp.log2(l), lse_ref.shape)

    qmap = lambda h, p, qt, kt: (qt[p], h)
    kmap = lambda h, p, qt, kt: (kt[p], h)
    grid_spec = pltpu.PrefetchScalarGridSpec(
        num_scalar_prefetch=2, grid=(HEADS, qt.shape[0]),
        in_specs=[pl.BlockSpec((tq, HC), qmap), pl.BlockSpec((tk, HC), kmap), pl.BlockSpec((tk, VH), kmap)],
        out_specs=[pl.BlockSpec((tq, VH), qmap), pl.BlockSpec((tq, LANES), qmap)],
        scratch_shapes=[pltpu.VMEM((tq, 1), F32), pltpu.VMEM((tq, 1), F32), pltpu.VMEM((tq, VH), F32)])
    return pl.pallas_call(body, grid_spec=grid_spec,
                          out_shape=[jax.ShapeDtypeStruct((s_len, HEADS * VH), MXU),
                                     jax.ShapeDtypeStruct((s_len, HEADS * LANES), F32)],
                          compiler_params=_cparams(2), name="flash_fwd")(qt, kt, qc, kc, v)


def _tile_dscores(q, k, v, do, lse, delta, qi, ki, tq, tk, masked):
    p = jnp.exp2(_mm_nt(q, k) - lse[:, :1])
    if masked:
        p = jnp.where(_visible(qi, ki, tq, tk), p, 0.0)
    return p, p * (_mm_nt(do, v) - delta[:, :1])


def _flash_bwd(qc, kc, v, do, lse, delta, tq, tk):
    s_len = qc.shape[0]
    qt, kt = _pair_tables(s_len, tq, tk, False)
    nq = s_len // tq

    def body(qt_ref, kt_ref, q_ref, k_ref, v_ref, do_ref, lse_ref, dl_ref, dq_ref, dk_ref, dv_ref, dk_sc, dv_sc):
        p_id = pl.program_id(1)
        qi, ki = qt_ref[p_id], kt_ref[p_id]
        rows = pl.ds(pl.multiple_of(qi * tq, tq), tq)

        @pl.when(qi == (ki * tk) // tq)
        def _():
            dk_sc[...] = jnp.zeros(dk_sc.shape, F32)
            dv_sc[...] = jnp.zeros(dv_sc.shape, F32)

        @pl.when(ki == 0)
        def _():
            dq_ref[rows, :] = jnp.zeros((tq, HC), F32)

        def update(masked):
            q, k, do = q_ref[...], k_ref[...], do_ref[...]
            p, ds = _tile_dscores(q, k, v_ref[...], do, lse_ref[...], dl_ref[...], qi, ki, tq, tk, masked)
            ds = ds.astype(MXU)
            dv_sc[...] += _mm_tn(p, do)
            dk_sc[...] += _mm_tn(ds, q)
            dq_ref[rows, :] += _mm(ds, k)

        _masked_and_not(qi, ki, tq, tk, update)

        @pl.when(qi == nq - 1)
        def _():
            dk_ref[...] = (dk_sc[...] * (1.0 / LOG2E)).astype(dk_ref.dtype)
            dv_ref[...] = dv_sc[...].astype(dv_ref.dtype)

    qmap = lambda h, p, qt, kt: (qt[p], h)
    kmap = lambda h, p, qt, kt: (kt[p], h)
    grid_spec = pltpu.PrefetchScalarGridSpec(
        num_scalar_prefetch=2, grid=(HEADS, qt.shape[0]),
        in_specs=[pl.BlockSpec((tq, HC), qmap), pl.BlockSpec((tk, HC), kmap), pl.BlockSpec((tk, VH), kmap),
                  pl.BlockSpec((tq, VH), qmap), pl.BlockSpec((tq, LANES), qmap), pl.BlockSpec((tq, LANES), qmap)],
        out_specs=[pl.BlockSpec((s_len, HC), lambda h, p, qt, kt: (0, h), pipeline_mode=pl.Buffered(1)),
                   pl.BlockSpec((tk, HC), kmap), pl.BlockSpec((tk, VH), kmap)],
        scratch_shapes=[pltpu.VMEM((tk, HC), F32), pltpu.VMEM((tk, VH), F32)])
    return pl.pallas_call(body, grid_spec=grid_spec,
                          out_shape=[jax.ShapeDtypeStruct((s_len, HEADS * HC), F32),
                                     jax.ShapeDtypeStruct((s_len, HEADS * HC), MXU),
                                     jax.ShapeDtypeStruct((s_len, HEADS * VH), MXU)],
                          compiler_params=_cparams(2), name="flash_bwd")(qt, kt, qc, kc, v, do, lse, delta)


def _attn_out_fwd(x, o, wo, tm):
    def fn(x, o, wo):
        return (x + _mm(o, wo),), ()

    return _rowcall("attn_out_fwd", fn, tm, [x, o], [wo], [(D, F32)], [])[0][0]


def _attn_out_bwd(dh, o, wo_t, tm):
    def fn(dh, o, wo_t):
        do = _mm(dh, wo_t)
        of = o.astype(F32)
        dl = [jnp.broadcast_to(jnp.sum(do[:, h * VH:(h + 1) * VH] * of[:, h * VH:(h + 1) * VH], axis=1, keepdims=True),
                               (dh.shape[0], LANES)) for h in range(HEADS)]
        return (do, jnp.concatenate(dl, axis=1)), (_mm_tn(o, dh),)

    return _rowcall("attn_out_bwd", fn, tm, [dh, o], [wo_t], [(HEADS * VH, MXU), (HEADS * LANES, F32)], [(HEADS * VH, D)])


def _shift_rows(a, k):
    return a if k == 0 else pltpu.roll(a, k % a.shape[0], 0)


def _stack_rows(rows):
    idx = lax.broadcasted_iota(jnp.int32, (8, rows[0].shape[1]), 0)
    out = jnp.zeros((8, rows[0].shape[1]), F32)
    for k, r in enumerate(rows):
        out = jnp.where(idx == k, r, out)
    return out


def _ffn_fwd(h, g, wup, cw, cb, wdown, tm):
    s_len = h.shape[0]
    nj = DFF // TC
    hb = tm // HALO

    def body(h_ref, hp_ref, g_ref, wv_ref, wg_ref, cwv_ref, cwg_ref, cbv_ref, cbg_ref, wd_ref, out_ref, act_ref,
             hn_sc, acc_sc):
        i, j = pl.program_id(0), pl.program_id(1)

        @pl.when(j == 0)
        def _():
            gg = g_ref[...]
            hp = _rms_fwd(hp_ref[...], gg)[0]
            hn_sc[:HALO, :] = jnp.where(i > 0, hp, 0.0).astype(MXU)
            hn_sc[HALO:, :] = _rms_fwd(h_ref[...], gg)[0].astype(MXU)
            acc_sc[...] = jnp.zeros(acc_sc.shape, F32)

        hn = hn_sc[...]

        def conv(w_ref, cw_ref, cb_ref):
            up = jnp.dot(hn, w_ref[...], preferred_element_type=F32)
            cwv = cw_ref[...]
            c = cwv[2:3] * up + cwv[1:2] * _shift_rows(up, 1) + cwv[0:1] * _shift_rows(up, 2)
            return c[HALO:] + cb_ref[...]

        cv = conv(wv_ref, cwv_ref, cbv_ref)
        cg = conv(wg_ref, cwg_ref, cbg_ref)
        act = cg * jax.nn.sigmoid(cg) * cv
        act_ref[...] = act.astype(act_ref.dtype)
        acc_sc[...] += _mm(act, wd_ref[...])

        @pl.when(j == nj - 1)
        def _():
            out_ref[...] = h_ref[...] + acc_sc[...]

    in_specs = [pl.BlockSpec((tm, D), lambda i, j: (i, 0)),
                pl.BlockSpec((HALO, D), lambda i, j: (jnp.maximum(i * hb - 1, 0), 0)),
                pl.BlockSpec((1, D), lambda i, j: (0, 0)),
                pl.BlockSpec((D, TC), lambda i, j: (0, j)), pl.BlockSpec((D, TC), lambda i, j: (0, j + nj)),
                pl.BlockSpec((3, TC), lambda i, j: (0, j)), pl.BlockSpec((3, TC), lambda i, j: (0, j + nj)),
                pl.BlockSpec((1, TC), lambda i, j: (0, j)), pl.BlockSpec((1, TC), lambda i, j: (0, j + nj)),
                pl.BlockSpec((TC, D), lambda i, j: (j, 0))]
    out_specs = [pl.BlockSpec((tm, D), lambda i, j: (i, 0)), pl.BlockSpec((tm, TC), lambda i, j: (i, j))]
    return pl.pallas_call(body, grid=(s_len // tm, nj), in_specs=in_specs, out_specs=out_specs,
                          out_shape=[jax.ShapeDtypeStruct((s_len, D), F32), jax.ShapeDtypeStruct((s_len, DFF), MXU)],
                          scratch_shapes=[pltpu.VMEM((tm + HALO, D), MXU), pltpu.VMEM((tm, D), F32)],
                          compiler_params=_cparams(2), name="ffn_fwd")(h, h, g, wup, wup, cw, cw, cb, cb, wdown)


def _ffn_bwd(h, dout, g, wup, cw, cb, wdown_t, wup_t, tm):
    s_len = h.shape[0]
    nj = DFF // TC
    ni = s_len // tm
    hb = tm // HALO
    rows = tm + 2 * HALO

    def body(h_ref, hp_ref, hx_ref, d_ref, dx_ref, g_ref, wv_ref, wg_ref, cwv_ref, cwg_ref, cbv_ref, cbg_ref,
             wdt_ref, wutv_ref, wutg_ref, din_ref, dupv_ref, dupg_ref, hn_ref, cacc_ref, dg_ref, hn_sc, d_sc, acc_sc):
        i, j = pl.program_id(0), pl.program_id(1)

        @pl.when(j == 0)
        def _():
            gg = g_ref[...]
            hn_sc[:HALO, :] = jnp.where(i > 0, _rms_fwd(hp_ref[...], gg)[0], 0.0).astype(MXU)
            hn = _rms_fwd(h_ref[...], gg)[0].astype(MXU)
            hn_sc[HALO:HALO + tm, :] = hn
            hn_ref[...] = hn
            hn_sc[HALO + tm:, :] = _rms_fwd(hx_ref[...], gg)[0].astype(MXU)
            d_sc[:tm, :] = d_ref[...].astype(MXU)
            d_sc[tm:, :] = jnp.where(i < ni - 1, dx_ref[...], 0.0).astype(MXU)
            acc_sc[...] = jnp.zeros(acc_sc.shape, F32)

        @pl.when(jnp.logical_and(i == 0, j == 0))
        def _():
            cacc_ref[...] = jnp.zeros(cacc_ref.shape, F32)
            dg_ref[...] = jnp.zeros(dg_ref.shape, F32)

        hn = hn_sc[...]
        dd = d_sc[...]

        def half(w_ref, cw_ref, cb_ref):
            up = jnp.dot(hn, w_ref[...], preferred_element_type=F32)
            cwv = cw_ref[...]
            u1, u2 = _shift_rows(up, 1), _shift_rows(up, 2)
            c = (cwv[2:3] * up + cwv[1:2] * u1 + cwv[0:1] * u2)[HALO:] + cb_ref[...]
            return c, (up[HALO:HALO + tm], u1[HALO:HALO + tm], u2[HALO:HALO + tm]), cwv

        cv, upv, cwv = half(wv_ref, cwv_ref, cbv_ref)
        cg, upg, cwg = half(wg_ref, cwg_ref, cbg_ref)
        dact = jnp.dot(dd, wdt_ref[...], preferred_element_type=F32)
        sg = jax.nn.sigmoid(cg)
        dcv = dact * (cg * sg)
        dcg = dact * cv * (sg * (1.0 + cg * (1.0 - sg)))

        def back(dc, ups, cwx, slot, dup_ref, wut_ref):
            dup = (cwx[2:3] * dc + cwx[1:2] * _shift_rows(dc, -1) + cwx[0:1] * _shift_rows(dc, -2))[:tm]
            dct = dc[:tm]
            cacc_ref[slot] += _stack_rows([jnp.sum(dct * ups[2], axis=0, keepdims=True),
                                           jnp.sum(dct * ups[1], axis=0, keepdims=True),
                                           jnp.sum(dct * ups[0], axis=0, keepdims=True),
                                           jnp.sum(dct, axis=0, keepdims=True)])
            dup_ref[...] = dup.astype(dup_ref.dtype)
            return _mm(dup, wut_ref[...])

        acc_sc[...] += back(dcv, upv, cwv, j, dupv_ref, wutv_ref) + back(dcg, upg, cwg, j + nj, dupg_ref, wutg_ref)

        @pl.when(j == nj - 1)
        def _():
            gg = g_ref[...]
            _, xh, r = _rms_fwd(h_ref[...], gg)
            dx, dgp = _rms_bwd(acc_sc[...], xh, r, gg)
            din_ref[...] = d_ref[...] + dx
            dg_ref[...] += dgp

    last_blk = s_len // HALO - 1
    in_specs = [pl.BlockSpec((tm, D), lambda i, j: (i, 0)),
                pl.BlockSpec((HALO, D), lambda i, j: (jnp.maximum(i * hb - 1, 0), 0)),
                pl.BlockSpec((HALO, D), lambda i, j: (jnp.minimum((i + 1) * hb, last_blk), 0)),
                pl.BlockSpec((tm, D), lambda i, j: (i, 0)),
                pl.BlockSpec((HALO, D), lambda i, j: (jnp.minimum((i + 1) * hb, last_blk), 0)),
                pl.BlockSpec((1, D), lambda i, j: (0, 0)),
                pl.BlockSpec((D, TC), lambda i, j: (0, j)), pl.BlockSpec((D, TC), lambda i, j: (0, j + nj)),
                pl.BlockSpec((3, TC), lambda i, j: (0, j)), pl.BlockSpec((3, TC), lambda i, j: (0, j + nj)),
                pl.BlockSpec((1, TC), lambda i, j: (0, j)), pl.BlockSpec((1, TC), lambda i, j: (0, j + nj)),
                pl.BlockSpec((D, TC), lambda i, j: (0, j)),
                pl.BlockSpec((TC, D), lambda i, j: (j, 0)), pl.BlockSpec((TC, D), lambda i, j: (j + nj, 0))]
    out_specs = [pl.BlockSpec((tm, D), lambda i, j: (i, 0)),
                 pl.BlockSpec((tm, TC), lambda i, j: (i, j)), pl.BlockSpec((tm, TC), lambda i, j: (i, j)),
                 pl.BlockSpec((tm, D), lambda i, j: (i, 0)),
                 pl.BlockSpec((2 * nj, 8, TC), lambda i, j: (0, 0, 0)),
                 pl.BlockSpec((1, D), lambda i, j: (0, 0))]
    out_shape = [jax.ShapeDtypeStruct((s_len, D), F32),
                 jax.ShapeDtypeStruct((s_len, DFF), MXU), jax.ShapeDtypeStruct((s_len, DFF), MXU),
                 jax.ShapeDtypeStruct((s_len, D), MXU),
                 jax.ShapeDtypeStruct((2 * nj, 8, TC), F32), jax.ShapeDtypeStruct((1, D), F32)]
    return pl.pallas_call(body, grid=(ni, nj), in_specs=in_specs, out_specs=out_specs, out_shape=out_shape,
                          scratch_shapes=[pltpu.VMEM((rows, D), MXU), pltpu.VMEM((tm + HALO, D), MXU),
                                          pltpu.VMEM((tm, D), F32)],
                          compiler_params=_cparams(2), name="ffn_bwd")(
        h, h, h, dout, dout, g, wup, wup, cw, cw, cb, cb, wdown_t, wup_t, wup_t)


def _matmul_tn(a, b, tn, ts, name):
    s_len, m = a.shape
    n = b.shape[1]

    def body(a_ref, b_ref, o_ref):
        @pl.when(pl.program_id(1) == 0)
        def _():
            o_ref[...] = jnp.zeros(o_ref.shape, F32)

        o_ref[...] += _mm_tn(a_ref[...], b_ref[...])

    return pl.pallas_call(body, grid=(n // tn, s_len // ts),
                          in_specs=[pl.BlockSpec((ts, m), lambda jn, k: (k, 0)), pl.BlockSpec((ts, tn), lambda jn, k: (k, jn))],
                          out_specs=pl.BlockSpec((m, tn), lambda jn, k: (0, jn)),
                          out_shape=jax.ShapeDtypeStruct((m, n), F32), compiler_params=_cparams(2), name=name)(a, b)


def _s5_coefs(lr, li, ldt):
    dt = jnp.exp(ldt)
    mag = jnp.exp(lr * dt)
    th = li * dt
    ar, ai = mag * jnp.cos(th), mag * jnp.sin(th)
    den = lr * lr + li * li
    nr = ar - 1.0
    cr = (nr * lr + ai * li) / den
    ci = (ai * lr - nr * li) / den
    return dt, mag, th, ar, ai, den, nr, cr, ci


def _s5_prep(lr, li, ldt, braw, seg):
    def body(lr_ref, li_ref, ldt_ref, b_ref, bb_ref, ap_ref):
        lr_, li_, ldt_ = lr_ref[0], li_ref[0], ldt_ref[0]
        dt, mag, th, ar, ai, den, nr, cr, ci = _s5_coefs(lr_, li_, ldt_)
        br, bi = b_ref[0, :, :SL], b_ref[0, :, SL:]
        bb_ref[0, :, :SL] = (cr * br - ci * bi).astype(bb_ref.dtype)
        bb_ref[0, :, SL:] = (cr * bi + ci * br).astype(bb_ref.dtype)
        for i in range(seg):
            m = jnp.exp((i + 1.0) * (lr_ * dt))
            ap_ref[0, i * 8:(i + 1) * 8, :SL] = jnp.broadcast_to(m * jnp.cos((i + 1.0) * th), (8, SL))
            ap_ref[0, i * 8:(i + 1) * 8, SL:] = jnp.broadcast_to(m * jnp.sin((i + 1.0) * th), (8, SL))

    vec = pl.BlockSpec((1, 1, SL), lambda k: (k, 0, 0))
    return pl.pallas_call(
        body, grid=(GB,), in_specs=[vec, vec, vec, pl.BlockSpec((1, LANES, 2 * SL), lambda k: (k, 0, 0))],
        out_specs=[pl.BlockSpec((1, LANES, 2 * SL), lambda k: (k, 0, 0)),
                   pl.BlockSpec((1, 8 * seg, 2 * SL), lambda k: (k, 0, 0))],
        out_shape=[jax.ShapeDtypeStruct((GB, LANES, 2 * SL), MXU), jax.ShapeDtypeStruct((GB, 8 * seg, 2 * SL), F32)],
        compiler_params=_cparams(1), name="s5_prep")(lr, li, ldt, braw)


def _s5_prep_bwd(lr, li, ldt, braw, dbb, da):
    def body(lr_ref, li_ref, ldt_ref, b_ref, dbb_ref, da_ref, dbraw_ref, dlr_ref, dli_ref, dldt_ref):
        lr_, li_, ldt_ = lr_ref[0], li_ref[0], ldt_ref[0]
        dt, mag, th, ar, ai, den, nr, cr, ci = _s5_coefs(lr_, li_, ldt_)
        br, bi = b_ref[0, :, :SL], b_ref[0, :, SL:]
        gbr, gbi = dbb_ref[0, :, :SL], dbb_ref[0, :, SL:]
        dbraw_ref[0, :, :SL] = cr * gbr + ci * gbi
        dbraw_ref[0, :, SL:] = cr * gbi - ci * gbr
        dcr = jnp.sum(gbr * br + gbi * bi, axis=0, keepdims=True)
        dci = jnp.sum(gbi * br - gbr * bi, axis=0, keepdims=True)
        dar = jnp.sum(da_ref[0, :, :SL], axis=0, keepdims=True)
        dai = jnp.sum(da_ref[0, :, SL:], axis=0, keepdims=True)
        g1, g2 = dcr / den, dci / den
        gden = -(dcr * cr + dci * ci) / den
        gar = dar + g1 * lr_ - g2 * li_
        gai = dai + g1 * li_ + g2 * lr_
        glr = g1 * nr + g2 * ai + 2.0 * lr_ * gden
        gli = g1 * ai - g2 * nr + 2.0 * li_ * gden
        gmag = gar * jnp.cos(th) + gai * jnp.sin(th)
        gth = gai * ar - gar * ai
        dlr_ref[0] = glr + gmag * mag * dt
        dli_ref[0] = gli + gth * dt
        dldt_ref[0] = (gmag * mag * lr_ + gth * li_) * dt

    vec = pl.BlockSpec((1, 1, SL), lambda k: (k, 0, 0))
    mat = pl.BlockSpec((1, LANES, 2 * SL), lambda k: (k, 0, 0))
    return pl.pallas_call(
        body, grid=(GB,), in_specs=[vec, vec, vec, mat, mat, pl.BlockSpec((1, 8, 2 * SL), lambda k: (k, 0, 0))],
        out_specs=[mat, vec, vec, vec],
        out_shape=[jax.ShapeDtypeStruct((GB, LANES, 2 * SL), F32)] + [jax.ShapeDtypeStruct((GB, 1, SL), F32)] * 3,
        compiler_params=_cparams(1), name="s5_prep_bwd")(lr, li, ldt, braw, dbb, da)


def _bcast_row(tile, j):
    return jnp.broadcast_to(tile[j:j + 1, :], tile.shape)


def _tile_rows(i):
    return pl.ds(pl.multiple_of(i * 8, 8), 8)


def _col(c):
    return slice(c * LANES, (c + 1) * LANES)


def _permute_rows(ref, seg):
    return jnp.concatenate([ref[pl.ds(i, 8, stride=seg), :] for i in range(seg)], axis=0)


def _unpermute_rows(src, dst, seg):
    for j in range(8):
        dst[j * seg:(j + 1) * seg, :] = src[pl.ds(j, seg, stride=8), :]


SCAN_UNROLL = 2


def _segment_scan(src, dst, ap, seg, reverse):
    half = NCOL // 2
    sign = -1.0 if reverse else 1.0
    a_r = [ap[0:8, _col(c)] for c in range(half)]
    a_i = [ap[0:8, _col(half + c)] for c in range(half)]

    def step(n, carry):
        i = seg - 1 - n if reverse else n
        rows = _tile_rows(i)
        out_r, out_i = [], []
        for c in range(half):
            xr, xi = carry[c], carry[half + c]
            nr = a_r[c] * xr - sign * a_i[c] * xi + src[rows, _col(c)]
            ni = a_r[c] * xi + sign * a_i[c] * xr + src[rows, _col(half + c)]
            dst[rows, _col(c)] = nr
            dst[rows, _col(half + c)] = ni
            out_r.append(nr)
            out_i.append(ni)
        return tuple(out_r + out_i)

    zero = jnp.zeros((8, LANES), F32)
    return lax.fori_loop(0, seg, step, (zero,) * NCOL, unroll=SCAN_UNROLL)


def _segment_entries(ends, cin, ap, seg, reverse):
    half = NCOL // 2
    sign = -1.0 if reverse else 1.0
    al_r = [ap[(seg - 1) * 8:seg * 8, _col(c)] for c in range(half)]
    al_i = [ap[(seg - 1) * 8:seg * 8, _col(half + c)] for c in range(half)]
    row = lax.broadcasted_iota(jnp.int32, (8, LANES), 0)
    ent, out = [None] * NCOL, [None] * NCOL
    for c in range(half):
        zr, zi = cin[c], cin[half + c]
        er, ei = jnp.zeros((8, LANES), F32), jnp.zeros((8, LANES), F32)
        for j in (range(7, -1, -1) if reverse else range(8)):
            er, ei = jnp.where(row == j, zr, er), jnp.where(row == j, zi, ei)
            fr, fi = _bcast_row(ends[c], j), _bcast_row(ends[half + c], j)
            zr, zi = (al_r[c] * zr - sign * al_i[c] * zi + fr, al_r[c] * zi + sign * al_i[c] * zr + fi)
        ent[c], ent[half + c] = er, ei
        out[c], out[half + c] = zr, zi
    return ent, out


def _chunk_states(u, bb_ref, ap, cin, bu_sc, x_sc, seg):
    half = NCOL // 2
    bu_sc[...] = _mm(u, bb_ref[0])
    ends = _segment_scan(bu_sc, x_sc, ap, seg, False)
    ent, out = _segment_entries(ends, cin, ap, seg, False)

    def fix(i, _):
        rows = _tile_rows(i)
        for c in range(half):
            pr, pi = ap[rows, _col(c)], ap[rows, _col(half + c)]
            x_sc[rows, _col(c)] += pr * ent[c] - pi * ent[half + c]
            x_sc[rows, _col(half + c)] += pr * ent[half + c] + pi * ent[c]
        return 0

    lax.fori_loop(0, seg, fix, 0, unroll=SCAN_UNROLL)
    return out


def _s5_scan_fwd(u, bb, apow, cblk, dskip, t_chunk):
    s_len = u.shape[0]
    nc = s_len // t_chunk
    seg = t_chunk // 8

    def body(u_ref, bb_ref, ap_ref, c_ref, d_ref, y_ref, xin_ref, bu_sc, x_sc, y_sc, carry_sc):
        @pl.when(pl.program_id(1) == 0)
        def _():
            carry_sc[...] = jnp.zeros(carry_sc.shape, F32)

        uu = _permute_rows(u_ref, seg)
        cin = [carry_sc[:, _col(c)] for c in range(NCOL)]
        xin_ref[0, 0] = carry_sc[...]
        out = _chunk_states(uu, bb_ref, ap_ref.at[0], cin, bu_sc, x_sc, seg)
        for c in range(NCOL):
            carry_sc[:, _col(c)] = out[c]
        y_sc[...] = _mm(x_sc[...], c_ref[0]) + d_ref[...] * uu
        _unpermute_rows(y_sc, y_ref, seg)

    in_specs = [pl.BlockSpec((t_chunk, LANES), lambda k, c: (c, k)),
                pl.BlockSpec((1, LANES, 2 * SL), lambda k, c: (k, 0, 0)),
                pl.BlockSpec((1, 8 * seg, 2 * SL), lambda k, c: (k, 0, 0)),
                pl.BlockSpec((1, 2 * SL, LANES), lambda k, c: (k, 0, 0)),
                pl.BlockSpec((1, LANES), lambda k, c: (0, k))]
    out_specs = [pl.BlockSpec((t_chunk, LANES), lambda k, c: (c, k)),
                 pl.BlockSpec((1, 1, 8, 2 * SL), lambda k, c: (k, c, 0, 0))]
    return pl.pallas_call(body, grid=(GB, nc), in_specs=in_specs, out_specs=out_specs,
                          out_shape=[jax.ShapeDtypeStruct((s_len, D), F32), jax.ShapeDtypeStruct((GB, nc, 8, 2 * SL), F32)],
                          scratch_shapes=[pltpu.VMEM((t_chunk, 2 * SL), F32), pltpu.VMEM((t_chunk, 2 * SL), F32),
                                          pltpu.VMEM((t_chunk, LANES), F32), pltpu.VMEM((8, 2 * SL), F32)],
                          compiler_params=_cparams(2), name="s5_scan_fwd")(u, bb, apow, cblk, dskip)


def _s5_scan_bwd(u, dy, xin, bb, bb_t, apow, cblk_t, dskip, t_chunk):
    s_len = u.shape[0]
    nc = s_len // t_chunk
    seg = t_chunk // 8
    half = NCOL // 2

    def body(u_ref, dy_ref, xin_ref, bb_ref, bbt_ref, ap_ref, ct_ref, d_ref, du_ref, dbb_ref, dc_ref, da_ref, dd_ref,
             bu_sc, x_sc, g_sc, y_sc, carry_sc):
        @pl.when(pl.program_id(1) == 0)
        def _():
            carry_sc[...] = jnp.zeros(carry_sc.shape, F32)
            dbb_ref[...] = jnp.zeros(dbb_ref.shape, F32)
            dc_ref[...] = jnp.zeros(dc_ref.shape, F32)
            da_ref[...] = jnp.zeros(da_ref.shape, F32)
            dd_ref[...] = jnp.zeros(dd_ref.shape, F32)

        ap = ap_ref.at[0]
        uu, dyy = _permute_rows(u_ref, seg), _permute_rows(dy_ref, seg)
        cin = [xin_ref[0, 0, :, _col(c)] for c in range(NCOL)]
        _chunk_states(uu, bb_ref, ap, cin, bu_sc, x_sc, seg)
        bu_sc[...] = _mm(dyy, ct_ref[0])
        ends = _segment_scan(bu_sc, g_sc, ap, seg, True)
        lam_in = [carry_sc[:, _col(c)] for c in range(NCOL)]
        ent, out = _segment_entries(ends, lam_in, ap, seg, True)
        for c in range(NCOL):
            carry_sc[:, _col(c)] = out[c]
        row = lax.broadcasted_iota(jnp.int32, (8, LANES), 0)
        xp0 = [jnp.where(row == 0, cin[c], pltpu.roll(x_sc[(seg - 1) * 8:seg * 8, _col(c)], 1, 0)) for c in range(NCOL)]

        def fix(i, acc):
            rows, prev, tab = _tile_rows(i), _tile_rows(jnp.maximum(i - 1, 0)), _tile_rows(seg - 1 - i)
            new = list(acc)
            for c in range(half):
                pr, pi = ap[tab, _col(c)], ap[tab, _col(half + c)]
                lr_ = g_sc[rows, _col(c)] + pr * ent[c] + pi * ent[half + c]
                li_ = g_sc[rows, _col(half + c)] + pr * ent[half + c] - pi * ent[c]
                g_sc[rows, _col(c)] = lr_
                g_sc[rows, _col(half + c)] = li_
                xr = jnp.where(i == 0, xp0[c], x_sc[prev, _col(c)])
                xi = jnp.where(i == 0, xp0[half + c], x_sc[prev, _col(half + c)])
                new[c] = acc[c] + lr_ * xr + li_ * xi
                new[half + c] = acc[half + c] + li_ * xr - lr_ * xi
            return tuple(new)

        zero = jnp.zeros((8, LANES), F32)
        dacc = lax.fori_loop(0, seg, fix, (zero,) * NCOL, unroll=SCAN_UNROLL)
        for c in range(NCOL):
            da_ref[0, :, _col(c)] += dacc[c]
        lam = g_sc[...]
        y_sc[...] = _mm(lam, bbt_ref[0]) + d_ref[...] * dyy
        _unpermute_rows(y_sc, du_ref, seg)
        dbb_ref[0] += _mm_tn(uu, lam)
        dc_ref[0] += _mm_tn(x_sc[...], dyy)
        dd_ref[0] += _stack_rows([jnp.sum(dyy * uu, axis=0, keepdims=True)])

    rev = lambda k, c: (nc - 1 - c, k)
    in_specs = [pl.BlockSpec((t_chunk, LANES), rev), pl.BlockSpec((t_chunk, LANES), rev),
                pl.BlockSpec((1, 1, 8, 2 * SL), lambda k, c: (k, nc - 1 - c, 0, 0)),
                pl.BlockSpec((1, LANES, 2 * SL), lambda k, c: (k, 0, 0)),
                pl.BlockSpec((1, 2 * SL, LANES), lambda k, c: (k, 0, 0)),
                pl.BlockSpec((1, 8 * seg, 2 * SL), lambda k, c: (k, 0, 0)),
                pl.BlockSpec((1, LANES, 2 * SL), lambda k, c: (k, 0, 0)),
                pl.BlockSpec((1, LANES), lambda k, c: (0, k))]
    out_specs = [pl.BlockSpec((t_chunk, LANES), rev),
                 pl.BlockSpec((1, LANES, 2 * SL), lambda k, c: (k, 0, 0)),
                 pl.BlockSpec((1, 2 * SL, LANES), lambda k, c: (k, 0, 0)),
                 pl.BlockSpec((1, 8, 2 * SL), lambda k, c: (k, 0, 0)),
                 pl.BlockSpec((1, 8, LANES), lambda k, c: (k, 0, 0))]
    out_shape = [jax.ShapeDtypeStruct((s_len, D), F32), jax.ShapeDtypeStruct((GB, LANES, 2 * SL), F32),
                 jax.ShapeDtypeStruct((GB, 2 * SL, LANES), F32), jax.ShapeDtypeStruct((GB, 8, 2 * SL), F32),
                 jax.ShapeDtypeStruct((GB, 8, LANES), F32)]
    return pl.pallas_call(body, grid=(GB, nc), in_specs=in_specs, out_specs=out_specs, out_shape=out_shape,
                          scratch_shapes=[pltpu.VMEM((t_chunk, 2 * SL), F32)] * 3 + [pltpu.VMEM((t_chunk, LANES), F32),
                                                                                        pltpu.VMEM((8, 2 * SL), F32)],
                          compiler_params=_cparams(2), name="s5_scan_bwd")(u, dy, xin, bb, bb_t, apow, cblk_t, dskip)


_GELU_K = math.sqrt(2.0 / math.pi)


def _gelu(y):
    t = jnp.tanh(_GELU_K * (y + 0.044715 * (y * y * y)))
    return 0.5 * y * (1.0 + t), 0.5 * (1.0 + t) + 0.5 * y * (1.0 - t * t) * (_GELU_K * (1.0 + 3 * 0.044715 * (y * y)))


def _s5_in_fwd(h, gmix, win, tm):
    def fn(h, gmix, win):
        return (_mm(_rms_fwd(h, gmix)[0], win),), ()

    return _rowcall("s5_in_fwd", fn, tm, [h], [gmix, win], [(D, F32)], [])[0][0]


def _s5_in_bwd(h, du, dh, gmix, win_t, tm):
    def fn(h, du, dh, gmix, win_t):
        hn, xh, r = _rms_fwd(h, gmix)
        dx, dg = _rms_bwd(_mm(du, win_t), xh, r, gmix)
        return (dh + dx,), (_mm_tn(hn, du), dg)

    return _rowcall("s5_in_bwd", fn, tm, [h, du, dh], [gmix, win_t], [(D, F32)], [(D, D), (1, D)])


def _s5_out_fwd(h, y, wglu, tm):
    def fn(h, y, wglu):
        z = _mm(_gelu(y)[0], wglu)
        return (h + z[:, :D] * jax.nn.sigmoid(z[:, D:]),), ()

    return _rowcall("s5_out_fwd", fn, tm, [h, y], [wglu], [(D, F32)], [])[0][0]


def _s5_out_bwd(dh, y, wglu, wglu_t, tm):
    def fn(dh, y, wglu, wglu_t):
        yg, dgelu = _gelu(y)
        z = _mm(yg, wglu)
        val, sg = z[:, :D], jax.nn.sigmoid(z[:, D:])
        dz = jnp.concatenate([dh * sg, dh * val * sg * (1.0 - sg)], axis=1)
        return (_mm(dz, wglu_t) * dgelu,), (_mm_tn(yg, dz),)

    return _rowcall("s5_out_bwd", fn, tm, [dh, y], [wglu, wglu_t], [(D, F32)], [(D, 2 * D)])


def _final_loss(h, tgt, gfin, tm):
    def fn(h, tgt, gfin):
        y, xh, r = _rms_fwd(h, gfin)
        err = y - tgt
        dx, dg = _rms_bwd(err * (1.0 / D), xh, r, gfin)
        return (dx,), (jnp.sum(err * err, axis=0, keepdims=True), dg)

    return _rowcall("final_loss", fn, tm, [h, tgt], [gfin], [(D, F32)], [(1, D), (1, D)])


def _bf(a):
    return a.astype(MXU)


def _s5_block_mats(b_re, b_im, c_re, c_im):
    gl = G // GB
    eye = jnp.eye(gl, dtype=F32)

    def b_blk(b):
        bt = b.reshape(GB, gl, P, C).transpose(0, 1, 3, 2)
        return (bt[:, :, :, None, :] * eye[None, :, None, :, None]).reshape(GB, gl * C, gl * P)

    def c_blk(cm):
        ct = cm.reshape(GB, gl, C, P).transpose(0, 1, 3, 2)
        return (ct[:, :, :, None, :] * eye[None, :, None, :, None]).reshape(GB, gl * P, gl * C)

    braw = jnp.concatenate([b_blk(b_re), b_blk(b_im)], axis=2)
    cblk = jnp.concatenate([c_blk(c_re), -c_blk(c_im)], axis=1)
    return braw, cblk


def _s5_unblock_b(d):
    gl = G // GB
    eye = jnp.eye(gl, dtype=F32)
    d5 = d.reshape(GB, gl, C, gl, P)
    return jnp.sum(d5 * eye[None, :, None, :, None], axis=3).transpose(0, 1, 3, 2).reshape(G, P, C)


def _s5_unblock_c(d):
    gl = G // GB
    eye = jnp.eye(gl, dtype=F32)
    d5 = d.reshape(GB, gl, P, gl, C)
    return jnp.sum(d5 * eye[None, :, None, :, None], axis=3).transpose(0, 1, 3, 2).reshape(G, C, P)


def _tiles(s_len):
    return (min(256, s_len), min(512, s_len), min(1024, s_len),
            (min(512, s_len), min(2048, s_len)), (min(1024, s_len), min(1024, s_len)),
            min(256, s_len), min(128, s_len))


def _sequence_step(x, pos, tgt, w):
    s_len = x.shape[0]
    tm_w, tm, t_chunk, t_fwd, t_bwd, tm_ff, tm_fb = _tiles(s_len)
    seg = t_chunk // 8
    row = lambda v: v.reshape(1, -1)

    wa = _bf(jnp.pad(w["mla_w_a"][0], ((0, 0), (0, AW - (QL + KVL + ROPE)))))
    wuq = _bf(jnp.pad(w["mla_w_uq"][0].reshape(QL, HEADS, NOPE + ROPE), ((0, 0), (0, 0), (0, HC - NOPE - ROPE))).reshape(QL, HEADS * HC))
    wukv = _bf(w["mla_w_ukv"][0])
    wo = _bf(w["mla_w_o"][0])
    inv = 1.0 / (ROPE_THETA ** (jnp.arange(0, ROPE, 2, dtype=F32) / ROPE))
    invf = jnp.concatenate([inv, inv, jnp.zeros((LANES - ROPE,), F32)]).reshape(1, LANES)
    gmix0, gmix1 = row(w["g_mix"][0]), row(w["g_mix"][1])
    gq, gkv = row(w["mla_g_q"][0]), row(w["mla_g_kv"][0])
    win, wglu = _bf(w["ssm_w_in"][0]), _bf(w["ssm_w_glu"][0])
    dskip = row(w["ssm_d"][0])
    lr = w["ssm_lambda_re"][0].reshape(GB, 1, SL)
    li = w["ssm_lambda_im"][0].reshape(GB, 1, SL)
    ldt = jnp.broadcast_to(w["ssm_log_dt"][0][:, None], (G, P)).reshape(GB, 1, SL)
    braw, cblk = _s5_block_mats(w["ssm_b_re"][0], w["ssm_b_im"][0], w["ssm_c_re"][0], w["ssm_c_im"][0])
    cblk = _bf(cblk)
    ffn = []
    for l in range(2):
        wup, wdown = _bf(w["ffn_w_up"][l]), _bf(w["ffn_w_down"][l])
        ffn.append(dict(g=row(w["g_ffn"][l]), wup=wup, wup_t=wup.T, wdown=wdown, wdown_t=wdown.T,
                        cw=w["ffn_conv_w"][l], cb=row(w["ffn_conv_b"][l])))

    qc, kc, v = _mla_front_fwd(x, pos, gmix0, wa, gq, gkv, wuq, wukv, invf, tm_w)
    o, lse = _flash_fwd(qc, kc, v, *t_fwd)
    h1 = _attn_out_fwd(x, o, wo, tm)
    f0 = ffn[0]
    h2, act0 = _ffn_fwd(h1, f0["g"], f0["wup"], f0["cw"], f0["cb"], f0["wdown"], tm_ff)
    bb, apow = _s5_prep(lr, li, ldt, braw, seg)
    u = _s5_in_fwd(h2, gmix1, win, tm)
    y, xin = _s5_scan_fwd(u, bb, apow, cblk, dskip, t_chunk)
    h3 = _s5_out_fwd(h2, y, wglu, tm)
    f1 = ffn[1]
    h4, act1 = _ffn_fwd(h3, f1["g"], f1["wup"], f1["cw"], f1["cb"], f1["wdown"], tm_ff)
    (dh4,), (sq, d_gfinal) = _final_loss(h4, tgt, row(w["g_final"]), tm)
    loss = 0.5 * jnp.sum(sq) / D

    grads = {}

    def ffn_back(hin, dout, act, f):
        din, dupv, dupg, hn, cacc, dg = _ffn_bwd(hin, dout, f["g"], f["wup"], f["cw"], f["cb"], f["wdown_t"], f["wup_t"], tm_fb)
        d_wup = jnp.concatenate([_matmul_tn(hn, dupv, DFF, min(512, s_len), "ffn_dwup_v"),
                                 _matmul_tn(hn, dupg, DFF, min(512, s_len), "ffn_dwup_g")], axis=1)
        d_wdown = _matmul_tn(act, dout, D, min(512, s_len), "ffn_dwdown")
        cflat = cacc.transpose(1, 0, 2).reshape(8, 2 * DFF)
        return din, d_wup, d_wdown, cflat[:3], cflat[3], dg[0]

    dh3, d_wup1, d_wdown1, d_cw1, d_cb1, d_gffn1 = ffn_back(h3, dh4, act1, f1)
    (dy,), (d_wglu,) = _s5_out_bwd(dh3, y, wglu, wglu.T, tm)
    du, d_bb, d_cblk, d_a, d_dsk = _s5_scan_bwd(u, dy, xin, bb, bb.transpose(0, 2, 1), apow, cblk.transpose(0, 2, 1), dskip, t_chunk)
    d_braw, d_lr, d_li, d_ldt = _s5_prep_bwd(lr, li, ldt, braw, d_bb, d_a)
    (dh2,), (d_win, d_gmix1) = _s5_in_bwd(h2, du, dh3, gmix1, win.T, tm)
    dh1, d_wup0, d_wdown0, d_cw0, d_cb0, d_gffn0 = ffn_back(h1, dh2, act0, f0)
    (do, delta), (d_wo,) = _attn_out_bwd(dh1, o, wo.T, tm)
    dqc, dkc, dv = _flash_bwd(qc, kc, v, do, lse, delta, *t_bwd)
    (dx,), (d_wa, d_wuq, d_wukv, d_gq, d_gkv, d_gmix0) = _mla_front_bwd(
        x, pos, dqc, dkc, dv, dh1, gmix0, wa, gq, gkv, wuq, wukv, invf, wa.T, wuq.T, wukv.T, tm_w)

    grads["mla_w_a"] = d_wa[None, :, :QL + KVL + ROPE]
    grads["mla_g_q"] = d_gq
    grads["mla_g_kv"] = d_gkv
    grads["mla_w_uq"] = d_wuq.reshape(QL, HEADS, HC)[:, :, :NOPE + ROPE].reshape(1, QL, HEADS * (NOPE + ROPE))
    grads["mla_w_ukv"] = d_wukv[None]
    grads["mla_w_o"] = d_wo[None]
    grads["ssm_w_in"] = d_win[None]
    grads["ssm_lambda_re"] = d_lr.reshape(1, G, P)
    grads["ssm_lambda_im"] = d_li.reshape(1, G, P)
    grads["ssm_log_dt"] = jnp.sum(d_ldt.reshape(G, P), axis=1)[None]
    grads["ssm_b_re"] = _s5_unblock_b(d_braw[:, :, :SL])[None]
    grads["ssm_b_im"] = _s5_unblock_b(d_braw[:, :, SL:])[None]
    grads["ssm_c_re"] = _s5_unblock_c(d_cblk[:, :SL, :])[None]
    grads["ssm_c_im"] = -_s5_unblock_c(d_cblk[:, SL:, :])[None]
    grads["ssm_d"] = jnp.sum(d_dsk, axis=1).reshape(1, D)
    grads["ssm_w_glu"] = d_wglu[None]
    grads["ffn_w_up"] = jnp.stack([d_wup0, d_wup1])
    grads["ffn_conv_w"] = jnp.stack([d_cw0, d_cw1])
    grads["ffn_conv_b"] = jnp.stack([d_cb0, d_cb1])
    grads["ffn_w_down"] = jnp.stack([d_wdown0, d_wdown1])
    grads["g_mix"] = jnp.concatenate([d_gmix0, d_gmix1], axis=0)
    grads["g_ffn"] = jnp.stack([d_gffn0, d_gffn1])
    grads["g_final"] = d_gfinal[0]
    return loss, dx, grads


MESH = pl.DeviceIdType.MESH
ANY = pl.BlockSpec(memory_space=pl.ANY)


def _gather_many(blocks, name):
    n = len(blocks)

    def body(*refs):
        ins, outs = refs[:n], refs[n:2 * n]
        send_sems, recv_sems, local_sems = refs[2 * n:]
        x, y, c = lax.axis_index("x"), lax.axis_index("y"), lax.axis_index("c")
        me, sibling = (x, y, c), (x, y, 1 - c)
        chips = [(1 - x, y), (x, 1 - y), (1 - x, 1 - y)]

        def copy(a, k, block, to, own=False):
            px, py, pc = block
            slot = outs[a].at[4 * px + 2 * py + pc]
            return pltpu.make_async_remote_copy(src_ref=ins[a] if own else slot, dst_ref=slot,
                                                send_sem=send_sems.at[7 * a + k], recv_sem=recv_sems.at[7 * a + k],
                                                device_id=to, device_id_type=MESH)

        mine = [pltpu.make_async_copy(ins[a], outs[a].at[4 * x + 2 * y + c], local_sems.at[a]) for a in range(n)]
        sent = []
        for a in range(n):
            mine[a].start()
            sent.append(copy(a, 0, me, sibling, own=True))
            sent += [copy(a, 1 + j, me, (*chip, c), own=True) for j, chip in enumerate(chips)]
        for cp in sent:
            cp.start()
        for j, chip in enumerate(chips):
            for a in range(n):
                copy(a, 1 + j, (*chip, c), me).wait_recv()
                passed = copy(a, 4 + j, (*chip, c), sibling)
                passed.start()
                sent.append(passed)
        for a in range(n):
            copy(a, 0, sibling, me).wait_recv()
            for j, chip in enumerate(chips):
                copy(a, 4 + j, (*chip, 1 - c), me).wait_recv()
        for cp in sent:
            cp.wait_send()
        for cp in mine:
            cp.wait()

    return pl.pallas_call(body, out_shape=[jax.ShapeDtypeStruct((NDEV,) + b.shape, b.dtype) for b in blocks],
                          in_specs=[ANY] * n, out_specs=[ANY] * n,
                          scratch_shapes=[pltpu.SemaphoreType.DMA((7 * n,)), pltpu.SemaphoreType.DMA((7 * n,)),
                                          pltpu.SemaphoreType.DMA((n,))],
                          name=name)(*blocks)


def _exchange_many(sends, name):
    n = len(sends)

    def body(*refs):
        ins, outs = refs[:n], refs[n:2 * n]
        send_sems, recv_sems, local_sems = refs[2 * n:]
        x, y, c = lax.axis_index("x"), lax.axis_index("y"), lax.axis_index("c")
        me = 4 * x + 2 * y + c
        local, copies = [], []
        for a in range(n):
            local.append(pltpu.make_async_copy(ins[a].at[me], outs[a].at[me], local_sems.at[a]))
            local[a].start()
            for m in range(1, NDEV):
                px = 1 - x if m & 4 else x
                py = 1 - y if m & 2 else y
                pc = 1 - c if m & 1 else c
                peer = 4 * px + 2 * py + pc
                sems = dict(send_sem=send_sems.at[7 * a + m - 1], recv_sem=recv_sems.at[7 * a + m - 1],
                            device_id=(px, py, pc), device_id_type=MESH)
                out = pltpu.make_async_remote_copy(src_ref=ins[a].at[peer], dst_ref=outs[a].at[me], **sems)
                out.start()
                arrival = pltpu.make_async_remote_copy(src_ref=ins[a].at[me], dst_ref=outs[a].at[peer], **sems)
                copies.append((out, arrival))
        for out, arrival in copies:
            arrival.wait_recv()
        for out, arrival in copies:
            out.wait_send()
        for cp in local:
            cp.wait()

    return pl.pallas_call(body, out_shape=[jax.ShapeDtypeStruct(s.shape, s.dtype) for s in sends],
                          in_specs=[ANY] * n, out_specs=[ANY] * n,
                          scratch_shapes=[pltpu.SemaphoreType.DMA((7 * n,)), pltpu.SemaphoreType.DMA((7 * n,)),
                                          pltpu.SemaphoreType.DMA((n,))],
                          name=name)(*sends)


def _adam_math(w, parts, m, v):
    g = parts[0].astype(F32)
    for k in range(1, NDEV):
        g = g + parts[k].astype(F32)
    m2 = ADAM_B1 * m + (1.0 - ADAM_B1) * g
    v2 = ADAM_B2 * v + (1.0 - ADAM_B2) * jnp.square(g)
    m_hat = m2 / (1.0 - ADAM_B1 ** ADAM_STEP)
    v_hat = v2 / (1.0 - ADAM_B2 ** ADAM_STEP)
    return g, -ADAM_LR * (m_hat / (jnp.sqrt(v_hat) + ADAM_EPS) + ADAM_WD * w), m2, v2


def _adamw_many(ws, parts, ms, vs, name):
    n = len(ws)

    def body(*refs):
        w_refs, p_refs, m_refs, v_refs, outs = refs[:n], refs[n:2 * n], refs[2 * n:3 * n], refs[3 * n:4 * n], refs[4 * n:]
        for a in range(n):
            res = _adam_math(w_refs[a][...], [p_refs[a][k] for k in range(NDEV)], m_refs[a][...], v_refs[a][...])
            for o, val in zip(outs[4 * a:4 * a + 4], res):
                o[...] = val

    out = pl.pallas_call(body, out_shape=[jax.ShapeDtypeStruct(w.shape, F32) for w in ws for _ in range(4)],
                         compiler_params=pltpu.CompilerParams(vmem_limit_bytes=VMEM_LIMIT), name=name)(*ws, *parts, *ms, *vs)
    return [out[4 * a:4 * a + 4] for a in range(n)]


def _adamw_rows(w, parts, m, v, tr, name):
    rows, cols = w.shape

    def body(w_ref, p_ref, m_ref, v_ref, g_ref, d_ref, m2_ref, v2_ref):
        res = _adam_math(w_ref[...], [p_ref[k] for k in range(NDEV)], m_ref[...], v_ref[...])
        for o, val in zip((g_ref, d_ref, m2_ref, v2_ref), res):
            o[...] = val

    flat = pl.BlockSpec((tr, cols), lambda i: (i, 0))
    return pl.pallas_call(body, grid=(rows // tr,),
                          in_specs=[flat, pl.BlockSpec((NDEV, tr, cols), lambda i: (0, i, 0)), flat, flat],
                          out_specs=[flat] * 4, out_shape=[jax.ShapeDtypeStruct((rows, cols), F32)] * 4,
                          compiler_params=_cparams(1), name=name)(w, parts, m, v)


SHARDED = (("mla_w_a", 1), ("mla_w_uq", 2), ("mla_w_ukv", 2), ("mla_w_o", 1), ("ssm_w_in", 1), ("ssm_d", 1),
           ("ssm_w_glu", 2), ("ffn_w_up", 2), ("ffn_conv_w", 2), ("ffn_w_down", 1))
WIRE_EXACT = ("ssm_d", "ffn_conv_w")
REPLICATED = ("mla_g_q", "mla_g_kv", "ssm_lambda_re", "ssm_lambda_im", "ssm_log_dt", "ssm_b_re", "ssm_b_im",
              "ssm_c_re", "ssm_c_im", "ffn_conv_b", "g_mix", "g_ffn", "g_final")
WEIGHTS = ("mla_w_a", "mla_g_q", "mla_g_kv", "mla_w_uq", "mla_w_ukv", "mla_w_o", "ssm_w_in", "ssm_lambda_re",
           "ssm_lambda_im", "ssm_log_dt", "ssm_b_re", "ssm_b_im", "ssm_c_re", "ssm_c_im", "ssm_d", "ssm_w_glu",
           "ffn_w_up", "ffn_conv_w", "ffn_conv_b", "ffn_w_down", "g_mix", "g_ffn", "g_final")


ADAM_ROW_TILED = (("ffn_w_up", 256), ("ffn_w_down", 176))


def _two_d(shape):
    if len(shape) == 1:
        return (1, shape[0])
    if len(shape) > 2 and shape[-1] < LANES:
        return (math.prod(shape[:-2]), shape[-2] * shape[-1])
    return (math.prod(shape[:-1]), shape[-1])


def kernel(x, positions, mla_w_a, mla_g_q, mla_g_kv, mla_w_uq, mla_w_ukv, mla_w_o, ssm_w_in, ssm_lambda_re, ssm_lambda_im, ssm_log_dt, ssm_b_re, ssm_b_im, ssm_c_re, ssm_c_im, ssm_d, ssm_w_glu, ffn_w_up, ffn_conv_w, ffn_conv_b, ffn_w_down, g_mix, g_ffn, g_final, loss_target, m_mla_w_a, m_mla_g_q, m_mla_g_kv, m_mla_w_uq, m_mla_w_ukv, m_mla_w_o, m_ssm_w_in, m_ssm_lambda_re, m_ssm_lambda_im, m_ssm_log_dt, m_ssm_b_re, m_ssm_b_im, m_ssm_c_re, m_ssm_c_im, m_ssm_d, m_ssm_w_glu, m_ffn_w_up, m_ffn_conv_w, m_ffn_conv_b, m_ffn_w_down, m_g_mix, m_g_ffn, m_g_final, v_mla_w_a, v_mla_g_q, v_mla_g_kv, v_mla_w_uq, v_mla_w_ukv, v_mla_w_o, v_ssm_w_in, v_ssm_lambda_re, v_ssm_lambda_im, v_ssm_log_dt, v_ssm_b_re, v_ssm_b_im, v_ssm_c_re, v_ssm_c_im, v_ssm_d, v_ssm_w_glu, v_ffn_w_up, v_ffn_conv_w, v_ffn_conv_b, v_ffn_w_down, v_g_mix, v_g_ffn, v_g_final):
    a = dict(locals())
    s_len = x.shape[1]
    sh_names = [n for n, _ in SHARDED]

    gathered = _gather_many([a[n] if n in WIRE_EXACT else a[n].astype(MXU) for n in sh_names], "gather_weights")
    w = {n: a[n] for n in REPLICATED}
    for (n, axis), g in zip(SHARDED, gathered):
        moved = jnp.moveaxis(g, 0, axis)
        w[n] = moved.reshape(moved.shape[:axis] + (-1,) + moved.shape[axis + 2:])

    loss, dx, grads = _sequence_step(x[0], positions.reshape(s_len, 1).astype(F32), loss_target[0], w)
    loss = lax.psum(loss, ("x", "y", "c"))

    sends = []
    for n, axis in SHARDED:
        full = grads[n]
        cut = full.reshape(full.shape[:axis] + (NDEV, -1) + full.shape[axis + 1:])
        sends.append(jnp.moveaxis(cut, axis, 0).astype(MXU))
    landed = _exchange_many(sends, "exchange_grads")
    rep_parts = _gather_many([grads[n] for n in REPLICATED], "gather_small_grads")

    def view(n, arr, lead=()):
        return arr.reshape(lead + _two_d(a[n].shape))

    out = {}

    def finish(n, res):
        for kind, val in zip(("grad_", "delta_", "new_m_", "new_v_"), res):
            out[kind + n] = val.reshape(a[n].shape)

    parts_of = dict(zip(sh_names, landed)) | dict(zip(REPLICATED, rep_parts))
    for n, tr in ADAM_ROW_TILED:
        finish(n, _adamw_rows(view(n, a[n]), view(n, parts_of[n], (NDEV,)), view(n, a["m_" + n]), view(n, a["v_" + n]),
                              tr, "adamw_" + n))
    tiled = [n for n, _ in ADAM_ROW_TILED]
    for names, tag in (([n for n in sh_names if n not in tiled], "adamw_sharded"), (list(REPLICATED), "adamw_replicated")):
        res = _adamw_many([view(n, a[n]) for n in names], [view(n, parts_of[n], (NDEV,)) for n in names],
                          [view(n, a["m_" + n]) for n in names], [view(n, a["v_" + n]) for n in names], tag)
        for n, r in zip(names, res):
            finish(n, r)
    return (loss, dx[None], *[out[kind + n] for kind in ("grad_", "delta_", "new_m_", "new_v_") for n in WEIGHTS])
```

```python
import functools
import math

import jax
import jax.numpy as jnp
from jax import lax
from jax.experimental import pallas as pl
from jax.experimental.pallas import tpu as pltpu

F32 = jnp.float32
MXU = jnp.bfloat16

D = 1024
HEADS = 8
NOPE = 128
ROPE = 64
VH = 128
QL = 384
KVL = 256
CHUNK = 64
ROPE_THETA = 10000.0
EPS = 1e-6
G, P, C = 64, 64, 16
DFF = 2816
ADAM_LR, ADAM_B1, ADAM_B2, ADAM_EPS, ADAM_WD, ADAM_STEP = 0.001, 0.9, 0.999, 1e-08, 0.01, 10

LANES = 128
AW = 768
HC = 256
GB = 8
SL = (G // GB) * P
NCOL = 2 * SL // LANES
TC = DFF
HALO = 16
NDEV = 8
VMEM_LIMIT = 56 * 1024 * 1024


def _mm(a, b):
    return jnp.dot(a.astype(MXU), b.astype(MXU), preferred_element_type=F32)


def _mm_tn(a, b):
    return lax.dot_general(a.astype(MXU), b.astype(MXU), (((0,), (0,)), ((), ())), preferred_element_type=F32)


def _mm_nt(a, b):
    return lax.dot_general(a.astype(MXU), b.astype(MXU), (((1,), (1,)), ((), ())), preferred_element_type=F32)


def _rms_fwd(x, g):
    r = lax.rsqrt(jnp.mean(x * x, axis=-1, keepdims=True) + EPS)
    xh = x * r
    return xh * g, xh, r


def _rms_bwd(dy, xh, r, g):
    dxh = dy * g
    dx = r * (dxh - xh * jnp.mean(dxh * xh, axis=-1, keepdims=True))
    return dx, jnp.sum(dy * xh, axis=0, keepdims=True)


def _rot_partner(b):
    lane = lax.broadcasted_iota(jnp.int32, b.shape, 1)
    return jnp.where(lane < ROPE // 2, -pltpu.roll(b, LANES - ROPE // 2, 1), pltpu.roll(b, ROPE // 2, 1))


def _rope_blk(b, cos2, sin2):
    return b * cos2 + _rot_partner(b) * sin2


def _unrope_blk(db, cos2, sin2):
    return db * cos2 - _rot_partner(db * sin2)


def _cparams(n_axes, vmem=VMEM_LIMIT):
    return pltpu.CompilerParams(dimension_semantics=("arbitrary",) * n_axes, vmem_limit_bytes=vmem)


def _rowcall(name, fn, tm, row_ins, consts, row_outs, acc_outs):
    n = row_ins[0].shape[0]
    n_in = len(row_ins) + len(consts)
    n_ro = len(row_outs)

    def body(*refs):
        ins, ro_refs, acc_refs = refs[:n_in], refs[n_in:n_in + n_ro], refs[n_in + n_ro:]
        ro, ao = fn(*[r[...] for r in ins])

        @pl.when(pl.program_id(0) == 0)
        def _():
            for r in acc_refs:
                r[...] = jnp.zeros(r.shape, r.dtype)

        for r, val in zip(ro_refs, ro):
            r[...] = val.astype(r.dtype)
        for r, val in zip(acc_refs, ao):
            r[...] += val

    in_specs = [pl.BlockSpec((tm, a.shape[1]), lambda i: (i, 0)) for a in row_ins]
    in_specs += [pl.BlockSpec(c.shape, lambda i, nd=c.ndim: (0,) * nd) for c in consts]
    out_specs = [pl.BlockSpec((tm, w), lambda i: (i, 0)) for w, _ in row_outs]
    out_specs += [pl.BlockSpec(s, lambda i, nd=len(s): (0,) * nd) for s in acc_outs]
    out_shape = [jax.ShapeDtypeStruct((n, w), dt) for w, dt in row_outs]
    out_shape += [jax.ShapeDtypeStruct(s, F32) for s in acc_outs]
    out = pl.pallas_call(body, grid=(n // tm,), in_specs=in_specs, out_specs=out_specs, out_shape=out_shape,
                         compiler_params=_cparams(1), name=name)(*row_ins, *consts)
    return list(out[:n_ro]), list(out[n_ro:])


def _mla_front_tile(x, pos, gmix, wa, gq, gkv, wuq, wukv, invf):
    hn, xh, r = _rms_fwd(x, gmix)
    a = _mm(hn, wa)
    cq, ckv, krb = a[:, :QL], a[:, QL:QL + KVL], a[:, QL + KVL:]
    cqn, cqh, rq = _rms_fwd(cq, gq)
    ckvn, ckvh, rkv = _rms_fwd(ckv, gkv)
    q = _mm(cqn, wuq)
    kv = _mm(ckvn, wukv)
    ang = pos * invf
    cos2, sin2 = jnp.cos(ang), jnp.sin(ang)
    krr = _rope_blk(krb, cos2, sin2)
    qp, kp, vp = [], [], []
    for h in range(HEADS):
        qp += [q[:, h * HC:h * HC + NOPE], _rope_blk(q[:, h * HC + NOPE:(h + 1) * HC], cos2, sin2)]
        kp += [kv[:, h * HC:h * HC + NOPE], krr]
        vp += [kv[:, h * HC + NOPE:(h + 1) * HC]]
    res = (hn, xh, r, cqn, cqh, rq, ckvn, ckvh, rkv, cos2, sin2)
    return jnp.concatenate(qp, axis=1), jnp.concatenate(kp, axis=1), jnp.concatenate(vp, axis=1), res


def _mla_front_fwd(x, pos, gmix, wa, gq, gkv, wuq, wukv, invf, tm):
    def fn(*args):
        qc, kc, v, _ = _mla_front_tile(*args)
        return (qc * Q_PRESCALE, kc, v), ()

    return _rowcall("mla_front_fwd", fn, tm, [x, pos], [gmix, wa, gq, gkv, wuq, wukv, invf],
                    [(HEADS * HC, MXU), (HEADS * HC, MXU), (HEADS * VH, MXU)], [])[0]


def _mla_front_bwd(x, pos, dqc, dkc, dv, dh, gmix, wa, gq, gkv, wuq, wukv, invf, wa_t, wuq_t, wukv_t, tm):
    def fn(x, pos, dqc, dkc, dv, dh, gmix, wa, gq, gkv, wuq, wukv, invf, wa_t, wuq_t, wukv_t):
        _, _, _, (hn, xh, r, cqn, cqh, rq, ckvn, ckvh, rkv, cos2, sin2) = _mla_front_tile(
            x, pos, gmix, wa, gq, gkv, wuq, wukv, invf)
        dqc, dkc, dv = dqc * SM_SCALE, dkc.astype(F32), dv.astype(F32)
        dqp, dkvp = [], []
        dkr = jnp.zeros((x.shape[0], LANES), F32)
        for h in range(HEADS):
            dqp += [dqc[:, h * HC:h * HC + NOPE], _unrope_blk(dqc[:, h * HC + NOPE:(h + 1) * HC], cos2, sin2)]
            dkvp += [dkc[:, h * HC:h * HC + NOPE], dv[:, h * VH:(h + 1) * VH]]
            dkr = dkr + dkc[:, h * HC + NOPE:(h + 1) * HC]
        dq = jnp.concatenate(dqp, axis=1)
        dkv = jnp.concatenate(dkvp, axis=1)
        dkrb = _unrope_blk(dkr, cos2, sin2)
        dcqn = _mm(dq, wuq_t)
        dckvn = _mm(dkv, wukv_t)
        d_wuq = _mm_tn(cqn, dq)
        d_wukv = _mm_tn(ckvn, dkv)
        dcq, d_gq = _rms_bwd(dcqn, cqh, rq, gq)
        dckv, d_gkv = _rms_bwd(dckvn, ckvh, rkv, gkv)
        da = jnp.concatenate([dcq, dckv, dkrb], axis=1)
        d_wa = _mm_tn(hn, da)
        dhn = _mm(da, wa_t)
        dx, d_gmix = _rms_bwd(dhn, xh, r, gmix)
        return (dh + dx,), (d_wa, d_wuq, d_wukv, d_gq, d_gkv, d_gmix)

    return _rowcall("mla_front_bwd", fn, tm, [x, pos, dqc, dkc, dv, dh],
                    [gmix, wa, gq, gkv, wuq, wukv, invf, wa_t, wuq_t, wukv_t], [(D, F32)],
                    [(D, AW), (QL, HEADS * HC), (KVL, HEADS * HC), (1, QL), (1, KVL), (1, D)])


SM_SCALE = (NOPE + ROPE) ** -0.5
LOG2E = 1.0 / math.log(2.0)
Q_PRESCALE = SM_SCALE * LOG2E


def _pair_tables(s_len, tq, tk, q_major):
    pairs = [(qi, ki) for qi in range(s_len // tq) for ki in range(s_len // tk) if ki * tk < (qi + 1) * tq]
    if not q_major:
        pairs.sort(key=lambda p: (p[1], p[0]))
    return (jnp.asarray([p[0] for p in pairs], jnp.int32), jnp.asarray([p[1] for p in pairs], jnp.int32))


def _last_key_tile(qi, tq, tk):
    return ((qi + 1) * tq - 1) // tk


def _visible(qi, ki, tq, tk):
    row = qi * (tq // CHUNK) + lax.broadcasted_iota(jnp.int32, (tq, tk), 0) // CHUNK
    col = ki * (tk // CHUNK) + lax.broadcasted_iota(jnp.int32, (tq, tk), 1) // CHUNK
    return col <= row


def _masked_and_not(qi, ki, tq, tk, fn):
    needs_mask = (ki + 1) * tk > qi * tq
    pl.when(needs_mask)(lambda: fn(True))
    pl.when(jnp.logical_not(needs_mask))(lambda: fn(False))


def _flash_fwd(qc, kc, v, tq, tk):
    s_len = qc.shape[0]
    qt, kt = _pair_tables(s_len, tq, tk, True)

    def body(qt_ref, kt_ref, q_ref, k_ref, v_ref, o_ref, lse_ref, m_sc, l_sc, acc_sc):
        p_id = pl.program_id(1)
        qi, ki = qt_ref[p_id], kt_ref[p_id]

        @pl.when(ki == 0)
        def _():
            m_sc[...] = jnp.full(m_sc.shape, -jnp.inf, F32)
            l_sc[...] = jnp.zeros(l_sc.shape, F32)
            acc_sc[...] = jnp.zeros(acc_sc.shape, F32)

        def update(masked):
            s = _mm_nt(q_ref[...], k_ref[...])
            if masked:
                s = jnp.where(_visible(qi, ki, tq, tk), s, -jnp.inf)
            m_old = m_sc[...]
            m_new = jnp.maximum(m_old, jnp.max(s, axis=1, keepdims=True))
            alpha = jnp.exp2(m_old - m_new)
            p = jnp.exp2(s - m_new)
            l_sc[...] = alpha * l_sc[...] + jnp.sum(p, axis=1, keepdims=True)
            acc_sc[...] = alpha * acc_sc[...] + _mm(p, v_ref[...])
            m_sc[...] = m_new

        _masked_and_not(qi, ki, tq, tk, update)

        @pl.when(ki == _last_key_tile(qi, tq, tk))
        def _():
            l = l_sc[...]
            o_ref[...] = (acc_sc[...] / l).astype(o_ref.dtype)
            lse_ref[...] = jnp.broadcast_to(m_sc[...] + jnp.log2(l), lse_ref.shape)

    qmap = lambda h, p, qt, kt: (qt[p], h)
    kmap = lambda h, p, qt, kt: (kt[p], h)
    grid_spec = pltpu.PrefetchScalarGridSpec(
        num_scalar_prefetch=2, grid=(HEADS, qt.shape[0]),
        in_specs=[pl.BlockSpec((tq, HC), qmap), pl.BlockSpec((tk, HC), kmap), pl.BlockSpec((tk, VH), kmap)],
        out_specs=[pl.BlockSpec((tq, VH), qmap), pl.BlockSpec((tq, LANES), qmap)],
        scratch_shapes=[pltpu.VMEM((tq, 1), F32), pltpu.VMEM((tq, 1), F32), pltpu.VMEM((tq, VH), F32)])
    return pl.pallas_call(body, grid_spec=grid_spec,
                          out_shape=[jax.ShapeDtypeStruct((s_len, HEADS * VH), MXU),
                                     jax.ShapeDtypeStruct((s_len, HEADS * LANES), F32)],
                          compiler_params=_cparams(2), name="flash_fwd")(qt, kt, qc, kc, v)


def _tile_dscores(q, k, v, do, lse, delta, qi, ki, tq, tk, masked):
    p = jnp.exp2(_mm_nt(q, k) - lse[:, :1])
    if masked:
        p = jnp.where(_visible(qi, ki, tq, tk), p, 0.0)
    return p, p * (_mm_nt(do, v) - delta[:, :1])


def _flash_bwd(qc, kc, v, do, lse, delta, tq, tk):
    s_len = qc.shape[0]
    qt, kt = _pair_tables(s_len, tq, tk, False)
    nq = s_len // tq

    def body(qt_ref, kt_ref, q_ref, k_ref, v_ref, do_ref, lse_ref, dl_ref, dq_ref, dk_ref, dv_ref, dk_sc, dv_sc):
        p_id = pl.program_id(1)
        qi, ki = qt_ref[p_id], kt_ref[p_id]
        rows = pl.ds(pl.multiple_of(qi * tq, tq), tq)

        @pl.when(qi == (ki * tk) // tq)
        def _():
            dk_sc[...] = jnp.zeros(dk_sc.shape, F32)
            dv_sc[...] = jnp.zeros(dv_sc.shape, F32)

        @pl.when(ki == 0)
        def _():
            dq_ref[rows, :] = jnp.zeros((tq, HC), F32)

        def update(masked):
            q, k, do = q_ref[...], k_ref[...], do_ref[...]
            p, ds = _tile_dscores(q, k, v_ref[...], do, lse_ref[...], dl_ref[...], qi, ki, tq, tk, masked)
            ds = ds.astype(MXU)
            dv_sc[...] += _mm_tn(p, do)
            dk_sc[...] += _mm_tn(ds, q)
            dq_ref[rows, :] += _mm(ds, k)

        _masked_and_not(qi, ki, tq, tk, update)

        @pl.when(qi == nq - 1)
        def _():
            dk_ref[...] = (dk_sc[...] * (1.0 / LOG2E)).astype(dk_ref.dtype)
            dv_ref[...] = dv_sc[...].astype(dv_ref.dtype)

    qmap = lambda h, p, qt, kt: (qt[p], h)
    kmap = lambda h, p, qt, kt: (kt[p], h)
    grid_spec = pltpu.PrefetchScalarGridSpec(
        num_scalar_prefetch=2, grid=(HEADS, qt.shape[0]),
        in_specs=[pl.BlockSpec((tq, HC), qmap), pl.BlockSpec((tk, HC), kmap), pl.BlockSpec((tk, VH), kmap),
                  pl.BlockSpec((tq, VH), qmap), pl.BlockSpec((tq, LANES), qmap), pl.BlockSpec((tq, LANES), qmap)],
        out_specs=[pl.BlockSpec((s_len, HC), lambda h, p, qt, kt: (0, h), pipeline_mode=pl.Buffered(1)),
                   pl.BlockSpec((tk, HC), kmap), pl.BlockSpec((tk, VH), kmap)],
        scratch_shapes=[pltpu.VMEM((tk, HC), F32), pltpu.VMEM((tk, VH), F32)])
    return pl.pallas_call(body, grid_spec=grid_spec,
                          out_shape=[jax.ShapeDtypeStruct((s_len, HEADS * HC), F32),
                                     jax.ShapeDtypeStruct((s_len, HEADS * HC), MXU),
                                     jax.ShapeDtypeStruct((s_len, HEADS * VH), MXU)],
                          compiler_params=_cparams(2), name="flash_bwd")(qt, kt, qc, kc, v, do, lse, delta)


def _attn_out_fwd(x, o, wo, tm):
    def fn(x, o, wo):
        return (x + _mm(o, wo),), ()

    return _rowcall("attn_out_fwd", fn, tm, [x, o], [wo], [(D, F32)], [])[0][0]


def _attn_out_bwd(dh, o, wo_t, tm):
    def fn(dh, o, wo_t):
        do = _mm(dh, wo_t)
        of = o.astype(F32)
        dl = [jnp.broadcast_to(jnp.sum(do[:, h * VH:(h + 1) * VH] * of[:, h * VH:(h + 1) * VH], axis=1, keepdims=True),
                               (dh.shape[0], LANES)) for h in range(HEADS)]
        return (do, jnp.concatenate(dl, axis=1)), (_mm_tn(o, dh),)

    return _rowcall("attn_out_bwd", fn, tm, [dh, o], [wo_t], [(HEADS * VH, MXU), (HEADS * LANES, F32)], [(HEADS * VH, D)])


def _shift_rows(a, k):
    return a if k == 0 else pltpu.roll(a, k % a.shape[0], 0)


def _stack_rows(rows):
    idx = lax.broadcasted_iota(jnp.int32, (8, rows[0].shape[1]), 0)
    out = jnp.zeros((8, rows[0].shape[1]), F32)
    for k, r in enumerate(rows):
        out = jnp.where(idx == k, r, out)
    return out


def _ffn_fwd(h, g, wup, cw, cb, wdown, tm):
    s_len = h.shape[0]
    nj = DFF // TC
    hb = tm // HALO

    def body(h_ref, hp_ref, g_ref, wv_ref, wg_ref, cwv_ref, cwg_ref, cbv_ref, cbg_ref, wd_ref, out_ref, act_ref,
             upv_ref, upg_ref, hn_sc, acc_sc):
        i, j = pl.program_id(0), pl.program_id(1)

        @pl.when(j == 0)
        def _():
            gg = g_ref[...]
            hp = _rms_fwd(hp_ref[...], gg)[0]
            hn_sc[:HALO, :] = jnp.where(i > 0, hp, 0.0).astype(MXU)
            hn_sc[HALO:, :] = _rms_fwd(h_ref[...], gg)[0].astype(MXU)
            acc_sc[...] = jnp.zeros(acc_sc.shape, F32)

        hn = hn_sc[...]

        def conv(w_ref, cw_ref, cb_ref, up_ref):
            up = jnp.dot(hn, w_ref[...], preferred_element_type=F32)
            up_ref[...] = up[HALO:].astype(up_ref.dtype)
            cwv = cw_ref[...]
            c = cwv[2:3] * up + cwv[1:2] * _shift_rows(up, 1) + cwv[0:1] * _shift_rows(up, 2)
            return c[HALO:] + cb_ref[...]

        cv = conv(wv_ref, cwv_ref, cbv_ref, upv_ref)
        cg = conv(wg_ref, cwg_ref, cbg_ref, upg_ref)
        act = cg * jax.nn.sigmoid(cg) * cv
        act_ref[...] = act.astype(act_ref.dtype)
        acc_sc[...] += _mm(act, wd_ref[...])

        @pl.when(j == nj - 1)
        def _():
            out_ref[...] = h_ref[...] + acc_sc[...]

    in_specs = [pl.BlockSpec((tm, D), lambda i, j: (i, 0)),
                pl.BlockSpec((HALO, D), lambda i, j: (jnp.maximum(i * hb - 1, 0), 0)),
                pl.BlockSpec((1, D), lambda i, j: (0, 0)),
                pl.BlockSpec((D, TC), lambda i, j: (0, j)), pl.BlockSpec((D, TC), lambda i, j: (0, j + nj)),
                pl.BlockSpec((3, TC), lambda i, j: (0, j)), pl.BlockSpec((3, TC), lambda i, j: (0, j + nj)),
                pl.BlockSpec((1, TC), lambda i, j: (0, j)), pl.BlockSpec((1, TC), lambda i, j: (0, j + nj)),
                pl.BlockSpec((TC, D), lambda i, j: (j, 0))]
    out_specs = [pl.BlockSpec((tm, D), lambda i, j: (i, 0))] + [pl.BlockSpec((tm, TC), lambda i, j: (i, j))] * 3
    return pl.pallas_call(body, grid=(s_len // tm, nj), in_specs=in_specs, out_specs=out_specs,
                          out_shape=[jax.ShapeDtypeStruct((s_len, D), F32)] + [jax.ShapeDtypeStruct((s_len, DFF), MXU)] * 3,
                          scratch_shapes=[pltpu.VMEM((tm + HALO, D), MXU), pltpu.VMEM((tm, D), F32)],
                          compiler_params=_cparams(2), name="ffn_fwd")(h, h, g, wup, wup, cw, cw, cb, cb, wdown)


def _ffn_bwd(h, dout, upv, upg, g, cw, cb, wdown_t, wup_t, tm):
    s_len = h.shape[0]
    ni = s_len // tm
    hb = tm // HALO

    def body(h_ref, d_ref, dx_ref, uvp_ref, uv_ref, uvx_ref, ugp_ref, ug_ref, ugx_ref, g_ref, cwv_ref, cwg_ref, cbv_ref, cbg_ref,
             wdt_ref, wutv_ref, wutg_ref, din_ref, dupv_ref, dupg_ref, hn_ref, cacc_ref, dg_ref):
        i = pl.program_id(0)

        @pl.when(i == 0)
        def _():
            cacc_ref[...] = jnp.zeros(cacc_ref.shape, F32)
            dg_ref[...] = jnp.zeros(dg_ref.shape, F32)

        gg = g_ref[...]
        hn, xh, r = _rms_fwd(h_ref[...], gg)
        hn_ref[...] = hn.astype(hn_ref.dtype)
        dd = jnp.concatenate([d_ref[...], jnp.where(i < ni - 1, dx_ref[...], 0.0)], axis=0).astype(MXU)
        dact = jnp.dot(dd, wdt_ref[...], preferred_element_type=F32)

        def half(p_ref, t_ref, x_ref, cw_ref, cb_ref):
            prev = p_ref[...]
            up = jnp.concatenate([jnp.where(i > 0, prev, jnp.zeros_like(prev)), t_ref[...], x_ref[...]], axis=0).astype(F32)
            cwv = cw_ref[...]
            u1, u2 = _shift_rows(up, 1), _shift_rows(up, 2)
            c = (cwv[2:3] * up + cwv[1:2] * u1 + cwv[0:1] * u2)[HALO:] + cb_ref[...]
            return c, (up[HALO:HALO + tm], u1[HALO:HALO + tm], u2[HALO:HALO + tm]), cwv

        cv, upsv, cwv = half(uvp_ref, uv_ref, uvx_ref, cwv_ref, cbv_ref)
        cg, upsg, cwg = half(ugp_ref, ug_ref, ugx_ref, cwg_ref, cbg_ref)
        sg = jax.nn.sigmoid(cg)
        dcv = dact * (cg * sg)
        dcg = dact * cv * (sg * (1.0 + cg * (1.0 - sg)))

        def back(dc, ups, cwx, slot, dup_ref, wut_ref):
            dup = (cwx[2:3] * dc + cwx[1:2] * _shift_rows(dc, -1) + cwx[0:1] * _shift_rows(dc, -2))[:tm]
            dct = dc[:tm]
            cacc_ref[slot] += _stack_rows([jnp.sum(dct * ups[2], axis=0, keepdims=True),
                                           jnp.sum(dct * ups[1], axis=0, keepdims=True),
                                           jnp.sum(dct * ups[0], axis=0, keepdims=True),
                                           jnp.sum(dct, axis=0, keepdims=True)])
            dup_ref[...] = dup.astype(dup_ref.dtype)
            return _mm(dup, wut_ref[...])

        dhn = back(dcv, upsv, cwv, 0, dupv_ref, wutv_ref) + back(dcg, upsg, cwg, 1, dupg_ref, wutg_ref)
        dx, dgp = _rms_bwd(dhn, xh, r, gg)
        din_ref[...] = d_ref[...] + dx
        dg_ref[...] += dgp

    last_blk = s_len // HALO - 1
    tile = lambda w: pl.BlockSpec((tm, w), lambda i: (i, 0))
    prev = lambda w: pl.BlockSpec((HALO, w), lambda i: (jnp.maximum(i * hb - 1, 0), 0))
    nxt = lambda w: pl.BlockSpec((HALO, w), lambda i: (jnp.minimum((i + 1) * hb, last_blk), 0))
    const = lambda s, j=0: pl.BlockSpec(s, lambda i: (0, j))
    in_specs = [tile(D), tile(D), nxt(D), prev(DFF), tile(DFF), nxt(DFF), prev(DFF), tile(DFF), nxt(DFF),
                const((1, D)), const((3, DFF)), const((3, DFF), 1), const((1, DFF)), const((1, DFF), 1),
                const((D, DFF)), pl.BlockSpec((DFF, D), lambda i: (0, 0)), pl.BlockSpec((DFF, D), lambda i: (1, 0))]
    out_specs = [tile(D), tile(DFF), tile(DFF), tile(D), pl.BlockSpec((2, 8, DFF), lambda i: (0, 0, 0)), const((1, D))]
    out_shape = [jax.ShapeDtypeStruct((s_len, D), F32), jax.ShapeDtypeStruct((s_len, DFF), MXU),
                 jax.ShapeDtypeStruct((s_len, DFF), MXU), jax.ShapeDtypeStruct((s_len, D), MXU),
                 jax.ShapeDtypeStruct((2, 8, DFF), F32), jax.ShapeDtypeStruct((1, D), F32)]
    return pl.pallas_call(body, grid=(ni,), in_specs=in_specs, out_specs=out_specs, out_shape=out_shape,
                          compiler_params=_cparams(1), name="ffn_bwd")(
        h, dout, dout, upv, upv, upv, upg, upg, upg, g, cw, cw, cb, cb, wdown_t, wup_t, wup_t)


def _matmul_tn(a, b, tn, ts, name):
    s_len, m = a.shape
    n = b.shape[1]

    def body(a_ref, b_ref, o_ref):
        @pl.when(pl.program_id(1) == 0)
        def _():
            o_ref[...] = jnp.zeros(o_ref.shape, F32)

        o_ref[...] += _mm_tn(a_ref[...], b_ref[...])

    return pl.pallas_call(body, grid=(n // tn, s_len // ts),
                          in_specs=[pl.BlockSpec((ts, m), lambda jn, k: (k, 0)), pl.BlockSpec((ts, tn), lambda jn, k: (k, jn))],
                          out_specs=pl.BlockSpec((m, tn), lambda jn, k: (0, jn)),
                          out_shape=jax.ShapeDtypeStruct((m, n), F32), compiler_params=_cparams(2), name=name)(a, b)


def _s5_coefs(lr, li, ldt):
    dt = jnp.exp(ldt)
    mag = jnp.exp(lr * dt)
    th = li * dt
    ar, ai = mag * jnp.cos(th), mag * jnp.sin(th)
    den = lr * lr + li * li
    nr = ar - 1.0
    cr = (nr * lr + ai * li) / den
    ci = (ai * lr - nr * li) / den
    return dt, mag, th, ar, ai, den, nr, cr, ci


def _s5_prep(lr, li, ldt, braw, seg):
    def body(lr_ref, li_ref, ldt_ref, b_ref, bb_ref, ap_ref):
        lr_, li_, ldt_ = lr_ref[0], li_ref[0], ldt_ref[0]
        dt, mag, th, ar, ai, den, nr, cr, ci = _s5_coefs(lr_, li_, ldt_)
        br, bi = b_ref[0, :, :SL], b_ref[0, :, SL:]
        bb_ref[0, :, :SL] = (cr * br - ci * bi).astype(bb_ref.dtype)
        bb_ref[0, :, SL:] = (cr * bi + ci * br).astype(bb_ref.dtype)
        for i in range(seg):
            m = jnp.exp((i + 1.0) * (lr_ * dt))
            ap_ref[0, i * 8:(i + 1) * 8, :SL] = jnp.broadcast_to(m * jnp.cos((i + 1.0) * th), (8, SL))
            ap_ref[0, i * 8:(i + 1) * 8, SL:] = jnp.broadcast_to(m * jnp.sin((i + 1.0) * th), (8, SL))

    vec = pl.BlockSpec((1, 1, SL), lambda k: (k, 0, 0))
    return pl.pallas_call(
        body, grid=(GB,), in_specs=[vec, vec, vec, pl.BlockSpec((1, LANES, 2 * SL), lambda k: (k, 0, 0))],
        out_specs=[pl.BlockSpec((1, LANES, 2 * SL), lambda k: (k, 0, 0)),
                   pl.BlockSpec((1, 8 * seg, 2 * SL), lambda k: (k, 0, 0))],
        out_shape=[jax.ShapeDtypeStruct((GB, LANES, 2 * SL), MXU), jax.ShapeDtypeStruct((GB, 8 * seg, 2 * SL), F32)],
        compiler_params=_cparams(1), name="s5_prep")(lr, li, ldt, braw)


def _s5_prep_bwd(lr, li, ldt, braw, dbb, da):
    def body(lr_ref, li_ref, ldt_ref, b_ref, dbb_ref, da_ref, dbraw_ref, dlr_ref, dli_ref, dldt_ref):
        lr_, li_, ldt_ = lr_ref[0], li_ref[0], ldt_ref[0]
        dt, mag, th, ar, ai, den, nr, cr, ci = _s5_coefs(lr_, li_, ldt_)
        br, bi = b_ref[0, :, :SL], b_ref[0, :, SL:]
        gbr, gbi = dbb_ref[0, :, :SL], dbb_ref[0, :, SL:]
        dbraw_ref[0, :, :SL] = cr * gbr + ci * gbi
        dbraw_ref[0, :, SL:] = cr * gbi - ci * gbr
        dcr = jnp.sum(gbr * br + gbi * bi, axis=0, keepdims=True)
        dci = jnp.sum(gbi * br - gbr * bi, axis=0, keepdims=True)
        dar = jnp.sum(da_ref[0, :, :SL], axis=0, keepdims=True)
        dai = jnp.sum(da_ref[0, :, SL:], axis=0, keepdims=True)
        g1, g2 = dcr / den, dci / den
        gden = -(dcr * cr + dci * ci) / den
        gar = dar + g1 * lr_ - g2 * li_
        gai = dai + g1 * li_ + g2 * lr_
        glr = g1 * nr + g2 * ai + 2.0 * lr_ * gden
        gli = g1 * ai - g2 * nr + 2.0 * li_ * gden
        gmag = gar * jnp.cos(th) + gai * jnp.sin(th)
        gth = gai * ar - gar * ai
        dlr_ref[0] = glr + gmag * mag * dt
        dli_ref[0] = gli + gth * dt
        dldt_ref[0] = (gmag * mag * lr_ + gth * li_) * dt

    vec = pl.BlockSpec((1, 1, SL), lambda k: (k, 0, 0))
    mat = pl.BlockSpec((1, LANES, 2 * SL), lambda k: (k, 0, 0))
    return pl.pallas_call(
        body, grid=(GB,), in_specs=[vec, vec, vec, mat, mat, pl.BlockSpec((1, 8, 2 * SL), lambda k: (k, 0, 0))],
        out_specs=[mat, vec, vec, vec],
        out_shape=[jax.ShapeDtypeStruct((GB, LANES, 2 * SL), F32)] + [jax.ShapeDtypeStruct((GB, 1, SL), F32)] * 3,
        compiler_params=_cparams(1), name="s5_prep_bwd")(lr, li, ldt, braw, dbb, da)


def _bcast_row(tile, j):
    return jnp.broadcast_to(tile[j:j + 1, :], tile.shape)


def _tile_rows(i):
    return pl.ds(pl.multiple_of(i * 8, 8), 8)


def _col(c):
    return slice(c * LANES, (c + 1) * LANES)


def _permute_rows(ref, seg):
    return jnp.concatenate([ref[pl.ds(i, 8, stride=seg), :] for i in range(seg)], axis=0)


def _unpermute_rows(src, dst, seg):
    for j in range(8):
        dst[j * seg:(j + 1) * seg, :] = src[pl.ds(j, seg, stride=8), :]


SCAN_UNROLL = 2


def _segment_scan(src, dst, ap, seg, reverse):
    half = NCOL // 2
    sign = -1.0 if reverse else 1.0
    a_r = [ap[0:8, _col(c)] for c in range(half)]
    a_i = [ap[0:8, _col(half + c)] for c in range(half)]

    def step(n, carry):
        i = seg - 1 - n if reverse else n
        rows = _tile_rows(i)
        out_r, out_i = [], []
        for c in range(half):
            xr, xi = carry[c], carry[half + c]
            nr = a_r[c] * xr - sign * a_i[c] * xi + src[rows, _col(c)]
            ni = a_r[c] * xi + sign * a_i[c] * xr + src[rows, _col(half + c)]
            dst[rows, _col(c)] = nr
            dst[rows, _col(half + c)] = ni
            out_r.append(nr)
            out_i.append(ni)
        return tuple(out_r + out_i)

    zero = jnp.zeros((8, LANES), F32)
    return lax.fori_loop(0, seg, step, (zero,) * NCOL, unroll=SCAN_UNROLL)


def _segment_entries(ends, cin, ap, seg, reverse):
    half = NCOL // 2
    sign = -1.0 if reverse else 1.0
    al_r = [ap[(seg - 1) * 8:seg * 8, _col(c)] for c in range(half)]
    al_i = [ap[(seg - 1) * 8:seg * 8, _col(half + c)] for c in range(half)]
    row = lax.broadcasted_iota(jnp.int32, (8, LANES), 0)
    ent, out = [None] * NCOL, [None] * NCOL
    for c in range(half):
        zr, zi = cin[c], cin[half + c]
        er, ei = jnp.zeros((8, LANES), F32), jnp.zeros((8, LANES), F32)
        for j in (range(7, -1, -1) if reverse else range(8)):
            er, ei = jnp.where(row == j, zr, er), jnp.where(row == j, zi, ei)
            fr, fi = _bcast_row(ends[c], j), _bcast_row(ends[half + c], j)
            zr, zi = (al_r[c] * zr - sign * al_i[c] * zi + fr, al_r[c] * zi + sign * al_i[c] * zr + fi)
        ent[c], ent[half + c] = er, ei
        out[c], out[half + c] = zr, zi
    return ent, out


def _chunk_states(u, bb_ref, ap, cin, bu_sc, x_sc, seg):
    half = NCOL // 2
    bu_sc[...] = _mm(u, bb_ref[0])
    ends = _segment_scan(bu_sc, x_sc, ap, seg, False)
    ent, out = _segment_entries(ends, cin, ap, seg, False)

    def fix(i, _):
        rows = _tile_rows(i)
        for c in range(half):
            pr, pi = ap[rows, _col(c)], ap[rows, _col(half + c)]
            x_sc[rows, _col(c)] += pr * ent[c] - pi * ent[half + c]
            x_sc[rows, _col(half + c)] += pr * ent[half + c] + pi * ent[c]
        return 0

    lax.fori_loop(0, seg, fix, 0, unroll=SCAN_UNROLL)
    return out


def _s5_scan_fwd(u, bb, apow, cblk, dskip, t_chunk):
    s_len = u.shape[0]
    nc = s_len // t_chunk
    seg = t_chunk // 8

    def body(u_ref, bb_ref, ap_ref, c_ref, d_ref, y_ref, xin_ref, bu_sc, x_sc, y_sc, carry_sc):
        @pl.when(pl.program_id(1) == 0)
        def _():
            carry_sc[...] = jnp.zeros(carry_sc.shape, F32)

        uu = _permute_rows(u_ref, seg)
        cin = [carry_sc[:, _col(c)] for c in range(NCOL)]
        xin_ref[0, 0] = carry_sc[...]
        out = _chunk_states(uu, bb_ref, ap_ref.at[0], cin, bu_sc, x_sc, seg)
        for c in range(NCOL):
            carry_sc[:, _col(c)] = out[c]
        y_sc[...] = _mm(x_sc[...], c_ref[0]) + d_ref[...] * uu
        _unpermute_rows(y_sc, y_ref, seg)

    in_specs = [pl.BlockSpec((t_chunk, LANES), lambda k, c: (c, k)),
                pl.BlockSpec((1, LANES, 2 * SL), lambda k, c: (k, 0, 0)),
                pl.BlockSpec((1, 8 * seg, 2 * SL), lambda k, c: (k, 0, 0)),
                pl.BlockSpec((1, 2 * SL, LANES), lambda k, c: (k, 0, 0)),
                pl.BlockSpec((1, LANES), lambda k, c: (0, k))]
    out_specs = [pl.BlockSpec((t_chunk, LANES), lambda k, c: (c, k)),
                 pl.BlockSpec((1, 1, 8, 2 * SL), lambda k, c: (k, c, 0, 0))]
    return pl.pallas_call(body, grid=(GB, nc), in_specs=in_specs, out_specs=out_specs,
                          out_shape=[jax.ShapeDtypeStruct((s_len, D), F32), jax.ShapeDtypeStruct((GB, nc, 8, 2 * SL), F32)],
                          scratch_shapes=[pltpu.VMEM((t_chunk, 2 * SL), F32), pltpu.VMEM((t_chunk, 2 * SL), F32),
                                          pltpu.VMEM((t_chunk, LANES), F32), pltpu.VMEM((8, 2 * SL), F32)],
                          compiler_params=_cparams(2), name="s5_scan_fwd")(u, bb, apow, cblk, dskip)


def _s5_scan_bwd(u, dy, xin, bb, bb_t, apow, cblk_t, dskip, t_chunk):
    s_len = u.shape[0]
    nc = s_len // t_chunk
    seg = t_chunk // 8
    half = NCOL // 2

    def body(u_ref, dy_ref, xin_ref, bb_ref, bbt_ref, ap_ref, ct_ref, d_ref, du_ref, dbb_ref, dc_ref, da_ref, dd_ref,
             bu_sc, x_sc, g_sc, y_sc, carry_sc):
        @pl.when(pl.program_id(1) == 0)
        def _():
            carry_sc[...] = jnp.zeros(carry_sc.shape, F32)
            dbb_ref[...] = jnp.zeros(dbb_ref.shape, F32)
            dc_ref[...] = jnp.zeros(dc_ref.shape, F32)
            da_ref[...] = jnp.zeros(da_ref.shape, F32)
            dd_ref[...] = jnp.zeros(dd_ref.shape, F32)

        ap = ap_ref.at[0]
        uu, dyy = _permute_rows(u_ref, seg), _permute_rows(dy_ref, seg)
        cin = [xin_ref[0, 0, :, _col(c)] for c in range(NCOL)]
        _chunk_states(uu, bb_ref, ap, cin, bu_sc, x_sc, seg)
        bu_sc[...] = _mm(dyy, ct_ref[0])
        ends = _segment_scan(bu_sc, g_sc, ap, seg, True)
        lam_in = [carry_sc[:, _col(c)] for c in range(NCOL)]
        ent, out = _segment_entries(ends, lam_in, ap, seg, True)
        for c in range(NCOL):
            carry_sc[:, _col(c)] = out[c]
        row = lax.broadcasted_iota(jnp.int32, (8, LANES), 0)
        xp0 = [jnp.where(row == 0, cin[c], pltpu.roll(x_sc[(seg - 1) * 8:seg * 8, _col(c)], 1, 0)) for c in range(NCOL)]

        def fix(i, acc):
            rows, prev, tab = _tile_rows(i), _tile_rows(jnp.maximum(i - 1, 0)), _tile_rows(seg - 1 - i)
            new = list(acc)
            for c in range(half):
                pr, pi = ap[tab, _col(c)], ap[tab, _col(half + c)]
                lr_ = g_sc[rows, _col(c)] + pr * ent[c] + pi * ent[half + c]
                li_ = g_sc[rows, _col(half + c)] + pr * ent[half + c] - pi * ent[c]
                g_sc[rows, _col(c)] = lr_
                g_sc[rows, _col(half + c)] = li_
                xr = jnp.where(i == 0, xp0[c], x_sc[prev, _col(c)])
                xi = jnp.where(i == 0, xp0[half + c], x_sc[prev, _col(half + c)])
                new[c] = acc[c] + lr_ * xr + li_ * xi
                new[half + c] = acc[half + c] + li_ * xr - lr_ * xi
            return tuple(new)

        zero = jnp.zeros((8, LANES), F32)
        dacc = lax.fori_loop(0, seg, fix, (zero,) * NCOL, unroll=SCAN_UNROLL)
        for c in range(NCOL):
            da_ref[0, :, _col(c)] += dacc[c]
        lam = g_sc[...]
        y_sc[...] = _mm(lam, bbt_ref[0]) + d_ref[...] * dyy
        _unpermute_rows(y_sc, du_ref, seg)
        dbb_ref[0] += _mm_tn(uu, lam)
        dc_ref[0] += _mm_tn(x_sc[...], dyy)
        dd_ref[0] += _stack_rows([jnp.sum(dyy * uu, axis=0, keepdims=True)])

    rev = lambda k, c: (nc - 1 - c, k)
    in_specs = [pl.BlockSpec((t_chunk, LANES), rev), pl.BlockSpec((t_chunk, LANES), rev),
                pl.BlockSpec((1, 1, 8, 2 * SL), lambda k, c: (k, nc - 1 - c, 0, 0)),
                pl.BlockSpec((1, LANES, 2 * SL), lambda k, c: (k, 0, 0)),
                pl.BlockSpec((1, 2 * SL, LANES), lambda k, c: (k, 0, 0)),
                pl.BlockSpec((1, 8 * seg, 2 * SL), lambda k, c: (k, 0, 0)),
                pl.BlockSpec((1, LANES, 2 * SL), lambda k, c: (k, 0, 0)),
                pl.BlockSpec((1, LANES), lambda k, c: (0, k))]
    out_specs = [pl.BlockSpec((t_chunk, LANES), rev),
                 pl.BlockSpec((1, LANES, 2 * SL), lambda k, c: (k, 0, 0)),
                 pl.BlockSpec((1, 2 * SL, LANES), lambda k, c: (k, 0, 0)),
                 pl.BlockSpec((1, 8, 2 * SL), lambda k, c: (k, 0, 0)),
                 pl.BlockSpec((1, 8, LANES), lambda k, c: (k, 0, 0))]
    out_shape = [jax.ShapeDtypeStruct((s_len, D), F32), jax.ShapeDtypeStruct((GB, LANES, 2 * SL), F32),
                 jax.ShapeDtypeStruct((GB, 2 * SL, LANES), F32), jax.ShapeDtypeStruct((GB, 8, 2 * SL), F32),
                 jax.ShapeDtypeStruct((GB, 8, LANES), F32)]
    return pl.pallas_call(body, grid=(GB, nc), in_specs=in_specs, out_specs=out_specs, out_shape=out_shape,
                          scratch_shapes=[pltpu.VMEM((t_chunk, 2 * SL), F32)] * 3 + [pltpu.VMEM((t_chunk, LANES), F32),
                                                                                        pltpu.VMEM((8, 2 * SL), F32)],
                          compiler_params=_cparams(2), name="s5_scan_bwd")(u, dy, xin, bb, bb_t, apow, cblk_t, dskip)


_GELU_K = math.sqrt(2.0 / math.pi)


def _gelu(y):
    t = jnp.tanh(_GELU_K * (y + 0.044715 * (y * y * y)))
    return 0.5 * y * (1.0 + t), 0.5 * (1.0 + t) + 0.5 * y * (1.0 - t * t) * (_GELU_K * (1.0 + 3 * 0.044715 * (y * y)))


def _s5_in_fwd(h, gmix, win, tm):
    def fn(h, gmix, win):
        return (_mm(_rms_fwd(h, gmix)[0], win),), ()

    return _rowcall("s5_in_fwd", fn, tm, [h], [gmix, win], [(D, F32)], [])[0][0]


def _s5_in_bwd(h, du, dh, gmix, win_t, tm):
    def fn(h, du, dh, gmix, win_t):
        hn, xh, r = _rms_fwd(h, gmix)
        dx, dg = _rms_bwd(_mm(du, win_t), xh, r, gmix)
        return (dh + dx,), (_mm_tn(hn, du), dg)

    return _rowcall("s5_in_bwd", fn, tm, [h, du, dh], [gmix, win_t], [(D, F32)], [(D, D), (1, D)])


def _s5_out_fwd(h, y, wglu, tm):
    def fn(h, y, wglu):
        z = _mm(_gelu(y)[0], wglu)
        return (h + z[:, :D] * jax.nn.sigmoid(z[:, D:]),), ()

    return _rowcall("s5_out_fwd", fn, tm, [h, y], [wglu], [(D, F32)], [])[0][0]


def _s5_out_bwd(dh, y, wglu, wglu_t, tm):
    def fn(dh, y, wglu, wglu_t):
        yg, dgelu = _gelu(y)
        z = _mm(yg, wglu)
        val, sg = z[:, :D], jax.nn.sigmoid(z[:, D:])
        dz = jnp.concatenate([dh * sg, dh * val * sg * (1.0 - sg)], axis=1)
        return (_mm(dz, wglu_t) * dgelu,), (_mm_tn(yg, dz),)

    return _rowcall("s5_out_bwd", fn, tm, [dh, y], [wglu, wglu_t], [(D, F32)], [(D, 2 * D)])


def _final_loss(h, tgt, gfin, tm):
    def fn(h, tgt, gfin):
        y, xh, r = _rms_fwd(h, gfin)
        err = y - tgt
        dx, dg = _rms_bwd(err * (1.0 / D), xh, r, gfin)
        return (dx,), (jnp.sum(err * err, axis=0, keepdims=True), dg)

    return _rowcall("final_loss", fn, tm, [h, tgt], [gfin], [(D, F32)], [(1, D), (1, D)])


def _bf(a):
    return a.astype(MXU)


def _s5_block_mats(b_re, b_im, c_re, c_im):
    gl = G // GB
    eye = jnp.eye(gl, dtype=F32)

    def b_blk(b):
        bt = b.reshape(GB, gl, P, C).transpose(0, 1, 3, 2)
        return (bt[:, :, :, None, :] * eye[None, :, None, :, None]).reshape(GB, gl * C, gl * P)

    def c_blk(cm):
        ct = cm.reshape(GB, gl, C, P).transpose(0, 1, 3, 2)
        return (ct[:, :, :, None, :] * eye[None, :, None, :, None]).reshape(GB, gl * P, gl * C)

    braw = jnp.concatenate([b_blk(b_re), b_blk(b_im)], axis=2)
    cblk = jnp.concatenate([c_blk(c_re), -c_blk(c_im)], axis=1)
    return braw, cblk


def _s5_unblock_b(d):
    gl = G // GB
    eye = jnp.eye(gl, dtype=F32)
    d5 = d.reshape(GB, gl, C, gl, P)
    return jnp.sum(d5 * eye[None, :, None, :, None], axis=3).transpose(0, 1, 3, 2).reshape(G, P, C)


def _s5_unblock_c(d):
    gl = G // GB
    eye = jnp.eye(gl, dtype=F32)
    d5 = d.reshape(GB, gl, P, gl, C)
    return jnp.sum(d5 * eye[None, :, None, :, None], axis=3).transpose(0, 1, 3, 2).reshape(G, C, P)


def _tiles(s_len):
    return (min(256, s_len), min(512, s_len), min(1024, s_len),
            (min(512, s_len), min(2048, s_len)), (min(1024, s_len), min(1024, s_len)),
            min(256, s_len), min(256, s_len))


def _sequence_step(x, pos, tgt, w):
    s_len = x.shape[0]
    tm_w, tm, t_chunk, t_fwd, t_bwd, tm_ff, tm_fb = _tiles(s_len)
    seg = t_chunk // 8
    row = lambda v: v.reshape(1, -1)

    wa = _bf(jnp.pad(w["mla_w_a"][0], ((0, 0), (0, AW - (QL + KVL + ROPE)))))
    wuq = _bf(jnp.pad(w["mla_w_uq"][0].reshape(QL, HEADS, NOPE + ROPE), ((0, 0), (0, 0), (0, HC - NOPE - ROPE))).reshape(QL, HEADS * HC))
    wukv = _bf(w["mla_w_ukv"][0])
    wo = _bf(w["mla_w_o"][0])
    inv = 1.0 / (ROPE_THETA ** (jnp.arange(0, ROPE, 2, dtype=F32) / ROPE))
    invf = jnp.concatenate([inv, inv, jnp.zeros((LANES - ROPE,), F32)]).reshape(1, LANES)
    gmix0, gmix1 = row(w["g_mix"][0]), row(w["g_mix"][1])
    gq, gkv = row(w["mla_g_q"][0]), row(w["mla_g_kv"][0])
    win, wglu = _bf(w["ssm_w_in"][0]), _bf(w["ssm_w_glu"][0])
    dskip = row(w["ssm_d"][0])
    lr = w["ssm_lambda_re"][0].reshape(GB, 1, SL)
    li = w["ssm_lambda_im"][0].reshape(GB, 1, SL)
    ldt = jnp.broadcast_to(w["ssm_log_dt"][0][:, None], (G, P)).reshape(GB, 1, SL)
    braw, cblk = _s5_block_mats(w["ssm_b_re"][0], w["ssm_b_im"][0], w["ssm_c_re"][0], w["ssm_c_im"][0])
    cblk = _bf(cblk)
    ffn = []
    for l in range(2):
        wup, wdown = _bf(w["ffn_w_up"][l]), _bf(w["ffn_w_down"][l])
        ffn.append(dict(g=row(w["g_ffn"][l]), wup=wup, wup_t=wup.T, wdown=wdown, wdown_t=wdown.T,
                        cw=w["ffn_conv_w"][l], cb=row(w["ffn_conv_b"][l])))

    qc, kc, v = _mla_front_fwd(x, pos, gmix0, wa, gq, gkv, wuq, wukv, invf, tm_w)
    o, lse = _flash_fwd(qc, kc, v, *t_fwd)
    h1 = _attn_out_fwd(x, o, wo, tm)
    f0 = ffn[0]
    h2, *saved0 = _ffn_fwd(h1, f0["g"], f0["wup"], f0["cw"], f0["cb"], f0["wdown"], tm_ff)
    bb, apow = _s5_prep(lr, li, ldt, braw, seg)
    u = _s5_in_fwd(h2, gmix1, win, tm)
    y, xin = _s5_scan_fwd(u, bb, apow, cblk, dskip, t_chunk)
    h3 = _s5_out_fwd(h2, y, wglu, tm)
    f1 = ffn[1]
    h4, *saved1 = _ffn_fwd(h3, f1["g"], f1["wup"], f1["cw"], f1["cb"], f1["wdown"], tm_ff)
    (dh4,), (sq, d_gfinal) = _final_loss(h4, tgt, row(w["g_final"]), tm)
    loss = 0.5 * jnp.sum(sq) / D

    grads = {}

    def ffn_back(hin, dout, saved, f):
        act, upv, upg = saved
        din, dupv, dupg, hn, cacc, dg = _ffn_bwd(hin, dout, upv, upg, f["g"], f["cw"], f["cb"], f["wdown_t"], f["wup_t"], tm_fb)
        d_wup = jnp.concatenate([_matmul_tn(hn, dupv, DFF, min(512, s_len), "ffn_dwup_v"),
                                 _matmul_tn(hn, dupg, DFF, min(512, s_len), "ffn_dwup_g")], axis=1)
        d_wdown = _matmul_tn(act, dout, D, min(512, s_len), "ffn_dwdown")
        cflat = cacc.transpose(1, 0, 2).reshape(8, 2 * DFF)
        return din, d_wup, d_wdown, cflat[:3], cflat[3], dg[0]

    dh3, d_wup1, d_wdown1, d_cw1, d_cb1, d_gffn1 = ffn_back(h3, dh4, saved1, f1)
    (dy,), (d_wglu,) = _s5_out_bwd(dh3, y, wglu, wglu.T, tm)
    du, d_bb, d_cblk, d_a, d_dsk = _s5_scan_bwd(u, dy, xin, bb, bb.transpose(0, 2, 1), apow, cblk.transpose(0, 2, 1), dskip, t_chunk)
    d_braw, d_lr, d_li, d_ldt = _s5_prep_bwd(lr, li, ldt, braw, d_bb, d_a)
    (dh2,), (d_win, d_gmix1) = _s5_in_bwd(h2, du, dh3, gmix1, win.T, tm)
    dh1, d_wup0, d_wdown0, d_cw0, d_cb0, d_gffn0 = ffn_back(h1, dh2, saved0, f0)
    (do, delta), (d_wo,) = _attn_out_bwd(dh1, o, wo.T, tm)
    dqc, dkc, dv = _flash_bwd(qc, kc, v, do, lse, delta, *t_bwd)
    (dx,), (d_wa, d_wuq, d_wukv, d_gq, d_gkv, d_gmix0) = _mla_front_bwd(
        x, pos, dqc, dkc, dv, dh1, gmix0, wa, gq, gkv, wuq, wukv, invf, wa.T, wuq.T, wukv.T, tm_w)

    grads["mla_w_a"] = d_wa[None, :, :QL + KVL + ROPE]
    grads["mla_g_q"] = d_gq
    grads["mla_g_kv"] = d_gkv
    grads["mla_w_uq"] = d_wuq.reshape(QL, HEADS, HC)[:, :, :NOPE + ROPE].reshape(1, QL, HEADS * (NOPE + ROPE))
    grads["mla_w_ukv"] = d_wukv[None]
    grads["mla_w_o"] = d_wo[None]
    grads["ssm_w_in"] = d_win[None]
    grads["ssm_lambda_re"] = d_lr.reshape(1, G, P)
    grads["ssm_lambda_im"] = d_li.reshape(1, G, P)
    grads["ssm_log_dt"] = jnp.sum(d_ldt.reshape(G, P), axis=1)[None]
    grads["ssm_b_re"] = _s5_unblock_b(d_braw[:, :, :SL])[None]
    grads["ssm_b_im"] = _s5_unblock_b(d_braw[:, :, SL:])[None]
    grads["ssm_c_re"] = _s5_unblock_c(d_cblk[:, :SL, :])[None]
    grads["ssm_c_im"] = -_s5_unblock_c(d_cblk[:, SL:, :])[None]
    grads["ssm_d"] = jnp.sum(d_dsk, axis=1).reshape(1, D)
    grads["ssm_w_glu"] = d_wglu[None]
    grads["ffn_w_up"] = jnp.stack([d_wup0, d_wup1])
    grads["ffn_conv_w"] = jnp.stack([d_cw0, d_cw1])
    grads["ffn_conv_b"] = jnp.stack([d_cb0, d_cb1])
    grads["ffn_w_down"] = jnp.stack([d_wdown0, d_wdown1])
    grads["g_mix"] = jnp.concatenate([d_gmix0, d_gmix1], axis=0)
    grads["g_ffn"] = jnp.stack([d_gffn0, d_gffn1])
    grads["g_final"] = d_gfinal[0]
    return loss, dx, grads


MESH = pl.DeviceIdType.MESH
ANY = pl.BlockSpec(memory_space=pl.ANY)


def _gather_many(blocks, name):
    n = len(blocks)

    def body(*refs):
        ins, outs = refs[:n], refs[n:2 * n]
        send_sems, recv_sems, local_sems = refs[2 * n:]
        x, y, c = lax.axis_index("x"), lax.axis_index("y"), lax.axis_index("c")
        me, sibling = (x, y, c), (x, y, 1 - c)
        chips = [(1 - x, y), (x, 1 - y), (1 - x, 1 - y)]

        def copy(a, k, block, to, own=False):
            px, py, pc = block
            slot = outs[a].at[4 * px + 2 * py + pc]
            return pltpu.make_async_remote_copy(src_ref=ins[a] if own else slot, dst_ref=slot,
                                                send_sem=send_sems.at[7 * a + k], recv_sem=recv_sems.at[7 * a + k],
                                                device_id=to, device_id_type=MESH)

        mine = [pltpu.make_async_copy(ins[a], outs[a].at[4 * x + 2 * y + c], local_sems.at[a]) for a in range(n)]
        sent = []
        for a in range(n):
            mine[a].start()
            sent.append(copy(a, 0, me, sibling, own=True))
            sent += [copy(a, 1 + j, me, (*chip, c), own=True) for j, chip in enumerate(chips)]
        for cp in sent:
            cp.start()
        for j, chip in enumerate(chips):
            for a in range(n):
                copy(a, 1 + j, (*chip, c), me).wait_recv()
                passed = copy(a, 4 + j, (*chip, c), sibling)
                passed.start()
                sent.append(passed)
        for a in range(n):
            copy(a, 0, sibling, me).wait_recv()
            for j, chip in enumerate(chips):
                copy(a, 4 + j, (*chip, 1 - c), me).wait_recv()
        for cp in sent:
            cp.wait_send()
        for cp in mine:
            cp.wait()

    return pl.pallas_call(body, out_shape=[jax.ShapeDtypeStruct((NDEV,) + b.shape, b.dtype) for b in blocks],
                          in_specs=[ANY] * n, out_specs=[ANY] * n,
                          scratch_shapes=[pltpu.SemaphoreType.DMA((7 * n,)), pltpu.SemaphoreType.DMA((7 * n,)),
                                          pltpu.SemaphoreType.DMA((n,))],
                          name=name)(*blocks)


def _exchange_many(sends, casts, name):
    n, tot = len(sends), len(sends) + len(casts)

    def body(*refs):
        ins, outs = refs[:tot], refs[tot:2 * tot]
        send_sems, recv_sems, local_sems = refs[2 * tot:]
        x, y, c = lax.axis_index("x"), lax.axis_index("y"), lax.axis_index("c")
        me = 4 * x + 2 * y + c
        local, copies = [], []
        for a in range(tot):
            own = ins[a].at[me] if a < n else ins[a]
            local.append(pltpu.make_async_copy(own, outs[a].at[me], local_sems.at[a]))
            local[a].start()
            for m in range(1, NDEV):
                px = 1 - x if m & 4 else x
                py = 1 - y if m & 2 else y
                pc = 1 - c if m & 1 else c
                peer = 4 * px + 2 * py + pc
                sems = dict(send_sem=send_sems.at[7 * a + m - 1], recv_sem=recv_sems.at[7 * a + m - 1],
                            device_id=(px, py, pc), device_id_type=MESH)
                out = pltpu.make_async_remote_copy(src_ref=ins[a].at[peer] if a < n else ins[a], dst_ref=outs[a].at[me], **sems)
                out.start()
                arrival = pltpu.make_async_remote_copy(src_ref=own, dst_ref=outs[a].at[peer], **sems)
                copies.append((out, arrival))
        for out, arrival in copies:
            arrival.wait_recv()
        for out, arrival in copies:
            out.wait_send()
        for cp in local:
            cp.wait()

    out_shape = [jax.ShapeDtypeStruct(s.shape, s.dtype) for s in sends]
    out_shape += [jax.ShapeDtypeStruct((NDEV,) + b.shape, b.dtype) for b in casts]
    out = pl.pallas_call(body, out_shape=out_shape, in_specs=[ANY] * tot, out_specs=[ANY] * tot,
                         scratch_shapes=[pltpu.SemaphoreType.DMA((7 * tot,)), pltpu.SemaphoreType.DMA((7 * tot,)),
                                         pltpu.SemaphoreType.DMA((tot,))],
                         name=name)(*sends, *casts)
    return out[:n], out[n:]


def _adam_math(w, parts, m, v):
    g = parts[0].astype(F32)
    for k in range(1, NDEV):
        g = g + parts[k].astype(F32)
    m2 = ADAM_B1 * m + (1.0 - ADAM_B1) * g
    v2 = ADAM_B2 * v + (1.0 - ADAM_B2) * jnp.square(g)
    m_hat = m2 / (1.0 - ADAM_B1 ** ADAM_STEP)
    v_hat = v2 / (1.0 - ADAM_B2 ** ADAM_STEP)
    return g, -ADAM_LR * (m_hat / (jnp.sqrt(v_hat) + ADAM_EPS) + ADAM_WD * w), m2, v2


def _adamw_many(ws, parts, ms, vs, name):
    n = len(ws)

    def body(*refs):
        w_refs, p_refs, m_refs, v_refs, outs = refs[:n], refs[n:2 * n], refs[2 * n:3 * n], refs[3 * n:4 * n], refs[4 * n:]
        for a in range(n):
            res = _adam_math(w_refs[a][...], [p_refs[a][k] for k in range(NDEV)], m_refs[a][...], v_refs[a][...])
            for o, val in zip(outs[4 * a:4 * a + 4], res):
                o[...] = val

    out = pl.pallas_call(body, out_shape=[jax.ShapeDtypeStruct(w.shape, F32) for w in ws for _ in range(4)],
                         compiler_params=pltpu.CompilerParams(vmem_limit_bytes=VMEM_LIMIT), name=name)(*ws, *parts, *ms, *vs)
    return [out[4 * a:4 * a + 4] for a in range(n)]


def _adamw_rows(w, parts, m, v, tr, name):
    rows, cols = w.shape

    def body(w_ref, p_ref, m_ref, v_ref, g_ref, d_ref, m2_ref, v2_ref):
        res = _adam_math(w_ref[...], [p_ref[k] for k in range(NDEV)], m_ref[...], v_ref[...])
        for o, val in zip((g_ref, d_ref, m2_ref, v2_ref), res):
            o[...] = val

    flat = pl.BlockSpec((tr, cols), lambda i: (i, 0))
    return pl.pallas_call(body, grid=(rows // tr,),
                          in_specs=[flat, pl.BlockSpec((NDEV, tr, cols), lambda i: (0, i, 0)), flat, flat],
                          out_specs=[flat] * 4, out_shape=[jax.ShapeDtypeStruct((rows, cols), F32)] * 4,
                          compiler_params=_cparams(1), name=name)(w, parts, m, v)


SHARDED = (("mla_w_a", 1), ("mla_w_uq", 2), ("mla_w_ukv", 2), ("mla_w_o", 1), ("ssm_w_in", 1), ("ssm_d", 1),
           ("ssm_w_glu", 2), ("ffn_w_up", 2), ("ffn_conv_w", 2), ("ffn_w_down", 1))
WIRE_EXACT = ("ssm_d", "ffn_conv_w")
REPLICATED = ("mla_g_q", "mla_g_kv", "ssm_lambda_re", "ssm_lambda_im", "ssm_log_dt", "ssm_b_re", "ssm_b_im",
              "ssm_c_re", "ssm_c_im", "ffn_conv_b", "g_mix", "g_ffn", "g_final")
WEIGHTS = ("mla_w_a", "mla_g_q", "mla_g_kv", "mla_w_uq", "mla_w_ukv", "mla_w_o", "ssm_w_in", "ssm_lambda_re",
           "ssm_lambda_im", "ssm_log_dt", "ssm_b_re", "ssm_b_im", "ssm_c_re", "ssm_c_im", "ssm_d", "ssm_w_glu",
           "ffn_w_up", "ffn_conv_w", "ffn_conv_b", "ffn_w_down", "g_mix", "g_ffn", "g_final")


ADAM_ROW_TILED = (("ffn_w_up", 256), ("ffn_w_down", 176))


def _two_d(shape):
    if len(shape) == 1:
        return (1, shape[0])
    if len(shape) > 2 and shape[-1] < LANES:
        return (math.prod(shape[:-2]), shape[-2] * shape[-1])
    return (math.prod(shape[:-1]), shape[-1])


def kernel(x, positions, mla_w_a, mla_g_q, mla_g_kv, mla_w_uq, mla_w_ukv, mla_w_o, ssm_w_in, ssm_lambda_re, ssm_lambda_im, ssm_log_dt, ssm_b_re, ssm_b_im, ssm_c_re, ssm_c_im, ssm_d, ssm_w_glu, ffn_w_up, ffn_conv_w, ffn_conv_b, ffn_w_down, g_mix, g_ffn, g_final, loss_target, m_mla_w_a, m_mla_g_q, m_mla_g_kv, m_mla_w_uq, m_mla_w_ukv, m_mla_w_o, m_ssm_w_in, m_ssm_lambda_re, m_ssm_lambda_im, m_ssm_log_dt, m_ssm_b_re, m_ssm_b_im, m_ssm_c_re, m_ssm_c_im, m_ssm_d, m_ssm_w_glu, m_ffn_w_up, m_ffn_conv_w, m_ffn_conv_b, m_ffn_w_down, m_g_mix, m_g_ffn, m_g_final, v_mla_w_a, v_mla_g_q, v_mla_g_kv, v_mla_w_uq, v_mla_w_ukv, v_mla_w_o, v_ssm_w_in, v_ssm_lambda_re, v_ssm_lambda_im, v_ssm_log_dt, v_ssm_b_re, v_ssm_b_im, v_ssm_c_re, v_ssm_c_im, v_ssm_d, v_ssm_w_glu, v_ffn_w_up, v_ffn_conv_w, v_ffn_conv_b, v_ffn_w_down, v_g_mix, v_g_ffn, v_g_final):
    a = dict(locals())
    s_len = x.shape[1]
    sh_names = [n for n, _ in SHARDED]

    gathered = _gather_many([a[n] if n in WIRE_EXACT else a[n].astype(MXU) for n in sh_names], "gather_weights")
    w = {n: a[n] for n in REPLICATED}
    for (n, axis), g in zip(SHARDED, gathered):
        moved = jnp.moveaxis(g, 0, axis)
        w[n] = moved.reshape(moved.shape[:axis] + (-1,) + moved.shape[axis + 2:])

    loss, dx, grads = _sequence_step(x[0], positions.reshape(s_len, 1).astype(F32), loss_target[0], w)
    loss = lax.psum(loss, ("x", "y", "c"))

    sends = []
    for n, axis in SHARDED:
        full = grads[n]
        cut = full.reshape(full.shape[:axis] + (NDEV, -1) + full.shape[axis + 1:])
        sends.append(jnp.moveaxis(cut, axis, 0).astype(MXU))
    landed, rep_parts = _exchange_many(sends, [grads[n] for n in REPLICATED], "exchange_grads")

    def view(n, arr, lead=()):
        return arr.reshape(lead + _two_d(a[n].shape))

    out = {}

    def finish(n, res):
        for kind, val in zip(("grad_", "delta_", "new_m_", "new_v_"), res):
            out[kind + n] = val.reshape(a[n].shape)

    parts_of = dict(zip(sh_names, landed)) | dict(zip(REPLICATED, rep_parts))
    for n, tr in ADAM_ROW_TILED:
        finish(n, _adamw_rows(view(n, a[n]), view(n, parts_of[n], (NDEV,)), view(n, a["m_" + n]), view(n, a["v_" + n]),
                              tr, "adamw_" + n))
    tiled = [n for n, _ in ADAM_ROW_TILED]
    for names, tag in (([n for n in sh_names if n not in tiled], "adamw_sharded"), (list(REPLICATED), "adamw_replicated")):
        res = _adamw_many([view(n, a[n]) for n in names], [view(n, parts_of[n], (NDEV,)) for n in names],
                          [view(n, a["m_" + n]) for n in names], [view(n, a["v_" + n]) for n in names], tag)
        for n, r in zip(names, res):
            finish(n, r)
    return (loss, dx[None], *[out[kind + n] for kind in ("grad_", "delta_", "new_m_", "new_v_") for n in WEIGHTS])
```

```python
import functools
import math

import jax
import jax.numpy as jnp
from jax import lax
from jax.experimental import pallas as pl
from jax.experimental.pallas import tpu as pltpu

F32 = jnp.float32
MXU = jnp.bfloat16

D = 1024
HEADS = 8
NOPE = 128
ROPE = 64
VH = 128
QL = 384
KVL = 256
CHUNK = 64
ROPE_THETA = 10000.0
EPS = 1e-6
G, P, C = 64, 64, 16
DFF = 2816
ADAM_LR, ADAM_B1, ADAM_B2, ADAM_EPS, ADAM_WD, ADAM_STEP = 0.001, 0.9, 0.999, 1e-08, 0.01, 10

LANES = 128
AW = 768
HC = 256
GB = 8
SL = (G // GB) * P
NCOL = 2 * SL // LANES
TC = DFF
HALO = 16
NDEV = 8
VMEM_LIMIT = 56 * 1024 * 1024


def _mm(a, b):
    return jnp.dot(a.astype(MXU), b.astype(MXU), preferred_element_type=F32)


def _mm_tn(a, b):
    return lax.dot_general(a.astype(MXU), b.astype(MXU), (((0,), (0,)), ((), ())), preferred_element_type=F32)


def _mm_nt(a, b):
    return lax.dot_general(a.astype(MXU), b.astype(MXU), (((1,), (1,)), ((), ())), preferred_element_type=F32)


def _rms_fwd(x, g):
    r = lax.rsqrt(jnp.mean(x * x, axis=-1, keepdims=True) + EPS)
    xh = x * r
    return xh * g, xh, r


def _rms_bwd(dy, xh, r, g):
    dxh = dy * g
    dx = r * (dxh - xh * jnp.mean(dxh * xh, axis=-1, keepdims=True))
    return dx, jnp.sum(dy * xh, axis=0, keepdims=True)


def _rot_partner(b):
    lane = lax.broadcasted_iota(jnp.int32, b.shape, 1)
    return jnp.where(lane < ROPE // 2, -pltpu.roll(b, LANES - ROPE // 2, 1), pltpu.roll(b, ROPE // 2, 1))


def _rope_blk(b, cos2, sin2):
    return b * cos2 + _rot_partner(b) * sin2


def _unrope_blk(db, cos2, sin2):
    return db * cos2 - _rot_partner(db * sin2)


def _cparams(n_axes, vmem=VMEM_LIMIT):
    return pltpu.CompilerParams(dimension_semantics=("arbitrary",) * n_axes, vmem_limit_bytes=vmem)


def _rowcall(name, fn, tm, row_ins, consts, row_outs, acc_outs):
    n = row_ins[0].shape[0]
    n_in = len(row_ins) + len(consts)
    n_ro = len(row_outs)

    def body(*refs):
        ins, ro_refs, acc_refs = refs[:n_in], refs[n_in:n_in + n_ro], refs[n_in + n_ro:]
        ro, ao = fn(*[r[...] for r in ins])

        @pl.when(pl.program_id(0) == 0)
        def _():
            for r in acc_refs:
                r[...] = jnp.zeros(r.shape, r.dtype)

        for r, val in zip(ro_refs, ro):
            r[...] = val.astype(r.dtype)
        for r, val in zip(acc_refs, ao):
            r[...] += val

    in_specs = [pl.BlockSpec((tm, a.shape[1]), lambda i: (i, 0)) for a in row_ins]
    in_specs += [pl.BlockSpec(c.shape, lambda i, nd=c.ndim: (0,) * nd) for c in consts]
    out_specs = [pl.BlockSpec((tm, w), lambda i: (i, 0)) for w, _ in row_outs]
    out_specs += [pl.BlockSpec(s, lambda i, nd=len(s): (0,) * nd) for s in acc_outs]
    out_shape = [jax.ShapeDtypeStruct((n, w), dt) for w, dt in row_outs]
    out_shape += [jax.ShapeDtypeStruct(s, F32) for s in acc_outs]
    out = pl.pallas_call(body, grid=(n // tm,), in_specs=in_specs, out_specs=out_specs, out_shape=out_shape,
                         compiler_params=_cparams(1), name=name)(*row_ins, *consts)
    return list(out[:n_ro]), list(out[n_ro:])


def _mla_front_tile(x, pos, gmix, wa, gq, gkv, wuq, wukv, invf):
    hn, xh, r = _rms_fwd(x, gmix)
    a = _mm(hn, wa)
    cq, ckv, krb = a[:, :QL], a[:, QL:QL + KVL], a[:, QL + KVL:]
    cqn, cqh, rq = _rms_fwd(cq, gq)
    ckvn, ckvh, rkv = _rms_fwd(ckv, gkv)
    q = _mm(cqn, wuq)
    kv = _mm(ckvn, wukv)
    ang = pos * invf
    cos2, sin2 = jnp.cos(ang), jnp.sin(ang)
    krr = _rope_blk(krb, cos2, sin2)
    qp, kp, vp = [], [], []
    for h in range(HEADS):
        qp += [q[:, h * HC:h * HC + NOPE], _rope_blk(q[:, h * HC + NOPE:(h + 1) * HC], cos2, sin2)]
        kp += [kv[:, h * HC:h * HC + NOPE], krr]
        vp += [kv[:, h * HC + NOPE:(h + 1) * HC]]
    res = (hn, xh, r, cqn, cqh, rq, ckvn, ckvh, rkv, cos2, sin2)
    return jnp.concatenate(qp, axis=1), jnp.concatenate(kp, axis=1), jnp.concatenate(vp, axis=1), res


def _mla_front_fwd(x, pos, gmix, wa, gq, gkv, wuq, wukv, invf, tm):
    def fn(*args):
        qc, kc, v, _ = _mla_front_tile(*args)
        return (qc * Q_PRESCALE, kc, v), ()

    return _rowcall("mla_front_fwd", fn, tm, [x, pos], [gmix, wa, gq, gkv, wuq, wukv, invf],
                    [(HEADS * HC, MXU), (HEADS * HC, MXU), (HEADS * VH, MXU)], [])[0]


def _mla_front_bwd(x, pos, dqc, dkc, dv, dh, gmix, wa, gq, gkv, wuq, wukv, invf, wa_t, wuq_t, wukv_t, tm):
    def fn(x, pos, dqc, dkc, dv, dh, gmix, wa, gq, gkv, wuq, wukv, invf, wa_t, wuq_t, wukv_t):
        _, _, _, (hn, xh, r, cqn, cqh, rq, ckvn, ckvh, rkv, cos2, sin2) = _mla_front_tile(
            x, pos, gmix, wa, gq, gkv, wuq, wukv, invf)
        dqc, dkc, dv = dqc * SM_SCALE, dkc.astype(F32), dv.astype(F32)
        dqp, dkvp = [], []
        dkr = jnp.zeros((x.shape[0], LANES), F32)
        for h in range(HEADS):
            dqp += [dqc[:, h * HC:h * HC + NOPE], _unrope_blk(dqc[:, h * HC + NOPE:(h + 1) * HC], cos2, sin2)]
            dkvp += [dkc[:, h * HC:h * HC + NOPE], dv[:, h * VH:(h + 1) * VH]]
            dkr = dkr + dkc[:, h * HC + NOPE:(h + 1) * HC]
        dq = jnp.concatenate(dqp, axis=1)
        dkv = jnp.concatenate(dkvp, axis=1)
        dkrb = _unrope_blk(dkr, cos2, sin2)
        dcqn = _mm(dq, wuq_t)
        dckvn = _mm(dkv, wukv_t)
        d_wuq = _mm_tn(cqn, dq)
        d_wukv = _mm_tn(ckvn, dkv)
        dcq, d_gq = _rms_bwd(dcqn, cqh, rq, gq)
        dckv, d_gkv = _rms_bwd(dckvn, ckvh, rkv, gkv)
        da = jnp.concatenate([dcq, dckv, dkrb], axis=1)
        d_wa = _mm_tn(hn, da)
        dhn = _mm(da, wa_t)
        dx, d_gmix = _rms_bwd(dhn, xh, r, gmix)
        return (dh + dx,), (d_wa, d_wuq, d_wukv, d_gq, d_gkv, d_gmix)

    return _rowcall("mla_front_bwd", fn, tm, [x, pos, dqc, dkc, dv, dh],
                    [gmix, wa, gq, gkv, wuq, wukv, invf, wa_t, wuq_t, wukv_t], [(D, F32)],
                    [(D, AW), (QL, HEADS * HC), (KVL, HEADS * HC), (1, QL), (1, KVL), (1, D)])


SM_SCALE = (NOPE + ROPE) ** -0.5
LOG2E = 1.0 / math.log(2.0)
Q_PRESCALE = SM_SCALE * LOG2E


def _pair_tables(s_len, tq, tk, q_major):
    pairs = [(qi, ki) for qi in range(s_len // tq) for ki in range(s_len // tk) if ki * tk < (qi + 1) * tq]
    if not q_major:
        pairs.sort(key=lambda p: (p[1], p[0]))
    return (jnp.asarray([p[0] for p in pairs], jnp.int32), jnp.asarray([p[1] for p in pairs], jnp.int32))


def _last_key_tile(qi, tq, tk):
    return ((qi + 1) * tq - 1) // tk


def _visible(qi, ki, tq, tk):
    row = qi * (tq // CHUNK) + lax.broadcasted_iota(jnp.int32, (tq, tk), 0) // CHUNK
    col = ki * (tk // CHUNK) + lax.broadcasted_iota(jnp.int32, (tq, tk), 1) // CHUNK
    return col <= row


def _masked_and_not(qi, ki, tq, tk, fn):
    needs_mask = (ki + 1) * tk > qi * tq
    pl.when(needs_mask)(lambda: fn(True))
    pl.when(jnp.logical_not(needs_mask))(lambda: fn(False))


def _first_and_last_step(n_steps):
    first = jnp.logical_and(pl.program_id(0) == 0, pl.program_id(1) == 0)
    last = jnp.logical_and(pl.program_id(0) == HEADS - 1, pl.program_id(1) == n_steps - 1)
    return first, last


def _flash_fwd(qc, kc, v, tq, tk, riders):
    s_len = qc.shape[0]
    qt, kt = _pair_tables(s_len, tq, tk, True)
    nr = len(riders)

    def body(qt_ref, kt_ref, q_ref, k_ref, v_ref, *rest):
        r_in, (o_ref, lse_ref), r_out = rest[:nr], rest[nr:nr + 2], rest[nr + 2:2 * nr + 2]
        m_sc, l_sc, acc_sc = rest[2 * nr + 2:2 * nr + 5]
        sems = rest[2 * nr + 5:]
        p_id = pl.program_id(1)
        qi, ki = qt_ref[p_id], kt_ref[p_id]
        first, last = _first_and_last_step(qt.shape[0])
        if nr:
            pl.when(first)(lambda: _gather_issue(r_in, r_out, sems))

        @pl.when(ki == 0)
        def _():
            m_sc[...] = jnp.full(m_sc.shape, -jnp.inf, F32)
            l_sc[...] = jnp.zeros(l_sc.shape, F32)
            acc_sc[...] = jnp.zeros(acc_sc.shape, F32)

        def update(masked):
            s = _mm_nt(q_ref[...], k_ref[...])
            if masked:
                s = jnp.where(_visible(qi, ki, tq, tk), s, -jnp.inf)
            m_old = m_sc[...]
            m_new = jnp.maximum(m_old, jnp.max(s, axis=1, keepdims=True))
            alpha = jnp.exp2(m_old - m_new)
            p = jnp.exp2(s - m_new)
            l_sc[...] = alpha * l_sc[...] + jnp.sum(p, axis=1, keepdims=True)
            acc_sc[...] = alpha * acc_sc[...] + _mm(p, v_ref[...])
            m_sc[...] = m_new

        _masked_and_not(qi, ki, tq, tk, update)

        @pl.when(ki == _last_key_tile(qi, tq, tk))
        def _():
            l = l_sc[...]
            o_ref[...] = (acc_sc[...] / l).astype(o_ref.dtype)
            lse_ref[...] = jnp.broadcast_to(m_sc[...] + jnp.log2(l), lse_ref.shape)

        if nr:
            pl.when(last)(lambda: _gather_finish(r_in, r_out, sems))

    qmap = lambda h, p, qt, kt: (qt[p], h)
    kmap = lambda h, p, qt, kt: (kt[p], h)
    grid_spec = pltpu.PrefetchScalarGridSpec(
        num_scalar_prefetch=2, grid=(HEADS, qt.shape[0]),
        in_specs=[pl.BlockSpec((tq, HC), qmap), pl.BlockSpec((tk, HC), kmap), pl.BlockSpec((tk, VH), kmap)] + [ANY] * nr,
        out_specs=[pl.BlockSpec((tq, VH), qmap), pl.BlockSpec((tq, LANES), qmap)] + [ANY] * nr,
        scratch_shapes=[pltpu.VMEM((tq, 1), F32), pltpu.VMEM((tq, 1), F32), pltpu.VMEM((tq, VH), F32)] + _comm_scratch(nr))
    out = pl.pallas_call(body, grid_spec=grid_spec,
                         out_shape=[jax.ShapeDtypeStruct((s_len, HEADS * VH), MXU),
                                    jax.ShapeDtypeStruct((s_len, HEADS * LANES), F32)] + _gathered_shapes(riders),
                         compiler_params=_cparams(2), name="flash_fwd")(qt, kt, qc, kc, v, *riders)
    return out[0], out[1], out[2:]


def _tile_dscores(q, k, v, do, lse, delta, qi, ki, tq, tk, masked):
    p = jnp.exp2(_mm_nt(q, k) - lse[:, :1])
    if masked:
        p = jnp.where(_visible(qi, ki, tq, tk), p, 0.0)
    return p, p * (_mm_nt(do, v) - delta[:, :1])


def _flash_bwd(qc, kc, v, do, lse, delta, tq, tk, riders):
    s_len = qc.shape[0]
    qt, kt = _pair_tables(s_len, tq, tk, False)
    nq = s_len // tq
    nr = len(riders)

    def body(qt_ref, kt_ref, q_ref, k_ref, v_ref, do_ref, lse_ref, dl_ref, *rest):
        r_in, (dq_ref, dk_ref, dv_ref), r_out = rest[:nr], rest[nr:nr + 3], rest[nr + 3:2 * nr + 3]
        dk_sc, dv_sc = rest[2 * nr + 3:2 * nr + 5]
        sems = rest[2 * nr + 5:]
        p_id = pl.program_id(1)
        qi, ki = qt_ref[p_id], kt_ref[p_id]
        rows = pl.ds(pl.multiple_of(qi * tq, tq), tq)
        first, last = _first_and_last_step(qt.shape[0])
        if nr:
            pl.when(first)(lambda: _exchange_issue(r_in, r_out, sems, nr))

        @pl.when(qi == (ki * tk) // tq)
        def _():
            dk_sc[...] = jnp.zeros(dk_sc.shape, F32)
            dv_sc[...] = jnp.zeros(dv_sc.shape, F32)

        @pl.when(ki == 0)
        def _():
            dq_ref[rows, :] = jnp.zeros((tq, HC), F32)

        def update(masked):
            q, k, do = q_ref[...], k_ref[...], do_ref[...]
            p, ds = _tile_dscores(q, k, v_ref[...], do, lse_ref[...], dl_ref[...], qi, ki, tq, tk, masked)
            ds = ds.astype(MXU)
            dv_sc[...] += _mm_tn(p, do)
            dk_sc[...] += _mm_tn(ds, q)
            dq_ref[rows, :] += _mm(ds, k)

        _masked_and_not(qi, ki, tq, tk, update)

        @pl.when(qi == nq - 1)
        def _():
            dk_ref[...] = (dk_sc[...] * (1.0 / LOG2E)).astype(dk_ref.dtype)
            dv_ref[...] = dv_sc[...].astype(dv_ref.dtype)

        if nr:
            pl.when(last)(lambda: _exchange_finish(r_in, r_out, sems, nr))

    qmap = lambda h, p, qt, kt: (qt[p], h)
    kmap = lambda h, p, qt, kt: (kt[p], h)
    grid_spec = pltpu.PrefetchScalarGridSpec(
        num_scalar_prefetch=2, grid=(HEADS, qt.shape[0]),
        in_specs=[pl.BlockSpec((tq, HC), qmap), pl.BlockSpec((tk, HC), kmap), pl.BlockSpec((tk, VH), kmap),
                  pl.BlockSpec((tq, VH), qmap), pl.BlockSpec((tq, LANES), qmap), pl.BlockSpec((tq, LANES), qmap)] + [ANY] * nr,
        out_specs=[pl.BlockSpec((s_len, HC), lambda h, p, qt, kt: (0, h), pipeline_mode=pl.Buffered(1)),
                   pl.BlockSpec((tk, HC), kmap), pl.BlockSpec((tk, VH), kmap)] + [ANY] * nr,
        scratch_shapes=[pltpu.VMEM((tk, HC), F32), pltpu.VMEM((tk, VH), F32)] + _comm_scratch(nr))
    out = pl.pallas_call(body, grid_spec=grid_spec,
                         out_shape=[jax.ShapeDtypeStruct((s_len, HEADS * HC), F32),
                                    jax.ShapeDtypeStruct((s_len, HEADS * HC), MXU),
                                    jax.ShapeDtypeStruct((s_len, HEADS * VH), MXU)] + _exchanged_shapes(riders, []),
                         compiler_params=_cparams(2), name="flash_bwd")(qt, kt, qc, kc, v, do, lse, delta, *riders)
    return out[0], out[1], out[2], out[3:]


def _attn_out_fwd(x, o, wo, tm):
    def fn(x, o, wo):
        return (x + _mm(o, wo),), ()

    return _rowcall("attn_out_fwd", fn, tm, [x, o], [wo], [(D, F32)], [])[0][0]


def _attn_out_bwd(dh, o, wo_t, tm):
    def fn(dh, o, wo_t):
        do = _mm(dh, wo_t)
        of = o.astype(F32)
        dl = [jnp.broadcast_to(jnp.sum(do[:, h * VH:(h + 1) * VH] * of[:, h * VH:(h + 1) * VH], axis=1, keepdims=True),
                               (dh.shape[0], LANES)) for h in range(HEADS)]
        return (do, jnp.concatenate(dl, axis=1)), (_mm_tn(o, dh),)

    return _rowcall("attn_out_bwd", fn, tm, [dh, o], [wo_t], [(HEADS * VH, MXU), (HEADS * LANES, F32)], [(HEADS * VH, D)])


def _shift_rows(a, k):
    return a if k == 0 else pltpu.roll(a, k % a.shape[0], 0)


def _stack_rows(rows):
    idx = lax.broadcasted_iota(jnp.int32, (8, rows[0].shape[1]), 0)
    out = jnp.zeros((8, rows[0].shape[1]), F32)
    for k, r in enumerate(rows):
        out = jnp.where(idx == k, r, out)
    return out


def _ffn_fwd(h, g, wup, cw, cb, wdown, tm):
    s_len = h.shape[0]
    nj = DFF // TC
    hb = tm // HALO

    def body(h_ref, hp_ref, g_ref, wv_ref, wg_ref, cwv_ref, cwg_ref, cbv_ref, cbg_ref, wd_ref, out_ref, act_ref,
             upv_ref, upg_ref, hn_sc, acc_sc):
        i, j = pl.program_id(0), pl.program_id(1)

        @pl.when(j == 0)
        def _():
            gg = g_ref[...]
            hp = _rms_fwd(hp_ref[...], gg)[0]
            hn_sc[:HALO, :] = jnp.where(i > 0, hp, 0.0).astype(MXU)
            hn_sc[HALO:, :] = _rms_fwd(h_ref[...], gg)[0].astype(MXU)
            acc_sc[...] = jnp.zeros(acc_sc.shape, F32)

        hn = hn_sc[...]

        def conv(w_ref, cw_ref, cb_ref, up_ref):
            up = jnp.dot(hn, w_ref[...], preferred_element_type=F32)
            up_ref[...] = up[HALO:].astype(up_ref.dtype)
            cwv = cw_ref[...]
            c = cwv[2:3] * up + cwv[1:2] * _shift_rows(up, 1) + cwv[0:1] * _shift_rows(up, 2)
            return c[HALO:] + cb_ref[...]

        cv = conv(wv_ref, cwv_ref, cbv_ref, upv_ref)
        cg = conv(wg_ref, cwg_ref, cbg_ref, upg_ref)
        act = cg * jax.nn.sigmoid(cg) * cv
        act_ref[...] = act.astype(act_ref.dtype)
        acc_sc[...] += _mm(act, wd_ref[...])

        @pl.when(j == nj - 1)
        def _():
            out_ref[...] = h_ref[...] + acc_sc[...]

    in_specs = [pl.BlockSpec((tm, D), lambda i, j: (i, 0)),
                pl.BlockSpec((HALO, D), lambda i, j: (jnp.maximum(i * hb - 1, 0), 0)),
                pl.BlockSpec((1, D), lambda i, j: (0, 0)),
                pl.BlockSpec((D, TC), lambda i, j: (0, j)), pl.BlockSpec((D, TC), lambda i, j: (0, j + nj)),
                pl.BlockSpec((3, TC), lambda i, j: (0, j)), pl.BlockSpec((3, TC), lambda i, j: (0, j + nj)),
                pl.BlockSpec((1, TC), lambda i, j: (0, j)), pl.BlockSpec((1, TC), lambda i, j: (0, j + nj)),
                pl.BlockSpec((TC, D), lambda i, j: (j, 0))]
    out_specs = [pl.BlockSpec((tm, D), lambda i, j: (i, 0))] + [pl.BlockSpec((tm, TC), lambda i, j: (i, j))] * 3
    return pl.pallas_call(body, grid=(s_len // tm, nj), in_specs=in_specs, out_specs=out_specs,
                          out_shape=[jax.ShapeDtypeStruct((s_len, D), F32)] + [jax.ShapeDtypeStruct((s_len, DFF), MXU)] * 3,
                          scratch_shapes=[pltpu.VMEM((tm + HALO, D), MXU), pltpu.VMEM((tm, D), F32)],
                          compiler_params=_cparams(2), name="ffn_fwd")(h, h, g, wup, wup, cw, cw, cb, cb, wdown)


def _ffn_bwd(h, dout, upv, upg, g, cw, cb, wdown_t, wup_t, tm):
    s_len = h.shape[0]
    ni = s_len // tm
    hb = tm // HALO

    def body(h_ref, d_ref, dx_ref, uvp_ref, uv_ref, uvx_ref, ugp_ref, ug_ref, ugx_ref, g_ref, cwv_ref, cwg_ref, cbv_ref, cbg_ref,
             wdt_ref, wutv_ref, wutg_ref, din_ref, dupv_ref, dupg_ref, hn_ref, cacc_ref, dg_ref):
        i = pl.program_id(0)

        @pl.when(i == 0)
        def _():
            cacc_ref[...] = jnp.zeros(cacc_ref.shape, F32)
            dg_ref[...] = jnp.zeros(dg_ref.shape, F32)

        gg = g_ref[...]
        hn, xh, r = _rms_fwd(h_ref[...], gg)
        hn_ref[...] = hn.astype(hn_ref.dtype)
        dd = jnp.concatenate([d_ref[...], jnp.where(i < ni - 1, dx_ref[...], 0.0)], axis=0).astype(MXU)
        dact = jnp.dot(dd, wdt_ref[...], preferred_element_type=F32)

        def half(p_ref, t_ref, x_ref, cw_ref, cb_ref):
            prev = p_ref[...]
            up = jnp.concatenate([jnp.where(i > 0, prev, jnp.zeros_like(prev)), t_ref[...], x_ref[...]], axis=0).astype(F32)
            cwv = cw_ref[...]
            u1, u2 = _shift_rows(up, 1), _shift_rows(up, 2)
            c = (cwv[2:3] * up + cwv[1:2] * u1 + cwv[0:1] * u2)[HALO:] + cb_ref[...]
            return c, (up[HALO:HALO + tm], u1[HALO:HALO + tm], u2[HALO:HALO + tm]), cwv

        cv, upsv, cwv = half(uvp_ref, uv_ref, uvx_ref, cwv_ref, cbv_ref)
        cg, upsg, cwg = half(ugp_ref, ug_ref, ugx_ref, cwg_ref, cbg_ref)
        sg = jax.nn.sigmoid(cg)
        dcv = dact * (cg * sg)
        dcg = dact * cv * (sg * (1.0 + cg * (1.0 - sg)))

        def back(dc, ups, cwx, slot, dup_ref, wut_ref):
            dup = (cwx[2:3] * dc + cwx[1:2] * _shift_rows(dc, -1) + cwx[0:1] * _shift_rows(dc, -2))[:tm]
            dct = dc[:tm]
            cacc_ref[slot] += _stack_rows([jnp.sum(dct * ups[2], axis=0, keepdims=True),
                                           jnp.sum(dct * ups[1], axis=0, keepdims=True),
                                           jnp.sum(dct * ups[0], axis=0, keepdims=True),
                                           jnp.sum(dct, axis=0, keepdims=True)])
            dup_ref[...] = dup.astype(dup_ref.dtype)
            return _mm(dup, wut_ref[...])

        dhn = back(dcv, upsv, cwv, 0, dupv_ref, wutv_ref) + back(dcg, upsg, cwg, 1, dupg_ref, wutg_ref)
        dx, dgp = _rms_bwd(dhn, xh, r, gg)
        din_ref[...] = d_ref[...] + dx
        dg_ref[...] += dgp

    last_blk = s_len // HALO - 1
    tile = lambda w: pl.BlockSpec((tm, w), lambda i: (i, 0))
    prev = lambda w: pl.BlockSpec((HALO, w), lambda i: (jnp.maximum(i * hb - 1, 0), 0))
    nxt = lambda w: pl.BlockSpec((HALO, w), lambda i: (jnp.minimum((i + 1) * hb, last_blk), 0))
    const = lambda s, j=0: pl.BlockSpec(s, lambda i: (0, j))
    in_specs = [tile(D), tile(D), nxt(D), prev(DFF), tile(DFF), nxt(DFF), prev(DFF), tile(DFF), nxt(DFF),
                const((1, D)), const((3, DFF)), const((3, DFF), 1), const((1, DFF)), const((1, DFF), 1),
                const((D, DFF)), pl.BlockSpec((DFF, D), lambda i: (0, 0)), pl.BlockSpec((DFF, D), lambda i: (1, 0))]
    out_specs = [tile(D), tile(DFF), tile(DFF), tile(D), pl.BlockSpec((2, 8, DFF), lambda i: (0, 0, 0)), const((1, D))]
    out_shape = [jax.ShapeDtypeStruct((s_len, D), F32), jax.ShapeDtypeStruct((s_len, DFF), MXU),
                 jax.ShapeDtypeStruct((s_len, DFF), MXU), jax.ShapeDtypeStruct((s_len, D), MXU),
                 jax.ShapeDtypeStruct((2, 8, DFF), F32), jax.ShapeDtypeStruct((1, D), F32)]
    return pl.pallas_call(body, grid=(ni,), in_specs=in_specs, out_specs=out_specs, out_shape=out_shape,
                          compiler_params=_cparams(1), name="ffn_bwd")(
        h, dout, dout, upv, upv, upv, upg, upg, upg, g, cw, cw, cb, cb, wdown_t, wup_t, wup_t)


def _matmul_tn(a, b, tn, ts, name):
    s_len, m = a.shape
    n = b.shape[1]

    def body(a_ref, b_ref, o_ref):
        @pl.when(pl.program_id(1) == 0)
        def _():
            o_ref[...] = jnp.zeros(o_ref.shape, F32)

        o_ref[...] += _mm_tn(a_ref[...], b_ref[...])

    return pl.pallas_call(body, grid=(n // tn, s_len // ts),
                          in_specs=[pl.BlockSpec((ts, m), lambda jn, k: (k, 0)), pl.BlockSpec((ts, tn), lambda jn, k: (k, jn))],
                          out_specs=pl.BlockSpec((m, tn), lambda jn, k: (0, jn)),
                          out_shape=jax.ShapeDtypeStruct((m, n), F32), compiler_params=_cparams(2), name=name)(a, b)


def _s5_coefs(lr, li, ldt):
    dt = jnp.exp(ldt)
    mag = jnp.exp(lr * dt)
    th = li * dt
    ar, ai = mag * jnp.cos(th), mag * jnp.sin(th)
    den = lr * lr + li * li
    nr = ar - 1.0
    cr = (nr * lr + ai * li) / den
    ci = (ai * lr - nr * li) / den
    return dt, mag, th, ar, ai, den, nr, cr, ci


def _s5_prep(lr, li, ldt, braw, seg):
    def body(lr_ref, li_ref, ldt_ref, b_ref, bb_ref, ap_ref):
        lr_, li_, ldt_ = lr_ref[0], li_ref[0], ldt_ref[0]
        dt, mag, th, ar, ai, den, nr, cr, ci = _s5_coefs(lr_, li_, ldt_)
        br, bi = b_ref[0, :, :SL], b_ref[0, :, SL:]
        bb_ref[0, :, :SL] = (cr * br - ci * bi).astype(bb_ref.dtype)
        bb_ref[0, :, SL:] = (cr * bi + ci * br).astype(bb_ref.dtype)
        for i in range(seg):
            m = jnp.exp((i + 1.0) * (lr_ * dt))
            ap_ref[0, i * 8:(i + 1) * 8, :SL] = jnp.broadcast_to(m * jnp.cos((i + 1.0) * th), (8, SL))
            ap_ref[0, i * 8:(i + 1) * 8, SL:] = jnp.broadcast_to(m * jnp.sin((i + 1.0) * th), (8, SL))

    vec = pl.BlockSpec((1, 1, SL), lambda k: (k, 0, 0))
    return pl.pallas_call(
        body, grid=(GB,), in_specs=[vec, vec, vec, pl.BlockSpec((1, LANES, 2 * SL), lambda k: (k, 0, 0))],
        out_specs=[pl.BlockSpec((1, LANES, 2 * SL), lambda k: (k, 0, 0)),
                   pl.BlockSpec((1, 8 * seg, 2 * SL), lambda k: (k, 0, 0))],
        out_shape=[jax.ShapeDtypeStruct((GB, LANES, 2 * SL), MXU), jax.ShapeDtypeStruct((GB, 8 * seg, 2 * SL), F32)],
        compiler_params=_cparams(1), name="s5_prep")(lr, li, ldt, braw)


def _s5_prep_bwd(lr, li, ldt, braw, dbb, da):
    def body(lr_ref, li_ref, ldt_ref, b_ref, dbb_ref, da_ref, dbraw_ref, dlr_ref, dli_ref, dldt_ref):
        lr_, li_, ldt_ = lr_ref[0], li_ref[0], ldt_ref[0]
        dt, mag, th, ar, ai, den, nr, cr, ci = _s5_coefs(lr_, li_, ldt_)
        br, bi = b_ref[0, :, :SL], b_ref[0, :, SL:]
        gbr, gbi = dbb_ref[0, :, :SL], dbb_ref[0, :, SL:]
        dbraw_ref[0, :, :SL] = cr * gbr + ci * gbi
        dbraw_ref[0, :, SL:] = cr * gbi - ci * gbr
        dcr = jnp.sum(gbr * br + gbi * bi, axis=0, keepdims=True)
        dci = jnp.sum(gbi * br - gbr * bi, axis=0, keepdims=True)
        dar = jnp.sum(da_ref[0, :, :SL], axis=0, keepdims=True)
        dai = jnp.sum(da_ref[0, :, SL:], axis=0, keepdims=True)
        g1, g2 = dcr / den, dci / den
        gden = -(dcr * cr + dci * ci) / den
        gar = dar + g1 * lr_ - g2 * li_
        gai = dai + g1 * li_ + g2 * lr_
        glr = g1 * nr + g2 * ai + 2.0 * lr_ * gden
        gli = g1 * ai - g2 * nr + 2.0 * li_ * gden
        gmag = gar * jnp.cos(th) + gai * jnp.sin(th)
        gth = gai * ar - gar * ai
        dlr_ref[0] = glr + gmag * mag * dt
        dli_ref[0] = gli + gth * dt
        dldt_ref[0] = (gmag * mag * lr_ + gth * li_) * dt

    vec = pl.BlockSpec((1, 1, SL), lambda k: (k, 0, 0))
    mat = pl.BlockSpec((1, LANES, 2 * SL), lambda k: (k, 0, 0))
    return pl.pallas_call(
        body, grid=(GB,), in_specs=[vec, vec, vec, mat, mat, pl.BlockSpec((1, 8, 2 * SL), lambda k: (k, 0, 0))],
        out_specs=[mat, vec, vec, vec],
        out_shape=[jax.ShapeDtypeStruct((GB, LANES, 2 * SL), F32)] + [jax.ShapeDtypeStruct((GB, 1, SL), F32)] * 3,
        compiler_params=_cparams(1), name="s5_prep_bwd")(lr, li, ldt, braw, dbb, da)


def _bcast_row(tile, j):
    return jnp.broadcast_to(tile[j:j + 1, :], tile.shape)


def _tile_rows(i):
    return pl.ds(pl.multiple_of(i * 8, 8), 8)


def _col(c):
    return slice(c * LANES, (c + 1) * LANES)


def _permute_rows(ref, seg):
    return jnp.concatenate([ref[pl.ds(i, 8, stride=seg), :] for i in range(seg)], axis=0)


def _unpermute_rows(src, dst, seg):
    for j in range(8):
        dst[j * seg:(j + 1) * seg, :] = src[pl.ds(j, seg, stride=8), :]


SCAN_UNROLL = 2


def _segment_scan(src, dst, ap, seg, reverse):
    half = NCOL // 2
    sign = -1.0 if reverse else 1.0
    a_r = [ap[0:8, _col(c)] for c in range(half)]
    a_i = [ap[0:8, _col(half + c)] for c in range(half)]

    def step(n, carry):
        i = seg - 1 - n if reverse else n
        rows = _tile_rows(i)
        out_r, out_i = [], []
        for c in range(half):
            xr, xi = carry[c], carry[half + c]
            nr = a_r[c] * xr - sign * a_i[c] * xi + src[rows, _col(c)]
            ni = a_r[c] * xi + sign * a_i[c] * xr + src[rows, _col(half + c)]
            dst[rows, _col(c)] = nr
            dst[rows, _col(half + c)] = ni
            out_r.append(nr)
            out_i.append(ni)
        return tuple(out_r + out_i)

    zero = jnp.zeros((8, LANES), F32)
    return lax.fori_loop(0, seg, step, (zero,) * NCOL, unroll=SCAN_UNROLL)


def _segment_entries(ends, cin, ap, seg, reverse):
    half = NCOL // 2
    sign = -1.0 if reverse else 1.0
    al_r = [ap[(seg - 1) * 8:seg * 8, _col(c)] for c in range(half)]
    al_i = [ap[(seg - 1) * 8:seg * 8, _col(half + c)] for c in range(half)]
    row = lax.broadcasted_iota(jnp.int32, (8, LANES), 0)
    ent, out = [None] * NCOL, [None] * NCOL
    for c in range(half):
        zr, zi = cin[c], cin[half + c]
        er, ei = jnp.zeros((8, LANES), F32), jnp.zeros((8, LANES), F32)
        for j in (range(7, -1, -1) if reverse else range(8)):
            er, ei = jnp.where(row == j, zr, er), jnp.where(row == j, zi, ei)
            fr, fi = _bcast_row(ends[c], j), _bcast_row(ends[half + c], j)
            zr, zi = (al_r[c] * zr - sign * al_i[c] * zi + fr, al_r[c] * zi + sign * al_i[c] * zr + fi)
        ent[c], ent[half + c] = er, ei
        out[c], out[half + c] = zr, zi
    return ent, out


def _chunk_states(u, bb_ref, ap, cin, bu_sc, x_sc, seg):
    half = NCOL // 2
    bu_sc[...] = _mm(u, bb_ref[0])
    ends = _segment_scan(bu_sc, x_sc, ap, seg, False)
    ent, out = _segment_entries(ends, cin, ap, seg, False)

    def fix(i, _):
        rows = _tile_rows(i)
        for c in range(half):
            pr, pi = ap[rows, _col(c)], ap[rows, _col(half + c)]
            x_sc[rows, _col(c)] += pr * ent[c] - pi * ent[half + c]
            x_sc[rows, _col(half + c)] += pr * ent[half + c] + pi * ent[c]
        return 0

    lax.fori_loop(0, seg, fix, 0, unroll=SCAN_UNROLL)
    return out


def _s5_scan_fwd(u, bb, apow, cblk, dskip, t_chunk):
    s_len = u.shape[0]
    nc = s_len // t_chunk
    seg = t_chunk // 8

    def body(u_ref, bb_ref, ap_ref, c_ref, d_ref, y_ref, xin_ref, bu_sc, x_sc, y_sc, carry_sc):
        @pl.when(pl.program_id(1) == 0)
        def _():
            carry_sc[...] = jnp.zeros(carry_sc.shape, F32)

        uu = _permute_rows(u_ref, seg)
        cin = [carry_sc[:, _col(c)] for c in range(NCOL)]
        xin_ref[0, 0] = carry_sc[...]
        out = _chunk_states(uu, bb_ref, ap_ref.at[0], cin, bu_sc, x_sc, seg)
        for c in range(NCOL):
            carry_sc[:, _col(c)] = out[c]
        y_sc[...] = _mm(x_sc[...], c_ref[0]) + d_ref[...] * uu
        _unpermute_rows(y_sc, y_ref, seg)

    in_specs = [pl.BlockSpec((t_chunk, LANES), lambda k, c: (c, k)),
                pl.BlockSpec((1, LANES, 2 * SL), lambda k, c: (k, 0, 0)),
                pl.BlockSpec((1, 8 * seg, 2 * SL), lambda k, c: (k, 0, 0)),
                pl.BlockSpec((1, 2 * SL, LANES), lambda k, c: (k, 0, 0)),
                pl.BlockSpec((1, LANES), lambda k, c: (0, k))]
    out_specs = [pl.BlockSpec((t_chunk, LANES), lambda k, c: (c, k)),
                 pl.BlockSpec((1, 1, 8, 2 * SL), lambda k, c: (k, c, 0, 0))]
    return pl.pallas_call(body, grid=(GB, nc), in_specs=in_specs, out_specs=out_specs,
                          out_shape=[jax.ShapeDtypeStruct((s_len, D), F32), jax.ShapeDtypeStruct((GB, nc, 8, 2 * SL), F32)],
                          scratch_shapes=[pltpu.VMEM((t_chunk, 2 * SL), F32), pltpu.VMEM((t_chunk, 2 * SL), F32),
                                          pltpu.VMEM((t_chunk, LANES), F32), pltpu.VMEM((8, 2 * SL), F32)],
                          compiler_params=_cparams(2), name="s5_scan_fwd")(u, bb, apow, cblk, dskip)


def _s5_scan_bwd(u, dy, xin, bb, bb_t, apow, cblk_t, dskip, t_chunk):
    s_len = u.shape[0]
    nc = s_len // t_chunk
    seg = t_chunk // 8
    half = NCOL // 2

    def body(u_ref, dy_ref, xin_ref, bb_ref, bbt_ref, ap_ref, ct_ref, d_ref, du_ref, dbb_ref, dc_ref, da_ref, dd_ref,
             bu_sc, x_sc, g_sc, y_sc, carry_sc):
        @pl.when(pl.program_id(1) == 0)
        def _():
            carry_sc[...] = jnp.zeros(carry_sc.shape, F32)
            dbb_ref[...] = jnp.zeros(dbb_ref.shape, F32)
            dc_ref[...] = jnp.zeros(dc_ref.shape, F32)
            da_ref[...] = jnp.zeros(da_ref.shape, F32)
            dd_ref[...] = jnp.zeros(dd_ref.shape, F32)

        ap = ap_ref.at[0]
        uu, dyy = _permute_rows(u_ref, seg), _permute_rows(dy_ref, seg)
        cin = [xin_ref[0, 0, :, _col(c)] for c in range(NCOL)]
        _chunk_states(uu, bb_ref, ap, cin, bu_sc, x_sc, seg)
        bu_sc[...] = _mm(dyy, ct_ref[0])
        ends = _segment_scan(bu_sc, g_sc, ap, seg, True)
        lam_in = [carry_sc[:, _col(c)] for c in range(NCOL)]
        ent, out = _segment_entries(ends, lam_in, ap, seg, True)
        for c in range(NCOL):
            carry_sc[:, _col(c)] = out[c]
        row = lax.broadcasted_iota(jnp.int32, (8, LANES), 0)
        xp0 = [jnp.where(row == 0, cin[c], pltpu.roll(x_sc[(seg - 1) * 8:seg * 8, _col(c)], 1, 0)) for c in range(NCOL)]

        def fix(i, acc):
            rows, prev, tab = _tile_rows(i), _tile_rows(jnp.maximum(i - 1, 0)), _tile_rows(seg - 1 - i)
            new = list(acc)
            for c in range(half):
                pr, pi = ap[tab, _col(c)], ap[tab, _col(half + c)]
                lr_ = g_sc[rows, _col(c)] + pr * ent[c] + pi * ent[half + c]
                li_ = g_sc[rows, _col(half + c)] + pr * ent[half + c] - pi * ent[c]
                g_sc[rows, _col(c)] = lr_
                g_sc[rows, _col(half + c)] = li_
                xr = jnp.where(i == 0, xp0[c], x_sc[prev, _col(c)])
                xi = jnp.where(i == 0, xp0[half + c], x_sc[prev, _col(half + c)])
                new[c] = acc[c] + lr_ * xr + li_ * xi
                new[half + c] = acc[half + c] + li_ * xr - lr_ * xi
            return tuple(new)

        zero = jnp.zeros((8, LANES), F32)
        dacc = lax.fori_loop(0, seg, fix, (zero,) * NCOL, unroll=SCAN_UNROLL)
        for c in range(NCOL):
            da_ref[0, :, _col(c)] += dacc[c]
        lam = g_sc[...]
        y_sc[...] = _mm(lam, bbt_ref[0]) + d_ref[...] * dyy
        _unpermute_rows(y_sc, du_ref, seg)
        dbb_ref[0] += _mm_tn(uu, lam)
        dc_ref[0] += _mm_tn(x_sc[...], dyy)
        dd_ref[0] += _stack_rows([jnp.sum(dyy * uu, axis=0, keepdims=True)])

    rev = lambda k, c: (nc - 1 - c, k)
    in_specs = [pl.BlockSpec((t_chunk, LANES), rev), pl.BlockSpec((t_chunk, LANES), rev),
                pl.BlockSpec((1, 1, 8, 2 * SL), lambda k, c: (k, nc - 1 - c, 0, 0)),
                pl.BlockSpec((1, LANES, 2 * SL), lambda k, c: (k, 0, 0)),
                pl.BlockSpec((1, 2 * SL, LANES), lambda k, c: (k, 0, 0)),
                pl.BlockSpec((1, 8 * seg, 2 * SL), lambda k, c: (k, 0, 0)),
                pl.BlockSpec((1, LANES, 2 * SL), lambda k, c: (k, 0, 0)),
                pl.BlockSpec((1, LANES), lambda k, c: (0, k))]
    out_specs = [pl.BlockSpec((t_chunk, LANES), rev),
                 pl.BlockSpec((1, LANES, 2 * SL), lambda k, c: (k, 0, 0)),
                 pl.BlockSpec((1, 2 * SL, LANES), lambda k, c: (k, 0, 0)),
                 pl.BlockSpec((1, 8, 2 * SL), lambda k, c: (k, 0, 0)),
                 pl.BlockSpec((1, 8, LANES), lambda k, c: (k, 0, 0))]
    out_shape = [jax.ShapeDtypeStruct((s_len, D), F32), jax.ShapeDtypeStruct((GB, LANES, 2 * SL), F32),
                 jax.ShapeDtypeStruct((GB, 2 * SL, LANES), F32), jax.ShapeDtypeStruct((GB, 8, 2 * SL), F32),
                 jax.ShapeDtypeStruct((GB, 8, LANES), F32)]
    return pl.pallas_call(body, grid=(GB, nc), in_specs=in_specs, out_specs=out_specs, out_shape=out_shape,
                          scratch_shapes=[pltpu.VMEM((t_chunk, 2 * SL), F32)] * 3 + [pltpu.VMEM((t_chunk, LANES), F32),
                                                                                        pltpu.VMEM((8, 2 * SL), F32)],
                          compiler_params=_cparams(2), name="s5_scan_bwd")(u, dy, xin, bb, bb_t, apow, cblk_t, dskip)


_GELU_K = math.sqrt(2.0 / math.pi)


def _gelu(y):
    t = jnp.tanh(_GELU_K * (y + 0.044715 * (y * y * y)))
    return 0.5 * y * (1.0 + t), 0.5 * (1.0 + t) + 0.5 * y * (1.0 - t * t) * (_GELU_K * (1.0 + 3 * 0.044715 * (y * y)))


def _s5_in_fwd(h, gmix, win, tm):
    def fn(h, gmix, win):
        return (_mm(_rms_fwd(h, gmix)[0], win),), ()

    return _rowcall("s5_in_fwd", fn, tm, [h], [gmix, win], [(D, F32)], [])[0][0]


def _s5_in_bwd(h, du, dh, gmix, win_t, tm):
    def fn(h, du, dh, gmix, win_t):
        hn, xh, r = _rms_fwd(h, gmix)
        dx, dg = _rms_bwd(_mm(du, win_t), xh, r, gmix)
        return (dh + dx,), (_mm_tn(hn, du), dg)

    return _rowcall("s5_in_bwd", fn, tm, [h, du, dh], [gmix, win_t], [(D, F32)], [(D, D), (1, D)])


def _s5_out_fwd(h, y, wglu, tm):
    def fn(h, y, wglu):
        z = _mm(_gelu(y)[0], wglu)
        return (h + z[:, :D] * jax.nn.sigmoid(z[:, D:]),), ()

    return _rowcall("s5_out_fwd", fn, tm, [h, y], [wglu], [(D, F32)], [])[0][0]


def _s5_out_bwd(dh, y, wglu, wglu_t, tm):
    def fn(dh, y, wglu, wglu_t):
        yg, dgelu = _gelu(y)
        z = _mm(yg, wglu)
        val, sg = z[:, :D], jax.nn.sigmoid(z[:, D:])
        dz = jnp.concatenate([dh * sg, dh * val * sg * (1.0 - sg)], axis=1)
        return (_mm(dz, wglu_t) * dgelu,), (_mm_tn(yg, dz),)

    return _rowcall("s5_out_bwd", fn, tm, [dh, y], [wglu, wglu_t], [(D, F32)], [(D, 2 * D)])


def _final_loss(h, tgt, gfin, tm):
    def fn(h, tgt, gfin):
        y, xh, r = _rms_fwd(h, gfin)
        err = y - tgt
        dx, dg = _rms_bwd(err * (1.0 / D), xh, r, gfin)
        return (dx,), (jnp.sum(err * err, axis=0, keepdims=True), dg)

    return _rowcall("final_loss", fn, tm, [h, tgt], [gfin], [(D, F32)], [(1, D), (1, D)])


def _bf(a):
    return a.astype(MXU)


def _s5_block_mats(b_re, b_im, c_re, c_im):
    gl = G // GB
    eye = jnp.eye(gl, dtype=F32)

    def b_blk(b):
        bt = b.reshape(GB, gl, P, C).transpose(0, 1, 3, 2)
        return (bt[:, :, :, None, :] * eye[None, :, None, :, None]).reshape(GB, gl * C, gl * P)

    def c_blk(cm):
        ct = cm.reshape(GB, gl, C, P).transpose(0, 1, 3, 2)
        return (ct[:, :, :, None, :] * eye[None, :, None, :, None]).reshape(GB, gl * P, gl * C)

    braw = jnp.concatenate([b_blk(b_re), b_blk(b_im)], axis=2)
    cblk = jnp.concatenate([c_blk(c_re), -c_blk(c_im)], axis=1)
    return braw, cblk


def _s5_unblock_b(d):
    gl = G // GB
    eye = jnp.eye(gl, dtype=F32)
    d5 = d.reshape(GB, gl, C, gl, P)
    return jnp.sum(d5 * eye[None, :, None, :, None], axis=3).transpose(0, 1, 3, 2).reshape(G, P, C)


def _s5_unblock_c(d):
    gl = G // GB
    eye = jnp.eye(gl, dtype=F32)
    d5 = d.reshape(GB, gl, P, gl, C)
    return jnp.sum(d5 * eye[None, :, None, :, None], axis=3).transpose(0, 1, 3, 2).reshape(G, C, P)


def _tiles(s_len):
    return (min(256, s_len), min(512, s_len), min(1024, s_len),
            (min(512, s_len), min(2048, s_len)), (min(1024, s_len), min(1024, s_len)),
            min(256, s_len), min(256, s_len))


def _merge_shards(g, axis):
    moved = jnp.moveaxis(g, 0, axis)
    return moved.reshape(moved.shape[:axis] + (-1,) + moved.shape[axis + 2:])


def _split_shards(full, axis):
    cut = full.reshape(full.shape[:axis] + (NDEV, -1) + full.shape[axis + 1:])
    return jnp.moveaxis(cut, axis, 0)


def _sequence_step(x, pos, tgt, w, late=()):
    s_len = x.shape[0]
    tm_w, tm, t_chunk, t_fwd, t_bwd, tm_ff, tm_fb = _tiles(s_len)
    seg = t_chunk // 8
    row = lambda v: v.reshape(1, -1)

    wa = _bf(jnp.pad(w["mla_w_a"][0], ((0, 0), (0, AW - (QL + KVL + ROPE)))))
    wuq = _bf(jnp.pad(w["mla_w_uq"][0].reshape(QL, HEADS, NOPE + ROPE), ((0, 0), (0, 0), (0, HC - NOPE - ROPE))).reshape(QL, HEADS * HC))
    wukv = _bf(w["mla_w_ukv"][0])
    inv = 1.0 / (ROPE_THETA ** (jnp.arange(0, ROPE, 2, dtype=F32) / ROPE))
    invf = jnp.concatenate([inv, inv, jnp.zeros((LANES - ROPE,), F32)]).reshape(1, LANES)
    gmix0, gmix1 = row(w["g_mix"][0]), row(w["g_mix"][1])
    gq, gkv = row(w["mla_g_q"][0]), row(w["mla_g_kv"][0])
    lr = w["ssm_lambda_re"][0].reshape(GB, 1, SL)
    li = w["ssm_lambda_im"][0].reshape(GB, 1, SL)
    ldt = jnp.broadcast_to(w["ssm_log_dt"][0][:, None], (G, P)).reshape(GB, 1, SL)
    braw, cblk = _s5_block_mats(w["ssm_b_re"][0], w["ssm_b_im"][0], w["ssm_c_re"][0], w["ssm_c_im"][0])
    cblk = _bf(cblk)

    qc, kc, v = _mla_front_fwd(x, pos, gmix0, wa, gq, gkv, wuq, wukv, invf, tm_w)
    o, lse, gathered = _flash_fwd(qc, kc, v, *t_fwd, [shard for _, _, shard in late])
    w = dict(w)
    for (n, axis, _), g in zip(late, gathered):
        w[n] = _merge_shards(g, axis)
    wo = _bf(w["mla_w_o"][0])
    win, wglu = _bf(w["ssm_w_in"][0]), _bf(w["ssm_w_glu"][0])
    dskip = row(w["ssm_d"][0])
    ffn = []
    for l in range(2):
        wup, wdown = _bf(w["ffn_w_up"][l]), _bf(w["ffn_w_down"][l])
        ffn.append(dict(g=row(w["g_ffn"][l]), wup=wup, wup_t=wup.T, wdown=wdown, wdown_t=wdown.T,
                        cw=w["ffn_conv_w"][l], cb=row(w["ffn_conv_b"][l])))
    h1 = _attn_out_fwd(x, o, wo, tm)
    f0 = ffn[0]
    h2, *saved0 = _ffn_fwd(h1, f0["g"], f0["wup"], f0["cw"], f0["cb"], f0["wdown"], tm_ff)
    bb, apow = _s5_prep(lr, li, ldt, braw, seg)
    u = _s5_in_fwd(h2, gmix1, win, tm)
    y, xin = _s5_scan_fwd(u, bb, apow, cblk, dskip, t_chunk)
    h3 = _s5_out_fwd(h2, y, wglu, tm)
    f1 = ffn[1]
    h4, *saved1 = _ffn_fwd(h3, f1["g"], f1["wup"], f1["cw"], f1["cb"], f1["wdown"], tm_ff)
    (dh4,), (sq, d_gfinal) = _final_loss(h4, tgt, row(w["g_final"]), tm)
    loss = 0.5 * jnp.sum(sq) / D

    grads = {}

    def ffn_back(hin, dout, saved, f):
        act, upv, upg = saved
        din, dupv, dupg, hn, cacc, dg = _ffn_bwd(hin, dout, upv, upg, f["g"], f["cw"], f["cb"], f["wdown_t"], f["wup_t"], tm_fb)
        d_wup = jnp.concatenate([_matmul_tn(hn, dupv, DFF, min(512, s_len), "ffn_dwup_v"),
                                 _matmul_tn(hn, dupg, DFF, min(512, s_len), "ffn_dwup_g")], axis=1)
        d_wdown = _matmul_tn(act, dout, D, min(512, s_len), "ffn_dwdown")
        cflat = cacc.transpose(1, 0, 2).reshape(8, 2 * DFF)
        return din, d_wup, d_wdown, cflat[:3], cflat[3], dg[0]

    dh3, d_wup1, d_wdown1, d_cw1, d_cb1, d_gffn1 = ffn_back(h3, dh4, saved1, f1)
    (dy,), (d_wglu,) = _s5_out_bwd(dh3, y, wglu, wglu.T, tm)
    du, d_bb, d_cblk, d_a, d_dsk = _s5_scan_bwd(u, dy, xin, bb, bb.transpose(0, 2, 1), apow, cblk.transpose(0, 2, 1), dskip, t_chunk)
    d_braw, d_lr, d_li, d_ldt = _s5_prep_bwd(lr, li, ldt, braw, d_bb, d_a)
    (dh2,), (d_win, d_gmix1) = _s5_in_bwd(h2, du, dh3, gmix1, win.T, tm)
    dh1, d_wup0, d_wdown0, d_cw0, d_cb0, d_gffn0 = ffn_back(h1, dh2, saved0, f0)
    (do, delta), (d_wo,) = _attn_out_bwd(dh1, o, wo.T, tm)

    grads["mla_w_o"] = d_wo[None]
    grads["ssm_w_in"] = d_win[None]
    grads["ssm_lambda_re"] = d_lr.reshape(1, G, P)
    grads["ssm_lambda_im"] = d_li.reshape(1, G, P)
    grads["ssm_log_dt"] = jnp.sum(d_ldt.reshape(G, P), axis=1)[None]
    grads["ssm_b_re"] = _s5_unblock_b(d_braw[:, :, :SL])[None]
    grads["ssm_b_im"] = _s5_unblock_b(d_braw[:, :, SL:])[None]
    grads["ssm_c_re"] = _s5_unblock_c(d_cblk[:, :SL, :])[None]
    grads["ssm_c_im"] = -_s5_unblock_c(d_cblk[:, SL:, :])[None]
    grads["ssm_d"] = jnp.sum(d_dsk, axis=1).reshape(1, D)
    grads["ssm_w_glu"] = d_wglu[None]
    grads["ffn_w_up"] = jnp.stack([d_wup0, d_wup1])
    grads["ffn_conv_w"] = jnp.stack([d_cw0, d_cw1])
    grads["ffn_conv_b"] = jnp.stack([d_cb0, d_cb1])
    grads["ffn_w_down"] = jnp.stack([d_wdown0, d_wdown1])
    grads["g_ffn"] = jnp.stack([d_gffn0, d_gffn1])
    grads["g_final"] = d_gfinal[0]

    sends = [_split_shards(grads[n], axis).astype(MXU) for n, axis, _ in late]
    dqc, dkc, dv, landed = _flash_bwd(qc, kc, v, do, lse, delta, *t_bwd, sends)
    (dx,), (d_wa, d_wuq, d_wukv, d_gq, d_gkv, d_gmix0) = _mla_front_bwd(
        x, pos, dqc, dkc, dv, dh1, gmix0, wa, gq, gkv, wuq, wukv, invf, wa.T, wuq.T, wukv.T, tm_w)
    grads["mla_w_a"] = d_wa[None, :, :QL + KVL + ROPE]
    grads["mla_g_q"] = d_gq
    grads["mla_g_kv"] = d_gkv
    grads["mla_w_uq"] = d_wuq.reshape(QL, HEADS, HC)[:, :, :NOPE + ROPE].reshape(1, QL, HEADS * (NOPE + ROPE))
    grads["mla_w_ukv"] = d_wukv[None]
    grads["g_mix"] = jnp.concatenate([d_gmix0, d_gmix1], axis=0)
    return loss, dx, grads, landed


MESH = pl.DeviceIdType.MESH
ANY = pl.BlockSpec(memory_space=pl.ANY)


def _gather_many(blocks, name):
    n = len(blocks)

    def body(*refs):
        _gather_issue(refs[:n], refs[n:2 * n], refs[2 * n:])
        _gather_finish(refs[:n], refs[n:2 * n], refs[2 * n:])

    return pl.pallas_call(body, out_shape=_gathered_shapes(blocks), in_specs=[ANY] * n, out_specs=[ANY] * n,
                          scratch_shapes=_comm_scratch(n), name=name)(*blocks)


def _comm_scratch(n):
    if n == 0:
        return []
    return [pltpu.SemaphoreType.DMA((7 * n,)), pltpu.SemaphoreType.DMA((7 * n,)), pltpu.SemaphoreType.DMA((n,))]


def _gathered_shapes(blocks):
    return [jax.ShapeDtypeStruct((NDEV,) + b.shape, b.dtype) for b in blocks]


def _gather_copies(ins, outs, sems):
    send_sems, recv_sems, local_sems = sems
    x, y, c = lax.axis_index("x"), lax.axis_index("y"), lax.axis_index("c")
    me, sibling = (x, y, c), (x, y, 1 - c)
    chips = [(1 - x, y), (x, 1 - y), (1 - x, 1 - y)]

    def copy(a, k, block, to, own=False):
        px, py, pc = block
        slot = outs[a].at[4 * px + 2 * py + pc]
        return pltpu.make_async_remote_copy(src_ref=ins[a] if own else slot, dst_ref=slot,
                                            send_sem=send_sems.at[7 * a + k], recv_sem=recv_sems.at[7 * a + k],
                                            device_id=to, device_id_type=MESH)

    mine = [pltpu.make_async_copy(ins[a], outs[a].at[4 * x + 2 * y + c], local_sems.at[a]) for a in range(len(ins))]
    first = []
    for a in range(len(ins)):
        first.append(copy(a, 0, me, sibling, own=True))
        first += [copy(a, 1 + j, me, (*chip, c), own=True) for j, chip in enumerate(chips)]
    return copy, mine, first, me, sibling, chips, c


def _gather_issue(ins, outs, sems):
    _, mine, first, *_ = _gather_copies(ins, outs, sems)
    for cp in mine + first:
        cp.start()


def _gather_finish(ins, outs, sems):
    copy, mine, first, me, sibling, chips, c = _gather_copies(ins, outs, sems)
    passed = []
    for j, chip in enumerate(chips):
        for a in range(len(ins)):
            copy(a, 1 + j, (*chip, c), me).wait_recv()
            passed.append(copy(a, 4 + j, (*chip, c), sibling))
            passed[-1].start()
    for a in range(len(ins)):
        copy(a, 0, sibling, me).wait_recv()
        for j, chip in enumerate(chips):
            copy(a, 4 + j, (*chip, 1 - c), me).wait_recv()
    for cp in first + passed:
        cp.wait_send()
    for cp in mine:
        cp.wait()


def _exchange_many(sends, casts, name):
    n, tot = len(sends), len(sends) + len(casts)

    def body(*refs):
        _exchange_issue(refs[:tot], refs[tot:2 * tot], refs[2 * tot:], n)
        _exchange_finish(refs[:tot], refs[tot:2 * tot], refs[2 * tot:], n)

    out = pl.pallas_call(body, out_shape=_exchanged_shapes(sends, casts), in_specs=[ANY] * tot, out_specs=[ANY] * tot,
                         scratch_shapes=_comm_scratch(tot), name=name)(*sends, *casts)
    return out[:n], out[n:]


def _exchanged_shapes(sends, casts):
    return [jax.ShapeDtypeStruct(s.shape, s.dtype) for s in sends] + _gathered_shapes(casts)


def _exchange_copies(ins, outs, sems, n, with_arrivals):
    send_sems, recv_sems, local_sems = sems
    x, y, c = lax.axis_index("x"), lax.axis_index("y"), lax.axis_index("c")
    me = 4 * x + 2 * y + c
    local, sent, arrivals = [], [], []
    for a in range(len(ins)):
        own = ins[a].at[me] if a < n else ins[a]
        local.append(pltpu.make_async_copy(own, outs[a].at[me], local_sems.at[a]))
        for m in range(1, NDEV):
            px = 1 - x if m & 4 else x
            py = 1 - y if m & 2 else y
            pc = 1 - c if m & 1 else c
            peer = 4 * px + 2 * py + pc
            to = dict(send_sem=send_sems.at[7 * a + m - 1], recv_sem=recv_sems.at[7 * a + m - 1],
                      device_id=(px, py, pc), device_id_type=MESH)
            sent.append(pltpu.make_async_remote_copy(src_ref=ins[a].at[peer] if a < n else ins[a], dst_ref=outs[a].at[me], **to))
            if with_arrivals:
                arrivals.append(pltpu.make_async_remote_copy(src_ref=own, dst_ref=outs[a].at[peer], **to))
    return local, sent, arrivals


def _exchange_issue(ins, outs, sems, n):
    local, sent, _ = _exchange_copies(ins, outs, sems, n, False)
    for cp in local + sent:
        cp.start()


def _exchange_finish(ins, outs, sems, n):
    local, sent, arrivals = _exchange_copies(ins, outs, sems, n, True)
    for cp in arrivals:
        cp.wait_recv()
    for cp in sent:
        cp.wait_send()
    for cp in local:
        cp.wait()


def _adam_math(w, parts, m, v):
    g = parts[0].astype(F32)
    for k in range(1, NDEV):
        g = g + parts[k].astype(F32)
    m2 = ADAM_B1 * m + (1.0 - ADAM_B1) * g
    v2 = ADAM_B2 * v + (1.0 - ADAM_B2) * jnp.square(g)
    m_hat = m2 / (1.0 - ADAM_B1 ** ADAM_STEP)
    v_hat = v2 / (1.0 - ADAM_B2 ** ADAM_STEP)
    return g, -ADAM_LR * (m_hat / (jnp.sqrt(v_hat) + ADAM_EPS) + ADAM_WD * w), m2, v2


def _adamw_many(ws, parts, ms, vs, name):
    n = len(ws)

    def body(*refs):
        w_refs, p_refs, m_refs, v_refs, outs = refs[:n], refs[n:2 * n], refs[2 * n:3 * n], refs[3 * n:4 * n], refs[4 * n:]
        for a in range(n):
            res = _adam_math(w_refs[a][...], [p_refs[a][k] for k in range(NDEV)], m_refs[a][...], v_refs[a][...])
            for o, val in zip(outs[4 * a:4 * a + 4], res):
                o[...] = val

    out = pl.pallas_call(body, out_shape=[jax.ShapeDtypeStruct(w.shape, F32) for w in ws for _ in range(4)],
                         compiler_params=pltpu.CompilerParams(vmem_limit_bytes=VMEM_LIMIT), name=name)(*ws, *parts, *ms, *vs)
    return [out[4 * a:4 * a + 4] for a in range(n)]


def _adamw_rows(w, parts, m, v, tr, name):
    rows, cols = w.shape

    def body(w_ref, p_ref, m_ref, v_ref, g_ref, d_ref, m2_ref, v2_ref):
        res = _adam_math(w_ref[...], [p_ref[k] for k in range(NDEV)], m_ref[...], v_ref[...])
        for o, val in zip((g_ref, d_ref, m2_ref, v2_ref), res):
            o[...] = val

    flat = pl.BlockSpec((tr, cols), lambda i: (i, 0))
    return pl.pallas_call(body, grid=(rows // tr,),
                          in_specs=[flat, pl.BlockSpec((NDEV, tr, cols), lambda i: (0, i, 0)), flat, flat],
                          out_specs=[flat] * 4, out_shape=[jax.ShapeDtypeStruct((rows, cols), F32)] * 4,
                          compiler_params=_cparams(1), name=name)(w, parts, m, v)


SHARDED = (("mla_w_a", 1), ("mla_w_uq", 2), ("mla_w_ukv", 2), ("mla_w_o", 1), ("ssm_w_in", 1), ("ssm_d", 1),
           ("ssm_w_glu", 2), ("ffn_w_up", 2), ("ffn_conv_w", 2), ("ffn_w_down", 1))
WIRE_EXACT = ("ssm_d", "ffn_conv_w")
GATHERED_FIRST = ("mla_w_a", "mla_w_uq", "mla_w_ukv")
REPLICATED = ("mla_g_q", "mla_g_kv", "ssm_lambda_re", "ssm_lambda_im", "ssm_log_dt", "ssm_b_re", "ssm_b_im",
              "ssm_c_re", "ssm_c_im", "ffn_conv_b", "g_mix", "g_ffn", "g_final")
WEIGHTS = ("mla_w_a", "mla_g_q", "mla_g_kv", "mla_w_uq", "mla_w_ukv", "mla_w_o", "ssm_w_in", "ssm_lambda_re",
           "ssm_lambda_im", "ssm_log_dt", "ssm_b_re", "ssm_b_im", "ssm_c_re", "ssm_c_im", "ssm_d", "ssm_w_glu",
           "ffn_w_up", "ffn_conv_w", "ffn_conv_b", "ffn_w_down", "g_mix", "g_ffn", "g_final")


ADAM_ROW_TILED = (("ffn_w_up", 256), ("ffn_w_down", 176))


def _two_d(shape):
    if len(shape) == 1:
        return (1, shape[0])
    if len(shape) > 2 and shape[-1] < LANES:
        return (math.prod(shape[:-2]), shape[-2] * shape[-1])
    return (math.prod(shape[:-1]), shape[-1])


def kernel(x, positions, mla_w_a, mla_g_q, mla_g_kv, mla_w_uq, mla_w_ukv, mla_w_o, ssm_w_in, ssm_lambda_re, ssm_lambda_im, ssm_log_dt, ssm_b_re, ssm_b_im, ssm_c_re, ssm_c_im, ssm_d, ssm_w_glu, ffn_w_up, ffn_conv_w, ffn_conv_b, ffn_w_down, g_mix, g_ffn, g_final, loss_target, m_mla_w_a, m_mla_g_q, m_mla_g_kv, m_mla_w_uq, m_mla_w_ukv, m_mla_w_o, m_ssm_w_in, m_ssm_lambda_re, m_ssm_lambda_im, m_ssm_log_dt, m_ssm_b_re, m_ssm_b_im, m_ssm_c_re, m_ssm_c_im, m_ssm_d, m_ssm_w_glu, m_ffn_w_up, m_ffn_conv_w, m_ffn_conv_b, m_ffn_w_down, m_g_mix, m_g_ffn, m_g_final, v_mla_w_a, v_mla_g_q, v_mla_g_kv, v_mla_w_uq, v_mla_w_ukv, v_mla_w_o, v_ssm_w_in, v_ssm_lambda_re, v_ssm_lambda_im, v_ssm_log_dt, v_ssm_b_re, v_ssm_b_im, v_ssm_c_re, v_ssm_c_im, v_ssm_d, v_ssm_w_glu, v_ffn_w_up, v_ffn_conv_w, v_ffn_conv_b, v_ffn_w_down, v_g_mix, v_g_ffn, v_g_final):
    a = dict(locals())
    s_len = x.shape[1]
    sh_names = [n for n, _ in SHARDED]

    def wire(n):
        return a[n] if n in WIRE_EXACT else a[n].astype(MXU)

    early = [(n, axis) for n, axis in SHARDED if n in GATHERED_FIRST]
    late = [(n, axis, wire(n)) for n, axis in SHARDED if n not in GATHERED_FIRST]
    w = {n: a[n] for n in REPLICATED}
    for (n, axis), g in zip(early, _gather_many([wire(n) for n, _ in early], "gather_weights")):
        w[n] = _merge_shards(g, axis)

    loss, dx, grads, landed_late = _sequence_step(x[0], positions.reshape(s_len, 1).astype(F32), loss_target[0], w, late)
    loss = lax.psum(loss, ("x", "y", "c"))

    rep_sizes = [math.prod(a[n].shape) for n in REPLICATED]
    packed = jnp.concatenate([grads[n].reshape(-1) for n in REPLICATED])
    landed_early, (rep_all,) = _exchange_many([_split_shards(grads[n], axis).astype(MXU) for n, axis in early], [packed],
                                              "exchange_grads")
    parts_of = dict(zip([n for n, _, _ in late], landed_late)) | dict(zip([n for n, _ in early], landed_early))
    off = 0
    for n, size in zip(REPLICATED, rep_sizes):
        parts_of[n] = rep_all[:, off:off + size]
        off += size

    def view(n, arr, lead=()):
        return arr.reshape(lead + _two_d(a[n].shape))

    out = {}

    def finish(n, res):
        for kind, val in zip(("grad_", "delta_", "new_m_", "new_v_"), res):
            out[kind + n] = val.reshape(a[n].shape)

    for n, tr in ADAM_ROW_TILED:
        finish(n, _adamw_rows(view(n, a[n]), view(n, parts_of[n], (NDEV,)), view(n, a["m_" + n]), view(n, a["v_" + n]),
                              tr, "adamw_" + n))
    tiled = [n for n, _ in ADAM_ROW_TILED]
    for names, tag in (([n for n in sh_names if n not in tiled], "adamw_sharded"), (list(REPLICATED), "adamw_replicated")):
        res = _adamw_many([view(n, a[n]) for n in names], [view(n, parts_of[n], (NDEV,)) for n in names],
                          [view(n, a["m_" + n]) for n in names], [view(n, a["v_" + n]) for n in names], tag)
        for n, r in zip(names, res):
            finish(n, r)
    return (loss, dx[None], *[out[kind + n] for kind in ("grad_", "delta_", "new_m_", "new_v_") for n in WEIGHTS])
```

```python
import functools
import math

import jax
import jax.numpy as jnp
from jax import lax
from jax.experimental import pallas as pl
from jax.experimental.pallas import tpu as pltpu

F32 = jnp.float32
MXU = jnp.bfloat16

D = 1024
HEADS = 8
NOPE = 128
ROPE = 64
VH = 128
QL = 384
KVL = 256
CHUNK = 64
ROPE_THETA = 10000.0
EPS = 1e-6
G, P, C = 64, 64, 16
DFF = 2816
ADAM_LR, ADAM_B1, ADAM_B2, ADAM_EPS, ADAM_WD, ADAM_STEP = 0.001, 0.9, 0.999, 1e-08, 0.01, 10

LANES = 128
AW = 768
HC = 256
GB = 8
SL = (G // GB) * P
NCOL = 2 * SL // LANES
TC = DFF
HALO = 16
NDEV = 8
VMEM_LIMIT = 56 * 1024 * 1024


def _mm(a, b):
    return jnp.dot(a.astype(MXU), b.astype(MXU), preferred_element_type=F32)


def _mm_tn(a, b):
    return lax.dot_general(a.astype(MXU), b.astype(MXU), (((0,), (0,)), ((), ())), preferred_element_type=F32)


def _mm_nt(a, b):
    return lax.dot_general(a.astype(MXU), b.astype(MXU), (((1,), (1,)), ((), ())), preferred_element_type=F32)


def _rms_fwd(x, g):
    r = lax.rsqrt(jnp.mean(x * x, axis=-1, keepdims=True) + EPS)
    xh = x * r
    return xh * g, xh, r


def _rms_bwd(dy, xh, r, g):
    dxh = dy * g
    dx = r * (dxh - xh * jnp.mean(dxh * xh, axis=-1, keepdims=True))
    return dx, jnp.sum(dy * xh, axis=0, keepdims=True)


def _rot_partner(b):
    lane = lax.broadcasted_iota(jnp.int32, b.shape, 1)
    return jnp.where(lane < ROPE // 2, -pltpu.roll(b, LANES - ROPE // 2, 1), pltpu.roll(b, ROPE // 2, 1))


def _rope_blk(b, cos2, sin2):
    return b * cos2 + _rot_partner(b) * sin2


def _unrope_blk(db, cos2, sin2):
    return db * cos2 - _rot_partner(db * sin2)


def _cparams(n_axes, vmem=VMEM_LIMIT):
    return pltpu.CompilerParams(dimension_semantics=("arbitrary",) * n_axes, vmem_limit_bytes=vmem)


def _rowcall(name, fn, tm, row_ins, consts, row_outs, acc_outs):
    n = row_ins[0].shape[0]
    n_in = len(row_ins) + len(consts)
    n_ro = len(row_outs)

    def body(*refs):
        ins, ro_refs, acc_refs = refs[:n_in], refs[n_in:n_in + n_ro], refs[n_in + n_ro:]
        ro, ao = fn(*[r[...] for r in ins])

        @pl.when(pl.program_id(0) == 0)
        def _():
            for r in acc_refs:
                r[...] = jnp.zeros(r.shape, r.dtype)

        for r, val in zip(ro_refs, ro):
            r[...] = val.astype(r.dtype)
        for r, val in zip(acc_refs, ao):
            r[...] += val

    in_specs = [pl.BlockSpec((tm, a.shape[1]), lambda i: (i, 0)) for a in row_ins]
    in_specs += [pl.BlockSpec(c.shape, lambda i, nd=c.ndim: (0,) * nd) for c in consts]
    out_specs = [pl.BlockSpec((tm, w), lambda i: (i, 0)) for w, _ in row_outs]
    out_specs += [pl.BlockSpec(s, lambda i, nd=len(s): (0,) * nd) for s in acc_outs]
    out_shape = [jax.ShapeDtypeStruct((n, w), dt) for w, dt in row_outs]
    out_shape += [jax.ShapeDtypeStruct(s, F32) for s in acc_outs]
    out = pl.pallas_call(body, grid=(n // tm,), in_specs=in_specs, out_specs=out_specs, out_shape=out_shape,
                         compiler_params=_cparams(1), name=name)(*row_ins, *consts)
    return list(out[:n_ro]), list(out[n_ro:])


def _mla_front_tile(x, pos, gmix, wa, gq, gkv, wuq, wukv, invf):
    hn, xh, r = _rms_fwd(x, gmix)
    a = _mm(hn, wa)
    cq, ckv, krb = a[:, :QL], a[:, QL:QL + KVL], a[:, QL + KVL:]
    cqn, cqh, rq = _rms_fwd(cq, gq)
    ckvn, ckvh, rkv = _rms_fwd(ckv, gkv)
    q = _mm(cqn, wuq)
    kv = _mm(ckvn, wukv)
    ang = pos * invf
    cos2, sin2 = jnp.cos(ang), jnp.sin(ang)
    krr = _rope_blk(krb, cos2, sin2)
    qp, kp, vp = [], [], []
    for h in range(HEADS):
        qp += [q[:, h * HC:h * HC + NOPE], _rope_blk(q[:, h * HC + NOPE:(h + 1) * HC], cos2, sin2)]
        kp += [kv[:, h * HC:h * HC + NOPE], krr]
        vp += [kv[:, h * HC + NOPE:(h + 1) * HC]]
    res = (hn, xh, r, cqn, cqh, rq, ckvn, ckvh, rkv, cos2, sin2)
    return jnp.concatenate(qp, axis=1), jnp.concatenate(kp, axis=1), jnp.concatenate(vp, axis=1), res


def _mla_front_fwd(x, pos, gmix, wa, gq, gkv, wuq, wukv, invf, tm):
    def fn(*args):
        qc, kc, v, _ = _mla_front_tile(*args)
        return (qc * Q_PRESCALE, kc, v), ()

    return _rowcall("mla_front_fwd", fn, tm, [x, pos], [gmix, wa, gq, gkv, wuq, wukv, invf],
                    [(HEADS * HC, MXU), (HEADS * HC, MXU), (HEADS * VH, MXU)], [])[0]


def _mla_front_bwd(x, pos, dqc, dkc, dv, dh, gmix, wa, gq, gkv, wuq, wukv, invf, wa_t, wuq_t, wukv_t, tm):
    def fn(x, pos, dqc, dkc, dv, dh, gmix, wa, gq, gkv, wuq, wukv, invf, wa_t, wuq_t, wukv_t):
        _, _, _, (hn, xh, r, cqn, cqh, rq, ckvn, ckvh, rkv, cos2, sin2) = _mla_front_tile(
            x, pos, gmix, wa, gq, gkv, wuq, wukv, invf)
        dqc, dkc, dv = dqc * SM_SCALE, dkc.astype(F32), dv.astype(F32)
        dqp, dkvp = [], []
        dkr = jnp.zeros((x.shape[0], LANES), F32)
        for h in range(HEADS):
            dqp += [dqc[:, h * HC:h * HC + NOPE], _unrope_blk(dqc[:, h * HC + NOPE:(h + 1) * HC], cos2, sin2)]
            dkvp += [dkc[:, h * HC:h * HC + NOPE], dv[:, h * VH:(h + 1) * VH]]
            dkr = dkr + dkc[:, h * HC + NOPE:(h + 1) * HC]
        dq = jnp.concatenate(dqp, axis=1)
        dkv = jnp.concatenate(dkvp, axis=1)
        dkrb = _unrope_blk(dkr, cos2, sin2)
        dcqn = _mm(dq, wuq_t)
        dckvn = _mm(dkv, wukv_t)
        d_wuq = _mm_tn(cqn, dq)
        d_wukv = _mm_tn(ckvn, dkv)
        dcq, d_gq = _rms_bwd(dcqn, cqh, rq, gq)
        dckv, d_gkv = _rms_bwd(dckvn, ckvh, rkv, gkv)
        da = jnp.concatenate([dcq, dckv, dkrb], axis=1)
        d_wa = _mm_tn(hn, da)
        dhn = _mm(da, wa_t)
        dx, d_gmix = _rms_bwd(dhn, xh, r, gmix)
        return (dh + dx,), (d_wa, d_wuq, d_wukv, d_gq, d_gkv, d_gmix)

    return _rowcall("mla_front_bwd", fn, tm, [x, pos, dqc, dkc, dv, dh],
                    [gmix, wa, gq, gkv, wuq, wukv, invf, wa_t, wuq_t, wukv_t], [(D, F32)],
                    [(D, AW), (QL, HEADS * HC), (KVL, HEADS * HC), (1, QL), (1, KVL), (1, D)])


SM_SCALE = (NOPE + ROPE) ** -0.5
LOG2E = 1.0 / math.log(2.0)
Q_PRESCALE = SM_SCALE * LOG2E


def _pair_tables(s_len, tq, tk, q_major):
    pairs = [(qi, ki) for qi in range(s_len // tq) for ki in range(s_len // tk) if ki * tk < (qi + 1) * tq]
    if not q_major:
        pairs.sort(key=lambda p: (p[1], p[0]))
    return (jnp.asarray([p[0] for p in pairs], jnp.int32), jnp.asarray([p[1] for p in pairs], jnp.int32))


def _last_key_tile(qi, tq, tk):
    return ((qi + 1) * tq - 1) // tk


def _visible(qi, ki, tq, tk, width=None):
    shape = (tq, tk if width is None else width)
    row = qi * (tq // CHUNK) + lax.broadcasted_iota(jnp.int32, shape, 0) // CHUNK
    col = ki * (tk // CHUNK) + lax.broadcasted_iota(jnp.int32, shape, 1) // CHUNK
    return col <= row


def _masked_and_not(qi, ki, tq, tk, fn):
    needs_mask = (ki + 1) * tk > qi * tq
    pl.when(needs_mask)(lambda: fn(True))
    pl.when(jnp.logical_not(needs_mask))(lambda: fn(False))


def _by_visible_width(qi, ki, tq, tk, fn):
    if tk % tq or tk == tq:
        _masked_and_not(qi, ki, tq, tk, lambda masked: fn(tk, masked))
        return
    blocks = tk // tq
    seen = qi + 1 - ki * blocks
    pl.when(seen > blocks)(lambda: fn(tk, False))
    for j in range(1, blocks + 1):
        pl.when(seen == j)(functools.partial(fn, j * tq, True))


def _first_and_last_step(n_steps):
    first = jnp.logical_and(pl.program_id(0) == 0, pl.program_id(1) == 0)
    last = jnp.logical_and(pl.program_id(0) == HEADS - 1, pl.program_id(1) == n_steps - 1)
    return first, last


def _flash_fwd(qc, kc, v, tq, tk, riders):
    s_len = qc.shape[0]
    qt, kt = _pair_tables(s_len, tq, tk, True)
    nr = len(riders)

    def body(qt_ref, kt_ref, q_ref, k_ref, v_ref, *rest):
        r_in, (o_ref, lse_ref), r_out = rest[:nr], rest[nr:nr + 2], rest[nr + 2:2 * nr + 2]
        m_sc, l_sc, acc_sc = rest[2 * nr + 2:2 * nr + 5]
        sems = rest[2 * nr + 5:]
        p_id = pl.program_id(1)
        qi, ki = qt_ref[p_id], kt_ref[p_id]
        first, last = _first_and_last_step(qt.shape[0])
        if nr:
            pl.when(first)(lambda: _gather_issue(r_in, r_out, sems))

        @pl.when(ki == 0)
        def _():
            m_sc[...] = jnp.full(m_sc.shape, -jnp.inf, F32)
            l_sc[...] = jnp.zeros(l_sc.shape, F32)
            acc_sc[...] = jnp.zeros(acc_sc.shape, F32)

        def update(width, masked):
            s = _mm_nt(q_ref[...], k_ref[:width, :])
            if masked:
                s = jnp.where(_visible(qi, ki, tq, tk, width), s, -jnp.inf)
            m_old = m_sc[...]
            m_new = jnp.maximum(m_old, jnp.max(s, axis=1, keepdims=True))
            alpha = jnp.exp2(m_old - m_new)
            p = jnp.exp2(s - m_new)
            l_sc[...] = alpha * l_sc[...] + jnp.sum(p, axis=1, keepdims=True)
            acc_sc[...] = alpha * acc_sc[...] + _mm(p, v_ref[:width, :])
            m_sc[...] = m_new

        _by_visible_width(qi, ki, tq, tk, update)

        @pl.when(ki == _last_key_tile(qi, tq, tk))
        def _():
            l = l_sc[...]
            o_ref[...] = (acc_sc[...] / l).astype(o_ref.dtype)
            lse_ref[...] = jnp.broadcast_to(m_sc[...] + jnp.log2(l), lse_ref.shape)

        if nr:
            pl.when(last)(lambda: _gather_finish(r_in, r_out, sems))

    qmap = lambda h, p, qt, kt: (qt[p], h)
    kmap = lambda h, p, qt, kt: (kt[p], h)
    grid_spec = pltpu.PrefetchScalarGridSpec(
        num_scalar_prefetch=2, grid=(HEADS, qt.shape[0]),
        in_specs=[pl.BlockSpec((tq, HC), qmap), pl.BlockSpec((tk, HC), kmap), pl.BlockSpec((tk, VH), kmap)] + [ANY] * nr,
        out_specs=[pl.BlockSpec((tq, VH), qmap), pl.BlockSpec((tq, LANES), qmap)] + [ANY] * nr,
        scratch_shapes=[pltpu.VMEM((tq, 1), F32), pltpu.VMEM((tq, 1), F32), pltpu.VMEM((tq, VH), F32)] + _comm_scratch(nr))
    out = pl.pallas_call(body, grid_spec=grid_spec,
                         out_shape=[jax.ShapeDtypeStruct((s_len, HEADS * VH), MXU),
                                    jax.ShapeDtypeStruct((s_len, HEADS * LANES), F32)] + _gathered_shapes(riders),
                         compiler_params=_cparams(2), name="flash_fwd")(qt, kt, qc, kc, v, *riders)
    return out[0], out[1], out[2:]


def _tile_dscores(q, k, v, do, lse, delta, qi, ki, tq, tk, width, masked):
    p = jnp.exp2(_mm_nt(q, k) - lse[:, :1])
    if masked:
        p = jnp.where(_visible(qi, ki, tq, tk, width), p, 0.0)
    return p, p * (_mm_nt(do, v) - delta[:, :1])


def _flash_bwd(qc, kc, v, do, lse, delta, tq, tk, riders, casts=()):
    s_len = qc.shape[0]
    qt, kt = _pair_tables(s_len, tq, tk, False)
    nq = s_len // tq
    n_rows, nr = len(riders), len(riders) + len(casts)

    def body(qt_ref, kt_ref, q_ref, k_ref, v_ref, do_ref, lse_ref, dl_ref, *rest):
        r_in, (dq_ref, dk_ref, dv_ref), r_out = rest[:nr], rest[nr:nr + 3], rest[nr + 3:2 * nr + 3]
        dk_sc, dv_sc = rest[2 * nr + 3:2 * nr + 5]
        sems = rest[2 * nr + 5:]
        p_id = pl.program_id(1)
        qi, ki = qt_ref[p_id], kt_ref[p_id]
        rows = pl.ds(pl.multiple_of(qi * tq, tq), tq)
        first, last = _first_and_last_step(qt.shape[0])
        if nr:
            pl.when(first)(lambda: _exchange_issue(r_in, r_out, sems, n_rows))

        @pl.when(qi == (ki * tk) // tq)
        def _():
            dk_sc[...] = jnp.zeros(dk_sc.shape, F32)
            dv_sc[...] = jnp.zeros(dv_sc.shape, F32)

        @pl.when(ki == 0)
        def _():
            dq_ref[rows, :] = jnp.zeros((tq, HC), F32)

        def update(width, masked):
            q, k, do = q_ref[...], k_ref[:width, :], do_ref[...]
            p, ds = _tile_dscores(q, k, v_ref[:width, :], do, lse_ref[...], dl_ref[...], qi, ki, tq, tk, width, masked)
            ds = ds.astype(MXU)
            dv_sc[:width, :] += _mm_tn(p, do)
            dk_sc[:width, :] += _mm_tn(ds, q)
            dq_ref[rows, :] += _mm(ds, k)

        _by_visible_width(qi, ki, tq, tk, update)

        @pl.when(qi == nq - 1)
        def _():
            dk_ref[...] = (dk_sc[...] * (1.0 / LOG2E)).astype(dk_ref.dtype)
            dv_ref[...] = dv_sc[...].astype(dv_ref.dtype)

        if nr:
            pl.when(last)(lambda: _exchange_finish(r_in, r_out, sems, n_rows))

    qmap = lambda h, p, qt, kt: (qt[p], h)
    kmap = lambda h, p, qt, kt: (kt[p], h)
    grid_spec = pltpu.PrefetchScalarGridSpec(
        num_scalar_prefetch=2, grid=(HEADS, qt.shape[0]),
        in_specs=[pl.BlockSpec((tq, HC), qmap), pl.BlockSpec((tk, HC), kmap), pl.BlockSpec((tk, VH), kmap),
                  pl.BlockSpec((tq, VH), qmap), pl.BlockSpec((tq, LANES), qmap), pl.BlockSpec((tq, LANES), qmap)] + [ANY] * nr,
        out_specs=[pl.BlockSpec((s_len, HC), lambda h, p, qt, kt: (0, h), pipeline_mode=pl.Buffered(1)),
                   pl.BlockSpec((tk, HC), kmap), pl.BlockSpec((tk, VH), kmap)] + [ANY] * nr,
        scratch_shapes=[pltpu.VMEM((tk, HC), F32), pltpu.VMEM((tk, VH), F32)] + _comm_scratch(nr))
    out = pl.pallas_call(body, grid_spec=grid_spec,
                         out_shape=[jax.ShapeDtypeStruct((s_len, HEADS * HC), F32),
                                    jax.ShapeDtypeStruct((s_len, HEADS * HC), MXU),
                                    jax.ShapeDtypeStruct((s_len, HEADS * VH), MXU)] + _exchanged_shapes(riders, casts),
                         compiler_params=_cparams(2), name="flash_bwd")(qt, kt, qc, kc, v, do, lse, delta, *riders, *casts)
    return out[0], out[1], out[2], out[3:3 + n_rows], out[3 + n_rows:]


def _attn_out_fwd(x, o, wo, tm):
    def fn(x, o, wo):
        return (x + _mm(o, wo),), ()

    return _rowcall("attn_out_fwd", fn, tm, [x, o], [wo], [(D, F32)], [])[0][0]


def _attn_out_bwd(dh, o, wo_t, tm):
    def fn(dh, o, wo_t):
        do = _mm(dh, wo_t)
        of = o.astype(F32)
        dl = [jnp.broadcast_to(jnp.sum(do[:, h * VH:(h + 1) * VH] * of[:, h * VH:(h + 1) * VH], axis=1, keepdims=True),
                               (dh.shape[0], LANES)) for h in range(HEADS)]
        return (do, jnp.concatenate(dl, axis=1)), (_mm_tn(o, dh),)

    return _rowcall("attn_out_bwd", fn, tm, [dh, o], [wo_t], [(HEADS * VH, MXU), (HEADS * LANES, F32)], [(HEADS * VH, D)])


def _shift_rows(a, k):
    return a if k == 0 else pltpu.roll(a, k % a.shape[0], 0)


def _stack_rows(rows):
    idx = lax.broadcasted_iota(jnp.int32, (8, rows[0].shape[1]), 0)
    out = jnp.zeros((8, rows[0].shape[1]), F32)
    for k, r in enumerate(rows):
        out = jnp.where(idx == k, r, out)
    return out


def _ffn_fwd(h, g, wup, cw, cb, wdown, tm):
    s_len = h.shape[0]
    nj = DFF // TC
    hb = tm // HALO

    def body(h_ref, hp_ref, g_ref, wv_ref, wg_ref, cwv_ref, cwg_ref, cbv_ref, cbg_ref, wd_ref, out_ref, act_ref,
             upv_ref, upg_ref, hn_sc, acc_sc):
        i, j = pl.program_id(0), pl.program_id(1)

        @pl.when(j == 0)
        def _():
            gg = g_ref[...]
            hp = _rms_fwd(hp_ref[...], gg)[0]
            hn_sc[:HALO, :] = jnp.where(i > 0, hp, 0.0).astype(MXU)
            hn_sc[HALO:, :] = _rms_fwd(h_ref[...], gg)[0].astype(MXU)
            acc_sc[...] = jnp.zeros(acc_sc.shape, F32)

        hn = hn_sc[...]

        def conv(w_ref, cw_ref, cb_ref, up_ref):
            up = jnp.dot(hn, w_ref[...], preferred_element_type=F32)
            up_ref[...] = up[HALO:].astype(up_ref.dtype)
            cwv = cw_ref[...]
            c = cwv[2:3] * up + cwv[1:2] * _shift_rows(up, 1) + cwv[0:1] * _shift_rows(up, 2)
            return c[HALO:] + cb_ref[...]

        cv = conv(wv_ref, cwv_ref, cbv_ref, upv_ref)
        cg = conv(wg_ref, cwg_ref, cbg_ref, upg_ref)
        act = cg * jax.nn.sigmoid(cg) * cv
        act_ref[...] = act.astype(act_ref.dtype)
        acc_sc[...] += _mm(act, wd_ref[...])

        @pl.when(j == nj - 1)
        def _():
            out_ref[...] = h_ref[...] + acc_sc[...]

    in_specs = [pl.BlockSpec((tm, D), lambda i, j: (i, 0)),
                pl.BlockSpec((HALO, D), lambda i, j: (jnp.maximum(i * hb - 1, 0), 0)),
                pl.BlockSpec((1, D), lambda i, j: (0, 0)),
                pl.BlockSpec((D, TC), lambda i, j: (0, j)), pl.BlockSpec((D, TC), lambda i, j: (0, j + nj)),
                pl.BlockSpec((3, TC), lambda i, j: (0, j)), pl.BlockSpec((3, TC), lambda i, j: (0, j + nj)),
                pl.BlockSpec((1, TC), lambda i, j: (0, j)), pl.BlockSpec((1, TC), lambda i, j: (0, j + nj)),
                pl.BlockSpec((TC, D), lambda i, j: (j, 0))]
    out_specs = [pl.BlockSpec((tm, D), lambda i, j: (i, 0))] + [pl.BlockSpec((tm, TC), lambda i, j: (i, j))] * 3
    return pl.pallas_call(body, grid=(s_len // tm, nj), in_specs=in_specs, out_specs=out_specs,
                          out_shape=[jax.ShapeDtypeStruct((s_len, D), F32)] + [jax.ShapeDtypeStruct((s_len, DFF), MXU)] * 3,
                          scratch_shapes=[pltpu.VMEM((tm + HALO, D), MXU), pltpu.VMEM((tm, D), F32)],
                          compiler_params=_cparams(2), name="ffn_fwd")(h, h, g, wup, wup, cw, cw, cb, cb, wdown)


def _ffn_bwd(h, dout, upv, upg, g, cw, cb, wdown_t, wup_t, tm):
    s_len = h.shape[0]
    ni = s_len // tm
    hb = tm // HALO

    def body(h_ref, d_ref, dx_ref, uvp_ref, uv_ref, uvx_ref, ugp_ref, ug_ref, ugx_ref, g_ref, cwv_ref, cwg_ref, cbv_ref, cbg_ref,
             wdt_ref, wutv_ref, wutg_ref, din_ref, dupv_ref, dupg_ref, hn_ref, cacc_ref, dg_ref):
        i = pl.program_id(0)

        @pl.when(i == 0)
        def _():
            cacc_ref[...] = jnp.zeros(cacc_ref.shape, F32)
            dg_ref[...] = jnp.zeros(dg_ref.shape, F32)

        gg = g_ref[...]
        hn, xh, r = _rms_fwd(h_ref[...], gg)
        hn_ref[...] = hn.astype(hn_ref.dtype)
        dd = jnp.concatenate([d_ref[...], jnp.where(i < ni - 1, dx_ref[...], 0.0)], axis=0).astype(MXU)
        dact = jnp.dot(dd, wdt_ref[...], preferred_element_type=F32)

        def half(p_ref, t_ref, x_ref, cw_ref, cb_ref):
            prev = p_ref[...]
            up = jnp.concatenate([jnp.where(i > 0, prev, jnp.zeros_like(prev)), t_ref[...], x_ref[...]], axis=0).astype(F32)
            cwv = cw_ref[...]
            u1, u2 = _shift_rows(up, 1), _shift_rows(up, 2)
            c = (cwv[2:3] * up + cwv[1:2] * u1 + cwv[0:1] * u2)[HALO:] + cb_ref[...]
            return c, (up[HALO:HALO + tm], u1[HALO:HALO + tm], u2[HALO:HALO + tm]), cwv

        cv, upsv, cwv = half(uvp_ref, uv_ref, uvx_ref, cwv_ref, cbv_ref)
        cg, upsg, cwg = half(ugp_ref, ug_ref, ugx_ref, cwg_ref, cbg_ref)
        sg = jax.nn.sigmoid(cg)
        dcv = dact * (cg * sg)
        dcg = dact * cv * (sg * (1.0 + cg * (1.0 - sg)))

        def back(dc, ups, cwx, slot, dup_ref, wut_ref):
            dup = (cwx[2:3] * dc + cwx[1:2] * _shift_rows(dc, -1) + cwx[0:1] * _shift_rows(dc, -2))[:tm]
            dct = dc[:tm]
            cacc_ref[slot] += _stack_rows([jnp.sum(dct * ups[2], axis=0, keepdims=True),
                                           jnp.sum(dct * ups[1], axis=0, keepdims=True),
                                           jnp.sum(dct * ups[0], axis=0, keepdims=True),
                                           jnp.sum(dct, axis=0, keepdims=True)])
            dup_ref[...] = dup.astype(dup_ref.dtype)
            return _mm(dup, wut_ref[...])

        dhn = back(dcv, upsv, cwv, 0, dupv_ref, wutv_ref) + back(dcg, upsg, cwg, 1, dupg_ref, wutg_ref)
        dx, dgp = _rms_bwd(dhn, xh, r, gg)
        din_ref[...] = d_ref[...] + dx
        dg_ref[...] += dgp

    last_blk = s_len // HALO - 1
    tile = lambda w: pl.BlockSpec((tm, w), lambda i: (i, 0))
    prev = lambda w: pl.BlockSpec((HALO, w), lambda i: (jnp.maximum(i * hb - 1, 0), 0))
    nxt = lambda w: pl.BlockSpec((HALO, w), lambda i: (jnp.minimum((i + 1) * hb, last_blk), 0))
    const = lambda s, j=0: pl.BlockSpec(s, lambda i: (0, j))
    in_specs = [tile(D), tile(D), nxt(D), prev(DFF), tile(DFF), nxt(DFF), prev(DFF), tile(DFF), nxt(DFF),
                const((1, D)), const((3, DFF)), const((3, DFF), 1), const((1, DFF)), const((1, DFF), 1),
                const((D, DFF)), pl.BlockSpec((DFF, D), lambda i: (0, 0)), pl.BlockSpec((DFF, D), lambda i: (1, 0))]
    out_specs = [tile(D), tile(DFF), tile(DFF), tile(D), pl.BlockSpec((2, 8, DFF), lambda i: (0, 0, 0)), const((1, D))]
    out_shape = [jax.ShapeDtypeStruct((s_len, D), F32), jax.ShapeDtypeStruct((s_len, DFF), MXU),
                 jax.ShapeDtypeStruct((s_len, DFF), MXU), jax.ShapeDtypeStruct((s_len, D), MXU),
                 jax.ShapeDtypeStruct((2, 8, DFF), F32), jax.ShapeDtypeStruct((1, D), F32)]
    return pl.pallas_call(body, grid=(ni,), in_specs=in_specs, out_specs=out_specs, out_shape=out_shape,
                          compiler_params=_cparams(1), name="ffn_bwd")(
        h, dout, dout, upv, upv, upv, upg, upg, upg, g, cw, cw, cb, cb, wdown_t, wup_t, wup_t)


def _matmul_tn(a, b, tn, ts, name):
    s_len, m = a.shape
    n = b.shape[1]

    def body(a_ref, b_ref, o_ref):
        @pl.when(pl.program_id(1) == 0)
        def _():
            o_ref[...] = jnp.zeros(o_ref.shape, F32)

        o_ref[...] += _mm_tn(a_ref[...], b_ref[...])

    return pl.pallas_call(body, grid=(n // tn, s_len // ts),
                          in_specs=[pl.BlockSpec((ts, m), lambda jn, k: (k, 0)), pl.BlockSpec((ts, tn), lambda jn, k: (k, jn))],
                          out_specs=pl.BlockSpec((m, tn), lambda jn, k: (0, jn)),
                          out_shape=jax.ShapeDtypeStruct((m, n), F32), compiler_params=_cparams(2), name=name)(a, b)


def _s5_coefs(lr, li, ldt):
    dt = jnp.exp(ldt)
    mag = jnp.exp(lr * dt)
    th = li * dt
    ar, ai = mag * jnp.cos(th), mag * jnp.sin(th)
    den = lr * lr + li * li
    nr = ar - 1.0
    cr = (nr * lr + ai * li) / den
    ci = (ai * lr - nr * li) / den
    return dt, mag, th, ar, ai, den, nr, cr, ci


def _s5_prep(lr, li, ldt, braw, seg):
    def body(lr_ref, li_ref, ldt_ref, b_ref, bb_ref, ap_ref):
        lr_, li_, ldt_ = lr_ref[0], li_ref[0], ldt_ref[0]
        dt, mag, th, ar, ai, den, nr, cr, ci = _s5_coefs(lr_, li_, ldt_)
        br, bi = b_ref[0, :, :SL], b_ref[0, :, SL:]
        bb_ref[0, :, :SL] = (cr * br - ci * bi).astype(bb_ref.dtype)
        bb_ref[0, :, SL:] = (cr * bi + ci * br).astype(bb_ref.dtype)
        for i in range(seg):
            m = jnp.exp((i + 1.0) * (lr_ * dt))
            ap_ref[0, i * 8:(i + 1) * 8, :SL] = jnp.broadcast_to(m * jnp.cos((i + 1.0) * th), (8, SL))
            ap_ref[0, i * 8:(i + 1) * 8, SL:] = jnp.broadcast_to(m * jnp.sin((i + 1.0) * th), (8, SL))

    vec = pl.BlockSpec((1, 1, SL), lambda k: (k, 0, 0))
    return pl.pallas_call(
        body, grid=(GB,), in_specs=[vec, vec, vec, pl.BlockSpec((1, LANES, 2 * SL), lambda k: (k, 0, 0))],
        out_specs=[pl.BlockSpec((1, LANES, 2 * SL), lambda k: (k, 0, 0)),
                   pl.BlockSpec((1, 8 * seg, 2 * SL), lambda k: (k, 0, 0))],
        out_shape=[jax.ShapeDtypeStruct((GB, LANES, 2 * SL), MXU), jax.ShapeDtypeStruct((GB, 8 * seg, 2 * SL), F32)],
        compiler_params=_cparams(1), name="s5_prep")(lr, li, ldt, braw)


def _s5_prep_bwd(lr, li, ldt, braw, dbb, da):
    def body(lr_ref, li_ref, ldt_ref, b_ref, dbb_ref, da_ref, dbraw_ref, dlr_ref, dli_ref, dldt_ref):
        lr_, li_, ldt_ = lr_ref[0], li_ref[0], ldt_ref[0]
        dt, mag, th, ar, ai, den, nr, cr, ci = _s5_coefs(lr_, li_, ldt_)
        br, bi = b_ref[0, :, :SL], b_ref[0, :, SL:]
        gbr, gbi = dbb_ref[0, :, :SL], dbb_ref[0, :, SL:]
        dbraw_ref[0, :, :SL] = cr * gbr + ci * gbi
        dbraw_ref[0, :, SL:] = cr * gbi - ci * gbr
        dcr = jnp.sum(gbr * br + gbi * bi, axis=0, keepdims=True)
        dci = jnp.sum(gbi * br - gbr * bi, axis=0, keepdims=True)
        dar = jnp.sum(da_ref[0, :, :SL], axis=0, keepdims=True)
        dai = jnp.sum(da_ref[0, :, SL:], axis=0, keepdims=True)
        g1, g2 = dcr / den, dci / den
        gden = -(dcr * cr + dci * ci) / den
        gar = dar + g1 * lr_ - g2 * li_
        gai = dai + g1 * li_ + g2 * lr_
        glr = g1 * nr + g2 * ai + 2.0 * lr_ * gden
        gli = g1 * ai - g2 * nr + 2.0 * li_ * gden
        gmag = gar * jnp.cos(th) + gai * jnp.sin(th)
        gth = gai * ar - gar * ai
        dlr_ref[0] = glr + gmag * mag * dt
        dli_ref[0] = gli + gth * dt
        dldt_ref[0] = (gmag * mag * lr_ + gth * li_) * dt

    vec = pl.BlockSpec((1, 1, SL), lambda k: (k, 0, 0))
    mat = pl.BlockSpec((1, LANES, 2 * SL), lambda k: (k, 0, 0))
    return pl.pallas_call(
        body, grid=(GB,), in_specs=[vec, vec, vec, mat, mat, pl.BlockSpec((1, 8, 2 * SL), lambda k: (k, 0, 0))],
        out_specs=[mat, vec, vec, vec],
        out_shape=[jax.ShapeDtypeStruct((GB, LANES, 2 * SL), F32)] + [jax.ShapeDtypeStruct((GB, 1, SL), F32)] * 3,
        compiler_params=_cparams(1), name="s5_prep_bwd")(lr, li, ldt, braw, dbb, da)


def _bcast_row(tile, j):
    return jnp.broadcast_to(tile[j:j + 1, :], tile.shape)


def _tile_rows(i):
    return pl.ds(pl.multiple_of(i * 8, 8), 8)


def _col(c):
    return slice(c * LANES, (c + 1) * LANES)


def _permute_rows(ref, seg):
    return jnp.concatenate([ref[pl.ds(i, 8, stride=seg), :] for i in range(seg)], axis=0)


def _unpermute_rows(src, dst, seg):
    for j in range(8):
        dst[j * seg:(j + 1) * seg, :] = src[pl.ds(j, seg, stride=8), :]


SCAN_UNROLL = 2


def _segment_scan(src, dst, ap, seg, reverse):
    half = NCOL // 2
    sign = -1.0 if reverse else 1.0
    a_r = [ap[0:8, _col(c)] for c in range(half)]
    a_i = [ap[0:8, _col(half + c)] for c in range(half)]

    def step(n, carry):
        i = seg - 1 - n if reverse else n
        rows = _tile_rows(i)
        out_r, out_i = [], []
        for c in range(half):
            xr, xi = carry[c], carry[half + c]
            nr = a_r[c] * xr - sign * a_i[c] * xi + src[rows, _col(c)]
            ni = a_r[c] * xi + sign * a_i[c] * xr + src[rows, _col(half + c)]
            dst[rows, _col(c)] = nr
            dst[rows, _col(half + c)] = ni
            out_r.append(nr)
            out_i.append(ni)
        return tuple(out_r + out_i)

    zero = jnp.zeros((8, LANES), F32)
    return lax.fori_loop(0, seg, step, (zero,) * NCOL, unroll=SCAN_UNROLL)


def _segment_entries(ends, cin, ap, seg, reverse):
    half = NCOL // 2
    sign = -1.0 if reverse else 1.0
    al_r = [ap[(seg - 1) * 8:seg * 8, _col(c)] for c in range(half)]
    al_i = [ap[(seg - 1) * 8:seg * 8, _col(half + c)] for c in range(half)]
    row = lax.broadcasted_iota(jnp.int32, (8, LANES), 0)
    ent, out = [None] * NCOL, [None] * NCOL
    for c in range(half):
        zr, zi = cin[c], cin[half + c]
        er, ei = jnp.zeros((8, LANES), F32), jnp.zeros((8, LANES), F32)
        for j in (range(7, -1, -1) if reverse else range(8)):
            er, ei = jnp.where(row == j, zr, er), jnp.where(row == j, zi, ei)
            fr, fi = _bcast_row(ends[c], j), _bcast_row(ends[half + c], j)
            zr, zi = (al_r[c] * zr - sign * al_i[c] * zi + fr, al_r[c] * zi + sign * al_i[c] * zr + fi)
        ent[c], ent[half + c] = er, ei
        out[c], out[half + c] = zr, zi
    return ent, out


def _chunk_states(u, bb_ref, ap, cin, bu_sc, x_sc, seg):
    half = NCOL // 2
    bu_sc[...] = _mm(u, bb_ref[0])
    ends = _segment_scan(bu_sc, x_sc, ap, seg, False)
    ent, out = _segment_entries(ends, cin, ap, seg, False)

    def fix(i, _):
        rows = _tile_rows(i)
        for c in range(half):
            pr, pi = ap[rows, _col(c)], ap[rows, _col(half + c)]
            x_sc[rows, _col(c)] += pr * ent[c] - pi * ent[half + c]
            x_sc[rows, _col(half + c)] += pr * ent[half + c] + pi * ent[c]
        return 0

    lax.fori_loop(0, seg, fix, 0, unroll=SCAN_UNROLL)
    return out


def _s5_scan_fwd(u, bb, apow, cblk, dskip, t_chunk):
    s_len = u.shape[0]
    nc = s_len // t_chunk
    seg = t_chunk // 8

    def body(u_ref, bb_ref, ap_ref, c_ref, d_ref, y_ref, xin_ref, bu_sc, x_sc, y_sc, carry_sc):
        @pl.when(pl.program_id(1) == 0)
        def _():
            carry_sc[...] = jnp.zeros(carry_sc.shape, F32)

        uu = _permute_rows(u_ref, seg)
        cin = [carry_sc[:, _col(c)] for c in range(NCOL)]
        xin_ref[0, 0] = carry_sc[...]
        out = _chunk_states(uu, bb_ref, ap_ref.at[0], cin, bu_sc, x_sc, seg)
        for c in range(NCOL):
            carry_sc[:, _col(c)] = out[c]
        y_sc[...] = _mm(x_sc[...], c_ref[0]) + d_ref[...] * uu
        _unpermute_rows(y_sc, y_ref, seg)

    in_specs = [pl.BlockSpec((t_chunk, LANES), lambda k, c: (c, k)),
                pl.BlockSpec((1, LANES, 2 * SL), lambda k, c: (k, 0, 0)),
                pl.BlockSpec((1, 8 * seg, 2 * SL), lambda k, c: (k, 0, 0)),
                pl.BlockSpec((1, 2 * SL, LANES), lambda k, c: (k, 0, 0)),
                pl.BlockSpec((1, LANES), lambda k, c: (0, k))]
    out_specs = [pl.BlockSpec((t_chunk, LANES), lambda k, c: (c, k)),
                 pl.BlockSpec((1, 1, 8, 2 * SL), lambda k, c: (k, c, 0, 0))]
    return pl.pallas_call(body, grid=(GB, nc), in_specs=in_specs, out_specs=out_specs,
                          out_shape=[jax.ShapeDtypeStruct((s_len, D), F32), jax.ShapeDtypeStruct((GB, nc, 8, 2 * SL), F32)],
                          scratch_shapes=[pltpu.VMEM((t_chunk, 2 * SL), F32), pltpu.VMEM((t_chunk, 2 * SL), F32),
                                          pltpu.VMEM((t_chunk, LANES), F32), pltpu.VMEM((8, 2 * SL), F32)],
                          compiler_params=_cparams(2), name="s5_scan_fwd")(u, bb, apow, cblk, dskip)


def _s5_scan_bwd(u, dy, xin, bb, bb_t, apow, cblk_t, dskip, t_chunk):
    s_len = u.shape[0]
    nc = s_len // t_chunk
    seg = t_chunk // 8
    half = NCOL // 2

    def body(u_ref, dy_ref, xin_ref, bb_ref, bbt_ref, ap_ref, ct_ref, d_ref, du_ref, dbb_ref, dc_ref, da_ref, dd_ref,
             bu_sc, x_sc, g_sc, y_sc, carry_sc):
        @pl.when(pl.program_id(1) == 0)
        def _():
            carry_sc[...] = jnp.zeros(carry_sc.shape, F32)
            dbb_ref[...] = jnp.zeros(dbb_ref.shape, F32)
            dc_ref[...] = jnp.zeros(dc_ref.shape, F32)
            da_ref[...] = jnp.zeros(da_ref.shape, F32)
            dd_ref[...] = jnp.zeros(dd_ref.shape, F32)

        ap = ap_ref.at[0]
        uu, dyy = _permute_rows(u_ref, seg), _permute_rows(dy_ref, seg)
        cin = [xin_ref[0, 0, :, _col(c)] for c in range(NCOL)]
        _chunk_states(uu, bb_ref, ap, cin, bu_sc, x_sc, seg)
        bu_sc[...] = _mm(dyy, ct_ref[0])
        ends = _segment_scan(bu_sc, g_sc, ap, seg, True)
        lam_in = [carry_sc[:, _col(c)] for c in range(NCOL)]
        ent, out = _segment_entries(ends, lam_in, ap, seg, True)
        for c in range(NCOL):
            carry_sc[:, _col(c)] = out[c]
        row = lax.broadcasted_iota(jnp.int32, (8, LANES), 0)
        xp0 = [jnp.where(row == 0, cin[c], pltpu.roll(x_sc[(seg - 1) * 8:seg * 8, _col(c)], 1, 0)) for c in range(NCOL)]

        def fix(i, acc):
            rows, prev, tab = _tile_rows(i), _tile_rows(jnp.maximum(i - 1, 0)), _tile_rows(seg - 1 - i)
            new = list(acc)
            for c in range(half):
                pr, pi = ap[tab, _col(c)], ap[tab, _col(half + c)]
                lr_ = g_sc[rows, _col(c)] + pr * ent[c] + pi * ent[half + c]
                li_ = g_sc[rows, _col(half + c)] + pr * ent[half + c] - pi * ent[c]
                g_sc[rows, _col(c)] = lr_
                g_sc[rows, _col(half + c)] = li_
                xr = jnp.where(i == 0, xp0[c], x_sc[prev, _col(c)])
                xi = jnp.where(i == 0, xp0[half + c], x_sc[prev, _col(half + c)])
                new[c] = acc[c] + lr_ * xr + li_ * xi
                new[half + c] = acc[half + c] + li_ * xr - lr_ * xi
            return tuple(new)

        zero = jnp.zeros((8, LANES), F32)
        dacc = lax.fori_loop(0, seg, fix, (zero,) * NCOL, unroll=SCAN_UNROLL)
        for c in range(NCOL):
            da_ref[0, :, _col(c)] += dacc[c]
        lam = g_sc[...]
        y_sc[...] = _mm(lam, bbt_ref[0]) + d_ref[...] * dyy
        _unpermute_rows(y_sc, du_ref, seg)
        dbb_ref[0] += _mm_tn(uu, lam)
        dc_ref[0] += _mm_tn(x_sc[...], dyy)
        dd_ref[0] += _stack_rows([jnp.sum(dyy * uu, axis=0, keepdims=True)])

    rev = lambda k, c: (nc - 1 - c, k)
    in_specs = [pl.BlockSpec((t_chunk, LANES), rev), pl.BlockSpec((t_chunk, LANES), rev),
                pl.BlockSpec((1, 1, 8, 2 * SL), lambda k, c: (k, nc - 1 - c, 0, 0)),
                pl.BlockSpec((1, LANES, 2 * SL), lambda k, c: (k, 0, 0)),
                pl.BlockSpec((1, 2 * SL, LANES), lambda k, c: (k, 0, 0)),
                pl.BlockSpec((1, 8 * seg, 2 * SL), lambda k, c: (k, 0, 0)),
                pl.BlockSpec((1, LANES, 2 * SL), lambda k, c: (k, 0, 0)),
                pl.BlockSpec((1, LANES), lambda k, c: (0, k))]
    out_specs = [pl.BlockSpec((t_chunk, LANES), rev),
                 pl.BlockSpec((1, LANES, 2 * SL), lambda k, c: (k, 0, 0)),
                 pl.BlockSpec((1, 2 * SL, LANES), lambda k, c: (k, 0, 0)),
                 pl.BlockSpec((1, 8, 2 * SL), lambda k, c: (k, 0, 0)),
                 pl.BlockSpec((1, 8, LANES), lambda k, c: (k, 0, 0))]
    out_shape = [jax.ShapeDtypeStruct((s_len, D), F32), jax.ShapeDtypeStruct((GB, LANES, 2 * SL), F32),
                 jax.ShapeDtypeStruct((GB, 2 * SL, LANES), F32), jax.ShapeDtypeStruct((GB, 8, 2 * SL), F32),
                 jax.ShapeDtypeStruct((GB, 8, LANES), F32)]
    return pl.pallas_call(body, grid=(GB, nc), in_specs=in_specs, out_specs=out_specs, out_shape=out_shape,
                          scratch_shapes=[pltpu.VMEM((t_chunk, 2 * SL), F32)] * 3 + [pltpu.VMEM((t_chunk, LANES), F32),
                                                                                        pltpu.VMEM((8, 2 * SL), F32)],
                          compiler_params=_cparams(2), name="s5_scan_bwd")(u, dy, xin, bb, bb_t, apow, cblk_t, dskip)


_GELU_K = math.sqrt(2.0 / math.pi)


def _gelu(y):
    t = jnp.tanh(_GELU_K * (y + 0.044715 * (y * y * y)))
    return 0.5 * y * (1.0 + t), 0.5 * (1.0 + t) + 0.5 * y * (1.0 - t * t) * (_GELU_K * (1.0 + 3 * 0.044715 * (y * y)))


def _s5_in_fwd(h, gmix, win, tm):
    def fn(h, gmix, win):
        return (_mm(_rms_fwd(h, gmix)[0], win),), ()

    return _rowcall("s5_in_fwd", fn, tm, [h], [gmix, win], [(D, F32)], [])[0][0]


def _s5_in_bwd(h, du, dh, gmix, win_t, tm):
    def fn(h, du, dh, gmix, win_t):
        hn, xh, r = _rms_fwd(h, gmix)
        dx, dg = _rms_bwd(_mm(du, win_t), xh, r, gmix)
        return (dh + dx,), (_mm_tn(hn, du), dg)

    return _rowcall("s5_in_bwd", fn, tm, [h, du, dh], [gmix, win_t], [(D, F32)], [(D, D), (1, D)])


def _s5_out_fwd(h, y, wglu, tm):
    def fn(h, y, wglu):
        z = _mm(_gelu(y)[0], wglu)
        return (h + z[:, :D] * jax.nn.sigmoid(z[:, D:]),), ()

    return _rowcall("s5_out_fwd", fn, tm, [h, y], [wglu], [(D, F32)], [])[0][0]


def _s5_out_bwd(dh, y, wglu, wglu_t, tm):
    def fn(dh, y, wglu, wglu_t):
        yg, dgelu = _gelu(y)
        z = _mm(yg, wglu)
        val, sg = z[:, :D], jax.nn.sigmoid(z[:, D:])
        dz = jnp.concatenate([dh * sg, dh * val * sg * (1.0 - sg)], axis=1)
        return (_mm(dz, wglu_t) * dgelu,), (_mm_tn(yg, dz),)

    return _rowcall("s5_out_bwd", fn, tm, [dh, y], [wglu, wglu_t], [(D, F32)], [(D, 2 * D)])


def _final_loss(h, tgt, gfin, tm):
    def fn(h, tgt, gfin):
        y, xh, r = _rms_fwd(h, gfin)
        err = y - tgt
        dx, dg = _rms_bwd(err * (1.0 / D), xh, r, gfin)
        return (dx,), (jnp.sum(err * err, axis=0, keepdims=True), dg)

    return _rowcall("final_loss", fn, tm, [h, tgt], [gfin], [(D, F32)], [(1, D), (1, D)])


def _bf(a):
    return a.astype(MXU)


def _s5_block_mats(b_re, b_im, c_re, c_im):
    gl = G // GB
    eye = jnp.eye(gl, dtype=F32)

    def b_blk(b):
        bt = b.reshape(GB, gl, P, C).transpose(0, 1, 3, 2)
        return (bt[:, :, :, None, :] * eye[None, :, None, :, None]).reshape(GB, gl * C, gl * P)

    def c_blk(cm):
        ct = cm.reshape(GB, gl, C, P).transpose(0, 1, 3, 2)
        return (ct[:, :, :, None, :] * eye[None, :, None, :, None]).reshape(GB, gl * P, gl * C)

    braw = jnp.concatenate([b_blk(b_re), b_blk(b_im)], axis=2)
    cblk = jnp.concatenate([c_blk(c_re), -c_blk(c_im)], axis=1)
    return braw, cblk


def _s5_unblock_b(d):
    gl = G // GB
    eye = jnp.eye(gl, dtype=F32)
    d5 = d.reshape(GB, gl, C, gl, P)
    return jnp.sum(d5 * eye[None, :, None, :, None], axis=3).transpose(0, 1, 3, 2).reshape(G, P, C)


def _s5_unblock_c(d):
    gl = G // GB
    eye = jnp.eye(gl, dtype=F32)
    d5 = d.reshape(GB, gl, P, gl, C)
    return jnp.sum(d5 * eye[None, :, None, :, None], axis=3).transpose(0, 1, 3, 2).reshape(G, C, P)


def _tiles(s_len):
    return (min(256, s_len), min(512, s_len), min(1024, s_len),
            (min(1024, s_len), min(4096, s_len)), (min(1024, s_len), min(2048, s_len)),
            min(256, s_len), min(256, s_len))


def _merge_shards(g, axis):
    moved = jnp.moveaxis(g, 0, axis)
    return moved.reshape(moved.shape[:axis] + (-1,) + moved.shape[axis + 2:])


def _split_shards(full, axis):
    cut = full.reshape(full.shape[:axis] + (NDEV, -1) + full.shape[axis + 1:])
    return jnp.moveaxis(cut, axis, 0)


def _sequence_step(x, pos, tgt, w, late=(), late_small=()):
    s_len = x.shape[0]
    tm_w, tm, t_chunk, t_fwd, t_bwd, tm_ff, tm_fb = _tiles(s_len)
    seg = t_chunk // 8
    row = lambda v: v.reshape(1, -1)

    wa = _bf(jnp.pad(w["mla_w_a"][0], ((0, 0), (0, AW - (QL + KVL + ROPE)))))
    wuq = _bf(jnp.pad(w["mla_w_uq"][0].reshape(QL, HEADS, NOPE + ROPE), ((0, 0), (0, 0), (0, HC - NOPE - ROPE))).reshape(QL, HEADS * HC))
    wukv = _bf(w["mla_w_ukv"][0])
    inv = 1.0 / (ROPE_THETA ** (jnp.arange(0, ROPE, 2, dtype=F32) / ROPE))
    invf = jnp.concatenate([inv, inv, jnp.zeros((LANES - ROPE,), F32)]).reshape(1, LANES)
    gmix0, gmix1 = row(w["g_mix"][0]), row(w["g_mix"][1])
    gq, gkv = row(w["mla_g_q"][0]), row(w["mla_g_kv"][0])
    lr = w["ssm_lambda_re"][0].reshape(GB, 1, SL)
    li = w["ssm_lambda_im"][0].reshape(GB, 1, SL)
    ldt = jnp.broadcast_to(w["ssm_log_dt"][0][:, None], (G, P)).reshape(GB, 1, SL)
    braw, cblk = _s5_block_mats(w["ssm_b_re"][0], w["ssm_b_im"][0], w["ssm_c_re"][0], w["ssm_c_im"][0])
    cblk = _bf(cblk)

    qc, kc, v = _mla_front_fwd(x, pos, gmix0, wa, gq, gkv, wuq, wukv, invf, tm_w)
    o, lse, gathered = _flash_fwd(qc, kc, v, *t_fwd, [shard for _, _, shard in late])
    w = dict(w)
    for (n, axis, _), g in zip(late, gathered):
        w[n] = _merge_shards(g, axis)
    wo = _bf(w["mla_w_o"][0])
    win, wglu = _bf(w["ssm_w_in"][0]), _bf(w["ssm_w_glu"][0])
    dskip = row(w["ssm_d"][0])
    ffn = []
    for l in range(2):
        wup, wdown = _bf(w["ffn_w_up"][l]), _bf(w["ffn_w_down"][l])
        ffn.append(dict(g=row(w["g_ffn"][l]), wup=wup, wup_t=wup.T, wdown=wdown, wdown_t=wdown.T,
                        cw=w["ffn_conv_w"][l], cb=row(w["ffn_conv_b"][l])))
    h1 = _attn_out_fwd(x, o, wo, tm)
    f0 = ffn[0]
    h2, *saved0 = _ffn_fwd(h1, f0["g"], f0["wup"], f0["cw"], f0["cb"], f0["wdown"], tm_ff)
    bb, apow = _s5_prep(lr, li, ldt, braw, seg)
    u = _s5_in_fwd(h2, gmix1, win, tm)
    y, xin = _s5_scan_fwd(u, bb, apow, cblk, dskip, t_chunk)
    h3 = _s5_out_fwd(h2, y, wglu, tm)
    f1 = ffn[1]
    h4, *saved1 = _ffn_fwd(h3, f1["g"], f1["wup"], f1["cw"], f1["cb"], f1["wdown"], tm_ff)
    (dh4,), (sq, d_gfinal) = _final_loss(h4, tgt, row(w["g_final"]), tm)
    loss = 0.5 * jnp.sum(sq) / D

    grads = {}

    def ffn_back(hin, dout, saved, f):
        act, upv, upg = saved
        din, dupv, dupg, hn, cacc, dg = _ffn_bwd(hin, dout, upv, upg, f["g"], f["cw"], f["cb"], f["wdown_t"], f["wup_t"], tm_fb)
        d_wup = jnp.concatenate([_matmul_tn(hn, dupv, DFF, min(512, s_len), "ffn_dwup_v"),
                                 _matmul_tn(hn, dupg, DFF, min(512, s_len), "ffn_dwup_g")], axis=1)
        d_wdown = _matmul_tn(act, dout, D, min(512, s_len), "ffn_dwdown")
        cflat = cacc.transpose(1, 0, 2).reshape(8, 2 * DFF)
        return din, d_wup, d_wdown, cflat[:3], cflat[3], dg[0]

    dh3, d_wup1, d_wdown1, d_cw1, d_cb1, d_gffn1 = ffn_back(h3, dh4, saved1, f1)
    (dy,), (d_wglu,) = _s5_out_bwd(dh3, y, wglu, wglu.T, tm)
    du, d_bb, d_cblk, d_a, d_dsk = _s5_scan_bwd(u, dy, xin, bb, bb.transpose(0, 2, 1), apow, cblk.transpose(0, 2, 1), dskip, t_chunk)
    d_braw, d_lr, d_li, d_ldt = _s5_prep_bwd(lr, li, ldt, braw, d_bb, d_a)
    (dh2,), (d_win, d_gmix1) = _s5_in_bwd(h2, du, dh3, gmix1, win.T, tm)
    dh1, d_wup0, d_wdown0, d_cw0, d_cb0, d_gffn0 = ffn_back(h1, dh2, saved0, f0)
    (do, delta), (d_wo,) = _attn_out_bwd(dh1, o, wo.T, tm)

    grads["mla_w_o"] = d_wo[None]
    grads["ssm_w_in"] = d_win[None]
    grads["ssm_lambda_re"] = d_lr.reshape(1, G, P)
    grads["ssm_lambda_im"] = d_li.reshape(1, G, P)
    grads["ssm_log_dt"] = jnp.sum(d_ldt.reshape(G, P), axis=1)[None]
    grads["ssm_b_re"] = _s5_unblock_b(d_braw[:, :, :SL])[None]
    grads["ssm_b_im"] = _s5_unblock_b(d_braw[:, :, SL:])[None]
    grads["ssm_c_re"] = _s5_unblock_c(d_cblk[:, :SL, :])[None]
    grads["ssm_c_im"] = -_s5_unblock_c(d_cblk[:, SL:, :])[None]
    grads["ssm_d"] = jnp.sum(d_dsk, axis=1).reshape(1, D)
    grads["ssm_w_glu"] = d_wglu[None]
    grads["ffn_w_up"] = jnp.stack([d_wup0, d_wup1])
    grads["ffn_conv_w"] = jnp.stack([d_cw0, d_cw1])
    grads["ffn_conv_b"] = jnp.stack([d_cb0, d_cb1])
    grads["ffn_w_down"] = jnp.stack([d_wdown0, d_wdown1])
    grads["g_ffn"] = jnp.stack([d_gffn0, d_gffn1])
    grads["g_final"] = d_gfinal[0]

    sends = [_split_shards(grads[n], axis).astype(MXU) for n, axis, _ in late]
    casts = [jnp.concatenate([grads[n].reshape(-1) for n in late_small])] if late_small else []
    dqc, dkc, dv, landed, landed_small = _flash_bwd(qc, kc, v, do, lse, delta, *t_bwd, sends, casts)
    (dx,), (d_wa, d_wuq, d_wukv, d_gq, d_gkv, d_gmix0) = _mla_front_bwd(
        x, pos, dqc, dkc, dv, dh1, gmix0, wa, gq, gkv, wuq, wukv, invf, wa.T, wuq.T, wukv.T, tm_w)
    grads["mla_w_a"] = d_wa[None, :, :QL + KVL + ROPE]
    grads["mla_g_q"] = d_gq
    grads["mla_g_kv"] = d_gkv
    grads["mla_w_uq"] = d_wuq.reshape(QL, HEADS, HC)[:, :, :NOPE + ROPE].reshape(1, QL, HEADS * (NOPE + ROPE))
    grads["mla_w_ukv"] = d_wukv[None]
    grads["g_mix"] = jnp.concatenate([d_gmix0, d_gmix1], axis=0)
    return loss, dx, grads, landed, (landed_small[0] if late_small else None)


MESH = pl.DeviceIdType.MESH
ANY = pl.BlockSpec(memory_space=pl.ANY)


def _gather_many(blocks, name):
    n = len(blocks)

    def body(*refs):
        _gather_issue(refs[:n], refs[n:2 * n], refs[2 * n:])
        _gather_finish(refs[:n], refs[n:2 * n], refs[2 * n:])

    return pl.pallas_call(body, out_shape=_gathered_shapes(blocks), in_specs=[ANY] * n, out_specs=[ANY] * n,
                          scratch_shapes=_comm_scratch(n), name=name)(*blocks)


def _comm_scratch(n):
    if n == 0:
        return []
    return [pltpu.SemaphoreType.DMA((7 * n,)), pltpu.SemaphoreType.DMA((7 * n,)), pltpu.SemaphoreType.DMA((n,))]


def _gathered_shapes(blocks):
    return [jax.ShapeDtypeStruct((NDEV,) + b.shape, b.dtype) for b in blocks]


def _gather_copies(ins, outs, sems):
    send_sems, recv_sems, local_sems = sems
    x, y, c = lax.axis_index("x"), lax.axis_index("y"), lax.axis_index("c")
    me, sibling = (x, y, c), (x, y, 1 - c)
    chips = [(1 - x, y), (x, 1 - y), (1 - x, 1 - y)]

    def copy(a, k, block, to, own=False):
        px, py, pc = block
        slot = outs[a].at[4 * px + 2 * py + pc]
        return pltpu.make_async_remote_copy(src_ref=ins[a] if own else slot, dst_ref=slot,
                                            send_sem=send_sems.at[7 * a + k], recv_sem=recv_sems.at[7 * a + k],
                                            device_id=to, device_id_type=MESH)

    mine = [pltpu.make_async_copy(ins[a], outs[a].at[4 * x + 2 * y + c], local_sems.at[a]) for a in range(len(ins))]
    first = []
    for a in range(len(ins)):
        first.append(copy(a, 0, me, sibling, own=True))
        first += [copy(a, 1 + j, me, (*chip, c), own=True) for j, chip in enumerate(chips)]
    return copy, mine, first, me, sibling, chips, c


def _gather_issue(ins, outs, sems):
    _, mine, first, *_ = _gather_copies(ins, outs, sems)
    for cp in mine + first:
        cp.start()


def _gather_finish(ins, outs, sems):
    copy, mine, first, me, sibling, chips, c = _gather_copies(ins, outs, sems)
    passed = []
    for j, chip in enumerate(chips):
        for a in range(len(ins)):
            copy(a, 1 + j, (*chip, c), me).wait_recv()
            passed.append(copy(a, 4 + j, (*chip, c), sibling))
            passed[-1].start()
    for a in range(len(ins)):
        copy(a, 0, sibling, me).wait_recv()
        for j, chip in enumerate(chips):
            copy(a, 4 + j, (*chip, 1 - c), me).wait_recv()
    for cp in first + passed:
        cp.wait_send()
    for cp in mine:
        cp.wait()


def _exchange_many(sends, casts, name):
    n, tot = len(sends), len(sends) + len(casts)

    def body(*refs):
        _exchange_issue(refs[:tot], refs[tot:2 * tot], refs[2 * tot:], n)
        _exchange_finish(refs[:tot], refs[tot:2 * tot], refs[2 * tot:], n)

    out = pl.pallas_call(body, out_shape=_exchanged_shapes(sends, casts), in_specs=[ANY] * tot, out_specs=[ANY] * tot,
                         scratch_shapes=_comm_scratch(tot), name=name)(*sends, *casts)
    return out[:n], out[n:]


def _exchanged_shapes(sends, casts):
    return [jax.ShapeDtypeStruct(s.shape, s.dtype) for s in sends] + _gathered_shapes(casts)


def _exchange_copies(ins, outs, sems, n, with_arrivals):
    send_sems, recv_sems, local_sems = sems
    x, y, c = lax.axis_index("x"), lax.axis_index("y"), lax.axis_index("c")
    me = 4 * x + 2 * y + c
    local, sent, arrivals = [], [], []
    for a in range(len(ins)):
        own = ins[a].at[me] if a < n else ins[a]
        local.append(pltpu.make_async_copy(own, outs[a].at[me], local_sems.at[a]))
        for m in range(1, NDEV):
            px = 1 - x if m & 4 else x
            py = 1 - y if m & 2 else y
            pc = 1 - c if m & 1 else c
            peer = 4 * px + 2 * py + pc
            to = dict(send_sem=send_sems.at[7 * a + m - 1], recv_sem=recv_sems.at[7 * a + m - 1],
                      device_id=(px, py, pc), device_id_type=MESH)
            sent.append(pltpu.make_async_remote_copy(src_ref=ins[a].at[peer] if a < n else ins[a], dst_ref=outs[a].at[me], **to))
            if with_arrivals:
                arrivals.append(pltpu.make_async_remote_copy(src_ref=own, dst_ref=outs[a].at[peer], **to))
    return local, sent, arrivals


def _exchange_issue(ins, outs, sems, n):
    local, sent, _ = _exchange_copies(ins, outs, sems, n, False)
    for cp in local + sent:
        cp.start()


def _exchange_finish(ins, outs, sems, n):
    local, sent, arrivals = _exchange_copies(ins, outs, sems, n, True)
    for cp in arrivals:
        cp.wait_recv()
    for cp in sent:
        cp.wait_send()
    for cp in local:
        cp.wait()


def _adam_math(w, parts, m, v):
    g = parts[0].astype(F32)
    for k in range(1, NDEV):
        g = g + parts[k].astype(F32)
    m2 = ADAM_B1 * m + (1.0 - ADAM_B1) * g
    v2 = ADAM_B2 * v + (1.0 - ADAM_B2) * jnp.square(g)
    m_hat = m2 / (1.0 - ADAM_B1 ** ADAM_STEP)
    v_hat = v2 / (1.0 - ADAM_B2 ** ADAM_STEP)
    return g, -ADAM_LR * (m_hat / (jnp.sqrt(v_hat) + ADAM_EPS) + ADAM_WD * w), m2, v2


def _adamw_many(ws, parts, ms, vs, name):
    n = len(ws)

    def body(*refs):
        w_refs, p_refs, m_refs, v_refs, outs = refs[:n], refs[n:2 * n], refs[2 * n:3 * n], refs[3 * n:4 * n], refs[4 * n:]
        for a in range(n):
            res = _adam_math(w_refs[a][...], [p_refs[a][k] for k in range(NDEV)], m_refs[a][...], v_refs[a][...])
            for o, val in zip(outs[4 * a:4 * a + 4], res):
                o[...] = val

    out = pl.pallas_call(body, out_shape=[jax.ShapeDtypeStruct(w.shape, F32) for w in ws for _ in range(4)],
                         compiler_params=pltpu.CompilerParams(vmem_limit_bytes=VMEM_LIMIT), name=name)(*ws, *parts, *ms, *vs)
    return [out[4 * a:4 * a + 4] for a in range(n)]


def _adamw_rows(w, parts, m, v, tr, name):
    rows, cols = w.shape

    def body(w_ref, p_ref, m_ref, v_ref, g_ref, d_ref, m2_ref, v2_ref):
        res = _adam_math(w_ref[...], [p_ref[k] for k in range(NDEV)], m_ref[...], v_ref[...])
        for o, val in zip((g_ref, d_ref, m2_ref, v2_ref), res):
            o[...] = val

    flat = pl.BlockSpec((tr, cols), lambda i: (i, 0))
    return pl.pallas_call(body, grid=(rows // tr,),
                          in_specs=[flat, pl.BlockSpec((NDEV, tr, cols), lambda i: (0, i, 0)), flat, flat],
                          out_specs=[flat] * 4, out_shape=[jax.ShapeDtypeStruct((rows, cols), F32)] * 4,
                          compiler_params=_cparams(1), name=name)(w, parts, m, v)


SHARDED = (("mla_w_a", 1), ("mla_w_uq", 2), ("mla_w_ukv", 2), ("mla_w_o", 1), ("ssm_w_in", 1), ("ssm_d", 1),
           ("ssm_w_glu", 2), ("ffn_w_up", 2), ("ffn_conv_w", 2), ("ffn_w_down", 1))
WIRE_EXACT = ("ssm_d", "ffn_conv_w")
GATHERED_FIRST = ("mla_w_a", "mla_w_uq", "mla_w_ukv")
REPLICATED_LAST = ("mla_g_q", "mla_g_kv", "g_mix")
REPLICATED = ("mla_g_q", "mla_g_kv", "ssm_lambda_re", "ssm_lambda_im", "ssm_log_dt", "ssm_b_re", "ssm_b_im",
              "ssm_c_re", "ssm_c_im", "ffn_conv_b", "g_mix", "g_ffn", "g_final")
WEIGHTS = ("mla_w_a", "mla_g_q", "mla_g_kv", "mla_w_uq", "mla_w_ukv", "mla_w_o", "ssm_w_in", "ssm_lambda_re",
           "ssm_lambda_im", "ssm_log_dt", "ssm_b_re", "ssm_b_im", "ssm_c_re", "ssm_c_im", "ssm_d", "ssm_w_glu",
           "ffn_w_up", "ffn_conv_w", "ffn_conv_b", "ffn_w_down", "g_mix", "g_ffn", "g_final")


ADAM_ROW_TILED = (("ffn_w_up", 256), ("ffn_w_down", 176))


def _two_d(shape):
    if len(shape) == 1:
        return (1, shape[0])
    if len(shape) > 2 and shape[-1] < LANES:
        return (math.prod(shape[:-2]), shape[-2] * shape[-1])
    return (math.prod(shape[:-1]), shape[-1])


def kernel(x, positions, mla_w_a, mla_g_q, mla_g_kv, mla_w_uq, mla_w_ukv, mla_w_o, ssm_w_in, ssm_lambda_re, ssm_lambda_im, ssm_log_dt, ssm_b_re, ssm_b_im, ssm_c_re, ssm_c_im, ssm_d, ssm_w_glu, ffn_w_up, ffn_conv_w, ffn_conv_b, ffn_w_down, g_mix, g_ffn, g_final, loss_target, m_mla_w_a, m_mla_g_q, m_mla_g_kv, m_mla_w_uq, m_mla_w_ukv, m_mla_w_o, m_ssm_w_in, m_ssm_lambda_re, m_ssm_lambda_im, m_ssm_log_dt, m_ssm_b_re, m_ssm_b_im, m_ssm_c_re, m_ssm_c_im, m_ssm_d, m_ssm_w_glu, m_ffn_w_up, m_ffn_conv_w, m_ffn_conv_b, m_ffn_w_down, m_g_mix, m_g_ffn, m_g_final, v_mla_w_a, v_mla_g_q, v_mla_g_kv, v_mla_w_uq, v_mla_w_ukv, v_mla_w_o, v_ssm_w_in, v_ssm_lambda_re, v_ssm_lambda_im, v_ssm_log_dt, v_ssm_b_re, v_ssm_b_im, v_ssm_c_re, v_ssm_c_im, v_ssm_d, v_ssm_w_glu, v_ffn_w_up, v_ffn_conv_w, v_ffn_conv_b, v_ffn_w_down, v_g_mix, v_g_ffn, v_g_final):
    a = dict(locals())
    s_len = x.shape[1]
    sh_names = [n for n, _ in SHARDED]

    def wire(n):
        return a[n] if n in WIRE_EXACT else a[n].astype(MXU)

    early = [(n, axis) for n, axis in SHARDED if n in GATHERED_FIRST]
    late = [(n, axis, wire(n)) for n, axis in SHARDED if n not in GATHERED_FIRST]
    w = {n: a[n] for n in REPLICATED}
    for (n, axis), g in zip(early, _gather_many([wire(n) for n, _ in early], "gather_weights")):
        w[n] = _merge_shards(g, axis)

    rep_late = [n for n in REPLICATED if n not in REPLICATED_LAST]
    loss, dx, grads, landed_late, rep_late_all = _sequence_step(
        x[0], positions.reshape(s_len, 1).astype(F32), loss_target[0], w, late, rep_late)
    loss = lax.psum(loss, ("x", "y", "c"))

    packed = jnp.concatenate([grads[n].reshape(-1) for n in REPLICATED_LAST])
    landed_early, (rep_last_all,) = _exchange_many([_split_shards(grads[n], axis).astype(MXU) for n, axis in early],
                                                   [packed], "exchange_grads")
    parts_of = dict(zip([n for n, _, _ in late], landed_late)) | dict(zip([n for n, _ in early], landed_early))
    for names, pack in ((rep_late, rep_late_all), (REPLICATED_LAST, rep_last_all)):
        off = 0
        for n in names:
            size = math.prod(a[n].shape)
            parts_of[n] = pack[:, off:off + size]
            off += size

    def view(n, arr, lead=()):
        return arr.reshape(lead + _two_d(a[n].shape))

    out = {}

    def finish(n, res):
        for kind, val in zip(("grad_", "delta_", "new_m_", "new_v_"), res):
            out[kind + n] = val.reshape(a[n].shape)

    for n, tr in ADAM_ROW_TILED:
        finish(n, _adamw_rows(view(n, a[n]), view(n, parts_of[n], (NDEV,)), view(n, a["m_" + n]), view(n, a["v_" + n]),
                              tr, "adamw_" + n))
    tiled = [n for n, _ in ADAM_ROW_TILED]
    for names, tag in (([n for n in sh_names if n not in tiled], "adamw_sharded"), (list(REPLICATED), "adamw_replicated")):
        res = _adamw_many([view(n, a[n]) for n in names], [view(n, parts_of[n], (NDEV,)) for n in names],
                          [view(n, a["m_" + n]) for n in names], [view(n, a["v_" + n]) for n in names], tag)
        for n, r in zip(names, res):
            finish(n, r)
    return (loss, dx[None], *[out[kind + n] for kind in ("grad_", "delta_", "new_m_", "new_v_") for n in WEIGHTS])
```

```python
import functools
import math

import jax
import jax.numpy as jnp
from jax import lax
from jax.experimental import pallas as pl
from jax.experimental.pallas import tpu as pltpu

F32 = jnp.float32
MXU = jnp.bfloat16

D = 1024
HEADS = 8
NOPE = 128
ROPE = 64
VH = 128
QL = 384
KVL = 256
CHUNK = 64
ROPE_THETA = 10000.0
EPS = 1e-6
G, P, C = 64, 64, 16
DFF = 2816
ADAM_LR, ADAM_B1, ADAM_B2, ADAM_EPS, ADAM_WD, ADAM_STEP = 0.001, 0.9, 0.999, 1e-08, 0.01, 10

LANES = 128
AW = 768
HC = 256
GB = 8
SL = (G // GB) * P
NCOL = 2 * SL // LANES
TC = DFF
HALO = 16
NDEV = 8
VMEM_LIMIT = 56 * 1024 * 1024


def _mm(a, b):
    return jnp.dot(a.astype(MXU), b.astype(MXU), preferred_element_type=F32)


def _mm_tn(a, b):
    return lax.dot_general(a.astype(MXU), b.astype(MXU), (((0,), (0,)), ((), ())), preferred_element_type=F32)


def _mm_nt(a, b):
    return lax.dot_general(a.astype(MXU), b.astype(MXU), (((1,), (1,)), ((), ())), preferred_element_type=F32)


def _rms_fwd(x, g):
    r = lax.rsqrt(jnp.mean(x * x, axis=-1, keepdims=True) + EPS)
    xh = x * r
    return xh * g, xh, r


def _rms_bwd(dy, xh, r, g):
    dxh = dy * g
    dx = r * (dxh - xh * jnp.mean(dxh * xh, axis=-1, keepdims=True))
    return dx, jnp.sum(dy * xh, axis=0, keepdims=True)


def _rot_partner(b):
    lane = lax.broadcasted_iota(jnp.int32, b.shape, 1)
    return jnp.where(lane < ROPE // 2, -pltpu.roll(b, LANES - ROPE // 2, 1), pltpu.roll(b, ROPE // 2, 1))


def _rope_blk(b, cos2, sin2):
    return b * cos2 + _rot_partner(b) * sin2


def _unrope_blk(db, cos2, sin2):
    return db * cos2 - _rot_partner(db * sin2)


def _cparams(n_axes, vmem=VMEM_LIMIT):
    return pltpu.CompilerParams(dimension_semantics=("arbitrary",) * n_axes, vmem_limit_bytes=vmem)


def _rowcall(name, fn, tm, row_ins, consts, row_outs, acc_outs):
    n = row_ins[0].shape[0]
    n_in = len(row_ins) + len(consts)
    n_ro = len(row_outs)

    def body(*refs):
        ins, ro_refs, acc_refs = refs[:n_in], refs[n_in:n_in + n_ro], refs[n_in + n_ro:]
        ro, ao = fn(*[r[...] for r in ins])

        @pl.when(pl.program_id(0) == 0)
        def _():
            for r in acc_refs:
                r[...] = jnp.zeros(r.shape, r.dtype)

        for r, val in zip(ro_refs, ro):
            r[...] = val.astype(r.dtype)
        for r, val in zip(acc_refs, ao):
            r[...] += val

    in_specs = [pl.BlockSpec((tm, a.shape[1]), lambda i: (i, 0)) for a in row_ins]
    in_specs += [pl.BlockSpec(c.shape, lambda i, nd=c.ndim: (0,) * nd) for c in consts]
    out_specs = [pl.BlockSpec((tm, w), lambda i: (i, 0)) for w, _ in row_outs]
    out_specs += [pl.BlockSpec(s, lambda i, nd=len(s): (0,) * nd) for s in acc_outs]
    out_shape = [jax.ShapeDtypeStruct((n, w), dt) for w, dt in row_outs]
    out_shape += [jax.ShapeDtypeStruct(s, F32) for s in acc_outs]
    out = pl.pallas_call(body, grid=(n // tm,), in_specs=in_specs, out_specs=out_specs, out_shape=out_shape,
                         compiler_params=_cparams(1), name=name)(*row_ins, *consts)
    return list(out[:n_ro]), list(out[n_ro:])


def _mla_front_tile(x, pos, gmix, wa, gq, gkv, wuq, wukv, invf):
    hn, xh, r = _rms_fwd(x, gmix)
    a = _mm(hn, wa)
    cq, ckv, krb = a[:, :QL], a[:, QL:QL + KVL], a[:, QL + KVL:]
    cqn, cqh, rq = _rms_fwd(cq, gq)
    ckvn, ckvh, rkv = _rms_fwd(ckv, gkv)
    q = _mm(cqn, wuq)
    kv = _mm(ckvn, wukv)
    ang = pos * invf
    cos2, sin2 = jnp.cos(ang), jnp.sin(ang)
    krr = _rope_blk(krb, cos2, sin2)
    qp, kp, vp = [], [], []
    for h in range(HEADS):
        qp += [q[:, h * HC:h * HC + NOPE], _rope_blk(q[:, h * HC + NOPE:(h + 1) * HC], cos2, sin2)]
        kp += [kv[:, h * HC:h * HC + NOPE], krr]
        vp += [kv[:, h * HC + NOPE:(h + 1) * HC]]
    res = (hn, xh, r, cqn, cqh, rq, ckvn, ckvh, rkv, cos2, sin2)
    return jnp.concatenate(qp, axis=1), jnp.concatenate(kp, axis=1), jnp.concatenate(vp, axis=1), res


def _mla_front_fwd(x, pos, gmix, wa, gq, gkv, wuq, wukv, invf, tm):
    def fn(*args):
        qc, kc, v, _ = _mla_front_tile(*args)
        return (qc * Q_PRESCALE, kc, v), ()

    return _rowcall("mla_front_fwd", fn, tm, [x, pos], [gmix, wa, gq, gkv, wuq, wukv, invf],
                    [(HEADS * HC, MXU), (HEADS * HC, MXU), (HEADS * VH, MXU)], [])[0]


def _mla_front_bwd(x, pos, dqc, dkc, dv, dh, gmix, wa, gq, gkv, wuq, wukv, invf, wa_t, wuq_t, wukv_t, tm):
    def fn(x, pos, dqc, dkc, dv, dh, gmix, wa, gq, gkv, wuq, wukv, invf, wa_t, wuq_t, wukv_t):
        _, _, _, (hn, xh, r, cqn, cqh, rq, ckvn, ckvh, rkv, cos2, sin2) = _mla_front_tile(
            x, pos, gmix, wa, gq, gkv, wuq, wukv, invf)
        dqc, dkc, dv = dqc * SM_SCALE, dkc.astype(F32), dv.astype(F32)
        dqp, dkvp = [], []
        dkr = jnp.zeros((x.shape[0], LANES), F32)
        for h in range(HEADS):
            dqp += [dqc[:, h * HC:h * HC + NOPE], _unrope_blk(dqc[:, h * HC + NOPE:(h + 1) * HC], cos2, sin2)]
            dkvp += [dkc[:, h * HC:h * HC + NOPE], dv[:, h * VH:(h + 1) * VH]]
            dkr = dkr + dkc[:, h * HC + NOPE:(h + 1) * HC]
        dq = jnp.concatenate(dqp, axis=1)
        dkv = jnp.concatenate(dkvp, axis=1)
        dkrb = _unrope_blk(dkr, cos2, sin2)
        dcqn = _mm(dq, wuq_t)
        dckvn = _mm(dkv, wukv_t)
        d_wuq = _mm_tn(cqn, dq)
        d_wukv = _mm_tn(ckvn, dkv)
        dcq, d_gq = _rms_bwd(dcqn, cqh, rq, gq)
        dckv, d_gkv = _rms_bwd(dckvn, ckvh, rkv, gkv)
        da = jnp.concatenate([dcq, dckv, dkrb], axis=1)
        d_wa = _mm_tn(hn, da)
        dhn = _mm(da, wa_t)
        dx, d_gmix = _rms_bwd(dhn, xh, r, gmix)
        return (dh + dx,), (d_wa, d_wuq, d_wukv, d_gq, d_gkv, d_gmix)

    return _rowcall("mla_front_bwd", fn, tm, [x, pos, dqc, dkc, dv, dh],
                    [gmix, wa, gq, gkv, wuq, wukv, invf, wa_t, wuq_t, wukv_t], [(D, F32)],
                    [(D, AW), (QL, HEADS * HC), (KVL, HEADS * HC), (1, QL), (1, KVL), (1, D)])


SM_SCALE = (NOPE + ROPE) ** -0.5
LOG2E = 1.0 / math.log(2.0)
Q_PRESCALE = SM_SCALE * LOG2E


def _pair_tables(s_len, tq, tk, q_major):
    pairs = [(qi, ki) for qi in range(s_len // tq) for ki in range(s_len // tk) if ki * tk < (qi + 1) * tq]
    if not q_major:
        pairs.sort(key=lambda p: (p[1], p[0]))
    return (jnp.asarray([p[0] for p in pairs], jnp.int32), jnp.asarray([p[1] for p in pairs], jnp.int32))


def _last_key_tile(qi, tq, tk):
    return ((qi + 1) * tq - 1) // tk


def _visible(qi, ki, tq, tk, width=None):
    shape = (tq, tk if width is None else width)
    row = qi * (tq // CHUNK) + lax.broadcasted_iota(jnp.int32, shape, 0) // CHUNK
    col = ki * (tk // CHUNK) + lax.broadcasted_iota(jnp.int32, shape, 1) // CHUNK
    return col <= row


def _masked_and_not(qi, ki, tq, tk, fn):
    needs_mask = (ki + 1) * tk > qi * tq
    pl.when(needs_mask)(lambda: fn(True))
    pl.when(jnp.logical_not(needs_mask))(lambda: fn(False))


def _by_visible_width(qi, ki, tq, tk, fn):
    if tk % tq or tk == tq:
        _masked_and_not(qi, ki, tq, tk, lambda masked: fn(tk, masked))
        return
    blocks = tk // tq
    seen = qi + 1 - ki * blocks
    pl.when(seen > blocks)(lambda: fn(tk, False))
    for j in range(1, blocks + 1):
        pl.when(seen == j)(functools.partial(fn, j * tq, True))


def _first_and_last_step(n_steps):
    first = jnp.logical_and(pl.program_id(0) == 0, pl.program_id(1) == 0)
    last = jnp.logical_and(pl.program_id(0) == HEADS - 1, pl.program_id(1) == n_steps - 1)
    return first, last


def _flash_fwd(qc, kc, v, tq, tk, riders):
    s_len = qc.shape[0]
    qt, kt = _pair_tables(s_len, tq, tk, True)
    nr = len(riders)

    def body(qt_ref, kt_ref, q_ref, k_ref, v_ref, *rest):
        r_in, (o_ref, lse_ref), r_out = rest[:nr], rest[nr:nr + 2], rest[nr + 2:2 * nr + 2]
        m_sc, l_sc, acc_sc = rest[2 * nr + 2:2 * nr + 5]
        sems = rest[2 * nr + 5:]
        p_id = pl.program_id(1)
        qi, ki = qt_ref[p_id], kt_ref[p_id]
        first, last = _first_and_last_step(qt.shape[0])
        if nr:
            pl.when(first)(lambda: _gather_issue(r_in, r_out, sems))

        @pl.when(ki == 0)
        def _():
            m_sc[...] = jnp.full(m_sc.shape, -jnp.inf, F32)
            l_sc[...] = jnp.zeros(l_sc.shape, F32)
            acc_sc[...] = jnp.zeros(acc_sc.shape, F32)

        def update(width, masked):
            s = _mm_nt(q_ref[...], k_ref[:width, :])
            if masked:
                s = jnp.where(_visible(qi, ki, tq, tk, width), s, -jnp.inf)
            m_old = m_sc[...]
            m_new = jnp.maximum(m_old, jnp.max(s, axis=1, keepdims=True))
            alpha = jnp.exp2(m_old - m_new)
            p = jnp.exp2(s - m_new)
            l_sc[...] = alpha * l_sc[...] + jnp.sum(p, axis=1, keepdims=True)
            acc_sc[...] = alpha * acc_sc[...] + _mm(p, v_ref[:width, :])
            m_sc[...] = m_new

        _by_visible_width(qi, ki, tq, tk, update)

        @pl.when(ki == _last_key_tile(qi, tq, tk))
        def _():
            l = l_sc[...]
            o_ref[...] = (acc_sc[...] / l).astype(o_ref.dtype)
            lse_ref[...] = jnp.broadcast_to(m_sc[...] + jnp.log2(l), lse_ref.shape)

        if nr:
            pl.when(last)(lambda: _gather_finish(r_in, r_out, sems))

    qmap = lambda h, p, qt, kt: (qt[p], h)
    kmap = lambda h, p, qt, kt: (kt[p], h)
    grid_spec = pltpu.PrefetchScalarGridSpec(
        num_scalar_prefetch=2, grid=(HEADS, qt.shape[0]),
        in_specs=[pl.BlockSpec((tq, HC), qmap), pl.BlockSpec((tk, HC), kmap), pl.BlockSpec((tk, VH), kmap)] + [ANY] * nr,
        out_specs=[pl.BlockSpec((tq, VH), qmap), pl.BlockSpec((tq, LANES), qmap)] + [ANY] * nr,
        scratch_shapes=[pltpu.VMEM((tq, 1), F32), pltpu.VMEM((tq, 1), F32), pltpu.VMEM((tq, VH), F32)] + _comm_scratch(nr))
    out = pl.pallas_call(body, grid_spec=grid_spec,
                         out_shape=[jax.ShapeDtypeStruct((s_len, HEADS * VH), MXU),
                                    jax.ShapeDtypeStruct((s_len, HEADS * LANES), F32)] + _gathered_shapes(riders),
                         compiler_params=_cparams(2), name="flash_fwd")(qt, kt, qc, kc, v, *riders)
    return out[0], out[1], out[2:]


def _tile_dscores(q, k, v, do, lse, delta, qi, ki, tq, tk, width, masked):
    p = jnp.exp2(_mm_nt(q, k) - lse[:, :1])
    if masked:
        p = jnp.where(_visible(qi, ki, tq, tk, width), p, 0.0)
    return p, p * (_mm_nt(do, v) - delta[:, :1])


def _flash_bwd(qc, kc, v, do, lse, delta, tq, tk, riders, casts=()):
    s_len = qc.shape[0]
    qt, kt = _pair_tables(s_len, tq, tk, False)
    nq = s_len // tq
    n_rows, nr = len(riders), len(riders) + len(casts)

    def body(qt_ref, kt_ref, q_ref, k_ref, v_ref, do_ref, lse_ref, dl_ref, *rest):
        r_in, (dq_ref, dk_ref, dv_ref), r_out = rest[:nr], rest[nr:nr + 3], rest[nr + 3:2 * nr + 3]
        dk_sc, dv_sc = rest[2 * nr + 3:2 * nr + 5]
        sems = rest[2 * nr + 5:]
        p_id = pl.program_id(1)
        qi, ki = qt_ref[p_id], kt_ref[p_id]
        rows = pl.ds(pl.multiple_of(qi * tq, tq), tq)
        first, last = _first_and_last_step(qt.shape[0])
        if nr:
            pl.when(first)(lambda: _exchange_issue(r_in, r_out, sems, n_rows))

        @pl.when(qi == (ki * tk) // tq)
        def _():
            dk_sc[...] = jnp.zeros(dk_sc.shape, F32)
            dv_sc[...] = jnp.zeros(dv_sc.shape, F32)

        @pl.when(ki == 0)
        def _():
            dq_ref[rows, :] = jnp.zeros((tq, HC), F32)

        def update(width, masked):
            q, k, do = q_ref[...], k_ref[:width, :], do_ref[...]
            p, ds = _tile_dscores(q, k, v_ref[:width, :], do, lse_ref[...], dl_ref[...], qi, ki, tq, tk, width, masked)
            ds = ds.astype(MXU)
            dv_sc[:width, :] += _mm_tn(p, do)
            dk_sc[:width, :] += _mm_tn(ds, q)
            dq_ref[rows, :] += _mm(ds, k)

        _by_visible_width(qi, ki, tq, tk, update)

        @pl.when(qi == nq - 1)
        def _():
            dk_ref[...] = (dk_sc[...] * (1.0 / LOG2E)).astype(dk_ref.dtype)
            dv_ref[...] = dv_sc[...].astype(dv_ref.dtype)

        if nr:
            pl.when(last)(lambda: _exchange_finish(r_in, r_out, sems, n_rows))

    qmap = lambda h, p, qt, kt: (qt[p], h)
    kmap = lambda h, p, qt, kt: (kt[p], h)
    grid_spec = pltpu.PrefetchScalarGridSpec(
        num_scalar_prefetch=2, grid=(HEADS, qt.shape[0]),
        in_specs=[pl.BlockSpec((tq, HC), qmap), pl.BlockSpec((tk, HC), kmap), pl.BlockSpec((tk, VH), kmap),
                  pl.BlockSpec((tq, VH), qmap), pl.BlockSpec((tq, LANES), qmap), pl.BlockSpec((tq, LANES), qmap)] + [ANY] * nr,
        out_specs=[pl.BlockSpec((s_len, HC), lambda h, p, qt, kt: (0, h), pipeline_mode=pl.Buffered(1)),
                   pl.BlockSpec((tk, HC), kmap), pl.BlockSpec((tk, VH), kmap)] + [ANY] * nr,
        scratch_shapes=[pltpu.VMEM((tk, HC), F32), pltpu.VMEM((tk, VH), F32)] + _comm_scratch(nr))
    out = pl.pallas_call(body, grid_spec=grid_spec,
                         out_shape=[jax.ShapeDtypeStruct((s_len, HEADS * HC), F32),
                                    jax.ShapeDtypeStruct((s_len, HEADS * HC), MXU),
                                    jax.ShapeDtypeStruct((s_len, HEADS * VH), MXU)] + _exchanged_shapes(riders, casts),
                         compiler_params=_cparams(2), name="flash_bwd")(qt, kt, qc, kc, v, do, lse, delta, *riders, *casts)
    return out[0], out[1], out[2], out[3:3 + n_rows], out[3 + n_rows:]


def _attn_out_fwd(x, o, wo, tm):
    def fn(x, o, wo):
        return (x + _mm(o, wo),), ()

    return _rowcall("attn_out_fwd", fn, tm, [x, o], [wo], [(D, F32)], [])[0][0]


def _attn_out_bwd(dh, o, wo_t, tm):
    def fn(dh, o, wo_t):
        do = _mm(dh, wo_t)
        of = o.astype(F32)
        dl = [jnp.broadcast_to(jnp.sum(do[:, h * VH:(h + 1) * VH] * of[:, h * VH:(h + 1) * VH], axis=1, keepdims=True),
                               (dh.shape[0], LANES)) for h in range(HEADS)]
        return (do, jnp.concatenate(dl, axis=1)), (_mm_tn(o, dh),)

    return _rowcall("attn_out_bwd", fn, tm, [dh, o], [wo_t], [(HEADS * VH, MXU), (HEADS * LANES, F32)], [(HEADS * VH, D)])


def _shift_rows(a, k):
    return a if k == 0 else pltpu.roll(a, k % a.shape[0], 0)


def _stack_rows(rows):
    idx = lax.broadcasted_iota(jnp.int32, (8, rows[0].shape[1]), 0)
    out = jnp.zeros((8, rows[0].shape[1]), F32)
    for k, r in enumerate(rows):
        out = jnp.where(idx == k, r, out)
    return out


def _ffn_fwd(h, g, wup, cw, cb, wdown, tm):
    s_len = h.shape[0]
    nj = DFF // TC
    hb = tm // HALO

    def body(h_ref, hp_ref, g_ref, wv_ref, wg_ref, cwv_ref, cwg_ref, cbv_ref, cbg_ref, wd_ref, out_ref, act_ref,
             upv_ref, upg_ref, hn_sc, acc_sc):
        i, j = pl.program_id(0), pl.program_id(1)

        @pl.when(j == 0)
        def _():
            gg = g_ref[...]
            hp = _rms_fwd(hp_ref[...], gg)[0]
            hn_sc[:HALO, :] = jnp.where(i > 0, hp, 0.0).astype(MXU)
            hn_sc[HALO:, :] = _rms_fwd(h_ref[...], gg)[0].astype(MXU)
            acc_sc[...] = jnp.zeros(acc_sc.shape, F32)

        hn = hn_sc[...]

        def conv(w_ref, cw_ref, cb_ref, up_ref):
            up = jnp.dot(hn, w_ref[...], preferred_element_type=F32)
            up_ref[...] = up[HALO:].astype(up_ref.dtype)
            cwv = cw_ref[...]
            c = cwv[2:3] * up + cwv[1:2] * _shift_rows(up, 1) + cwv[0:1] * _shift_rows(up, 2)
            return c[HALO:] + cb_ref[...]

        cv = conv(wv_ref, cwv_ref, cbv_ref, upv_ref)
        cg = conv(wg_ref, cwg_ref, cbg_ref, upg_ref)
        act = cg * jax.nn.sigmoid(cg) * cv
        act_ref[...] = act.astype(act_ref.dtype)
        acc_sc[...] += _mm(act, wd_ref[...])

        @pl.when(j == nj - 1)
        def _():
            out_ref[...] = h_ref[...] + acc_sc[...]

    in_specs = [pl.BlockSpec((tm, D), lambda i, j: (i, 0)),
                pl.BlockSpec((HALO, D), lambda i, j: (jnp.maximum(i * hb - 1, 0), 0)),
                pl.BlockSpec((1, D), lambda i, j: (0, 0)),
                pl.BlockSpec((D, TC), lambda i, j: (0, j)), pl.BlockSpec((D, TC), lambda i, j: (0, j + nj)),
                pl.BlockSpec((3, TC), lambda i, j: (0, j)), pl.BlockSpec((3, TC), lambda i, j: (0, j + nj)),
                pl.BlockSpec((1, TC), lambda i, j: (0, j)), pl.BlockSpec((1, TC), lambda i, j: (0, j + nj)),
                pl.BlockSpec((TC, D), lambda i, j: (j, 0))]
    out_specs = [pl.BlockSpec((tm, D), lambda i, j: (i, 0))] + [pl.BlockSpec((tm, TC), lambda i, j: (i, j))] * 3
    return pl.pallas_call(body, grid=(s_len // tm, nj), in_specs=in_specs, out_specs=out_specs,
                          out_shape=[jax.ShapeDtypeStruct((s_len, D), F32)] + [jax.ShapeDtypeStruct((s_len, DFF), MXU)] * 3,
                          scratch_shapes=[pltpu.VMEM((tm + HALO, D), MXU), pltpu.VMEM((tm, D), F32)],
                          compiler_params=_cparams(2), name="ffn_fwd")(h, h, g, wup, wup, cw, cw, cb, cb, wdown)


def _ffn_bwd(h, dout, upv, upg, g, cw, cb, wdown, wup, tm):
    s_len = h.shape[0]
    ni = s_len // tm
    hb = tm // HALO

    def body(h_ref, d_ref, dx_ref, uvp_ref, uv_ref, uvx_ref, ugp_ref, ug_ref, ugx_ref, g_ref, cwv_ref, cwg_ref, cbv_ref, cbg_ref,
             wd_ref, wv_ref, wg_ref, din_ref, dupv_ref, dupg_ref, hn_ref, cacc_ref, dg_ref):
        i = pl.program_id(0)

        @pl.when(i == 0)
        def _():
            cacc_ref[...] = jnp.zeros(cacc_ref.shape, F32)
            dg_ref[...] = jnp.zeros(dg_ref.shape, F32)

        gg = g_ref[...]
        hn, xh, r = _rms_fwd(h_ref[...], gg)
        hn_ref[...] = hn.astype(hn_ref.dtype)
        dd = jnp.concatenate([d_ref[...], jnp.where(i < ni - 1, dx_ref[...], 0.0)], axis=0).astype(MXU)
        dact = _mm_nt(dd, wd_ref[...])

        def half(p_ref, t_ref, x_ref, cw_ref, cb_ref):
            prev = p_ref[...]
            up = jnp.concatenate([jnp.where(i > 0, prev, jnp.zeros_like(prev)), t_ref[...], x_ref[...]], axis=0).astype(F32)
            cwv = cw_ref[...]
            u1, u2 = _shift_rows(up, 1), _shift_rows(up, 2)
            c = (cwv[2:3] * up + cwv[1:2] * u1 + cwv[0:1] * u2)[HALO:] + cb_ref[...]
            return c, (up[HALO:HALO + tm], u1[HALO:HALO + tm], u2[HALO:HALO + tm]), cwv

        cv, upsv, cwv = half(uvp_ref, uv_ref, uvx_ref, cwv_ref, cbv_ref)
        cg, upsg, cwg = half(ugp_ref, ug_ref, ugx_ref, cwg_ref, cbg_ref)
        sg = jax.nn.sigmoid(cg)
        dcv = dact * (cg * sg)
        dcg = dact * cv * (sg * (1.0 + cg * (1.0 - sg)))

        def back(dc, ups, cwx, slot, dup_ref, w_ref):
            dup = (cwx[2:3] * dc + cwx[1:2] * _shift_rows(dc, -1) + cwx[0:1] * _shift_rows(dc, -2))[:tm]
            dct = dc[:tm]
            cacc_ref[slot] += _stack_rows([jnp.sum(dct * ups[2], axis=0, keepdims=True),
                                           jnp.sum(dct * ups[1], axis=0, keepdims=True),
                                           jnp.sum(dct * ups[0], axis=0, keepdims=True),
                                           jnp.sum(dct, axis=0, keepdims=True)])
            dup_ref[...] = dup.astype(dup_ref.dtype)
            return _mm_nt(dup, w_ref[...])

        dhn = back(dcv, upsv, cwv, 0, dupv_ref, wv_ref) + back(dcg, upsg, cwg, 1, dupg_ref, wg_ref)
        dx, dgp = _rms_bwd(dhn, xh, r, gg)
        din_ref[...] = d_ref[...] + dx
        dg_ref[...] += dgp

    last_blk = s_len // HALO - 1
    tile = lambda w: pl.BlockSpec((tm, w), lambda i: (i, 0))
    prev = lambda w: pl.BlockSpec((HALO, w), lambda i: (jnp.maximum(i * hb - 1, 0), 0))
    nxt = lambda w: pl.BlockSpec((HALO, w), lambda i: (jnp.minimum((i + 1) * hb, last_blk), 0))
    const = lambda s, j=0: pl.BlockSpec(s, lambda i: (0, j))
    in_specs = [tile(D), tile(D), nxt(D), prev(DFF), tile(DFF), nxt(DFF), prev(DFF), tile(DFF), nxt(DFF),
                const((1, D)), const((3, DFF)), const((3, DFF), 1), const((1, DFF)), const((1, DFF), 1),
                pl.BlockSpec((DFF, D), lambda i: (0, 0)), const((D, DFF)), const((D, DFF), 1)]
    out_specs = [tile(D), tile(DFF), tile(DFF), tile(D), pl.BlockSpec((2, 8, DFF), lambda i: (0, 0, 0)), const((1, D))]
    out_shape = [jax.ShapeDtypeStruct((s_len, D), F32), jax.ShapeDtypeStruct((s_len, DFF), MXU),
                 jax.ShapeDtypeStruct((s_len, DFF), MXU), jax.ShapeDtypeStruct((s_len, D), MXU),
                 jax.ShapeDtypeStruct((2, 8, DFF), F32), jax.ShapeDtypeStruct((1, D), F32)]
    return pl.pallas_call(body, grid=(ni,), in_specs=in_specs, out_specs=out_specs, out_shape=out_shape,
                          compiler_params=_cparams(1), name="ffn_bwd")(
        h, dout, dout, upv, upv, upv, upg, upg, upg, g, cw, cw, cb, cb, wdown, wup, wup)


def _matmul_tn(a, b, tn, ts, name):
    s_len, m = a.shape
    n = b.shape[1]

    def body(a_ref, b_ref, o_ref):
        @pl.when(pl.program_id(1) == 0)
        def _():
            o_ref[...] = jnp.zeros(o_ref.shape, F32)

        o_ref[...] += _mm_tn(a_ref[...], b_ref[...])

    return pl.pallas_call(body, grid=(n // tn, s_len // ts),
                          in_specs=[pl.BlockSpec((ts, m), lambda jn, k: (k, 0)), pl.BlockSpec((ts, tn), lambda jn, k: (k, jn))],
                          out_specs=pl.BlockSpec((m, tn), lambda jn, k: (0, jn)),
                          out_shape=jax.ShapeDtypeStruct((m, n), F32), compiler_params=_cparams(2), name=name)(a, b)


def _s5_coefs(lr, li, ldt):
    dt = jnp.exp(ldt)
    mag = jnp.exp(lr * dt)
    th = li * dt
    ar, ai = mag * jnp.cos(th), mag * jnp.sin(th)
    den = lr * lr + li * li
    nr = ar - 1.0
    cr = (nr * lr + ai * li) / den
    ci = (ai * lr - nr * li) / den
    return dt, mag, th, ar, ai, den, nr, cr, ci


def _s5_prep(lr, li, ldt, braw, seg):
    def body(lr_ref, li_ref, ldt_ref, b_ref, bb_ref, ap_ref):
        lr_, li_, ldt_ = lr_ref[0], li_ref[0], ldt_ref[0]
        dt, mag, th, ar, ai, den, nr, cr, ci = _s5_coefs(lr_, li_, ldt_)
        br, bi = b_ref[0, :, :SL], b_ref[0, :, SL:]
        bb_ref[0, :, :SL] = (cr * br - ci * bi).astype(bb_ref.dtype)
        bb_ref[0, :, SL:] = (cr * bi + ci * br).astype(bb_ref.dtype)
        for i in range(seg):
            m = jnp.exp((i + 1.0) * (lr_ * dt))
            ap_ref[0, i * 8:(i + 1) * 8, :SL] = jnp.broadcast_to(m * jnp.cos((i + 1.0) * th), (8, SL))
            ap_ref[0, i * 8:(i + 1) * 8, SL:] = jnp.broadcast_to(m * jnp.sin((i + 1.0) * th), (8, SL))

    vec = pl.BlockSpec((1, 1, SL), lambda k: (k, 0, 0))
    return pl.pallas_call(
        body, grid=(GB,), in_specs=[vec, vec, vec, pl.BlockSpec((1, LANES, 2 * SL), lambda k: (k, 0, 0))],
        out_specs=[pl.BlockSpec((1, LANES, 2 * SL), lambda k: (k, 0, 0)),
                   pl.BlockSpec((1, 8 * seg, 2 * SL), lambda k: (k, 0, 0))],
        out_shape=[jax.ShapeDtypeStruct((GB, LANES, 2 * SL), MXU), jax.ShapeDtypeStruct((GB, 8 * seg, 2 * SL), F32)],
        compiler_params=_cparams(1), name="s5_prep")(lr, li, ldt, braw)


def _s5_prep_bwd(lr, li, ldt, braw, dbb, da):
    def body(lr_ref, li_ref, ldt_ref, b_ref, dbb_ref, da_ref, dbraw_ref, dlr_ref, dli_ref, dldt_ref):
        lr_, li_, ldt_ = lr_ref[0], li_ref[0], ldt_ref[0]
        dt, mag, th, ar, ai, den, nr, cr, ci = _s5_coefs(lr_, li_, ldt_)
        br, bi = b_ref[0, :, :SL], b_ref[0, :, SL:]
        gbr, gbi = dbb_ref[0, :, :SL], dbb_ref[0, :, SL:]
        dbraw_ref[0, :, :SL] = cr * gbr + ci * gbi
        dbraw_ref[0, :, SL:] = cr * gbi - ci * gbr
        dcr = jnp.sum(gbr * br + gbi * bi, axis=0, keepdims=True)
        dci = jnp.sum(gbi * br - gbr * bi, axis=0, keepdims=True)
        dar = jnp.sum(da_ref[0, :, :SL], axis=0, keepdims=True)
        dai = jnp.sum(da_ref[0, :, SL:], axis=0, keepdims=True)
        g1, g2 = dcr / den, dci / den
        gden = -(dcr * cr + dci * ci) / den
        gar = dar + g1 * lr_ - g2 * li_
        gai = dai + g1 * li_ + g2 * lr_
        glr = g1 * nr + g2 * ai + 2.0 * lr_ * gden
        gli = g1 * ai - g2 * nr + 2.0 * li_ * gden
        gmag = gar * jnp.cos(th) + gai * jnp.sin(th)
        gth = gai * ar - gar * ai
        dlr_ref[0] = glr + gmag * mag * dt
        dli_ref[0] = gli + gth * dt
        dldt_ref[0] = (gmag * mag * lr_ + gth * li_) * dt

    vec = pl.BlockSpec((1, 1, SL), lambda k: (k, 0, 0))
    mat = pl.BlockSpec((1, LANES, 2 * SL), lambda k: (k, 0, 0))
    return pl.pallas_call(
        body, grid=(GB,), in_specs=[vec, vec, vec, mat, mat, pl.BlockSpec((1, 8, 2 * SL), lambda k: (k, 0, 0))],
        out_specs=[mat, vec, vec, vec],
        out_shape=[jax.ShapeDtypeStruct((GB, LANES, 2 * SL), F32)] + [jax.ShapeDtypeStruct((GB, 1, SL), F32)] * 3,
        compiler_params=_cparams(1), name="s5_prep_bwd")(lr, li, ldt, braw, dbb, da)


def _bcast_row(tile, j):
    return jnp.broadcast_to(tile[j:j + 1, :], tile.shape)


def _tile_rows(i):
    return pl.ds(pl.multiple_of(i * 8, 8), 8)


def _col(c):
    return slice(c * LANES, (c + 1) * LANES)


def _permute_rows(ref, seg):
    return jnp.concatenate([ref[pl.ds(i, 8, stride=seg), :] for i in range(seg)], axis=0)


def _unpermute_rows(src, dst, seg):
    for j in range(8):
        dst[j * seg:(j + 1) * seg, :] = src[pl.ds(j, seg, stride=8), :]


SCAN_UNROLL = 2


def _segment_scan(src, dst, ap, seg, reverse):
    half = NCOL // 2
    sign = -1.0 if reverse else 1.0
    a_r = [ap[0:8, _col(c)] for c in range(half)]
    a_i = [ap[0:8, _col(half + c)] for c in range(half)]

    def step(n, carry):
        i = seg - 1 - n if reverse else n
        rows = _tile_rows(i)
        out_r, out_i = [], []
        for c in range(half):
            xr, xi = carry[c], carry[half + c]
            nr = a_r[c] * xr - sign * a_i[c] * xi + src[rows, _col(c)]
            ni = a_r[c] * xi + sign * a_i[c] * xr + src[rows, _col(half + c)]
            dst[rows, _col(c)] = nr
            dst[rows, _col(half + c)] = ni
            out_r.append(nr)
            out_i.append(ni)
        return tuple(out_r + out_i)

    zero = jnp.zeros((8, LANES), F32)
    return lax.fori_loop(0, seg, step, (zero,) * NCOL, unroll=SCAN_UNROLL)


def _segment_entries(ends, cin, ap, seg, reverse):
    half = NCOL // 2
    sign = -1.0 if reverse else 1.0
    al_r = [ap[(seg - 1) * 8:seg * 8, _col(c)] for c in range(half)]
    al_i = [ap[(seg - 1) * 8:seg * 8, _col(half + c)] for c in range(half)]
    row = lax.broadcasted_iota(jnp.int32, (8, LANES), 0)
    ent, out = [None] * NCOL, [None] * NCOL
    for c in range(half):
        zr, zi = cin[c], cin[half + c]
        er, ei = jnp.zeros((8, LANES), F32), jnp.zeros((8, LANES), F32)
        for j in (range(7, -1, -1) if reverse else range(8)):
            er, ei = jnp.where(row == j, zr, er), jnp.where(row == j, zi, ei)
            fr, fi = _bcast_row(ends[c], j), _bcast_row(ends[half + c], j)
            zr, zi = (al_r[c] * zr - sign * al_i[c] * zi + fr, al_r[c] * zi + sign * al_i[c] * zr + fi)
        ent[c], ent[half + c] = er, ei
        out[c], out[half + c] = zr, zi
    return ent, out


def _chunk_states(u, bb_ref, ap, cin, bu_sc, x_sc, seg):
    half = NCOL // 2
    bu_sc[...] = _mm(u, bb_ref[0])
    ends = _segment_scan(bu_sc, x_sc, ap, seg, False)
    ent, out = _segment_entries(ends, cin, ap, seg, False)

    def fix(i, _):
        rows = _tile_rows(i)
        for c in range(half):
            pr, pi = ap[rows, _col(c)], ap[rows, _col(half + c)]
            x_sc[rows, _col(c)] += pr * ent[c] - pi * ent[half + c]
            x_sc[rows, _col(half + c)] += pr * ent[half + c] + pi * ent[c]
        return 0

    lax.fori_loop(0, seg, fix, 0, unroll=SCAN_UNROLL)
    return out


def _s5_scan_fwd(u, bb, apow, cblk, dskip, t_chunk):
    s_len = u.shape[0]
    nc = s_len // t_chunk
    seg = t_chunk // 8

    def body(u_ref, bb_ref, ap_ref, c_ref, d_ref, y_ref, xin_ref, bu_sc, x_sc, y_sc, carry_sc):
        @pl.when(pl.program_id(1) == 0)
        def _():
            carry_sc[...] = jnp.zeros(carry_sc.shape, F32)

        uu = _permute_rows(u_ref, seg)
        cin = [carry_sc[:, _col(c)] for c in range(NCOL)]
        xin_ref[0, 0] = carry_sc[...]
        out = _chunk_states(uu, bb_ref, ap_ref.at[0], cin, bu_sc, x_sc, seg)
        for c in range(NCOL):
            carry_sc[:, _col(c)] = out[c]
        y_sc[...] = _mm(x_sc[...], c_ref[0]) + d_ref[...] * uu
        _unpermute_rows(y_sc, y_ref, seg)

    in_specs = [pl.BlockSpec((t_chunk, LANES), lambda k, c: (c, k)),
                pl.BlockSpec((1, LANES, 2 * SL), lambda k, c: (k, 0, 0)),
                pl.BlockSpec((1, 8 * seg, 2 * SL), lambda k, c: (k, 0, 0)),
                pl.BlockSpec((1, 2 * SL, LANES), lambda k, c: (k, 0, 0)),
                pl.BlockSpec((1, LANES), lambda k, c: (0, k))]
    out_specs = [pl.BlockSpec((t_chunk, LANES), lambda k, c: (c, k)),
                 pl.BlockSpec((1, 1, 8, 2 * SL), lambda k, c: (k, c, 0, 0))]
    return pl.pallas_call(body, grid=(GB, nc), in_specs=in_specs, out_specs=out_specs,
                          out_shape=[jax.ShapeDtypeStruct((s_len, D), F32), jax.ShapeDtypeStruct((GB, nc, 8, 2 * SL), F32)],
                          scratch_shapes=[pltpu.VMEM((t_chunk, 2 * SL), F32), pltpu.VMEM((t_chunk, 2 * SL), F32),
                                          pltpu.VMEM((t_chunk, LANES), F32), pltpu.VMEM((8, 2 * SL), F32)],
                          compiler_params=_cparams(2), name="s5_scan_fwd")(u, bb, apow, cblk, dskip)


def _s5_scan_bwd(u, dy, xin, bb, bb_t, apow, cblk_t, dskip, t_chunk):
    s_len = u.shape[0]
    nc = s_len // t_chunk
    seg = t_chunk // 8
    half = NCOL // 2

    def body(u_ref, dy_ref, xin_ref, bb_ref, bbt_ref, ap_ref, ct_ref, d_ref, du_ref, dbb_ref, dc_ref, da_ref, dd_ref,
             bu_sc, x_sc, g_sc, y_sc, carry_sc):
        @pl.when(pl.program_id(1) == 0)
        def _():
            carry_sc[...] = jnp.zeros(carry_sc.shape, F32)
            dbb_ref[...] = jnp.zeros(dbb_ref.shape, F32)
            dc_ref[...] = jnp.zeros(dc_ref.shape, F32)
            da_ref[...] = jnp.zeros(da_ref.shape, F32)
            dd_ref[...] = jnp.zeros(dd_ref.shape, F32)

        ap = ap_ref.at[0]
        uu, dyy = _permute_rows(u_ref, seg), _permute_rows(dy_ref, seg)
        cin = [xin_ref[0, 0, :, _col(c)] for c in range(NCOL)]
        _chunk_states(uu, bb_ref, ap, cin, bu_sc, x_sc, seg)
        bu_sc[...] = _mm(dyy, ct_ref[0])
        ends = _segment_scan(bu_sc, g_sc, ap, seg, True)
        lam_in = [carry_sc[:, _col(c)] for c in range(NCOL)]
        ent, out = _segment_entries(ends, lam_in, ap, seg, True)
        for c in range(NCOL):
            carry_sc[:, _col(c)] = out[c]
        row = lax.broadcasted_iota(jnp.int32, (8, LANES), 0)
        xp0 = [jnp.where(row == 0, cin[c], pltpu.roll(x_sc[(seg - 1) * 8:seg * 8, _col(c)], 1, 0)) for c in range(NCOL)]

        def fix(i, acc):
            rows, prev, tab = _tile_rows(i), _tile_rows(jnp.maximum(i - 1, 0)), _tile_rows(seg - 1 - i)
            new = list(acc)
            for c in range(half):
                pr, pi = ap[tab, _col(c)], ap[tab, _col(half + c)]
                lr_ = g_sc[rows, _col(c)] + pr * ent[c] + pi * ent[half + c]
                li_ = g_sc[rows, _col(half + c)] + pr * ent[half + c] - pi * ent[c]
                g_sc[rows, _col(c)] = lr_
                g_sc[rows, _col(half + c)] = li_
                xr = jnp.where(i == 0, xp0[c], x_sc[prev, _col(c)])
                xi = jnp.where(i == 0, xp0[half + c], x_sc[prev, _col(half + c)])
                new[c] = acc[c] + lr_ * xr + li_ * xi
                new[half + c] = acc[half + c] + li_ * xr - lr_ * xi
            return tuple(new)

        zero = jnp.zeros((8, LANES), F32)
        dacc = lax.fori_loop(0, seg, fix, (zero,) * NCOL, unroll=SCAN_UNROLL)
        for c in range(NCOL):
            da_ref[0, :, _col(c)] += dacc[c]
        lam = g_sc[...]
        y_sc[...] = _mm(lam, bbt_ref[0]) + d_ref[...] * dyy
        _unpermute_rows(y_sc, du_ref, seg)
        dbb_ref[0] += _mm_tn(uu, lam)
        dc_ref[0] += _mm_tn(x_sc[...], dyy)
        dd_ref[0] += _stack_rows([jnp.sum(dyy * uu, axis=0, keepdims=True)])

    rev = lambda k, c: (nc - 1 - c, k)
    in_specs = [pl.BlockSpec((t_chunk, LANES), rev), pl.BlockSpec((t_chunk, LANES), rev),
                pl.BlockSpec((1, 1, 8, 2 * SL), lambda k, c: (k, nc - 1 - c, 0, 0)),
                pl.BlockSpec((1, LANES, 2 * SL), lambda k, c: (k, 0, 0)),
                pl.BlockSpec((1, 2 * SL, LANES), lambda k, c: (k, 0, 0)),
                pl.BlockSpec((1, 8 * seg, 2 * SL), lambda k, c: (k, 0, 0)),
                pl.BlockSpec((1, LANES, 2 * SL), lambda k, c: (k, 0, 0)),
                pl.BlockSpec((1, LANES), lambda k, c: (0, k))]
    out_specs = [pl.BlockSpec((t_chunk, LANES), rev),
                 pl.BlockSpec((1, LANES, 2 * SL), lambda k, c: (k, 0, 0)),
                 pl.BlockSpec((1, 2 * SL, LANES), lambda k, c: (k, 0, 0)),
                 pl.BlockSpec((1, 8, 2 * SL), lambda k, c: (k, 0, 0)),
                 pl.BlockSpec((1, 8, LANES), lambda k, c: (k, 0, 0))]
    out_shape = [jax.ShapeDtypeStruct((s_len, D), F32), jax.ShapeDtypeStruct((GB, LANES, 2 * SL), F32),
                 jax.ShapeDtypeStruct((GB, 2 * SL, LANES), F32), jax.ShapeDtypeStruct((GB, 8, 2 * SL), F32),
                 jax.ShapeDtypeStruct((GB, 8, LANES), F32)]
    return pl.pallas_call(body, grid=(GB, nc), in_specs=in_specs, out_specs=out_specs, out_shape=out_shape,
                          scratch_shapes=[pltpu.VMEM((t_chunk, 2 * SL), F32)] * 3 + [pltpu.VMEM((t_chunk, LANES), F32),
                                                                                        pltpu.VMEM((8, 2 * SL), F32)],
                          compiler_params=_cparams(2), name="s5_scan_bwd")(u, dy, xin, bb, bb_t, apow, cblk_t, dskip)


_GELU_K = math.sqrt(2.0 / math.pi)


def _gelu(y):
    t = jnp.tanh(_GELU_K * (y + 0.044715 * (y * y * y)))
    return 0.5 * y * (1.0 + t), 0.5 * (1.0 + t) + 0.5 * y * (1.0 - t * t) * (_GELU_K * (1.0 + 3 * 0.044715 * (y * y)))


def _s5_in_fwd(h, gmix, win, tm):
    def fn(h, gmix, win):
        return (_mm(_rms_fwd(h, gmix)[0], win),), ()

    return _rowcall("s5_in_fwd", fn, tm, [h], [gmix, win], [(D, F32)], [])[0][0]


def _s5_in_bwd(h, du, dh, gmix, win_t, tm):
    def fn(h, du, dh, gmix, win_t):
        hn, xh, r = _rms_fwd(h, gmix)
        dx, dg = _rms_bwd(_mm(du, win_t), xh, r, gmix)
        return (dh + dx,), (_mm_tn(hn, du), dg)

    return _rowcall("s5_in_bwd", fn, tm, [h, du, dh], [gmix, win_t], [(D, F32)], [(D, D), (1, D)])


def _s5_out_fwd(h, y, wglu, tm):
    def fn(h, y, wglu):
        z = _mm(_gelu(y)[0], wglu)
        return (h + z[:, :D] * jax.nn.sigmoid(z[:, D:]),), ()

    return _rowcall("s5_out_fwd", fn, tm, [h, y], [wglu], [(D, F32)], [])[0][0]


def _s5_out_bwd(dh, y, wglu, wglu_t, tm):
    def fn(dh, y, wglu, wglu_t):
        yg, dgelu = _gelu(y)
        z = _mm(yg, wglu)
        val, sg = z[:, :D], jax.nn.sigmoid(z[:, D:])
        dz = jnp.concatenate([dh * sg, dh * val * sg * (1.0 - sg)], axis=1)
        return (_mm(dz, wglu_t) * dgelu,), (_mm_tn(yg, dz),)

    return _rowcall("s5_out_bwd", fn, tm, [dh, y], [wglu, wglu_t], [(D, F32)], [(D, 2 * D)])


def _final_loss(h, tgt, gfin, tm):
    def fn(h, tgt, gfin):
        y, xh, r = _rms_fwd(h, gfin)
        err = y - tgt
        dx, dg = _rms_bwd(err * (1.0 / D), xh, r, gfin)
        return (dx,), (jnp.sum(err * err, axis=0, keepdims=True), dg)

    return _rowcall("final_loss", fn, tm, [h, tgt], [gfin], [(D, F32)], [(1, D), (1, D)])


def _bf(a):
    return a.astype(MXU)


def _s5_block_mats(b_re, b_im, c_re, c_im):
    gl = G // GB
    eye = jnp.eye(gl, dtype=F32)

    def b_blk(b):
        bt = b.reshape(GB, gl, P, C).transpose(0, 1, 3, 2)
        return (bt[:, :, :, None, :] * eye[None, :, None, :, None]).reshape(GB, gl * C, gl * P)

    def c_blk(cm):
        ct = cm.reshape(GB, gl, C, P).transpose(0, 1, 3, 2)
        return (ct[:, :, :, None, :] * eye[None, :, None, :, None]).reshape(GB, gl * P, gl * C)

    braw = jnp.concatenate([b_blk(b_re), b_blk(b_im)], axis=2)
    cblk = jnp.concatenate([c_blk(c_re), -c_blk(c_im)], axis=1)
    return braw, cblk


def _s5_unblock_b(d):
    gl = G // GB
    eye = jnp.eye(gl, dtype=F32)
    d5 = d.reshape(GB, gl, C, gl, P)
    return jnp.sum(d5 * eye[None, :, None, :, None], axis=3).transpose(0, 1, 3, 2).reshape(G, P, C)


def _s5_unblock_c(d):
    gl = G // GB
    eye = jnp.eye(gl, dtype=F32)
    d5 = d.reshape(GB, gl, P, gl, C)
    return jnp.sum(d5 * eye[None, :, None, :, None], axis=3).transpose(0, 1, 3, 2).reshape(G, C, P)


def _tiles(s_len):
    return (min(512, s_len), min(512, s_len), min(2048, s_len),
            (min(1024, s_len), min(4096, s_len)), (min(1024, s_len), min(2048, s_len)),
            min(256, s_len), min(256, s_len))


def _merge_shards(g, axis):
    moved = jnp.moveaxis(g, 0, axis)
    return moved.reshape(moved.shape[:axis] + (-1,) + moved.shape[axis + 2:])


def _split_shards(full, axis):
    cut = full.reshape(full.shape[:axis] + (NDEV, -1) + full.shape[axis + 1:])
    return jnp.moveaxis(cut, axis, 0)


def _sequence_step(x, pos, tgt, w, late=(), late_small=()):
    s_len = x.shape[0]
    tm_w, tm, t_chunk, t_fwd, t_bwd, tm_ff, tm_fb = _tiles(s_len)
    seg = t_chunk // 8
    row = lambda v: v.reshape(1, -1)

    wa = _bf(jnp.pad(w["mla_w_a"][0], ((0, 0), (0, AW - (QL + KVL + ROPE)))))
    wuq = _bf(jnp.pad(w["mla_w_uq"][0].reshape(QL, HEADS, NOPE + ROPE), ((0, 0), (0, 0), (0, HC - NOPE - ROPE))).reshape(QL, HEADS * HC))
    wukv = _bf(w["mla_w_ukv"][0])
    inv = 1.0 / (ROPE_THETA ** (jnp.arange(0, ROPE, 2, dtype=F32) / ROPE))
    invf = jnp.concatenate([inv, inv, jnp.zeros((LANES - ROPE,), F32)]).reshape(1, LANES)
    gmix0, gmix1 = row(w["g_mix"][0]), row(w["g_mix"][1])
    gq, gkv = row(w["mla_g_q"][0]), row(w["mla_g_kv"][0])
    lr = w["ssm_lambda_re"][0].reshape(GB, 1, SL)
    li = w["ssm_lambda_im"][0].reshape(GB, 1, SL)
    ldt = jnp.broadcast_to(w["ssm_log_dt"][0][:, None], (G, P)).reshape(GB, 1, SL)
    braw, cblk = _s5_block_mats(w["ssm_b_re"][0], w["ssm_b_im"][0], w["ssm_c_re"][0], w["ssm_c_im"][0])
    cblk = _bf(cblk)

    qc, kc, v = _mla_front_fwd(x, pos, gmix0, wa, gq, gkv, wuq, wukv, invf, tm_w)
    o, lse, gathered = _flash_fwd(qc, kc, v, *t_fwd, [shard for _, _, shard in late])
    w = dict(w)
    for (n, axis, _), g in zip(late, gathered):
        w[n] = _merge_shards(g, axis)
    wo = _bf(w["mla_w_o"][0])
    win, wglu = _bf(w["ssm_w_in"][0]), _bf(w["ssm_w_glu"][0])
    dskip = row(w["ssm_d"][0])
    ffn = []
    for l in range(2):
        wup, wdown = _bf(w["ffn_w_up"][l]), _bf(w["ffn_w_down"][l])
        ffn.append(dict(g=row(w["g_ffn"][l]), wup=wup, wdown=wdown, cw=w["ffn_conv_w"][l], cb=row(w["ffn_conv_b"][l])))
    h1 = _attn_out_fwd(x, o, wo, tm)
    f0 = ffn[0]
    h2, *saved0 = _ffn_fwd(h1, f0["g"], f0["wup"], f0["cw"], f0["cb"], f0["wdown"], tm_ff)
    bb, apow = _s5_prep(lr, li, ldt, braw, seg)
    u = _s5_in_fwd(h2, gmix1, win, tm)
    y, xin = _s5_scan_fwd(u, bb, apow, cblk, dskip, t_chunk)
    h3 = _s5_out_fwd(h2, y, wglu, tm)
    f1 = ffn[1]
    h4, *saved1 = _ffn_fwd(h3, f1["g"], f1["wup"], f1["cw"], f1["cb"], f1["wdown"], tm_ff)
    (dh4,), (sq, d_gfinal) = _final_loss(h4, tgt, row(w["g_final"]), tm)
    loss = 0.5 * jnp.sum(sq) / D

    grads = {}

    def ffn_back(hin, dout, saved, f):
        act, upv, upg = saved
        din, dupv, dupg, hn, cacc, dg = _ffn_bwd(hin, dout, upv, upg, f["g"], f["cw"], f["cb"], f["wdown"], f["wup"], tm_fb)
        d_wup = jnp.concatenate([_matmul_tn(hn, dupv, DFF, min(512, s_len), "ffn_dwup_v"),
                                 _matmul_tn(hn, dupg, DFF, min(512, s_len), "ffn_dwup_g")], axis=1)
        d_wdown = _matmul_tn(act, dout, D, min(512, s_len), "ffn_dwdown")
        cflat = cacc.transpose(1, 0, 2).reshape(8, 2 * DFF)
        return din, d_wup, d_wdown, cflat[:3], cflat[3], dg[0]

    dh3, d_wup1, d_wdown1, d_cw1, d_cb1, d_gffn1 = ffn_back(h3, dh4, saved1, f1)
    (dy,), (d_wglu,) = _s5_out_bwd(dh3, y, wglu, wglu.T, tm)
    du, d_bb, d_cblk, d_a, d_dsk = _s5_scan_bwd(u, dy, xin, bb, bb.transpose(0, 2, 1), apow, cblk.transpose(0, 2, 1), dskip, t_chunk)
    d_braw, d_lr, d_li, d_ldt = _s5_prep_bwd(lr, li, ldt, braw, d_bb, d_a)
    (dh2,), (d_win, d_gmix1) = _s5_in_bwd(h2, du, dh3, gmix1, win.T, tm)
    dh1, d_wup0, d_wdown0, d_cw0, d_cb0, d_gffn0 = ffn_back(h1, dh2, saved0, f0)
    (do, delta), (d_wo,) = _attn_out_bwd(dh1, o, wo.T, tm)

    grads["mla_w_o"] = d_wo[None]
    grads["ssm_w_in"] = d_win[None]
    grads["ssm_lambda_re"] = d_lr.reshape(1, G, P)
    grads["ssm_lambda_im"] = d_li.reshape(1, G, P)
    grads["ssm_log_dt"] = jnp.sum(d_ldt.reshape(G, P), axis=1)[None]
    grads["ssm_b_re"] = _s5_unblock_b(d_braw[:, :, :SL])[None]
    grads["ssm_b_im"] = _s5_unblock_b(d_braw[:, :, SL:])[None]
    grads["ssm_c_re"] = _s5_unblock_c(d_cblk[:, :SL, :])[None]
    grads["ssm_c_im"] = -_s5_unblock_c(d_cblk[:, SL:, :])[None]
    grads["ssm_d"] = jnp.sum(d_dsk, axis=1).reshape(1, D)
    grads["ssm_w_glu"] = d_wglu[None]
    grads["ffn_w_up"] = jnp.stack([d_wup0, d_wup1])
    grads["ffn_conv_w"] = jnp.stack([d_cw0, d_cw1])
    grads["ffn_conv_b"] = jnp.stack([d_cb0, d_cb1])
    grads["ffn_w_down"] = jnp.stack([d_wdown0, d_wdown1])
    grads["g_ffn"] = jnp.stack([d_gffn0, d_gffn1])
    grads["g_final"] = d_gfinal[0]

    sends = [_split_shards(grads[n], axis).astype(MXU) for n, axis, _ in late]
    casts = [jnp.concatenate([grads[n].reshape(-1) for n in late_small])] if late_small else []
    dqc, dkc, dv, landed, landed_small = _flash_bwd(qc, kc, v, do, lse, delta, *t_bwd, sends, casts)
    (dx,), (d_wa, d_wuq, d_wukv, d_gq, d_gkv, d_gmix0) = _mla_front_bwd(
        x, pos, dqc, dkc, dv, dh1, gmix0, wa, gq, gkv, wuq, wukv, invf, wa.T, wuq.T, wukv.T, tm_w)
    grads["mla_w_a"] = d_wa[None, :, :QL + KVL + ROPE]
    grads["mla_g_q"] = d_gq
    grads["mla_g_kv"] = d_gkv
    grads["mla_w_uq"] = d_wuq.reshape(QL, HEADS, HC)[:, :, :NOPE + ROPE].reshape(1, QL, HEADS * (NOPE + ROPE))
    grads["mla_w_ukv"] = d_wukv[None]
    grads["g_mix"] = jnp.concatenate([d_gmix0, d_gmix1], axis=0)
    return loss, dx, grads, landed, (landed_small[0] if late_small else None)


MESH = pl.DeviceIdType.MESH
ANY = pl.BlockSpec(memory_space=pl.ANY)


def _gather_many(blocks, name):
    n = len(blocks)

    def body(*refs):
        _gather_issue(refs[:n], refs[n:2 * n], refs[2 * n:])
        _gather_finish(refs[:n], refs[n:2 * n], refs[2 * n:])

    return pl.pallas_call(body, out_shape=_gathered_shapes(blocks), in_specs=[ANY] * n, out_specs=[ANY] * n,
                          scratch_shapes=_comm_scratch(n), name=name)(*blocks)


def _comm_scratch(n):
    if n == 0:
        return []
    return [pltpu.SemaphoreType.DMA((7 * n,)), pltpu.SemaphoreType.DMA((7 * n,)), pltpu.SemaphoreType.DMA((n,))]


def _gathered_shapes(blocks):
    return [jax.ShapeDtypeStruct((NDEV,) + b.shape, b.dtype) for b in blocks]


def _gather_copies(ins, outs, sems):
    send_sems, recv_sems, local_sems = sems
    x, y, c = lax.axis_index("x"), lax.axis_index("y"), lax.axis_index("c")
    me, sibling = (x, y, c), (x, y, 1 - c)
    chips = [(1 - x, y), (x, 1 - y), (1 - x, 1 - y)]

    def copy(a, k, block, to, own=False):
        px, py, pc = block
        slot = outs[a].at[4 * px + 2 * py + pc]
        return pltpu.make_async_remote_copy(src_ref=ins[a] if own else slot, dst_ref=slot,
                                            send_sem=send_sems.at[7 * a + k], recv_sem=recv_sems.at[7 * a + k],
                                            device_id=to, device_id_type=MESH)

    mine = [pltpu.make_async_copy(ins[a], outs[a].at[4 * x + 2 * y + c], local_sems.at[a]) for a in range(len(ins))]
    first = []
    for a in range(len(ins)):
        first.append(copy(a, 0, me, sibling, own=True))
        first += [copy(a, 1 + j, me, (*chip, c), own=True) for j, chip in enumerate(chips)]
    return copy, mine, first, me, sibling, chips, c


def _gather_issue(ins, outs, sems):
    _, mine, first, *_ = _gather_copies(ins, outs, sems)
    for cp in mine + first:
        cp.start()


def _gather_finish(ins, outs, sems):
    copy, mine, first, me, sibling, chips, c = _gather_copies(ins, outs, sems)
    passed = []
    for j, chip in enumerate(chips):
        for a in range(len(ins)):
            copy(a, 1 + j, (*chip, c), me).wait_recv()
            passed.append(copy(a, 4 + j, (*chip, c), sibling))
            passed[-1].start()
    for a in range(len(ins)):
        copy(a, 0, sibling, me).wait_recv()
        for j, chip in enumerate(chips):
            copy(a, 4 + j, (*chip, 1 - c), me).wait_recv()
    for cp in first + passed:
        cp.wait_send()
    for cp in mine:
        cp.wait()


def _exchange_many(sends, casts, name):
    n, tot = len(sends), len(sends) + len(casts)

    def body(*refs):
        _exchange_issue(refs[:tot], refs[tot:2 * tot], refs[2 * tot:], n)
        _exchange_finish(refs[:tot], refs[tot:2 * tot], refs[2 * tot:], n)

    out = pl.pallas_call(body, out_shape=_exchanged_shapes(sends, casts), in_specs=[ANY] * tot, out_specs=[ANY] * tot,
                         scratch_shapes=_comm_scratch(tot), name=name)(*sends, *casts)
    return out[:n], out[n:]


def _exchanged_shapes(sends, casts):
    return [jax.ShapeDtypeStruct(s.shape, s.dtype) for s in sends] + _gathered_shapes(casts)


def _exchange_copies(ins, outs, sems, n, with_arrivals):
    send_sems, recv_sems, local_sems = sems
    x, y, c = lax.axis_index("x"), lax.axis_index("y"), lax.axis_index("c")
    me = 4 * x + 2 * y + c
    local, sent, arrivals = [], [], []
    for a in range(len(ins)):
        own = ins[a].at[me] if a < n else ins[a]
        local.append(pltpu.make_async_copy(own, outs[a].at[me], local_sems.at[a]))
        for m in range(1, NDEV):
            px = 1 - x if m & 4 else x
            py = 1 - y if m & 2 else y
            pc = 1 - c if m & 1 else c
            peer = 4 * px + 2 * py + pc
            to = dict(send_sem=send_sems.at[7 * a + m - 1], recv_sem=recv_sems.at[7 * a + m - 1],
                      device_id=(px, py, pc), device_id_type=MESH)
            sent.append(pltpu.make_async_remote_copy(src_ref=ins[a].at[peer] if a < n else ins[a], dst_ref=outs[a].at[me], **to))
            if with_arrivals:
                arrivals.append(pltpu.make_async_remote_copy(src_ref=own, dst_ref=outs[a].at[peer], **to))
    return local, sent, arrivals


def _exchange_issue(ins, outs, sems, n):
    local, sent, _ = _exchange_copies(ins, outs, sems, n, False)
    for cp in local + sent:
        cp.start()


def _exchange_finish(ins, outs, sems, n):
    local, sent, arrivals = _exchange_copies(ins, outs, sems, n, True)
    for cp in arrivals:
        cp.wait_recv()
    for cp in sent:
        cp.wait_send()
    for cp in local:
        cp.wait()


def _adam_math(w, parts, m, v):
    g = parts[0].astype(F32)
    for k in range(1, NDEV):
        g = g + parts[k].astype(F32)
    m2 = ADAM_B1 * m + (1.0 - ADAM_B1) * g
    v2 = ADAM_B2 * v + (1.0 - ADAM_B2) * jnp.square(g)
    m_hat = m2 / (1.0 - ADAM_B1 ** ADAM_STEP)
    v_hat = v2 / (1.0 - ADAM_B2 ** ADAM_STEP)
    return g, -ADAM_LR * (m_hat / (jnp.sqrt(v_hat) + ADAM_EPS) + ADAM_WD * w), m2, v2


def _adamw_many(ws, parts, ms, vs, name):
    n = len(ws)

    def body(*refs):
        w_refs, p_refs, m_refs, v_refs, outs = refs[:n], refs[n:2 * n], refs[2 * n:3 * n], refs[3 * n:4 * n], refs[4 * n:]
        for a in range(n):
            res = _adam_math(w_refs[a][...], [p_refs[a][k] for k in range(NDEV)], m_refs[a][...], v_refs[a][...])
            for o, val in zip(outs[4 * a:4 * a + 4], res):
                o[...] = val

    out = pl.pallas_call(body, out_shape=[jax.ShapeDtypeStruct(w.shape, F32) for w in ws for _ in range(4)],
                         compiler_params=pltpu.CompilerParams(vmem_limit_bytes=VMEM_LIMIT), name=name)(*ws, *parts, *ms, *vs)
    return [out[4 * a:4 * a + 4] for a in range(n)]


def _adamw_rows(w, parts, m, v, tr, name):
    rows, cols = w.shape

    def body(w_ref, p_ref, m_ref, v_ref, g_ref, d_ref, m2_ref, v2_ref):
        res = _adam_math(w_ref[...], [p_ref[k] for k in range(NDEV)], m_ref[...], v_ref[...])
        for o, val in zip((g_ref, d_ref, m2_ref, v2_ref), res):
            o[...] = val

    flat = pl.BlockSpec((tr, cols), lambda i: (i, 0))
    return pl.pallas_call(body, grid=(rows // tr,),
                          in_specs=[flat, pl.BlockSpec((NDEV, tr, cols), lambda i: (0, i, 0)), flat, flat],
                          out_specs=[flat] * 4, out_shape=[jax.ShapeDtypeStruct((rows, cols), F32)] * 4,
                          compiler_params=_cparams(1), name=name)(w, parts, m, v)


SHARDED = (("mla_w_a", 1), ("mla_w_uq", 2), ("mla_w_ukv", 2), ("mla_w_o", 1), ("ssm_w_in", 1), ("ssm_d", 1),
           ("ssm_w_glu", 2), ("ffn_w_up", 2), ("ffn_conv_w", 2), ("ffn_w_down", 1))
WIRE_EXACT = ("ssm_d", "ffn_conv_w")
GATHERED_FIRST = ("mla_w_a", "mla_w_uq", "mla_w_ukv")
REPLICATED_LAST = ("mla_g_q", "mla_g_kv", "g_mix")
REPLICATED = ("mla_g_q", "mla_g_kv", "ssm_lambda_re", "ssm_lambda_im", "ssm_log_dt", "ssm_b_re", "ssm_b_im",
              "ssm_c_re", "ssm_c_im", "ffn_conv_b", "g_mix", "g_ffn", "g_final")
WEIGHTS = ("mla_w_a", "mla_g_q", "mla_g_kv", "mla_w_uq", "mla_w_ukv", "mla_w_o", "ssm_w_in", "ssm_lambda_re",
           "ssm_lambda_im", "ssm_log_dt", "ssm_b_re", "ssm_b_im", "ssm_c_re", "ssm_c_im", "ssm_d", "ssm_w_glu",
           "ffn_w_up", "ffn_conv_w", "ffn_conv_b", "ffn_w_down", "g_mix", "g_ffn", "g_final")


ADAM_ROW_TILED = (("ffn_w_up", 256), ("ffn_w_down", 176))


def _two_d(shape):
    if len(shape) == 1:
        return (1, shape[0])
    if len(shape) > 2 and shape[-1] < LANES:
        return (math.prod(shape[:-2]), shape[-2] * shape[-1])
    return (math.prod(shape[:-1]), shape[-1])


def kernel(x, positions, mla_w_a, mla_g_q, mla_g_kv, mla_w_uq, mla_w_ukv, mla_w_o, ssm_w_in, ssm_lambda_re, ssm_lambda_im, ssm_log_dt, ssm_b_re, ssm_b_im, ssm_c_re, ssm_c_im, ssm_d, ssm_w_glu, ffn_w_up, ffn_conv_w, ffn_conv_b, ffn_w_down, g_mix, g_ffn, g_final, loss_target, m_mla_w_a, m_mla_g_q, m_mla_g_kv, m_mla_w_uq, m_mla_w_ukv, m_mla_w_o, m_ssm_w_in, m_ssm_lambda_re, m_ssm_lambda_im, m_ssm_log_dt, m_ssm_b_re, m_ssm_b_im, m_ssm_c_re, m_ssm_c_im, m_ssm_d, m_ssm_w_glu, m_ffn_w_up, m_ffn_conv_w, m_ffn_conv_b, m_ffn_w_down, m_g_mix, m_g_ffn, m_g_final, v_mla_w_a, v_mla_g_q, v_mla_g_kv, v_mla_w_uq, v_mla_w_ukv, v_mla_w_o, v_ssm_w_in, v_ssm_lambda_re, v_ssm_lambda_im, v_ssm_log_dt, v_ssm_b_re, v_ssm_b_im, v_ssm_c_re, v_ssm_c_im, v_ssm_d, v_ssm_w_glu, v_ffn_w_up, v_ffn_conv_w, v_ffn_conv_b, v_ffn_w_down, v_g_mix, v_g_ffn, v_g_final):
    a = dict(locals())
    s_len = x.shape[1]
    sh_names = [n for n, _ in SHARDED]

    def wire(n):
        return a[n] if n in WIRE_EXACT else a[n].astype(MXU)

    early = [(n, axis) for n, axis in SHARDED if n in GATHERED_FIRST]
    late = [(n, axis, wire(n)) for n, axis in SHARDED if n not in GATHERED_FIRST]
    w = {n: a[n] for n in REPLICATED}
    for (n, axis), g in zip(early, _gather_many([wire(n) for n, _ in early], "gather_weights")):
        w[n] = _merge_shards(g, axis)

    rep_late = [n for n in REPLICATED if n not in REPLICATED_LAST]
    loss, dx, grads, landed_late, rep_late_all = _sequence_step(
        x[0], positions.reshape(s_len, 1).astype(F32), loss_target[0], w, late, rep_late)
    loss = lax.psum(loss, ("x", "y", "c"))

    packed = jnp.concatenate([grads[n].reshape(-1) for n in REPLICATED_LAST])
    landed_early, (rep_last_all,) = _exchange_many([_split_shards(grads[n], axis).astype(MXU) for n, axis in early],
                                                   [packed], "exchange_grads")
    parts_of = dict(zip([n for n, _, _ in late], landed_late)) | dict(zip([n for n, _ in early], landed_early))
    for names, pack in ((rep_late, rep_late_all), (REPLICATED_LAST, rep_last_all)):
        off = 0
        for n in names:
            size = math.prod(a[n].shape)
            parts_of[n] = pack[:, off:off + size]
            off += size

    def view(n, arr, lead=()):
        return arr.reshape(lead + _two_d(a[n].shape))

    out = {}

    def finish(n, res):
        for kind, val in zip(("grad_", "delta_", "new_m_", "new_v_"), res):
            out[kind + n] = val.reshape(a[n].shape)

    for n, tr in ADAM_ROW_TILED:
        finish(n, _adamw_rows(view(n, a[n]), view(n, parts_of[n], (NDEV,)), view(n, a["m_" + n]), view(n, a["v_" + n]),
                              tr, "adamw_" + n))
    tiled = [n for n, _ in ADAM_ROW_TILED]
    for names, tag in (([n for n in sh_names if n not in tiled], "adamw_sharded"), (list(REPLICATED), "adamw_replicated")):
        res = _adamw_many([view(n, a[n]) for n in names], [view(n, parts_of[n], (NDEV,)) for n in names],
                          [view(n, a["m_" + n]) for n in names], [view(n, a["v_" + n]) for n in names], tag)
        for n, r in zip(names, res):
            finish(n, r)
    return (loss, dx[None], *[out[kind + n] for kind in ("grad_", "delta_", "new_m_", "new_v_") for n in WEIGHTS])
```

```python
import functools
import math

import jax
import jax.numpy as jnp
from jax import lax
from jax.experimental import pallas as pl
from jax.experimental.pallas import tpu as pltpu

F32 = jnp.float32
MXU = jnp.bfloat16

D = 1024
HEADS = 8
NOPE = 128
ROPE = 64
VH = 128
QL = 384
KVL = 256
CHUNK = 64
ROPE_THETA = 10000.0
EPS = 1e-6
G, P, C = 64, 64, 16
DFF = 2816
ADAM_LR, ADAM_B1, ADAM_B2, ADAM_EPS, ADAM_WD, ADAM_STEP = 0.001, 0.9, 0.999, 1e-08, 0.01, 10

LANES = 128
AW = 768
HC = 256
GB = 8
SL = (G // GB) * P
NCOL = 2 * SL // LANES
TC = DFF
HALO = 16
NDEV = 8
VMEM_LIMIT = 56 * 1024 * 1024


def _mm(a, b):
    return jnp.dot(a.astype(MXU), b.astype(MXU), preferred_element_type=F32)


def _mm_tn(a, b):
    return lax.dot_general(a.astype(MXU), b.astype(MXU), (((0,), (0,)), ((), ())), preferred_element_type=F32)


def _mm_nt(a, b):
    return lax.dot_general(a.astype(MXU), b.astype(MXU), (((1,), (1,)), ((), ())), preferred_element_type=F32)


def _rms_fwd(x, g):
    r = lax.rsqrt(jnp.mean(x * x, axis=-1, keepdims=True) + EPS)
    xh = x * r
    return xh * g, xh, r


def _rms_bwd(dy, xh, r, g):
    dxh = dy * g
    dx = r * (dxh - xh * jnp.mean(dxh * xh, axis=-1, keepdims=True))
    return dx, jnp.sum(dy * xh, axis=0, keepdims=True)


def _rot_partner(b):
    lane = lax.broadcasted_iota(jnp.int32, b.shape, 1)
    return jnp.where(lane < ROPE // 2, -pltpu.roll(b, LANES - ROPE // 2, 1), pltpu.roll(b, ROPE // 2, 1))


def _rope_blk(b, cos2, sin2):
    return b * cos2 + _rot_partner(b) * sin2


def _unrope_blk(db, cos2, sin2):
    return db * cos2 - _rot_partner(db * sin2)


def _cparams(n_axes, vmem=VMEM_LIMIT):
    return pltpu.CompilerParams(dimension_semantics=("arbitrary",) * n_axes, vmem_limit_bytes=vmem)


def _rowcall(name, fn, tm, row_ins, consts, row_outs, acc_outs):
    n = row_ins[0].shape[0]
    n_in = len(row_ins) + len(consts)
    n_ro = len(row_outs)

    def body(*refs):
        ins, ro_refs, acc_refs = refs[:n_in], refs[n_in:n_in + n_ro], refs[n_in + n_ro:]
        ro, ao = fn(*[r[...] for r in ins])

        @pl.when(pl.program_id(0) == 0)
        def _():
            for r in acc_refs:
                r[...] = jnp.zeros(r.shape, r.dtype)

        for r, val in zip(ro_refs, ro):
            r[...] = val.astype(r.dtype)
        for r, val in zip(acc_refs, ao):
            r[...] += val

    in_specs = [pl.BlockSpec((tm, a.shape[1]), lambda i: (i, 0)) for a in row_ins]
    in_specs += [pl.BlockSpec(c.shape, lambda i, nd=c.ndim: (0,) * nd) for c in consts]
    out_specs = [pl.BlockSpec((tm, w), lambda i: (i, 0)) for w, _ in row_outs]
    out_specs += [pl.BlockSpec(s, lambda i, nd=len(s): (0,) * nd) for s in acc_outs]
    out_shape = [jax.ShapeDtypeStruct((n, w), dt) for w, dt in row_outs]
    out_shape += [jax.ShapeDtypeStruct(s, F32) for s in acc_outs]
    out = pl.pallas_call(body, grid=(n // tm,), in_specs=in_specs, out_specs=out_specs, out_shape=out_shape,
                         compiler_params=_cparams(1), name=name)(*row_ins, *consts)
    return list(out[:n_ro]), list(out[n_ro:])


def _mla_front_tile(x, pos, gmix, wa, gq, gkv, wuq, wukv, invf):
    hn, xh, r = _rms_fwd(x, gmix)
    a = _mm(hn, wa)
    cq, ckv, krb = a[:, :QL], a[:, QL:QL + KVL], a[:, QL + KVL:]
    cqn, cqh, rq = _rms_fwd(cq, gq)
    ckvn, ckvh, rkv = _rms_fwd(ckv, gkv)
    q = _mm(cqn, wuq)
    kv = _mm(ckvn, wukv)
    ang = pos * invf
    cos2, sin2 = jnp.cos(ang), jnp.sin(ang)
    krr = _rope_blk(krb, cos2, sin2)
    qp, kp, vp = [], [], []
    for h in range(HEADS):
        qp += [q[:, h * HC:h * HC + NOPE], _rope_blk(q[:, h * HC + NOPE:(h + 1) * HC], cos2, sin2)]
        kp += [kv[:, h * HC:h * HC + NOPE], krr]
        vp += [kv[:, h * HC + NOPE:(h + 1) * HC]]
    res = (hn, xh, r, cqn, cqh, rq, ckvn, ckvh, rkv, cos2, sin2)
    return jnp.concatenate(qp, axis=1), jnp.concatenate(kp, axis=1), jnp.concatenate(vp, axis=1), res


def _mla_front_fwd(x, pos, gmix, wa, gq, gkv, wuq, wukv, invf, tm):
    def fn(*args):
        qc, kc, v, _ = _mla_front_tile(*args)
        return (qc * Q_PRESCALE, kc, v), ()

    return _rowcall("mla_front_fwd", fn, tm, [x, pos], [gmix, wa, gq, gkv, wuq, wukv, invf],
                    [(HEADS * HC, MXU), (HEADS * HC, MXU), (HEADS * VH, MXU)], [])[0]


def _mla_front_bwd(x, pos, dqc, dkc, dv, dh, gmix, wa, gq, gkv, wuq, wukv, invf, wa_t, wuq_t, wukv_t, tm):
    def fn(x, pos, dqc, dkc, dv, dh, gmix, wa, gq, gkv, wuq, wukv, invf, wa_t, wuq_t, wukv_t):
        _, _, _, (hn, xh, r, cqn, cqh, rq, ckvn, ckvh, rkv, cos2, sin2) = _mla_front_tile(
            x, pos, gmix, wa, gq, gkv, wuq, wukv, invf)
        dqc, dkc, dv = dqc * SM_SCALE, dkc.astype(F32), dv.astype(F32)
        dqp, dkvp = [], []
        dkr = jnp.zeros((x.shape[0], LANES), F32)
        for h in range(HEADS):
            dqp += [dqc[:, h * HC:h * HC + NOPE], _unrope_blk(dqc[:, h * HC + NOPE:(h + 1) * HC], cos2, sin2)]
            dkvp += [dkc[:, h * HC:h * HC + NOPE], dv[:, h * VH:(h + 1) * VH]]
            dkr = dkr + dkc[:, h * HC + NOPE:(h + 1) * HC]
        dq = jnp.concatenate(dqp, axis=1)
        dkv = jnp.concatenate(dkvp, axis=1)
        dkrb = _unrope_blk(dkr, cos2, sin2)
        dcqn = _mm(dq, wuq_t)
        dckvn = _mm(dkv, wukv_t)
        d_wuq = _mm_tn(cqn, dq)
        d_wukv = _mm_tn(ckvn, dkv)
        dcq, d_gq = _rms_bwd(dcqn, cqh, rq, gq)
        dckv, d_gkv = _rms_bwd(dckvn, ckvh, rkv, gkv)
        da = jnp.concatenate([dcq, dckv, dkrb], axis=1)
        d_wa = _mm_tn(hn, da)
        dhn = _mm(da, wa_t)
        dx, d_gmix = _rms_bwd(dhn, xh, r, gmix)
        return (dh + dx,), (d_wa, d_wuq, d_wukv, d_gq, d_gkv, d_gmix)

    return _rowcall("mla_front_bwd", fn, tm, [x, pos, dqc, dkc, dv, dh],
                    [gmix, wa, gq, gkv, wuq, wukv, invf, wa_t, wuq_t, wukv_t], [(D, F32)],
                    [(D, AW), (QL, HEADS * HC), (KVL, HEADS * HC), (1, QL), (1, KVL), (1, D)])


SM_SCALE = (NOPE + ROPE) ** -0.5
LOG2E = 1.0 / math.log(2.0)
Q_PRESCALE = SM_SCALE * LOG2E


def _pair_tables(s_len, tq, tk, q_major):
    pairs = [(qi, ki) for qi in range(s_len // tq) for ki in range(s_len // tk) if ki * tk < (qi + 1) * tq]
    if not q_major:
        pairs.sort(key=lambda p: (p[1], p[0]))
    return (jnp.asarray([p[0] for p in pairs], jnp.int32), jnp.asarray([p[1] for p in pairs], jnp.int32))


def _last_key_tile(qi, tq, tk):
    return ((qi + 1) * tq - 1) // tk


def _visible(qi, ki, tq, tk, width=None):
    shape = (tq, tk if width is None else width)
    row = qi * (tq // CHUNK) + lax.broadcasted_iota(jnp.int32, shape, 0) // CHUNK
    col = ki * (tk // CHUNK) + lax.broadcasted_iota(jnp.int32, shape, 1) // CHUNK
    return col <= row


def _masked_and_not(qi, ki, tq, tk, fn):
    needs_mask = (ki + 1) * tk > qi * tq
    pl.when(needs_mask)(lambda: fn(True))
    pl.when(jnp.logical_not(needs_mask))(lambda: fn(False))


def _by_visible_width(qi, ki, tq, tk, fn):
    if tk % tq or tk == tq:
        _masked_and_not(qi, ki, tq, tk, lambda masked: fn(tk, masked))
        return
    blocks = tk // tq
    seen = qi + 1 - ki * blocks
    pl.when(seen > blocks)(lambda: fn(tk, False))
    for j in range(1, blocks + 1):
        pl.when(seen == j)(functools.partial(fn, j * tq, True))


def _first_and_last_step(n_steps):
    first = jnp.logical_and(pl.program_id(0) == 0, pl.program_id(1) == 0)
    last = jnp.logical_and(pl.program_id(0) == HEADS - 1, pl.program_id(1) == n_steps - 1)
    return first, last


def _flash_fwd(qc, kc, v, tq, tk, riders):
    s_len = qc.shape[0]
    qt, kt = _pair_tables(s_len, tq, tk, True)
    nr = len(riders)

    def body(qt_ref, kt_ref, q_ref, k_ref, v_ref, *rest):
        r_in, (o_ref, lse_ref), r_out = rest[:nr], rest[nr:nr + 2], rest[nr + 2:2 * nr + 2]
        m_sc, l_sc, acc_sc = rest[2 * nr + 2:2 * nr + 5]
        sems = rest[2 * nr + 5:]
        p_id = pl.program_id(1)
        qi, ki = qt_ref[p_id], kt_ref[p_id]
        first, last = _first_and_last_step(qt.shape[0])
        if nr:
            pl.when(first)(lambda: _gather_issue(r_in, r_out, sems))

        @pl.when(ki == 0)
        def _():
            m_sc[...] = jnp.full(m_sc.shape, -jnp.inf, F32)
            l_sc[...] = jnp.zeros(l_sc.shape, F32)
            acc_sc[...] = jnp.zeros(acc_sc.shape, F32)

        def update(width, masked):
            s = _mm_nt(q_ref[...], k_ref[:width, :])
            if masked:
                s = jnp.where(_visible(qi, ki, tq, tk, width), s, -jnp.inf)
            m_old = m_sc[...]
            m_new = jnp.maximum(m_old, jnp.max(s, axis=1, keepdims=True))
            alpha = jnp.exp2(m_old - m_new)
            p = jnp.exp2(s - m_new)
            l_sc[...] = alpha * l_sc[...] + jnp.sum(p, axis=1, keepdims=True)
            acc_sc[...] = alpha * acc_sc[...] + _mm(p, v_ref[:width, :])
            m_sc[...] = m_new

        _by_visible_width(qi, ki, tq, tk, update)

        @pl.when(ki == _last_key_tile(qi, tq, tk))
        def _():
            l = l_sc[...]
            o_ref[...] = (acc_sc[...] / l).astype(o_ref.dtype)
            lse_ref[...] = jnp.broadcast_to(m_sc[...] + jnp.log2(l), lse_ref.shape)

        if nr:
            pl.when(last)(lambda: _gather_finish(r_in, r_out, sems))

    qmap = lambda h, p, qt, kt: (qt[p], h)
    kmap = lambda h, p, qt, kt: (kt[p], h)
    grid_spec = pltpu.PrefetchScalarGridSpec(
        num_scalar_prefetch=2, grid=(HEADS, qt.shape[0]),
        in_specs=[pl.BlockSpec((tq, HC), qmap), pl.BlockSpec((tk, HC), kmap), pl.BlockSpec((tk, VH), kmap)] + [ANY] * nr,
        out_specs=[pl.BlockSpec((tq, VH), qmap), pl.BlockSpec((tq, LANES), qmap)] + [ANY] * nr,
        scratch_shapes=[pltpu.VMEM((tq, 1), F32), pltpu.VMEM((tq, 1), F32), pltpu.VMEM((tq, VH), F32)] + _comm_scratch(nr))
    out = pl.pallas_call(body, grid_spec=grid_spec,
                         out_shape=[jax.ShapeDtypeStruct((s_len, HEADS * VH), MXU),
                                    jax.ShapeDtypeStruct((s_len, HEADS * LANES), F32)] + _gathered_shapes(riders),
                         compiler_params=_cparams(2), name="flash_fwd")(qt, kt, qc, kc, v, *riders)
    return out[0], out[1], out[2:]


def _tile_dscores(q, k, v, do, lse, delta, qi, ki, tq, tk, width, masked):
    p = jnp.exp2(_mm_nt(q, k) - lse[:, :1])
    if masked:
        p = jnp.where(_visible(qi, ki, tq, tk, width), p, 0.0)
    return p, p * (_mm_nt(do, v) - delta[:, :1])


def _flash_bwd(qc, kc, v, do, lse, delta, tq, tk, riders, casts=()):
    s_len = qc.shape[0]
    qt, kt = _pair_tables(s_len, tq, tk, False)
    nq = s_len // tq
    n_rows, nr = len(riders), len(riders) + len(casts)

    def body(qt_ref, kt_ref, q_ref, k_ref, v_ref, do_ref, lse_ref, dl_ref, *rest):
        r_in, (dq_ref, dk_ref, dv_ref), r_out = rest[:nr], rest[nr:nr + 3], rest[nr + 3:2 * nr + 3]
        dk_sc, dv_sc = rest[2 * nr + 3:2 * nr + 5]
        sems = rest[2 * nr + 5:]
        p_id = pl.program_id(1)
        qi, ki = qt_ref[p_id], kt_ref[p_id]
        rows = pl.ds(pl.multiple_of(qi * tq, tq), tq)
        first, last = _first_and_last_step(qt.shape[0])
        if nr:
            pl.when(first)(lambda: _exchange_issue(r_in, r_out, sems, n_rows))

        @pl.when(qi == (ki * tk) // tq)
        def _():
            dk_sc[...] = jnp.zeros(dk_sc.shape, F32)
            dv_sc[...] = jnp.zeros(dv_sc.shape, F32)

        @pl.when(ki == 0)
        def _():
            dq_ref[rows, :] = jnp.zeros((tq, HC), F32)

        def update(width, masked):
            q, k, do = q_ref[...], k_ref[:width, :], do_ref[...]
            p, ds = _tile_dscores(q, k, v_ref[:width, :], do, lse_ref[...], dl_ref[...], qi, ki, tq, tk, width, masked)
            ds = ds.astype(MXU)
            dv_sc[:width, :] += _mm_tn(p, do)
            dk_sc[:width, :] += _mm_tn(ds, q)
            dq_ref[rows, :] += _mm(ds, k)

        _by_visible_width(qi, ki, tq, tk, update)

        @pl.when(qi == nq - 1)
        def _():
            dk_ref[...] = (dk_sc[...] * (1.0 / LOG2E)).astype(dk_ref.dtype)
            dv_ref[...] = dv_sc[...].astype(dv_ref.dtype)

        if nr:
            pl.when(last)(lambda: _exchange_finish(r_in, r_out, sems, n_rows))

    qmap = lambda h, p, qt, kt: (qt[p], h)
    kmap = lambda h, p, qt, kt: (kt[p], h)
    grid_spec = pltpu.PrefetchScalarGridSpec(
        num_scalar_prefetch=2, grid=(HEADS, qt.shape[0]),
        in_specs=[pl.BlockSpec((tq, HC), qmap), pl.BlockSpec((tk, HC), kmap), pl.BlockSpec((tk, VH), kmap),
                  pl.BlockSpec((tq, VH), qmap), pl.BlockSpec((tq, LANES), qmap), pl.BlockSpec((tq, LANES), qmap)] + [ANY] * nr,
        out_specs=[pl.BlockSpec((s_len, HC), lambda h, p, qt, kt: (0, h), pipeline_mode=pl.Buffered(1)),
                   pl.BlockSpec((tk, HC), kmap), pl.BlockSpec((tk, VH), kmap)] + [ANY] * nr,
        scratch_shapes=[pltpu.VMEM((tk, HC), F32), pltpu.VMEM((tk, VH), F32)] + _comm_scratch(nr))
    out = pl.pallas_call(body, grid_spec=grid_spec,
                         out_shape=[jax.ShapeDtypeStruct((s_len, HEADS * HC), F32),
                                    jax.ShapeDtypeStruct((s_len, HEADS * HC), MXU),
                                    jax.ShapeDtypeStruct((s_len, HEADS * VH), MXU)] + _exchanged_shapes(riders, casts),
                         compiler_params=_cparams(2), name="flash_bwd")(qt, kt, qc, kc, v, do, lse, delta, *riders, *casts)
    return out[0], out[1], out[2], out[3:3 + n_rows], out[3 + n_rows:]


def _attn_out_fwd(x, o, wo, tm):
    def fn(x, o, wo):
        return (x + _mm(o, wo),), ()

    return _rowcall("attn_out_fwd", fn, tm, [x, o], [wo], [(D, F32)], [])[0][0]


def _attn_out_bwd(dh, o, wo_t, tm):
    def fn(dh, o, wo_t):
        do = _mm(dh, wo_t)
        of = o.astype(F32)
        dl = [jnp.broadcast_to(jnp.sum(do[:, h * VH:(h + 1) * VH] * of[:, h * VH:(h + 1) * VH], axis=1, keepdims=True),
                               (dh.shape[0], LANES)) for h in range(HEADS)]
        return (do, jnp.concatenate(dl, axis=1)), (_mm_tn(o, dh),)

    return _rowcall("attn_out_bwd", fn, tm, [dh, o], [wo_t], [(HEADS * VH, MXU), (HEADS * LANES, F32)], [(HEADS * VH, D)])


def _shift_rows(a, k):
    return a if k == 0 else pltpu.roll(a, k % a.shape[0], 0)


def _stack_rows(rows):
    idx = lax.broadcasted_iota(jnp.int32, (8, rows[0].shape[1]), 0)
    out = jnp.zeros((8, rows[0].shape[1]), F32)
    for k, r in enumerate(rows):
        out = jnp.where(idx == k, r, out)
    return out


def _ffn_fwd(h, g, wup, cw, cb, wdown, tm):
    s_len = h.shape[0]
    nj = DFF // TC
    hb = tm // HALO

    def body(h_ref, hp_ref, g_ref, wv_ref, wg_ref, cwv_ref, cwg_ref, cbv_ref, cbg_ref, wd_ref, out_ref, act_ref,
             upv_ref, upg_ref, hn_sc, acc_sc):
        i, j = pl.program_id(0), pl.program_id(1)

        @pl.when(j == 0)
        def _():
            gg = g_ref[...]
            hp = _rms_fwd(hp_ref[...], gg)[0]
            hn_sc[:HALO, :] = jnp.where(i > 0, hp, 0.0).astype(MXU)
            hn_sc[HALO:, :] = _rms_fwd(h_ref[...], gg)[0].astype(MXU)
            acc_sc[...] = jnp.zeros(acc_sc.shape, F32)

        hn = hn_sc[...]

        def conv(w_ref, cw_ref, cb_ref, up_ref):
            up = jnp.dot(hn, w_ref[...], preferred_element_type=F32)
            up_ref[...] = up[HALO:].astype(up_ref.dtype)
            cwv = cw_ref[...]
            c = cwv[2:3] * up + cwv[1:2] * _shift_rows(up, 1) + cwv[0:1] * _shift_rows(up, 2)
            return c[HALO:] + cb_ref[...]

        cv = conv(wv_ref, cwv_ref, cbv_ref, upv_ref)
        cg = conv(wg_ref, cwg_ref, cbg_ref, upg_ref)
        act = cg * jax.nn.sigmoid(cg) * cv
        act_ref[...] = act.astype(act_ref.dtype)
        acc_sc[...] += _mm(act, wd_ref[...])

        @pl.when(j == nj - 1)
        def _():
            out_ref[...] = h_ref[...] + acc_sc[...]

    in_specs = [pl.BlockSpec((tm, D), lambda i, j: (i, 0)),
                pl.BlockSpec((HALO, D), lambda i, j: (jnp.maximum(i * hb - 1, 0), 0)),
                pl.BlockSpec((1, D), lambda i, j: (0, 0)),
                pl.BlockSpec((D, TC), lambda i, j: (0, j)), pl.BlockSpec((D, TC), lambda i, j: (0, j + nj)),
                pl.BlockSpec((3, TC), lambda i, j: (0, j)), pl.BlockSpec((3, TC), lambda i, j: (0, j + nj)),
                pl.BlockSpec((1, TC), lambda i, j: (0, j)), pl.BlockSpec((1, TC), lambda i, j: (0, j + nj)),
                pl.BlockSpec((TC, D), lambda i, j: (j, 0))]
    out_specs = [pl.BlockSpec((tm, D), lambda i, j: (i, 0))] + [pl.BlockSpec((tm, TC), lambda i, j: (i, j))] * 3
    return pl.pallas_call(body, grid=(s_len // tm, nj), in_specs=in_specs, out_specs=out_specs,
                          out_shape=[jax.ShapeDtypeStruct((s_len, D), F32)] + [jax.ShapeDtypeStruct((s_len, DFF), MXU)] * 3,
                          scratch_shapes=[pltpu.VMEM((tm + HALO, D), MXU), pltpu.VMEM((tm, D), F32)],
                          compiler_params=_cparams(2), name="ffn_fwd")(h, h, g, wup, wup, cw, cw, cb, cb, wdown)


def _ffn_bwd(h, dout, upv, upg, g, cw, cb, wdown_t, wup_t, tm):
    s_len = h.shape[0]
    ni = s_len // tm
    hb = tm // HALO

    def body(h_ref, d_ref, dx_ref, uvp_ref, uv_ref, uvx_ref, ugp_ref, ug_ref, ugx_ref, g_ref, cwv_ref, cwg_ref, cbv_ref, cbg_ref,
             wdt_ref, wutv_ref, wutg_ref, din_ref, dupv_ref, dupg_ref, hn_ref, cacc_ref, dg_ref):
        i = pl.program_id(0)

        @pl.when(i == 0)
        def _():
            cacc_ref[...] = jnp.zeros(cacc_ref.shape, F32)
            dg_ref[...] = jnp.zeros(dg_ref.shape, F32)

        gg = g_ref[...]
        hn, xh, r = _rms_fwd(h_ref[...], gg)
        hn_ref[...] = hn.astype(hn_ref.dtype)
        dd = jnp.concatenate([d_ref[...], jnp.where(i < ni - 1, dx_ref[...], 0.0)], axis=0).astype(MXU)
        dact = jnp.dot(dd, wdt_ref[...], preferred_element_type=F32)

        def half(p_ref, t_ref, x_ref, cw_ref, cb_ref):
            prev = p_ref[...]
            up = jnp.concatenate([jnp.where(i > 0, prev, jnp.zeros_like(prev)), t_ref[...], x_ref[...]], axis=0).astype(F32)
            cwv = cw_ref[...]
            u1, u2 = _shift_rows(up, 1), _shift_rows(up, 2)
            c = (cwv[2:3] * up + cwv[1:2] * u1 + cwv[0:1] * u2)[HALO:] + cb_ref[...]
            return c, (up[HALO:HALO + tm], u1[HALO:HALO + tm], u2[HALO:HALO + tm]), cwv

        cv, upsv, cwv = half(uvp_ref, uv_ref, uvx_ref, cwv_ref, cbv_ref)
        cg, upsg, cwg = half(ugp_ref, ug_ref, ugx_ref, cwg_ref, cbg_ref)
        sg = jax.nn.sigmoid(cg)
        dcv = dact * (cg * sg)
        dcg = dact * cv * (sg * (1.0 + cg * (1.0 - sg)))

        def back(dc, ups, cwx, slot, dup_ref, wut_ref):
            dup = (cwx[2:3] * dc + cwx[1:2] * _shift_rows(dc, -1) + cwx[0:1] * _shift_rows(dc, -2))[:tm]
            dct = dc[:tm]
            cacc_ref[slot] += _stack_rows([jnp.sum(dct * ups[2], axis=0, keepdims=True),
                                           jnp.sum(dct * ups[1], axis=0, keepdims=True),
                                           jnp.sum(dct * ups[0], axis=0, keepdims=True),
                                           jnp.sum(dct, axis=0, keepdims=True)])
            dup_ref[...] = dup.astype(dup_ref.dtype)
            return _mm(dup, wut_ref[...])

        dhn = back(dcv, upsv, cwv, 0, dupv_ref, wutv_ref) + back(dcg, upsg, cwg, 1, dupg_ref, wutg_ref)
        dx, dgp = _rms_bwd(dhn, xh, r, gg)
        din_ref[...] = d_ref[...] + dx
        dg_ref[...] += dgp

    last_blk = s_len // HALO - 1
    tile = lambda w: pl.BlockSpec((tm, w), lambda i: (i, 0))
    prev = lambda w: pl.BlockSpec((HALO, w), lambda i: (jnp.maximum(i * hb - 1, 0), 0))
    nxt = lambda w: pl.BlockSpec((HALO, w), lambda i: (jnp.minimum((i + 1) * hb, last_blk), 0))
    const = lambda s, j=0: pl.BlockSpec(s, lambda i: (0, j))
    in_specs = [tile(D), tile(D), nxt(D), prev(DFF), tile(DFF), nxt(DFF), prev(DFF), tile(DFF), nxt(DFF),
                const((1, D)), const((3, DFF)), const((3, DFF), 1), const((1, DFF)), const((1, DFF), 1),
                const((D, DFF)), pl.BlockSpec((DFF, D), lambda i: (0, 0)), pl.BlockSpec((DFF, D), lambda i: (1, 0))]
    out_specs = [tile(D), tile(DFF), tile(DFF), tile(D), pl.BlockSpec((2, 8, DFF), lambda i: (0, 0, 0)), const((1, D))]
    out_shape = [jax.ShapeDtypeStruct((s_len, D), F32), jax.ShapeDtypeStruct((s_len, DFF), MXU),
                 jax.ShapeDtypeStruct((s_len, DFF), MXU), jax.ShapeDtypeStruct((s_len, D), MXU),
                 jax.ShapeDtypeStruct((2, 8, DFF), F32), jax.ShapeDtypeStruct((1, D), F32)]
    return pl.pallas_call(body, grid=(ni,), in_specs=in_specs, out_specs=out_specs, out_shape=out_shape,
                          compiler_params=_cparams(1), name="ffn_bwd")(
        h, dout, dout, upv, upv, upv, upg, upg, upg, g, cw, cw, cb, cb, wdown_t, wup_t, wup_t)


def _matmul_tn(a, b, tn, ts, name):
    s_len, m = a.shape
    n = b.shape[1]

    def body(a_ref, b_ref, o_ref):
        @pl.when(pl.program_id(1) == 0)
        def _():
            o_ref[...] = jnp.zeros(o_ref.shape, F32)

        o_ref[...] += _mm_tn(a_ref[...], b_ref[...])

    return pl.pallas_call(body, grid=(n // tn, s_len // ts),
                          in_specs=[pl.BlockSpec((ts, m), lambda jn, k: (k, 0)), pl.BlockSpec((ts, tn), lambda jn, k: (k, jn))],
                          out_specs=pl.BlockSpec((m, tn), lambda jn, k: (0, jn)),
                          out_shape=jax.ShapeDtypeStruct((m, n), F32), compiler_params=_cparams(2), name=name)(a, b)


def _s5_coefs(lr, li, ldt):
    dt = jnp.exp(ldt)
    mag = jnp.exp(lr * dt)
    th = li * dt
    ar, ai = mag * jnp.cos(th), mag * jnp.sin(th)
    den = lr * lr + li * li
    nr = ar - 1.0
    cr = (nr * lr + ai * li) / den
    ci = (ai * lr - nr * li) / den
    return dt, mag, th, ar, ai, den, nr, cr, ci


def _s5_prep(lr, li, ldt, braw, seg):
    def body(lr_ref, li_ref, ldt_ref, b_ref, bb_ref, ap_ref):
        lr_, li_, ldt_ = lr_ref[0], li_ref[0], ldt_ref[0]
        dt, mag, th, ar, ai, den, nr, cr, ci = _s5_coefs(lr_, li_, ldt_)
        br, bi = b_ref[0, :, :SL], b_ref[0, :, SL:]
        bb_ref[0, :, :SL] = (cr * br - ci * bi).astype(bb_ref.dtype)
        bb_ref[0, :, SL:] = (cr * bi + ci * br).astype(bb_ref.dtype)
        for i in range(seg):
            m = jnp.exp((i + 1.0) * (lr_ * dt))
            ap_ref[0, i * 8:(i + 1) * 8, :SL] = jnp.broadcast_to(m * jnp.cos((i + 1.0) * th), (8, SL))
            ap_ref[0, i * 8:(i + 1) * 8, SL:] = jnp.broadcast_to(m * jnp.sin((i + 1.0) * th), (8, SL))

    vec = pl.BlockSpec((1, 1, SL), lambda k: (k, 0, 0))
    return pl.pallas_call(
        body, grid=(GB,), in_specs=[vec, vec, vec, pl.BlockSpec((1, LANES, 2 * SL), lambda k: (k, 0, 0))],
        out_specs=[pl.BlockSpec((1, LANES, 2 * SL), lambda k: (k, 0, 0)),
                   pl.BlockSpec((1, 8 * seg, 2 * SL), lambda k: (k, 0, 0))],
        out_shape=[jax.ShapeDtypeStruct((GB, LANES, 2 * SL), MXU), jax.ShapeDtypeStruct((GB, 8 * seg, 2 * SL), F32)],
        compiler_params=_cparams(1), name="s5_prep")(lr, li, ldt, braw)


def _s5_prep_bwd(lr, li, ldt, braw, dbb, da):
    def body(lr_ref, li_ref, ldt_ref, b_ref, dbb_ref, da_ref, dbraw_ref, dlr_ref, dli_ref, dldt_ref):
        lr_, li_, ldt_ = lr_ref[0], li_ref[0], ldt_ref[0]
        dt, mag, th, ar, ai, den, nr, cr, ci = _s5_coefs(lr_, li_, ldt_)
        br, bi = b_ref[0, :, :SL], b_ref[0, :, SL:]
        gbr, gbi = dbb_ref[0, :, :SL], dbb_ref[0, :, SL:]
        dbraw_ref[0, :, :SL] = cr * gbr + ci * gbi
        dbraw_ref[0, :, SL:] = cr * gbi - ci * gbr
        dcr = jnp.sum(gbr * br + gbi * bi, axis=0, keepdims=True)
        dci = jnp.sum(gbi * br - gbr * bi, axis=0, keepdims=True)
        dar = jnp.sum(da_ref[0, :, :SL], axis=0, keepdims=True)
        dai = jnp.sum(da_ref[0, :, SL:], axis=0, keepdims=True)
        g1, g2 = dcr / den, dci / den
        gden = -(dcr * cr + dci * ci) / den
        gar = dar + g1 * lr_ - g2 * li_
        gai = dai + g1 * li_ + g2 * lr_
        glr = g1 * nr + g2 * ai + 2.0 * lr_ * gden
        gli = g1 * ai - g2 * nr + 2.0 * li_ * gden
        gmag = gar * jnp.cos(th) + gai * jnp.sin(th)
        gth = gai * ar - gar * ai
        dlr_ref[0] = glr + gmag * mag * dt
        dli_ref[0] = gli + gth * dt
        dldt_ref[0] = (gmag * mag * lr_ + gth * li_) * dt

    vec = pl.BlockSpec((1, 1, SL), lambda k: (k, 0, 0))
    mat = pl.BlockSpec((1, LANES, 2 * SL), lambda k: (k, 0, 0))
    return pl.pallas_call(
        body, grid=(GB,), in_specs=[vec, vec, vec, mat, mat, pl.BlockSpec((1, 8, 2 * SL), lambda k: (k, 0, 0))],
        out_specs=[mat, vec, vec, vec],
        out_shape=[jax.ShapeDtypeStruct((GB, LANES, 2 * SL), F32)] + [jax.ShapeDtypeStruct((GB, 1, SL), F32)] * 3,
        compiler_params=_cparams(1), name="s5_prep_bwd")(lr, li, ldt, braw, dbb, da)


def _bcast_row(tile, j):
    return jnp.broadcast_to(tile[j:j + 1, :], tile.shape)


def _tile_rows(i):
    return pl.ds(pl.multiple_of(i * 8, 8), 8)


def _col(c):
    return slice(c * LANES, (c + 1) * LANES)


def _permute_rows(ref, seg):
    return jnp.concatenate([ref[pl.ds(i, 8, stride=seg), :] for i in range(seg)], axis=0)


def _unpermute_rows(src, dst, seg):
    for j in range(8):
        dst[j * seg:(j + 1) * seg, :] = src[pl.ds(j, seg, stride=8), :]


SCAN_UNROLL = 2


def _segment_scan(src, dst, ap, seg, reverse):
    half = NCOL // 2
    sign = -1.0 if reverse else 1.0
    a_r = [ap[0:8, _col(c)] for c in range(half)]
    a_i = [ap[0:8, _col(half + c)] for c in range(half)]

    def step(n, carry):
        i = seg - 1 - n if reverse else n
        rows = _tile_rows(i)
        out_r, out_i = [], []
        for c in range(half):
            xr, xi = carry[c], carry[half + c]
            nr = a_r[c] * xr - sign * a_i[c] * xi + src[rows, _col(c)]
            ni = a_r[c] * xi + sign * a_i[c] * xr + src[rows, _col(half + c)]
            dst[rows, _col(c)] = nr
            dst[rows, _col(half + c)] = ni
            out_r.append(nr)
            out_i.append(ni)
        return tuple(out_r + out_i)

    zero = jnp.zeros((8, LANES), F32)
    return lax.fori_loop(0, seg, step, (zero,) * NCOL, unroll=SCAN_UNROLL)


def _segment_entries(ends, cin, ap, seg, reverse):
    half = NCOL // 2
    sign = -1.0 if reverse else 1.0
    al_r = [ap[(seg - 1) * 8:seg * 8, _col(c)] for c in range(half)]
    al_i = [ap[(seg - 1) * 8:seg * 8, _col(half + c)] for c in range(half)]
    row = lax.broadcasted_iota(jnp.int32, (8, LANES), 0)
    ent, out = [None] * NCOL, [None] * NCOL
    for c in range(half):
        zr, zi = cin[c], cin[half + c]
        er, ei = jnp.zeros((8, LANES), F32), jnp.zeros((8, LANES), F32)
        for j in (range(7, -1, -1) if reverse else range(8)):
            er, ei = jnp.where(row == j, zr, er), jnp.where(row == j, zi, ei)
            fr, fi = _bcast_row(ends[c], j), _bcast_row(ends[half + c], j)
            zr, zi = (al_r[c] * zr - sign * al_i[c] * zi + fr, al_r[c] * zi + sign * al_i[c] * zr + fi)
        ent[c], ent[half + c] = er, ei
        out[c], out[half + c] = zr, zi
    return ent, out


def _chunk_states(u, bb_ref, ap, cin, bu_sc, x_sc, seg):
    half = NCOL // 2
    bu_sc[...] = _mm(u, bb_ref[0])
    ends = _segment_scan(bu_sc, x_sc, ap, seg, False)
    ent, out = _segment_entries(ends, cin, ap, seg, False)

    def fix(i, _):
        rows = _tile_rows(i)
        for c in range(half):
            pr, pi = ap[rows, _col(c)], ap[rows, _col(half + c)]
            x_sc[rows, _col(c)] += pr * ent[c] - pi * ent[half + c]
            x_sc[rows, _col(half + c)] += pr * ent[half + c] + pi * ent[c]
        return 0

    lax.fori_loop(0, seg, fix, 0, unroll=SCAN_UNROLL)
    return out


def _s5_scan_fwd(u, bb, apow, cblk, dskip, t_chunk):
    s_len = u.shape[0]
    nc = s_len // t_chunk
    seg = t_chunk // 8

    def body(u_ref, bb_ref, ap_ref, c_ref, d_ref, y_ref, xin_ref, xs_ref, bu_sc, x_sc, y_sc, carry_sc):
        @pl.when(pl.program_id(1) == 0)
        def _():
            carry_sc[...] = jnp.zeros(carry_sc.shape, F32)

        uu = _permute_rows(u_ref, seg)
        cin = [carry_sc[:, _col(c)] for c in range(NCOL)]
        xin_ref[0, 0] = carry_sc[...]
        out = _chunk_states(uu, bb_ref, ap_ref.at[0], cin, bu_sc, x_sc, seg)
        for c in range(NCOL):
            carry_sc[:, _col(c)] = out[c]
        xs = x_sc[...].astype(MXU)
        xs_ref[0] = xs
        y_sc[...] = _mm(xs, c_ref[0]) + d_ref[...] * uu
        _unpermute_rows(y_sc, y_ref, seg)

    in_specs = [pl.BlockSpec((t_chunk, LANES), lambda k, c: (c, k)),
                pl.BlockSpec((1, LANES, 2 * SL), lambda k, c: (k, 0, 0)),
                pl.BlockSpec((1, 8 * seg, 2 * SL), lambda k, c: (k, 0, 0)),
                pl.BlockSpec((1, 2 * SL, LANES), lambda k, c: (k, 0, 0)),
                pl.BlockSpec((1, LANES), lambda k, c: (0, k))]
    out_specs = [pl.BlockSpec((t_chunk, LANES), lambda k, c: (c, k)),
                 pl.BlockSpec((1, 1, 8, 2 * SL), lambda k, c: (k, c, 0, 0)),
                 pl.BlockSpec((1, t_chunk, 2 * SL), lambda k, c: (k, c, 0))]
    return pl.pallas_call(body, grid=(GB, nc), in_specs=in_specs, out_specs=out_specs,
                          out_shape=[jax.ShapeDtypeStruct((s_len, D), F32), jax.ShapeDtypeStruct((GB, nc, 8, 2 * SL), F32),
                                     jax.ShapeDtypeStruct((GB, s_len, 2 * SL), MXU)],
                          scratch_shapes=[pltpu.VMEM((t_chunk, 2 * SL), F32), pltpu.VMEM((t_chunk, 2 * SL), F32),
                                          pltpu.VMEM((t_chunk, LANES), F32), pltpu.VMEM((8, 2 * SL), F32)],
                          compiler_params=_cparams(2), name="s5_scan_fwd")(u, bb, apow, cblk, dskip)


def _s5_scan_bwd(u, dy, xin, xs, bb_t, apow, cblk_t, dskip, t_chunk):
    s_len = u.shape[0]
    nc = s_len // t_chunk
    seg = t_chunk // 8
    half = NCOL // 2

    def body(u_ref, dy_ref, xin_ref, xs_ref, bbt_ref, ap_ref, ct_ref, d_ref, du_ref, dbb_ref, dc_ref, da_ref, dd_ref,
             x_sc, g_sc, y_sc, carry_sc):
        @pl.when(pl.program_id(1) == 0)
        def _():
            carry_sc[...] = jnp.zeros(carry_sc.shape, F32)
            dbb_ref[...] = jnp.zeros(dbb_ref.shape, F32)
            dc_ref[...] = jnp.zeros(dc_ref.shape, F32)
            da_ref[...] = jnp.zeros(da_ref.shape, F32)
            dd_ref[...] = jnp.zeros(dd_ref.shape, F32)

        ap = ap_ref.at[0]
        uu, dyy = _permute_rows(u_ref, seg), _permute_rows(dy_ref, seg)
        cin = [xin_ref[0, 0, :, _col(c)] for c in range(NCOL)]
        x_sc[...] = xs_ref[0].astype(F32)
        g_sc[...] = _mm(dyy, ct_ref[0])
        ends = _segment_scan(g_sc, g_sc, ap, seg, True)
        lam_in = [carry_sc[:, _col(c)] for c in range(NCOL)]
        ent, out = _segment_entries(ends, lam_in, ap, seg, True)
        for c in range(NCOL):
            carry_sc[:, _col(c)] = out[c]
        row = lax.broadcasted_iota(jnp.int32, (8, LANES), 0)
        xp0 = [jnp.where(row == 0, cin[c], pltpu.roll(x_sc[(seg - 1) * 8:seg * 8, _col(c)], 1, 0)) for c in range(NCOL)]

        def fix(i, acc):
            rows, prev, tab = _tile_rows(i), _tile_rows(jnp.maximum(i - 1, 0)), _tile_rows(seg - 1 - i)
            new = list(acc)
            for c in range(half):
                pr, pi = ap[tab, _col(c)], ap[tab, _col(half + c)]
                lr_ = g_sc[rows, _col(c)] + pr * ent[c] + pi * ent[half + c]
                li_ = g_sc[rows, _col(half + c)] + pr * ent[half + c] - pi * ent[c]
                g_sc[rows, _col(c)] = lr_
                g_sc[rows, _col(half + c)] = li_
                xr = jnp.where(i == 0, xp0[c], x_sc[prev, _col(c)])
                xi = jnp.where(i == 0, xp0[half + c], x_sc[prev, _col(half + c)])
                new[c] = acc[c] + lr_ * xr + li_ * xi
                new[half + c] = acc[half + c] + li_ * xr - lr_ * xi
            return tuple(new)

        zero = jnp.zeros((8, LANES), F32)
        dacc = lax.fori_loop(0, seg, fix, (zero,) * NCOL, unroll=SCAN_UNROLL)
        for c in range(NCOL):
            da_ref[0, :, _col(c)] += dacc[c]
        lam = g_sc[...]
        y_sc[...] = _mm(lam, bbt_ref[0]) + d_ref[...] * dyy
        _unpermute_rows(y_sc, du_ref, seg)
        dbb_ref[0] += _mm_tn(uu, lam)
        dc_ref[0] += _mm_tn(xs_ref[0], dyy)
        dd_ref[0] += _stack_rows([jnp.sum(dyy * uu, axis=0, keepdims=True)])

    rev = lambda k, c: (nc - 1 - c, k)
    in_specs = [pl.BlockSpec((t_chunk, LANES), rev), pl.BlockSpec((t_chunk, LANES), rev),
                pl.BlockSpec((1, 1, 8, 2 * SL), lambda k, c: (k, nc - 1 - c, 0, 0)),
                pl.BlockSpec((1, t_chunk, 2 * SL), lambda k, c: (k, nc - 1 - c, 0)),
                pl.BlockSpec((1, 2 * SL, LANES), lambda k, c: (k, 0, 0)),
                pl.BlockSpec((1, 8 * seg, 2 * SL), lambda k, c: (k, 0, 0)),
                pl.BlockSpec((1, LANES, 2 * SL), lambda k, c: (k, 0, 0)),
                pl.BlockSpec((1, LANES), lambda k, c: (0, k))]
    out_specs = [pl.BlockSpec((t_chunk, LANES), rev),
                 pl.BlockSpec((1, LANES, 2 * SL), lambda k, c: (k, 0, 0)),
                 pl.BlockSpec((1, 2 * SL, LANES), lambda k, c: (k, 0, 0)),
                 pl.BlockSpec((1, 8, 2 * SL), lambda k, c: (k, 0, 0)),
                 pl.BlockSpec((1, 8, LANES), lambda k, c: (k, 0, 0))]
    out_shape = [jax.ShapeDtypeStruct((s_len, D), F32), jax.ShapeDtypeStruct((GB, LANES, 2 * SL), F32),
                 jax.ShapeDtypeStruct((GB, 2 * SL, LANES), F32), jax.ShapeDtypeStruct((GB, 8, 2 * SL), F32),
                 jax.ShapeDtypeStruct((GB, 8, LANES), F32)]
    return pl.pallas_call(body, grid=(GB, nc), in_specs=in_specs, out_specs=out_specs, out_shape=out_shape,
                          scratch_shapes=[pltpu.VMEM((t_chunk, 2 * SL), F32)] * 2 + [pltpu.VMEM((t_chunk, LANES), F32),
                                                                                        pltpu.VMEM((8, 2 * SL), F32)],
                          compiler_params=_cparams(2), name="s5_scan_bwd")(u, dy, xin, xs, bb_t, apow, cblk_t, dskip)


_GELU_K = math.sqrt(2.0 / math.pi)


def _gelu(y):
    t = jnp.tanh(_GELU_K * (y + 0.044715 * (y * y * y)))
    return 0.5 * y * (1.0 + t), 0.5 * (1.0 + t) + 0.5 * y * (1.0 - t * t) * (_GELU_K * (1.0 + 3 * 0.044715 * (y * y)))


def _s5_in_fwd(h, gmix, win, tm):
    def fn(h, gmix, win):
        return (_mm(_rms_fwd(h, gmix)[0], win),), ()

    return _rowcall("s5_in_fwd", fn, tm, [h], [gmix, win], [(D, F32)], [])[0][0]


def _s5_in_bwd(h, du, dh, gmix, win_t, tm):
    def fn(h, du, dh, gmix, win_t):
        hn, xh, r = _rms_fwd(h, gmix)
        dx, dg = _rms_bwd(_mm(du, win_t), xh, r, gmix)
        return (dh + dx,), (_mm_tn(hn, du), dg)

    return _rowcall("s5_in_bwd", fn, tm, [h, du, dh], [gmix, win_t], [(D, F32)], [(D, D), (1, D)])


def _s5_out_fwd(h, y, wglu, tm):
    def fn(h, y, wglu):
        z = _mm(_gelu(y)[0], wglu)
        return (h + z[:, :D] * jax.nn.sigmoid(z[:, D:]),), ()

    return _rowcall("s5_out_fwd", fn, tm, [h, y], [wglu], [(D, F32)], [])[0][0]


def _s5_out_bwd(dh, y, wglu, wglu_t, tm):
    def fn(dh, y, wglu, wglu_t):
        yg, dgelu = _gelu(y)
        z = _mm(yg, wglu)
        val, sg = z[:, :D], jax.nn.sigmoid(z[:, D:])
        dz = jnp.concatenate([dh * sg, dh * val * sg * (1.0 - sg)], axis=1)
        return (_mm(dz, wglu_t) * dgelu,), (_mm_tn(yg, dz),)

    return _rowcall("s5_out_bwd", fn, tm, [dh, y], [wglu, wglu_t], [(D, F32)], [(D, 2 * D)])


def _final_loss(h, tgt, gfin, tm):
    def fn(h, tgt, gfin):
        y, xh, r = _rms_fwd(h, gfin)
        err = y - tgt
        dx, dg = _rms_bwd(err * (1.0 / D), xh, r, gfin)
        return (dx,), (jnp.sum(err * err, axis=0, keepdims=True), dg)

    return _rowcall("final_loss", fn, tm, [h, tgt], [gfin], [(D, F32)], [(1, D), (1, D)])


def _bf(a):
    return a.astype(MXU)


def _s5_block_mats(b_re, b_im, c_re, c_im):
    gl = G // GB
    eye = jnp.eye(gl, dtype=F32)

    def b_blk(b):
        bt = b.reshape(GB, gl, P, C).transpose(0, 1, 3, 2)
        return (bt[:, :, :, None, :] * eye[None, :, None, :, None]).reshape(GB, gl * C, gl * P)

    def c_blk(cm):
        ct = cm.reshape(GB, gl, C, P).transpose(0, 1, 3, 2)
        return (ct[:, :, :, None, :] * eye[None, :, None, :, None]).reshape(GB, gl * P, gl * C)

    braw = jnp.concatenate([b_blk(b_re), b_blk(b_im)], axis=2)
    cblk = jnp.concatenate([c_blk(c_re), -c_blk(c_im)], axis=1)
    return braw, cblk


def _s5_unblock_b(d):
    gl = G // GB
    eye = jnp.eye(gl, dtype=F32)
    d5 = d.reshape(GB, gl, C, gl, P)
    return jnp.sum(d5 * eye[None, :, None, :, None], axis=3).transpose(0, 1, 3, 2).reshape(G, P, C)


def _s5_unblock_c(d):
    gl = G // GB
    eye = jnp.eye(gl, dtype=F32)
    d5 = d.reshape(GB, gl, P, gl, C)
    return jnp.sum(d5 * eye[None, :, None, :, None], axis=3).transpose(0, 1, 3, 2).reshape(G, C, P)


def _tiles(s_len):
    return (min(512, s_len), min(512, s_len), min(2048, s_len),
            (min(1024, s_len), min(4096, s_len)), (min(1024, s_len), min(2048, s_len)),
            min(256, s_len), min(256, s_len))


def _merge_shards(g, axis):
    moved = jnp.moveaxis(g, 0, axis)
    return moved.reshape(moved.shape[:axis] + (-1,) + moved.shape[axis + 2:])


def _split_shards(full, axis):
    cut = full.reshape(full.shape[:axis] + (NDEV, -1) + full.shape[axis + 1:])
    return jnp.moveaxis(cut, axis, 0)


def _sequence_step(x, pos, tgt, w, late=(), late_small=()):
    s_len = x.shape[0]
    tm_w, tm, t_chunk, t_fwd, t_bwd, tm_ff, tm_fb = _tiles(s_len)
    seg = t_chunk // 8
    row = lambda v: v.reshape(1, -1)

    wa = _bf(jnp.pad(w["mla_w_a"][0], ((0, 0), (0, AW - (QL + KVL + ROPE)))))
    wuq = _bf(jnp.pad(w["mla_w_uq"][0].reshape(QL, HEADS, NOPE + ROPE), ((0, 0), (0, 0), (0, HC - NOPE - ROPE))).reshape(QL, HEADS * HC))
    wukv = _bf(w["mla_w_ukv"][0])
    inv = 1.0 / (ROPE_THETA ** (jnp.arange(0, ROPE, 2, dtype=F32) / ROPE))
    invf = jnp.concatenate([inv, inv, jnp.zeros((LANES - ROPE,), F32)]).reshape(1, LANES)
    gmix0, gmix1 = row(w["g_mix"][0]), row(w["g_mix"][1])
    gq, gkv = row(w["mla_g_q"][0]), row(w["mla_g_kv"][0])
    lr = w["ssm_lambda_re"][0].reshape(GB, 1, SL)
    li = w["ssm_lambda_im"][0].reshape(GB, 1, SL)
    ldt = jnp.broadcast_to(w["ssm_log_dt"][0][:, None], (G, P)).reshape(GB, 1, SL)
    braw, cblk = _s5_block_mats(w["ssm_b_re"][0], w["ssm_b_im"][0], w["ssm_c_re"][0], w["ssm_c_im"][0])
    cblk = _bf(cblk)

    qc, kc, v = _mla_front_fwd(x, pos, gmix0, wa, gq, gkv, wuq, wukv, invf, tm_w)
    o, lse, gathered = _flash_fwd(qc, kc, v, *t_fwd, [shard for _, _, shard in late])
    w = dict(w)
    for (n, axis, _), g in zip(late, gathered):
        w[n] = _merge_shards(g, axis)
    wo = _bf(w["mla_w_o"][0])
    win, wglu = _bf(w["ssm_w_in"][0]), _bf(w["ssm_w_glu"][0])
    dskip = row(w["ssm_d"][0])
    ffn = []
    for l in range(2):
        wup, wdown = _bf(w["ffn_w_up"][l]), _bf(w["ffn_w_down"][l])
        ffn.append(dict(g=row(w["g_ffn"][l]), wup=wup, wup_t=wup.T, wdown=wdown, wdown_t=wdown.T,
                        cw=w["ffn_conv_w"][l], cb=row(w["ffn_conv_b"][l])))
    h1 = _attn_out_fwd(x, o, wo, tm)
    f0 = ffn[0]
    h2, *saved0 = _ffn_fwd(h1, f0["g"], f0["wup"], f0["cw"], f0["cb"], f0["wdown"], tm_ff)
    bb, apow = _s5_prep(lr, li, ldt, braw, seg)
    u = _s5_in_fwd(h2, gmix1, win, tm)
    y, xin, xs = _s5_scan_fwd(u, bb, apow, cblk, dskip, t_chunk)
    h3 = _s5_out_fwd(h2, y, wglu, tm)
    f1 = ffn[1]
    h4, *saved1 = _ffn_fwd(h3, f1["g"], f1["wup"], f1["cw"], f1["cb"], f1["wdown"], tm_ff)
    (dh4,), (sq, d_gfinal) = _final_loss(h4, tgt, row(w["g_final"]), tm)
    loss = 0.5 * jnp.sum(sq) / D

    grads = {}

    def ffn_back(hin, dout, saved, f):
        act, upv, upg = saved
        din, dupv, dupg, hn, cacc, dg = _ffn_bwd(hin, dout, upv, upg, f["g"], f["cw"], f["cb"], f["wdown_t"], f["wup_t"], tm_fb)
        d_wup = jnp.concatenate([_matmul_tn(hn, dupv, DFF, min(512, s_len), "ffn_dwup_v"),
                                 _matmul_tn(hn, dupg, DFF, min(512, s_len), "ffn_dwup_g")], axis=1)
        d_wdown = _matmul_tn(act, dout, D, min(512, s_len), "ffn_dwdown")
        cflat = cacc.transpose(1, 0, 2).reshape(8, 2 * DFF)
        return din, d_wup, d_wdown, cflat[:3], cflat[3], dg[0]

    dh3, d_wup1, d_wdown1, d_cw1, d_cb1, d_gffn1 = ffn_back(h3, dh4, saved1, f1)
    (dy,), (d_wglu,) = _s5_out_bwd(dh3, y, wglu, wglu.T, tm)
    du, d_bb, d_cblk, d_a, d_dsk = _s5_scan_bwd(u, dy, xin, xs, bb.transpose(0, 2, 1), apow, cblk.transpose(0, 2, 1), dskip, t_chunk)
    d_braw, d_lr, d_li, d_ldt = _s5_prep_bwd(lr, li, ldt, braw, d_bb, d_a)
    (dh2,), (d_win, d_gmix1) = _s5_in_bwd(h2, du, dh3, gmix1, win.T, tm)
    dh1, d_wup0, d_wdown0, d_cw0, d_cb0, d_gffn0 = ffn_back(h1, dh2, saved0, f0)
    (do, delta), (d_wo,) = _attn_out_bwd(dh1, o, wo.T, tm)

    grads["mla_w_o"] = d_wo[None]
    grads["ssm_w_in"] = d_win[None]
    grads["ssm_lambda_re"] = d_lr.reshape(1, G, P)
    grads["ssm_lambda_im"] = d_li.reshape(1, G, P)
    grads["ssm_log_dt"] = jnp.sum(d_ldt.reshape(G, P), axis=1)[None]
    grads["ssm_b_re"] = _s5_unblock_b(d_braw[:, :, :SL])[None]
    grads["ssm_b_im"] = _s5_unblock_b(d_braw[:, :, SL:])[None]
    grads["ssm_c_re"] = _s5_unblock_c(d_cblk[:, :SL, :])[None]
    grads["ssm_c_im"] = -_s5_unblock_c(d_cblk[:, SL:, :])[None]
    grads["ssm_d"] = jnp.sum(d_dsk, axis=1).reshape(1, D)
    grads["ssm_w_glu"] = d_wglu[None]
    grads["ffn_w_up"] = jnp.stack([d_wup0, d_wup1])
    grads["ffn_conv_w"] = jnp.stack([d_cw0, d_cw1])
    grads["ffn_conv_b"] = jnp.stack([d_cb0, d_cb1])
    grads["ffn_w_down"] = jnp.stack([d_wdown0, d_wdown1])
    grads["g_ffn"] = jnp.stack([d_gffn0, d_gffn1])
    grads["g_final"] = d_gfinal[0]

    sends = [_split_shards(grads[n], axis).astype(MXU) for n, axis, _ in late]
    casts = [jnp.concatenate([grads[n].reshape(-1) for n in late_small])] if late_small else []
    dqc, dkc, dv, landed, landed_small = _flash_bwd(qc, kc, v, do, lse, delta, *t_bwd, sends, casts)
    (dx,), (d_wa, d_wuq, d_wukv, d_gq, d_gkv, d_gmix0) = _mla_front_bwd(
        x, pos, dqc, dkc, dv, dh1, gmix0, wa, gq, gkv, wuq, wukv, invf, wa.T, wuq.T, wukv.T, tm_w)
    grads["mla_w_a"] = d_wa[None, :, :QL + KVL + ROPE]
    grads["mla_g_q"] = d_gq
    grads["mla_g_kv"] = d_gkv
    grads["mla_w_uq"] = d_wuq.reshape(QL, HEADS, HC)[:, :, :NOPE + ROPE].reshape(1, QL, HEADS * (NOPE + ROPE))
    grads["mla_w_ukv"] = d_wukv[None]
    grads["g_mix"] = jnp.concatenate([d_gmix0, d_gmix1], axis=0)
    return loss, dx, grads, landed, (landed_small[0] if late_small else None)


MESH = pl.DeviceIdType.MESH
ANY = pl.BlockSpec(memory_space=pl.ANY)


def _gather_many(blocks, name):
    n = len(blocks)

    def body(*refs):
        _gather_issue(refs[:n], refs[n:2 * n], refs[2 * n:])
        _gather_finish(refs[:n], refs[n:2 * n], refs[2 * n:])

    return pl.pallas_call(body, out_shape=_gathered_shapes(blocks), in_specs=[ANY] * n, out_specs=[ANY] * n,
                          scratch_shapes=_comm_scratch(n), name=name)(*blocks)


def _comm_scratch(n):
    if n == 0:
        return []
    return [pltpu.SemaphoreType.DMA((7 * n,)), pltpu.SemaphoreType.DMA((7 * n,)), pltpu.SemaphoreType.DMA((n,))]


def _gathered_shapes(blocks):
    return [jax.ShapeDtypeStruct((NDEV,) + b.shape, b.dtype) for b in blocks]


def _gather_copies(ins, outs, sems):
    send_sems, recv_sems, local_sems = sems
    x, y, c = lax.axis_index("x"), lax.axis_index("y"), lax.axis_index("c")
    me, sibling = (x, y, c), (x, y, 1 - c)
    chips = [(1 - x, y), (x, 1 - y), (1 - x, 1 - y)]

    def copy(a, k, block, to, own=False):
        px, py, pc = block
        slot = outs[a].at[4 * px + 2 * py + pc]
        return pltpu.make_async_remote_copy(src_ref=ins[a] if own else slot, dst_ref=slot,
                                            send_sem=send_sems.at[7 * a + k], recv_sem=recv_sems.at[7 * a + k],
                                            device_id=to, device_id_type=MESH)

    mine = [pltpu.make_async_copy(ins[a], outs[a].at[4 * x + 2 * y + c], local_sems.at[a]) for a in range(len(ins))]
    first = []
    for a in range(len(ins)):
        first.append(copy(a, 0, me, sibling, own=True))
        first += [copy(a, 1 + j, me, (*chip, c), own=True) for j, chip in enumerate(chips)]
    return copy, mine, first, me, sibling, chips, c


def _gather_issue(ins, outs, sems):
    _, mine, first, *_ = _gather_copies(ins, outs, sems)
    for cp in mine + first:
        cp.start()


def _gather_finish(ins, outs, sems):
    copy, mine, first, me, sibling, chips, c = _gather_copies(ins, outs, sems)
    passed = []
    for j, chip in enumerate(chips):
        for a in range(len(ins)):
            copy(a, 1 + j, (*chip, c), me).wait_recv()
            passed.append(copy(a, 4 + j, (*chip, c), sibling))
            passed[-1].start()
    for a in range(len(ins)):
        copy(a, 0, sibling, me).wait_recv()
        for j, chip in enumerate(chips):
            copy(a, 4 + j, (*chip, 1 - c), me).wait_recv()
    for cp in first + passed:
        cp.wait_send()
    for cp in mine:
        cp.wait()


def _exchange_many(sends, casts, name):
    n, tot = len(sends), len(sends) + len(casts)

    def body(*refs):
        _exchange_issue(refs[:tot], refs[tot:2 * tot], refs[2 * tot:], n)
        _exchange_finish(refs[:tot], refs[tot:2 * tot], refs[2 * tot:], n)

    out = pl.pallas_call(body, out_shape=_exchanged_shapes(sends, casts), in_specs=[ANY] * tot, out_specs=[ANY] * tot,
                         scratch_shapes=_comm_scratch(tot), name=name)(*sends, *casts)
    return out[:n], out[n:]


def _exchanged_shapes(sends, casts):
    return [jax.ShapeDtypeStruct(s.shape, s.dtype) for s in sends] + _gathered_shapes(casts)


def _exchange_copies(ins, outs, sems, n, with_arrivals):
    send_sems, recv_sems, local_sems = sems
    x, y, c = lax.axis_index("x"), lax.axis_index("y"), lax.axis_index("c")
    me = 4 * x + 2 * y + c
    local, sent, arrivals = [], [], []
    for a in range(len(ins)):
        own = ins[a].at[me] if a < n else ins[a]
        local.append(pltpu.make_async_copy(own, outs[a].at[me], local_sems.at[a]))
        for m in range(1, NDEV):
            px = 1 - x if m & 4 else x
            py = 1 - y if m & 2 else y
            pc = 1 - c if m & 1 else c
            peer = 4 * px + 2 * py + pc
            to = dict(send_sem=send_sems.at[7 * a + m - 1], recv_sem=recv_sems.at[7 * a + m - 1],
                      device_id=(px, py, pc), device_id_type=MESH)
            sent.append(pltpu.make_async_remote_copy(src_ref=ins[a].at[peer] if a < n else ins[a], dst_ref=outs[a].at[me], **to))
            if with_arrivals:
                arrivals.append(pltpu.make_async_remote_copy(src_ref=own, dst_ref=outs[a].at[peer], **to))
    return local, sent, arrivals


def _exchange_issue(ins, outs, sems, n):
    local, sent, _ = _exchange_copies(ins, outs, sems, n, False)
    for cp in local + sent:
        cp.start()


def _exchange_finish(ins, outs, sems, n):
    local, sent, arrivals = _exchange_copies(ins, outs, sems, n, True)
    for cp in arrivals:
        cp.wait_recv()
    for cp in sent:
        cp.wait_send()
    for cp in local:
        cp.wait()


def _adam_math(w, parts, m, v):
    g = parts[0].astype(F32)
    for k in range(1, NDEV):
        g = g + parts[k].astype(F32)
    m2 = ADAM_B1 * m + (1.0 - ADAM_B1) * g
    v2 = ADAM_B2 * v + (1.0 - ADAM_B2) * jnp.square(g)
    m_hat = m2 / (1.0 - ADAM_B1 ** ADAM_STEP)
    v_hat = v2 / (1.0 - ADAM_B2 ** ADAM_STEP)
    return g, -ADAM_LR * (m_hat / (jnp.sqrt(v_hat) + ADAM_EPS) + ADAM_WD * w), m2, v2


def _adamw_many(ws, parts, ms, vs, name):
    n = len(ws)

    def body(*refs):
        w_refs, p_refs, m_refs, v_refs, outs = refs[:n], refs[n:2 * n], refs[2 * n:3 * n], refs[3 * n:4 * n], refs[4 * n:]
        for a in range(n):
            res = _adam_math(w_refs[a][...], [p_refs[a][k] for k in range(NDEV)], m_refs[a][...], v_refs[a][...])
            for o, val in zip(outs[4 * a:4 * a + 4], res):
                o[...] = val

    out = pl.pallas_call(body, out_shape=[jax.ShapeDtypeStruct(w.shape, F32) for w in ws for _ in range(4)],
                         compiler_params=pltpu.CompilerParams(vmem_limit_bytes=VMEM_LIMIT), name=name)(*ws, *parts, *ms, *vs)
    return [out[4 * a:4 * a + 4] for a in range(n)]


def _adamw_rows(w, parts, m, v, tr, name):
    rows, cols = w.shape

    def body(w_ref, p_ref, m_ref, v_ref, g_ref, d_ref, m2_ref, v2_ref):
        res = _adam_math(w_ref[...], [p_ref[k] for k in range(NDEV)], m_ref[...], v_ref[...])
        for o, val in zip((g_ref, d_ref, m2_ref, v2_ref), res):
            o[...] = val

    flat = pl.BlockSpec((tr, cols), lambda i: (i, 0))
    return pl.pallas_call(body, grid=(rows // tr,),
                          in_specs=[flat, pl.BlockSpec((NDEV, tr, cols), lambda i: (0, i, 0)), flat, flat],
                          out_specs=[flat] * 4, out_shape=[jax.ShapeDtypeStruct((rows, cols), F32)] * 4,
                          compiler_params=_cparams(1), name=name)(w, parts, m, v)


SHARDED = (("mla_w_a", 1), ("mla_w_uq", 2), ("mla_w_ukv", 2), ("mla_w_o", 1), ("ssm_w_in", 1), ("ssm_d", 1),
           ("ssm_w_glu", 2), ("ffn_w_up", 2), ("ffn_conv_w", 2), ("ffn_w_down", 1))
WIRE_EXACT = ("ssm_d", "ffn_conv_w")
GATHERED_FIRST = ("mla_w_a", "mla_w_uq", "mla_w_ukv")
REPLICATED_LAST = ("mla_g_q", "mla_g_kv", "g_mix")
REPLICATED = ("mla_g_q", "mla_g_kv", "ssm_lambda_re", "ssm_lambda_im", "ssm_log_dt", "ssm_b_re", "ssm_b_im",
              "ssm_c_re", "ssm_c_im", "ffn_conv_b", "g_mix", "g_ffn", "g_final")
WEIGHTS = ("mla_w_a", "mla_g_q", "mla_g_kv", "mla_w_uq", "mla_w_ukv", "mla_w_o", "ssm_w_in", "ssm_lambda_re",
           "ssm_lambda_im", "ssm_log_dt", "ssm_b_re", "ssm_b_im", "ssm_c_re", "ssm_c_im", "ssm_d", "ssm_w_glu",
           "ffn_w_up", "ffn_conv_w", "ffn_conv_b", "ffn_w_down", "g_mix", "g_ffn", "g_final")


ADAM_ROW_TILED = (("ffn_w_up", 256), ("ffn_w_down", 176))


def _two_d(shape):
    if len(shape) == 1:
        return (1, shape[0])
    if len(shape) > 2 and shape[-1] < LANES:
        return (math.prod(shape[:-2]), shape[-2] * shape[-1])
    return (math.prod(shape[:-1]), shape[-1])


def kernel(x, positions, mla_w_a, mla_g_q, mla_g_kv, mla_w_uq, mla_w_ukv, mla_w_o, ssm_w_in, ssm_lambda_re, ssm_lambda_im, ssm_log_dt, ssm_b_re, ssm_b_im, ssm_c_re, ssm_c_im, ssm_d, ssm_w_glu, ffn_w_up, ffn_conv_w, ffn_conv_b, ffn_w_down, g_mix, g_ffn, g_final, loss_target, m_mla_w_a, m_mla_g_q, m_mla_g_kv, m_mla_w_uq, m_mla_w_ukv, m_mla_w_o, m_ssm_w_in, m_ssm_lambda_re, m_ssm_lambda_im, m_ssm_log_dt, m_ssm_b_re, m_ssm_b_im, m_ssm_c_re, m_ssm_c_im, m_ssm_d, m_ssm_w_glu, m_ffn_w_up, m_ffn_conv_w, m_ffn_conv_b, m_ffn_w_down, m_g_mix, m_g_ffn, m_g_final, v_mla_w_a, v_mla_g_q, v_mla_g_kv, v_mla_w_uq, v_mla_w_ukv, v_mla_w_o, v_ssm_w_in, v_ssm_lambda_re, v_ssm_lambda_im, v_ssm_log_dt, v_ssm_b_re, v_ssm_b_im, v_ssm_c_re, v_ssm_c_im, v_ssm_d, v_ssm_w_glu, v_ffn_w_up, v_ffn_conv_w, v_ffn_conv_b, v_ffn_w_down, v_g_mix, v_g_ffn, v_g_final):
    a = dict(locals())
    s_len = x.shape[1]
    sh_names = [n for n, _ in SHARDED]

    def wire(n):
        return a[n] if n in WIRE_EXACT else a[n].astype(MXU)

    early = [(n, axis) for n, axis in SHARDED if n in GATHERED_FIRST]
    late = [(n, axis, wire(n)) for n, axis in SHARDED if n not in GATHERED_FIRST]
    w = {n: a[n] for n in REPLICATED}
    for (n, axis), g in zip(early, _gather_many([wire(n) for n, _ in early], "gather_weights")):
        w[n] = _merge_shards(g, axis)

    rep_late = [n for n in REPLICATED if n not in REPLICATED_LAST]
    loss, dx, grads, landed_late, rep_late_all = _sequence_step(
        x[0], positions.reshape(s_len, 1).astype(F32), loss_target[0], w, late, rep_late)
    loss = lax.psum(loss, ("x", "y", "c"))

    packed = jnp.concatenate([grads[n].reshape(-1) for n in REPLICATED_LAST])
    landed_early, (rep_last_all,) = _exchange_many([_split_shards(grads[n], axis).astype(MXU) for n, axis in early],
                                                   [packed], "exchange_grads")
    parts_of = dict(zip([n for n, _, _ in late], landed_late)) | dict(zip([n for n, _ in early], landed_early))
    for names, pack in ((rep_late, rep_late_all), (REPLICATED_LAST, rep_last_all)):
        off = 0
        for n in names:
            size = math.prod(a[n].shape)
            parts_of[n] = pack[:, off:off + size]
            off += size

    def view(n, arr, lead=()):
        return arr.reshape(lead + _two_d(a[n].shape))

    out = {}

    def finish(n, res):
        for kind, val in zip(("grad_", "delta_", "new_m_", "new_v_"), res):
            out[kind + n] = val.reshape(a[n].shape)

    for n, tr in ADAM_ROW_TILED:
        finish(n, _adamw_rows(view(n, a[n]), view(n, parts_of[n], (NDEV,)), view(n, a["m_" + n]), view(n, a["v_" + n]),
                              tr, "adamw_" + n))
    tiled = [n for n, _ in ADAM_ROW_TILED]
    for names, tag in (([n for n in sh_names if n not in tiled], "adamw_sharded"), (list(REPLICATED), "adamw_replicated")):
        res = _adamw_many([view(n, a[n]) for n in names], [view(n, parts_of[n], (NDEV,)) for n in names],
                          [view(n, a["m_" + n]) for n in names], [view(n, a["v_" + n]) for n in names], tag)
        for n, r in zip(names, res):
            finish(n, r)
    return (loss, dx[None], *[out[kind + n] for kind in ("grad_", "delta_", "new_m_", "new_v_") for n in WEIGHTS])
```

```python
import functools
import math

import jax
import jax.numpy as jnp
from jax import lax
from jax.experimental import pallas as pl
from jax.experimental.pallas import tpu as pltpu

F32 = jnp.float32
MXU = jnp.bfloat16

D = 1024
HEADS = 8
NOPE = 128
ROPE = 64
VH = 128
QL = 384
KVL = 256
CHUNK = 64
ROPE_THETA = 10000.0
EPS = 1e-6
G, P, C = 64, 64, 16
DFF = 2816
ADAM_LR, ADAM_B1, ADAM_B2, ADAM_EPS, ADAM_WD, ADAM_STEP = 0.001, 0.9, 0.999, 1e-08, 0.01, 10

LANES = 128
AW = 768
HC = 256
GB = 8
SL = (G // GB) * P
NCOL = 2 * SL // LANES
TC = DFF
HALO = 16
NDEV = 8
VMEM_LIMIT = 56 * 1024 * 1024


def _mm(a, b):
    return jnp.dot(a.astype(MXU), b.astype(MXU), preferred_element_type=F32)


def _mm_tn(a, b):
    return lax.dot_general(a.astype(MXU), b.astype(MXU), (((0,), (0,)), ((), ())), preferred_element_type=F32)


def _mm_nt(a, b):
    return lax.dot_general(a.astype(MXU), b.astype(MXU), (((1,), (1,)), ((), ())), preferred_element_type=F32)


def _rms_fwd(x, g):
    r = lax.rsqrt(jnp.mean(x * x, axis=-1, keepdims=True) + EPS)
    xh = x * r
    return xh * g, xh, r


def _rms_bwd(dy, xh, r, g):
    dxh = dy * g
    dx = r * (dxh - xh * jnp.mean(dxh * xh, axis=-1, keepdims=True))
    return dx, jnp.sum(dy * xh, axis=0, keepdims=True)


def _rot_partner(b):
    lane = lax.broadcasted_iota(jnp.int32, b.shape, 1)
    return jnp.where(lane < ROPE // 2, -pltpu.roll(b, LANES - ROPE // 2, 1), pltpu.roll(b, ROPE // 2, 1))


def _rope_blk(b, cos2, sin2):
    return b * cos2 + _rot_partner(b) * sin2


def _unrope_blk(db, cos2, sin2):
    return db * cos2 - _rot_partner(db * sin2)


def _cparams(n_axes, vmem=VMEM_LIMIT):
    return pltpu.CompilerParams(dimension_semantics=("arbitrary",) * n_axes, vmem_limit_bytes=vmem)


def _rowcall(name, fn, tm, row_ins, consts, row_outs, acc_outs):
    n = row_ins[0].shape[0]
    n_in = len(row_ins) + len(consts)
    n_ro = len(row_outs)

    def body(*refs):
        ins, ro_refs, acc_refs = refs[:n_in], refs[n_in:n_in + n_ro], refs[n_in + n_ro:]
        ro, ao = fn(*[r[...] for r in ins])

        @pl.when(pl.program_id(0) == 0)
        def _():
            for r in acc_refs:
                r[...] = jnp.zeros(r.shape, r.dtype)

        for r, val in zip(ro_refs, ro):
            r[...] = val.astype(r.dtype)
        for r, val in zip(acc_refs, ao):
            r[...] += val

    in_specs = [pl.BlockSpec((tm, a.shape[1]), lambda i: (i, 0)) for a in row_ins]
    in_specs += [pl.BlockSpec(c.shape, lambda i, nd=c.ndim: (0,) * nd) for c in consts]
    out_specs = [pl.BlockSpec((tm, w), lambda i: (i, 0)) for w, _ in row_outs]
    out_specs += [pl.BlockSpec(s, lambda i, nd=len(s): (0,) * nd) for s in acc_outs]
    out_shape = [jax.ShapeDtypeStruct((n, w), dt) for w, dt in row_outs]
    out_shape += [jax.ShapeDtypeStruct(s, F32) for s in acc_outs]
    out = pl.pallas_call(body, grid=(n // tm,), in_specs=in_specs, out_specs=out_specs, out_shape=out_shape,
                         compiler_params=_cparams(1), name=name)(*row_ins, *consts)
    return list(out[:n_ro]), list(out[n_ro:])


def _mla_front_tile(x, pos, gmix, wa, gq, gkv, wuq, wukv, invf):
    hn, xh, r = _rms_fwd(x, gmix)
    a = _mm(hn, wa)
    cq, ckv, krb = a[:, :QL], a[:, QL:QL + KVL], a[:, QL + KVL:]
    cqn, cqh, rq = _rms_fwd(cq, gq)
    ckvn, ckvh, rkv = _rms_fwd(ckv, gkv)
    q = _mm(cqn, wuq)
    kv = _mm(ckvn, wukv)
    ang = pos * invf
    cos2, sin2 = jnp.cos(ang), jnp.sin(ang)
    krr = _rope_blk(krb, cos2, sin2)
    qp, kp, vp = [], [], []
    for h in range(HEADS):
        qp += [q[:, h * HC:h * HC + NOPE], _rope_blk(q[:, h * HC + NOPE:(h + 1) * HC], cos2, sin2)]
        kp += [kv[:, h * HC:h * HC + NOPE], krr]
        vp += [kv[:, h * HC + NOPE:(h + 1) * HC]]
    res = (hn, xh, r, cqn, cqh, rq, ckvn, ckvh, rkv, cos2, sin2)
    return jnp.concatenate(qp, axis=1), jnp.concatenate(kp, axis=1), jnp.concatenate(vp, axis=1), res


def _mla_front_fwd(x, pos, gmix, wa, gq, gkv, wuq, wukv, invf, tm):
    def fn(*args):
        qc, kc, v, _ = _mla_front_tile(*args)
        return (qc * Q_PRESCALE, kc, v), ()

    return _rowcall("mla_front_fwd", fn, tm, [x, pos], [gmix, wa, gq, gkv, wuq, wukv, invf],
                    [(HEADS * HC, MXU), (HEADS * HC, MXU), (HEADS * VH, MXU)], [])[0]


def _mla_front_bwd(x, pos, dqc, dkc, dv, dh, gmix, wa, gq, gkv, wuq, wukv, invf, wa_t, wuq_t, wukv_t, tm):
    def fn(x, pos, dqc, dkc, dv, dh, gmix, wa, gq, gkv, wuq, wukv, invf, wa_t, wuq_t, wukv_t):
        _, _, _, (hn, xh, r, cqn, cqh, rq, ckvn, ckvh, rkv, cos2, sin2) = _mla_front_tile(
            x, pos, gmix, wa, gq, gkv, wuq, wukv, invf)
        dqc, dkc, dv = dqc * SM_SCALE, dkc.astype(F32), dv.astype(F32)
        dqp, dkvp = [], []
        dkr = jnp.zeros((x.shape[0], LANES), F32)
        for h in range(HEADS):
            dqp += [dqc[:, h * HC:h * HC + NOPE], _unrope_blk(dqc[:, h * HC + NOPE:(h + 1) * HC], cos2, sin2)]
            dkvp += [dkc[:, h * HC:h * HC + NOPE], dv[:, h * VH:(h + 1) * VH]]
            dkr = dkr + dkc[:, h * HC + NOPE:(h + 1) * HC]
        dq = jnp.concatenate(dqp, axis=1)
        dkv = jnp.concatenate(dkvp, axis=1)
        dkrb = _unrope_blk(dkr, cos2, sin2)
        dcqn = _mm(dq, wuq_t)
        dckvn = _mm(dkv, wukv_t)
        d_wuq = _mm_tn(cqn, dq)
        d_wukv = _mm_tn(ckvn, dkv)
        dcq, d_gq = _rms_bwd(dcqn, cqh, rq, gq)
        dckv, d_gkv = _rms_bwd(dckvn, ckvh, rkv, gkv)
        da = jnp.concatenate([dcq, dckv, dkrb], axis=1)
        d_wa = _mm_tn(hn, da)
        dhn = _mm(da, wa_t)
        dx, d_gmix = _rms_bwd(dhn, xh, r, gmix)
        return (dh + dx,), (d_wa, d_wuq, d_wukv, d_gq, d_gkv, d_gmix)

    return _rowcall("mla_front_bwd", fn, tm, [x, pos, dqc, dkc, dv, dh],
                    [gmix, wa, gq, gkv, wuq, wukv, invf, wa_t, wuq_t, wukv_t], [(D, F32)],
                    [(D, AW), (QL, HEADS * HC), (KVL, HEADS * HC), (1, QL), (1, KVL), (1, D)])


SM_SCALE = (NOPE + ROPE) ** -0.5
LOG2E = 1.0 / math.log(2.0)
Q_PRESCALE = SM_SCALE * LOG2E


def _pair_tables(s_len, tq, tk, q_major):
    pairs = [(qi, ki) for qi in range(s_len // tq) for ki in range(s_len // tk) if ki * tk < (qi + 1) * tq]
    if not q_major:
        pairs.sort(key=lambda p: (p[1], p[0]))
    return (jnp.asarray([p[0] for p in pairs], jnp.int32), jnp.asarray([p[1] for p in pairs], jnp.int32))


def _last_key_tile(qi, tq, tk):
    return ((qi + 1) * tq - 1) // tk


def _visible(qi, ki, tq, tk, width=None):
    shape = (tq, tk if width is None else width)
    row = qi * (tq // CHUNK) + lax.broadcasted_iota(jnp.int32, shape, 0) // CHUNK
    col = ki * (tk // CHUNK) + lax.broadcasted_iota(jnp.int32, shape, 1) // CHUNK
    return col <= row


def _masked_and_not(qi, ki, tq, tk, fn):
    needs_mask = (ki + 1) * tk > qi * tq
    pl.when(needs_mask)(lambda: fn(True))
    pl.when(jnp.logical_not(needs_mask))(lambda: fn(False))


def _by_visible_width(qi, ki, tq, tk, fn):
    if tk % tq or tk == tq:
        _masked_and_not(qi, ki, tq, tk, lambda masked: fn(tk, masked))
        return
    blocks = tk // tq
    seen = qi + 1 - ki * blocks
    pl.when(seen > blocks)(lambda: fn(tk, False))
    for j in range(1, blocks + 1):
        pl.when(seen == j)(functools.partial(fn, j * tq, True))


def _first_and_last_step(n_steps):
    first = jnp.logical_and(pl.program_id(0) == 0, pl.program_id(1) == 0)
    last = jnp.logical_and(pl.program_id(0) == HEADS - 1, pl.program_id(1) == n_steps - 1)
    return first, last


def _flash_fwd(qc, kc, v, tq, tk, riders):
    s_len = qc.shape[0]
    qt, kt = _pair_tables(s_len, tq, tk, True)
    nr = len(riders)

    def body(qt_ref, kt_ref, q_ref, k_ref, v_ref, *rest):
        r_in, (o_ref, lse_ref), r_out = rest[:nr], rest[nr:nr + 2], rest[nr + 2:2 * nr + 2]
        m_sc, l_sc, acc_sc = rest[2 * nr + 2:2 * nr + 5]
        sems = rest[2 * nr + 5:]
        p_id = pl.program_id(1)
        qi, ki = qt_ref[p_id], kt_ref[p_id]
        first, last = _first_and_last_step(qt.shape[0])
        if nr:
            pl.when(first)(lambda: _gather_issue(r_in, r_out, sems))

        @pl.when(ki == 0)
        def _():
            m_sc[...] = jnp.full(m_sc.shape, -jnp.inf, F32)
            l_sc[...] = jnp.zeros(l_sc.shape, F32)
            acc_sc[...] = jnp.zeros(acc_sc.shape, F32)

        def update(width, masked):
            s = _mm_nt(q_ref[...], k_ref[:width, :])
            if masked:
                s = jnp.where(_visible(qi, ki, tq, tk, width), s, -jnp.inf)
            m_old = m_sc[...]
            m_new = jnp.maximum(m_old, jnp.max(s, axis=1, keepdims=True))
            alpha = jnp.exp2(m_old - m_new)
            p = jnp.exp2(s - m_new)
            l_sc[...] = alpha * l_sc[...] + jnp.sum(p, axis=1, keepdims=True)
            acc_sc[...] = alpha * acc_sc[...] + _mm(p, v_ref[:width, :])
            m_sc[...] = m_new

        _by_visible_width(qi, ki, tq, tk, update)

        @pl.when(ki == _last_key_tile(qi, tq, tk))
        def _():
            l = l_sc[...]
            o_ref[...] = (acc_sc[...] / l).astype(o_ref.dtype)
            lse_ref[...] = jnp.broadcast_to(m_sc[...] + jnp.log2(l), lse_ref.shape)

        if nr:
            pl.when(last)(lambda: _gather_finish(r_in, r_out, sems))

    qmap = lambda h, p, qt, kt: (qt[p], h)
    kmap = lambda h, p, qt, kt: (kt[p], h)
    grid_spec = pltpu.PrefetchScalarGridSpec(
        num_scalar_prefetch=2, grid=(HEADS, qt.shape[0]),
        in_specs=[pl.BlockSpec((tq, HC), qmap), pl.BlockSpec((tk, HC), kmap), pl.BlockSpec((tk, VH), kmap)] + [ANY] * nr,
        out_specs=[pl.BlockSpec((tq, VH), qmap), pl.BlockSpec((tq, LANES), qmap)] + [ANY] * nr,
        scratch_shapes=[pltpu.VMEM((tq, 1), F32), pltpu.VMEM((tq, 1), F32), pltpu.VMEM((tq, VH), F32)] + _comm_scratch(nr))
    out = pl.pallas_call(body, grid_spec=grid_spec,
                         out_shape=[jax.ShapeDtypeStruct((s_len, HEADS * VH), MXU),
                                    jax.ShapeDtypeStruct((s_len, HEADS * LANES), F32)] + _gathered_shapes(riders),
                         compiler_params=_cparams(2), name="flash_fwd")(qt, kt, qc, kc, v, *riders)
    return out[0], out[1], out[2:]


def _tile_dscores(q, k, v, do, lse, delta, qi, ki, tq, tk, width, masked):
    p = jnp.exp2(_mm_nt(q, k) - lse[:, :1])
    if masked:
        p = jnp.where(_visible(qi, ki, tq, tk, width), p, 0.0)
    return p, p * (_mm_nt(do, v) - delta[:, :1])


def _flash_bwd(qc, kc, v, do, lse, delta, tq, tk, riders, casts=()):
    s_len = qc.shape[0]
    qt, kt = _pair_tables(s_len, tq, tk, False)
    nq = s_len // tq
    n_rows, nr = len(riders), len(riders) + len(casts)

    def body(qt_ref, kt_ref, q_ref, k_ref, v_ref, do_ref, lse_ref, dl_ref, *rest):
        r_in, (dq_ref, dk_ref, dv_ref), r_out = rest[:nr], rest[nr:nr + 3], rest[nr + 3:2 * nr + 3]
        dk_sc, dv_sc = rest[2 * nr + 3:2 * nr + 5]
        sems = rest[2 * nr + 5:]
        p_id = pl.program_id(1)
        qi, ki = qt_ref[p_id], kt_ref[p_id]
        rows = pl.ds(pl.multiple_of(qi * tq, tq), tq)
        first, last = _first_and_last_step(qt.shape[0])
        if nr:
            pl.when(first)(lambda: _exchange_issue(r_in, r_out, sems, n_rows))

        @pl.when(qi == (ki * tk) // tq)
        def _():
            dk_sc[...] = jnp.zeros(dk_sc.shape, F32)
            dv_sc[...] = jnp.zeros(dv_sc.shape, F32)

        @pl.when(ki == 0)
        def _():
            dq_ref[rows, :] = jnp.zeros((tq, HC), F32)

        def update(width, masked):
            q, k, do = q_ref[...], k_ref[:width, :], do_ref[...]
            p, ds = _tile_dscores(q, k, v_ref[:width, :], do, lse_ref[...], dl_ref[...], qi, ki, tq, tk, width, masked)
            ds = ds.astype(MXU)
            dv_sc[:width, :] += _mm_tn(p, do)
            dk_sc[:width, :] += _mm_tn(ds, q)
            dq_ref[rows, :] += _mm(ds, k)

        _by_visible_width(qi, ki, tq, tk, update)

        @pl.when(qi == nq - 1)
        def _():
            dk_ref[...] = (dk_sc[...] * (1.0 / LOG2E)).astype(dk_ref.dtype)
            dv_ref[...] = dv_sc[...].astype(dv_ref.dtype)

        if nr:
            pl.when(last)(lambda: _exchange_finish(r_in, r_out, sems, n_rows))

    qmap = lambda h, p, qt, kt: (qt[p], h)
    kmap = lambda h, p, qt, kt: (kt[p], h)
    grid_spec = pltpu.PrefetchScalarGridSpec(
        num_scalar_prefetch=2, grid=(HEADS, qt.shape[0]),
        in_specs=[pl.BlockSpec((tq, HC), qmap), pl.BlockSpec((tk, HC), kmap), pl.BlockSpec((tk, VH), kmap),
                  pl.BlockSpec((tq, VH), qmap), pl.BlockSpec((tq, LANES), qmap), pl.BlockSpec((tq, LANES), qmap)] + [ANY] * nr,
        out_specs=[pl.BlockSpec((s_len, HC), lambda h, p, qt, kt: (0, h), pipeline_mode=pl.Buffered(1)),
                   pl.BlockSpec((tk, HC), kmap), pl.BlockSpec((tk, VH), kmap)] + [ANY] * nr,
        scratch_shapes=[pltpu.VMEM((tk, HC), F32), pltpu.VMEM((tk, VH), F32)] + _comm_scratch(nr))
    out = pl.pallas_call(body, grid_spec=grid_spec,
                         out_shape=[jax.ShapeDtypeStruct((s_len, HEADS * HC), F32),
                                    jax.ShapeDtypeStruct((s_len, HEADS * HC), MXU),
                                    jax.ShapeDtypeStruct((s_len, HEADS * VH), MXU)] + _exchanged_shapes(riders, casts),
                         compiler_params=_cparams(2), name="flash_bwd")(qt, kt, qc, kc, v, do, lse, delta, *riders, *casts)
    return out[0], out[1], out[2], out[3:3 + n_rows], out[3 + n_rows:]


def _attn_out_fwd(x, o, wo, tm):
    def fn(x, o, wo):
        return (x + _mm(o, wo),), ()

    return _rowcall("attn_out_fwd", fn, tm, [x, o], [wo], [(D, F32)], [])[0][0]


def _attn_out_bwd(dh, o, wo_t, tm):
    def fn(dh, o, wo_t):
        do = _mm(dh, wo_t)
        of = o.astype(F32)
        dl = [jnp.broadcast_to(jnp.sum(do[:, h * VH:(h + 1) * VH] * of[:, h * VH:(h + 1) * VH], axis=1, keepdims=True),
                               (dh.shape[0], LANES)) for h in range(HEADS)]
        return (do, jnp.concatenate(dl, axis=1)), (_mm_tn(o, dh),)

    return _rowcall("attn_out_bwd", fn, tm, [dh, o], [wo_t], [(HEADS * VH, MXU), (HEADS * LANES, F32)], [(HEADS * VH, D)])


def _shift_rows(a, k):
    return a if k == 0 else pltpu.roll(a, k % a.shape[0], 0)


def _stack_rows(rows):
    idx = lax.broadcasted_iota(jnp.int32, (8, rows[0].shape[1]), 0)
    out = jnp.zeros((8, rows[0].shape[1]), F32)
    for k, r in enumerate(rows):
        out = jnp.where(idx == k, r, out)
    return out


def _ffn_fwd(h, g, wup, cw, cb, wdown, tm):
    s_len = h.shape[0]
    nj = DFF // TC
    hb = tm // HALO

    def body(h_ref, hp_ref, g_ref, wv_ref, wg_ref, cwv_ref, cwg_ref, cbv_ref, cbg_ref, wd_ref, out_ref, act_ref,
             upv_ref, upg_ref, cv_ref, cg_ref, hn_sc, acc_sc):
        i, j = pl.program_id(0), pl.program_id(1)

        @pl.when(j == 0)
        def _():
            gg = g_ref[...]
            hp = _rms_fwd(hp_ref[...], gg)[0]
            hn_sc[:HALO, :] = jnp.where(i > 0, hp, 0.0).astype(MXU)
            hn_sc[HALO:, :] = _rms_fwd(h_ref[...], gg)[0].astype(MXU)
            acc_sc[...] = jnp.zeros(acc_sc.shape, F32)

        hn = hn_sc[...]

        def conv(w_ref, cw_ref, cb_ref, up_ref):
            up = jnp.dot(hn, w_ref[...], preferred_element_type=F32)
            up_ref[...] = up[HALO:].astype(up_ref.dtype)
            cwv = cw_ref[...]
            c = cwv[2:3] * up + cwv[1:2] * _shift_rows(up, 1) + cwv[0:1] * _shift_rows(up, 2)
            return c[HALO:] + cb_ref[...]

        cv = conv(wv_ref, cwv_ref, cbv_ref, upv_ref)
        cg = conv(wg_ref, cwg_ref, cbg_ref, upg_ref)
        cv_ref[...] = cv.astype(cv_ref.dtype)
        cg_ref[...] = cg.astype(cg_ref.dtype)
        act = cg * jax.nn.sigmoid(cg) * cv
        act_ref[...] = act.astype(act_ref.dtype)
        acc_sc[...] += _mm(act, wd_ref[...])

        @pl.when(j == nj - 1)
        def _():
            out_ref[...] = h_ref[...] + acc_sc[...]

    in_specs = [pl.BlockSpec((tm, D), lambda i, j: (i, 0)),
                pl.BlockSpec((HALO, D), lambda i, j: (jnp.maximum(i * hb - 1, 0), 0)),
                pl.BlockSpec((1, D), lambda i, j: (0, 0)),
                pl.BlockSpec((D, TC), lambda i, j: (0, j)), pl.BlockSpec((D, TC), lambda i, j: (0, j + nj)),
                pl.BlockSpec((3, TC), lambda i, j: (0, j)), pl.BlockSpec((3, TC), lambda i, j: (0, j + nj)),
                pl.BlockSpec((1, TC), lambda i, j: (0, j)), pl.BlockSpec((1, TC), lambda i, j: (0, j + nj)),
                pl.BlockSpec((TC, D), lambda i, j: (j, 0))]
    out_specs = [pl.BlockSpec((tm, D), lambda i, j: (i, 0))] + [pl.BlockSpec((tm, TC), lambda i, j: (i, j))] * 5
    return pl.pallas_call(body, grid=(s_len // tm, nj), in_specs=in_specs, out_specs=out_specs,
                          out_shape=[jax.ShapeDtypeStruct((s_len, D), F32)] + [jax.ShapeDtypeStruct((s_len, DFF), MXU)] * 5,
                          scratch_shapes=[pltpu.VMEM((tm + HALO, D), MXU), pltpu.VMEM((tm, D), F32)],
                          compiler_params=_cparams(2), name="ffn_fwd")(h, h, g, wup, wup, cw, cw, cb, cb, wdown)


def _ffn_bwd(h, dout, upv, upg, cv, cg, g, cw, wdown_t, wup_t, tm):
    s_len = h.shape[0]
    ni = s_len // tm
    hb = tm // HALO

    def body(h_ref, d_ref, dx_ref, uv_ref, ug_ref, cv_ref, cvx_ref, cg_ref, cgx_ref, g_ref, cwv_ref, cwg_ref,
             wdt_ref, wutv_ref, wutg_ref, din_ref, dupv_ref, dupg_ref, hn_ref, cacc_ref, dg_ref):
        i = pl.program_id(0)

        @pl.when(i == 0)
        def _():
            cacc_ref[...] = jnp.zeros(cacc_ref.shape, F32)
            dg_ref[...] = jnp.zeros(dg_ref.shape, F32)

        gg = g_ref[...]
        hn, xh, r = _rms_fwd(h_ref[...], gg)
        hn_ref[...] = hn.astype(hn_ref.dtype)
        dd = jnp.concatenate([d_ref[...], jnp.where(i < ni - 1, dx_ref[...], 0.0)], axis=0).astype(MXU)
        dact = jnp.dot(dd, wdt_ref[...], preferred_element_type=F32)
        cvv = jnp.concatenate([cv_ref[...], cvx_ref[...]], axis=0).astype(F32)
        cgg = jnp.concatenate([cg_ref[...], cgx_ref[...]], axis=0).astype(F32)
        sg = jax.nn.sigmoid(cgg)
        dcv = dact * (cgg * sg)
        dcg = dact * cvv * (sg * (1.0 + cgg * (1.0 - sg)))

        def back(dc, up_ref, cw_ref, slot, dup_ref, wut_ref):
            cwx = cw_ref[...]
            d1, d2 = _shift_rows(dc, -1), _shift_rows(dc, -2)
            dup_ref[...] = (cwx[2:3] * dc + cwx[1:2] * d1 + cwx[0:1] * d2)[:tm].astype(dup_ref.dtype)
            up = up_ref[...].astype(F32)
            cacc_ref[slot] += _stack_rows([jnp.sum(d2[:tm] * up, axis=0, keepdims=True),
                                           jnp.sum(d1[:tm] * up, axis=0, keepdims=True),
                                           jnp.sum(dc[:tm] * up, axis=0, keepdims=True),
                                           jnp.sum(dc[:tm], axis=0, keepdims=True)])
            return _mm(dup_ref[...], wut_ref[...])

        dhn = back(dcv, uv_ref, cwv_ref, 0, dupv_ref, wutv_ref) + back(dcg, ug_ref, cwg_ref, 1, dupg_ref, wutg_ref)
        dx, dgp = _rms_bwd(dhn, xh, r, gg)
        din_ref[...] = d_ref[...] + dx
        dg_ref[...] += dgp

    last_blk = s_len // HALO - 1
    tile = lambda w: pl.BlockSpec((tm, w), lambda i: (i, 0))
    nxt = lambda w: pl.BlockSpec((HALO, w), lambda i: (jnp.minimum((i + 1) * hb, last_blk), 0))
    const = lambda s, j=0: pl.BlockSpec(s, lambda i: (0, j))
    in_specs = [tile(D), tile(D), nxt(D), tile(DFF), tile(DFF), tile(DFF), nxt(DFF), tile(DFF), nxt(DFF),
                const((1, D)), const((3, DFF)), const((3, DFF), 1),
                const((D, DFF)), pl.BlockSpec((DFF, D), lambda i: (0, 0)), pl.BlockSpec((DFF, D), lambda i: (1, 0))]
    out_specs = [tile(D), tile(DFF), tile(DFF), tile(D), pl.BlockSpec((2, 8, DFF), lambda i: (0, 0, 0)), const((1, D))]
    out_shape = [jax.ShapeDtypeStruct((s_len, D), F32), jax.ShapeDtypeStruct((s_len, DFF), MXU),
                 jax.ShapeDtypeStruct((s_len, DFF), MXU), jax.ShapeDtypeStruct((s_len, D), MXU),
                 jax.ShapeDtypeStruct((2, 8, DFF), F32), jax.ShapeDtypeStruct((1, D), F32)]
    return pl.pallas_call(body, grid=(ni,), in_specs=in_specs, out_specs=out_specs, out_shape=out_shape,
                          compiler_params=_cparams(1), name="ffn_bwd")(
        h, dout, dout, upv, upg, cv, cv, cg, cg, g, cw, cw, wdown_t, wup_t, wup_t)


def _matmul_tn(a, b, tn, ts, name):
    s_len, m = a.shape
    n = b.shape[1]

    def body(a_ref, b_ref, o_ref):
        @pl.when(pl.program_id(1) == 0)
        def _():
            o_ref[...] = jnp.zeros(o_ref.shape, F32)

        o_ref[...] += _mm_tn(a_ref[...], b_ref[...])

    return pl.pallas_call(body, grid=(n // tn, s_len // ts),
                          in_specs=[pl.BlockSpec((ts, m), lambda jn, k: (k, 0)), pl.BlockSpec((ts, tn), lambda jn, k: (k, jn))],
                          out_specs=pl.BlockSpec((m, tn), lambda jn, k: (0, jn)),
                          out_shape=jax.ShapeDtypeStruct((m, n), F32), compiler_params=_cparams(2), name=name)(a, b)


def _s5_coefs(lr, li, ldt):
    dt = jnp.exp(ldt)
    mag = jnp.exp(lr * dt)
    th = li * dt
    ar, ai = mag * jnp.cos(th), mag * jnp.sin(th)
    den = lr * lr + li * li
    nr = ar - 1.0
    cr = (nr * lr + ai * li) / den
    ci = (ai * lr - nr * li) / den
    return dt, mag, th, ar, ai, den, nr, cr, ci


def _s5_prep(lr, li, ldt, braw, seg):
    def body(lr_ref, li_ref, ldt_ref, b_ref, bb_ref, ap_ref):
        lr_, li_, ldt_ = lr_ref[0], li_ref[0], ldt_ref[0]
        dt, mag, th, ar, ai, den, nr, cr, ci = _s5_coefs(lr_, li_, ldt_)
        br, bi = b_ref[0, :, :SL], b_ref[0, :, SL:]
        bb_ref[0, :, :SL] = (cr * br - ci * bi).astype(bb_ref.dtype)
        bb_ref[0, :, SL:] = (cr * bi + ci * br).astype(bb_ref.dtype)
        for i in range(seg):
            m = jnp.exp((i + 1.0) * (lr_ * dt))
            ap_ref[0, i * 8:(i + 1) * 8, :SL] = jnp.broadcast_to(m * jnp.cos((i + 1.0) * th), (8, SL))
            ap_ref[0, i * 8:(i + 1) * 8, SL:] = jnp.broadcast_to(m * jnp.sin((i + 1.0) * th), (8, SL))

    vec = pl.BlockSpec((1, 1, SL), lambda k: (k, 0, 0))
    return pl.pallas_call(
        body, grid=(GB,), in_specs=[vec, vec, vec, pl.BlockSpec((1, LANES, 2 * SL), lambda k: (k, 0, 0))],
        out_specs=[pl.BlockSpec((1, LANES, 2 * SL), lambda k: (k, 0, 0)),
                   pl.BlockSpec((1, 8 * seg, 2 * SL), lambda k: (k, 0, 0))],
        out_shape=[jax.ShapeDtypeStruct((GB, LANES, 2 * SL), MXU), jax.ShapeDtypeStruct((GB, 8 * seg, 2 * SL), F32)],
        compiler_params=_cparams(1), name="s5_prep")(lr, li, ldt, braw)


def _s5_prep_bwd(lr, li, ldt, braw, dbb, da):
    def body(lr_ref, li_ref, ldt_ref, b_ref, dbb_ref, da_ref, dbraw_ref, dlr_ref, dli_ref, dldt_ref):
        lr_, li_, ldt_ = lr_ref[0], li_ref[0], ldt_ref[0]
        dt, mag, th, ar, ai, den, nr, cr, ci = _s5_coefs(lr_, li_, ldt_)
        br, bi = b_ref[0, :, :SL], b_ref[0, :, SL:]
        gbr, gbi = dbb_ref[0, :, :SL], dbb_ref[0, :, SL:]
        dbraw_ref[0, :, :SL] = cr * gbr + ci * gbi
        dbraw_ref[0, :, SL:] = cr * gbi - ci * gbr
        dcr = jnp.sum(gbr * br + gbi * bi, axis=0, keepdims=True)
        dci = jnp.sum(gbi * br - gbr * bi, axis=0, keepdims=True)
        dar = jnp.sum(da_ref[0, :, :SL], axis=0, keepdims=True)
        dai = jnp.sum(da_ref[0, :, SL:], axis=0, keepdims=True)
        g1, g2 = dcr / den, dci / den
        gden = -(dcr * cr + dci * ci) / den
        gar = dar + g1 * lr_ - g2 * li_
        gai = dai + g1 * li_ + g2 * lr_
        glr = g1 * nr + g2 * ai + 2.0 * lr_ * gden
        gli = g1 * ai - g2 * nr + 2.0 * li_ * gden
        gmag = gar * jnp.cos(th) + gai * jnp.sin(th)
        gth = gai * ar - gar * ai
        dlr_ref[0] = glr + gmag * mag * dt
        dli_ref[0] = gli + gth * dt
        dldt_ref[0] = (gmag * mag * lr_ + gth * li_) * dt

    vec = pl.BlockSpec((1, 1, SL), lambda k: (k, 0, 0))
    mat = pl.BlockSpec((1, LANES, 2 * SL), lambda k: (k, 0, 0))
    return pl.pallas_call(
        body, grid=(GB,), in_specs=[vec, vec, vec, mat, mat, pl.BlockSpec((1, 8, 2 * SL), lambda k: (k, 0, 0))],
        out_specs=[mat, vec, vec, vec],
        out_shape=[jax.ShapeDtypeStruct((GB, LANES, 2 * SL), F32)] + [jax.ShapeDtypeStruct((GB, 1, SL), F32)] * 3,
        compiler_params=_cparams(1), name="s5_prep_bwd")(lr, li, ldt, braw, dbb, da)


def _bcast_row(tile, j):
    return jnp.broadcast_to(tile[j:j + 1, :], tile.shape)


def _tile_rows(i):
    return pl.ds(pl.multiple_of(i * 8, 8), 8)


def _col(c):
    return slice(c * LANES, (c + 1) * LANES)


def _permute_rows(ref, seg):
    return jnp.concatenate([ref[pl.ds(i, 8, stride=seg), :] for i in range(seg)], axis=0)


def _unpermute_rows(src, dst, seg):
    for j in range(8):
        dst[j * seg:(j + 1) * seg, :] = src[pl.ds(j, seg, stride=8), :]


SCAN_UNROLL = 2


def _segment_scan(src, dst, ap, seg, reverse):
    half = NCOL // 2
    sign = -1.0 if reverse else 1.0
    a_r = [ap[0:8, _col(c)] for c in range(half)]
    a_i = [ap[0:8, _col(half + c)] for c in range(half)]

    def step(n, carry):
        i = seg - 1 - n if reverse else n
        rows = _tile_rows(i)
        out_r, out_i = [], []
        for c in range(half):
            xr, xi = carry[c], carry[half + c]
            nr = a_r[c] * xr - sign * a_i[c] * xi + src[rows, _col(c)]
            ni = a_r[c] * xi + sign * a_i[c] * xr + src[rows, _col(half + c)]
            dst[rows, _col(c)] = nr
            dst[rows, _col(half + c)] = ni
            out_r.append(nr)
            out_i.append(ni)
        return tuple(out_r + out_i)

    zero = jnp.zeros((8, LANES), F32)
    return lax.fori_loop(0, seg, step, (zero,) * NCOL, unroll=SCAN_UNROLL)


def _segment_entries(ends, cin, ap, seg, reverse):
    half = NCOL // 2
    sign = -1.0 if reverse else 1.0
    al_r = [ap[(seg - 1) * 8:seg * 8, _col(c)] for c in range(half)]
    al_i = [ap[(seg - 1) * 8:seg * 8, _col(half + c)] for c in range(half)]
    row = lax.broadcasted_iota(jnp.int32, (8, LANES), 0)
    ent, out = [None] * NCOL, [None] * NCOL
    for c in range(half):
        zr, zi = cin[c], cin[half + c]
        er, ei = jnp.zeros((8, LANES), F32), jnp.zeros((8, LANES), F32)
        for j in (range(7, -1, -1) if reverse else range(8)):
            er, ei = jnp.where(row == j, zr, er), jnp.where(row == j, zi, ei)
            fr, fi = _bcast_row(ends[c], j), _bcast_row(ends[half + c], j)
            zr, zi = (al_r[c] * zr - sign * al_i[c] * zi + fr, al_r[c] * zi + sign * al_i[c] * zr + fi)
        ent[c], ent[half + c] = er, ei
        out[c], out[half + c] = zr, zi
    return ent, out


def _chunk_states(u, bb_ref, ap, cin, bu_sc, x_sc, seg):
    half = NCOL // 2
    bu_sc[...] = _mm(u, bb_ref[0])
    ends = _segment_scan(bu_sc, x_sc, ap, seg, False)
    ent, out = _segment_entries(ends, cin, ap, seg, False)

    def fix(i, _):
        rows = _tile_rows(i)
        for c in range(half):
            pr, pi = ap[rows, _col(c)], ap[rows, _col(half + c)]
            x_sc[rows, _col(c)] += pr * ent[c] - pi * ent[half + c]
            x_sc[rows, _col(half + c)] += pr * ent[half + c] + pi * ent[c]
        return 0

    lax.fori_loop(0, seg, fix, 0, unroll=SCAN_UNROLL)
    return out


def _s5_scan_fwd(u, bb, apow, cblk, dskip, t_chunk):
    s_len = u.shape[0]
    nc = s_len // t_chunk
    seg = t_chunk // 8

    def body(u_ref, bb_ref, ap_ref, c_ref, d_ref, y_ref, xin_ref, xs_ref, bu_sc, x_sc, y_sc, carry_sc):
        @pl.when(pl.program_id(1) == 0)
        def _():
            carry_sc[...] = jnp.zeros(carry_sc.shape, F32)

        uu = _permute_rows(u_ref, seg)
        cin = [carry_sc[:, _col(c)] for c in range(NCOL)]
        xin_ref[0, 0] = carry_sc[...]
        out = _chunk_states(uu, bb_ref, ap_ref.at[0], cin, bu_sc, x_sc, seg)
        for c in range(NCOL):
            carry_sc[:, _col(c)] = out[c]
        xs = x_sc[...].astype(MXU)
        xs_ref[0] = xs
        y_sc[...] = _mm(xs, c_ref[0]) + d_ref[...] * uu
        _unpermute_rows(y_sc, y_ref, seg)

    in_specs = [pl.BlockSpec((t_chunk, LANES), lambda k, c: (c, k)),
                pl.BlockSpec((1, LANES, 2 * SL), lambda k, c: (k, 0, 0)),
                pl.BlockSpec((1, 8 * seg, 2 * SL), lambda k, c: (k, 0, 0)),
                pl.BlockSpec((1, 2 * SL, LANES), lambda k, c: (k, 0, 0)),
                pl.BlockSpec((1, LANES), lambda k, c: (0, k))]
    out_specs = [pl.BlockSpec((t_chunk, LANES), lambda k, c: (c, k)),
                 pl.BlockSpec((1, 1, 8, 2 * SL), lambda k, c: (k, c, 0, 0)),
                 pl.BlockSpec((1, t_chunk, 2 * SL), lambda k, c: (k, c, 0))]
    return pl.pallas_call(body, grid=(GB, nc), in_specs=in_specs, out_specs=out_specs,
                          out_shape=[jax.ShapeDtypeStruct((s_len, D), F32), jax.ShapeDtypeStruct((GB, nc, 8, 2 * SL), F32),
                                     jax.ShapeDtypeStruct((GB, s_len, 2 * SL), MXU)],
                          scratch_shapes=[pltpu.VMEM((t_chunk, 2 * SL), F32), pltpu.VMEM((t_chunk, 2 * SL), F32),
                                          pltpu.VMEM((t_chunk, LANES), F32), pltpu.VMEM((8, 2 * SL), F32)],
                          compiler_params=_cparams(2), name="s5_scan_fwd")(u, bb, apow, cblk, dskip)


def _s5_scan_bwd(u, dy, xin, xs, bb_t, apow, cblk_t, dskip, t_chunk):
    s_len = u.shape[0]
    nc = s_len // t_chunk
    seg = t_chunk // 8
    half = NCOL // 2

    def body(u_ref, dy_ref, xin_ref, xs_ref, bbt_ref, ap_ref, ct_ref, d_ref, du_ref, dbb_ref, dc_ref, da_ref, dd_ref,
             x_sc, g_sc, y_sc, carry_sc):
        @pl.when(pl.program_id(1) == 0)
        def _():
            carry_sc[...] = jnp.zeros(carry_sc.shape, F32)
            dbb_ref[...] = jnp.zeros(dbb_ref.shape, F32)
            dc_ref[...] = jnp.zeros(dc_ref.shape, F32)
            da_ref[...] = jnp.zeros(da_ref.shape, F32)
            dd_ref[...] = jnp.zeros(dd_ref.shape, F32)

        ap = ap_ref.at[0]
        uu, dyy = _permute_rows(u_ref, seg), _permute_rows(dy_ref, seg)
        cin = [xin_ref[0, 0, :, _col(c)] for c in range(NCOL)]
        x_sc[...] = xs_ref[0].astype(F32)
        g_sc[...] = _mm(dyy, ct_ref[0])
        ends = _segment_scan(g_sc, g_sc, ap, seg, True)
        lam_in = [carry_sc[:, _col(c)] for c in range(NCOL)]
        ent, out = _segment_entries(ends, lam_in, ap, seg, True)
        for c in range(NCOL):
            carry_sc[:, _col(c)] = out[c]
        row = lax.broadcasted_iota(jnp.int32, (8, LANES), 0)
        xp0 = [jnp.where(row == 0, cin[c], pltpu.roll(x_sc[(seg - 1) * 8:seg * 8, _col(c)], 1, 0)) for c in range(NCOL)]

        def fix(i, acc):
            rows, prev, tab = _tile_rows(i), _tile_rows(jnp.maximum(i - 1, 0)), _tile_rows(seg - 1 - i)
            new = list(acc)
            for c in range(half):
                pr, pi = ap[tab, _col(c)], ap[tab, _col(half + c)]
                lr_ = g_sc[rows, _col(c)] + pr * ent[c] + pi * ent[half + c]
                li_ = g_sc[rows, _col(half + c)] + pr * ent[half + c] - pi * ent[c]
                g_sc[rows, _col(c)] = lr_
                g_sc[rows, _col(half + c)] = li_
                xr = jnp.where(i == 0, xp0[c], x_sc[prev, _col(c)])
                xi = jnp.where(i == 0, xp0[half + c], x_sc[prev, _col(half + c)])
                new[c] = acc[c] + lr_ * xr + li_ * xi
                new[half + c] = acc[half + c] + li_ * xr - lr_ * xi
            return tuple(new)

        zero = jnp.zeros((8, LANES), F32)
        dacc = lax.fori_loop(0, seg, fix, (zero,) * NCOL, unroll=SCAN_UNROLL)
        for c in range(NCOL):
            da_ref[0, :, _col(c)] += dacc[c]
        lam = g_sc[...]
        y_sc[...] = _mm(lam, bbt_ref[0]) + d_ref[...] * dyy
        _unpermute_rows(y_sc, du_ref, seg)
        dbb_ref[0] += _mm_tn(uu, lam)
        dc_ref[0] += _mm_tn(xs_ref[0], dyy)
        dd_ref[0] += _stack_rows([jnp.sum(dyy * uu, axis=0, keepdims=True)])

    rev = lambda k, c: (nc - 1 - c, k)
    in_specs = [pl.BlockSpec((t_chunk, LANES), rev), pl.BlockSpec((t_chunk, LANES), rev),
                pl.BlockSpec((1, 1, 8, 2 * SL), lambda k, c: (k, nc - 1 - c, 0, 0)),
                pl.BlockSpec((1, t_chunk, 2 * SL), lambda k, c: (k, nc - 1 - c, 0)),
                pl.BlockSpec((1, 2 * SL, LANES), lambda k, c: (k, 0, 0)),
                pl.BlockSpec((1, 8 * seg, 2 * SL), lambda k, c: (k, 0, 0)),
                pl.BlockSpec((1, LANES, 2 * SL), lambda k, c: (k, 0, 0)),
                pl.BlockSpec((1, LANES), lambda k, c: (0, k))]
    out_specs = [pl.BlockSpec((t_chunk, LANES), rev),
                 pl.BlockSpec((1, LANES, 2 * SL), lambda k, c: (k, 0, 0)),
                 pl.BlockSpec((1, 2 * SL, LANES), lambda k, c: (k, 0, 0)),
                 pl.BlockSpec((1, 8, 2 * SL), lambda k, c: (k, 0, 0)),
                 pl.BlockSpec((1, 8, LANES), lambda k, c: (k, 0, 0))]
    out_shape = [jax.ShapeDtypeStruct((s_len, D), F32), jax.ShapeDtypeStruct((GB, LANES, 2 * SL), F32),
                 jax.ShapeDtypeStruct((GB, 2 * SL, LANES), F32), jax.ShapeDtypeStruct((GB, 8, 2 * SL), F32),
                 jax.ShapeDtypeStruct((GB, 8, LANES), F32)]
    return pl.pallas_call(body, grid=(GB, nc), in_specs=in_specs, out_specs=out_specs, out_shape=out_shape,
                          scratch_shapes=[pltpu.VMEM((t_chunk, 2 * SL), F32)] * 2 + [pltpu.VMEM((t_chunk, LANES), F32),
                                                                                        pltpu.VMEM((8, 2 * SL), F32)],
                          compiler_params=_cparams(2), name="s5_scan_bwd")(u, dy, xin, xs, bb_t, apow, cblk_t, dskip)


_GELU_K = math.sqrt(2.0 / math.pi)


def _gelu(y):
    t = jnp.tanh(_GELU_K * (y + 0.044715 * (y * y * y)))
    return 0.5 * y * (1.0 + t), 0.5 * (1.0 + t) + 0.5 * y * (1.0 - t * t) * (_GELU_K * (1.0 + 3 * 0.044715 * (y * y)))


def _s5_in_fwd(h, gmix, win, tm):
    def fn(h, gmix, win):
        return (_mm(_rms_fwd(h, gmix)[0], win),), ()

    return _rowcall("s5_in_fwd", fn, tm, [h], [gmix, win], [(D, F32)], [])[0][0]


def _s5_in_bwd(h, du, dh, gmix, win_t, tm):
    def fn(h, du, dh, gmix, win_t):
        hn, xh, r = _rms_fwd(h, gmix)
        dx, dg = _rms_bwd(_mm(du, win_t), xh, r, gmix)
        return (dh + dx,), (_mm_tn(hn, du), dg)

    return _rowcall("s5_in_bwd", fn, tm, [h, du, dh], [gmix, win_t], [(D, F32)], [(D, D), (1, D)])


def _s5_out_fwd(h, y, wglu, tm):
    def fn(h, y, wglu):
        z = _mm(_gelu(y)[0], wglu)
        return (h + z[:, :D] * jax.nn.sigmoid(z[:, D:]), z), ()

    return _rowcall("s5_out_fwd", fn, tm, [h, y], [wglu], [(D, F32), (2 * D, MXU)], [])[0]


def _s5_out_bwd(dh, y, z, wglu_t, tm):
    def fn(dh, y, z, wglu_t):
        yg, dgelu = _gelu(y)
        z = z.astype(F32)
        val, sg = z[:, :D], jax.nn.sigmoid(z[:, D:])
        dz = jnp.concatenate([dh * sg, dh * val * sg * (1.0 - sg)], axis=1)
        return (_mm(dz, wglu_t) * dgelu,), (_mm_tn(yg, dz),)

    return _rowcall("s5_out_bwd", fn, tm, [dh, y, z], [wglu_t], [(D, F32)], [(D, 2 * D)])


def _final_loss(h, tgt, gfin, tm):
    def fn(h, tgt, gfin):
        y, xh, r = _rms_fwd(h, gfin)
        err = y - tgt
        dx, dg = _rms_bwd(err * (1.0 / D), xh, r, gfin)
        return (dx,), (jnp.sum(err * err, axis=0, keepdims=True), dg)

    return _rowcall("final_loss", fn, tm, [h, tgt], [gfin], [(D, F32)], [(1, D), (1, D)])


def _bf(a):
    return a.astype(MXU)


def _s5_block_mats(b_re, b_im, c_re, c_im):
    gl = G // GB
    eye = jnp.eye(gl, dtype=F32)

    def b_blk(b):
        bt = b.reshape(GB, gl, P, C).transpose(0, 1, 3, 2)
        return (bt[:, :, :, None, :] * eye[None, :, None, :, None]).reshape(GB, gl * C, gl * P)

    def c_blk(cm):
        ct = cm.reshape(GB, gl, C, P).transpose(0, 1, 3, 2)
        return (ct[:, :, :, None, :] * eye[None, :, None, :, None]).reshape(GB, gl * P, gl * C)

    braw = jnp.concatenate([b_blk(b_re), b_blk(b_im)], axis=2)
    cblk = jnp.concatenate([c_blk(c_re), -c_blk(c_im)], axis=1)
    return braw, cblk


def _s5_unblock_b(d):
    gl = G // GB
    eye = jnp.eye(gl, dtype=F32)
    d5 = d.reshape(GB, gl, C, gl, P)
    return jnp.sum(d5 * eye[None, :, None, :, None], axis=3).transpose(0, 1, 3, 2).reshape(G, P, C)


def _s5_unblock_c(d):
    gl = G // GB
    eye = jnp.eye(gl, dtype=F32)
    d5 = d.reshape(GB, gl, P, gl, C)
    return jnp.sum(d5 * eye[None, :, None, :, None], axis=3).transpose(0, 1, 3, 2).reshape(G, C, P)


def _tiles(s_len):
    return (min(512, s_len), min(512, s_len), min(2048, s_len),
            (min(1024, s_len), min(4096, s_len)), (min(1024, s_len), min(2048, s_len)),
            min(256, s_len), min(256, s_len))


def _merge_shards(g, axis):
    moved = jnp.moveaxis(g, 0, axis)
    return moved.reshape(moved.shape[:axis] + (-1,) + moved.shape[axis + 2:])


def _split_shards(full, axis):
    cut = full.reshape(full.shape[:axis] + (NDEV, -1) + full.shape[axis + 1:])
    return jnp.moveaxis(cut, axis, 0)


def _sequence_step(x, pos, tgt, w, late=(), late_small=()):
    s_len = x.shape[0]
    tm_w, tm, t_chunk, t_fwd, t_bwd, tm_ff, tm_fb = _tiles(s_len)
    seg = t_chunk // 8
    row = lambda v: v.reshape(1, -1)

    wa = _bf(jnp.pad(w["mla_w_a"][0], ((0, 0), (0, AW - (QL + KVL + ROPE)))))
    wuq = _bf(jnp.pad(w["mla_w_uq"][0].reshape(QL, HEADS, NOPE + ROPE), ((0, 0), (0, 0), (0, HC - NOPE - ROPE))).reshape(QL, HEADS * HC))
    wukv = _bf(w["mla_w_ukv"][0])
    inv = 1.0 / (ROPE_THETA ** (jnp.arange(0, ROPE, 2, dtype=F32) / ROPE))
    invf = jnp.concatenate([inv, inv, jnp.zeros((LANES - ROPE,), F32)]).reshape(1, LANES)
    gmix0, gmix1 = row(w["g_mix"][0]), row(w["g_mix"][1])
    gq, gkv = row(w["mla_g_q"][0]), row(w["mla_g_kv"][0])
    lr = w["ssm_lambda_re"][0].reshape(GB, 1, SL)
    li = w["ssm_lambda_im"][0].reshape(GB, 1, SL)
    ldt = jnp.broadcast_to(w["ssm_log_dt"][0][:, None], (G, P)).reshape(GB, 1, SL)
    braw, cblk = _s5_block_mats(w["ssm_b_re"][0], w["ssm_b_im"][0], w["ssm_c_re"][0], w["ssm_c_im"][0])
    cblk = _bf(cblk)

    qc, kc, v = _mla_front_fwd(x, pos, gmix0, wa, gq, gkv, wuq, wukv, invf, tm_w)
    o, lse, gathered = _flash_fwd(qc, kc, v, *t_fwd, [shard for _, _, shard in late])
    w = dict(w)
    for (n, axis, _), g in zip(late, gathered):
        w[n] = _merge_shards(g, axis)
    wo = _bf(w["mla_w_o"][0])
    win, wglu = _bf(w["ssm_w_in"][0]), _bf(w["ssm_w_glu"][0])
    dskip = row(w["ssm_d"][0])
    ffn = []
    for l in range(2):
        wup, wdown = _bf(w["ffn_w_up"][l]), _bf(w["ffn_w_down"][l])
        ffn.append(dict(g=row(w["g_ffn"][l]), wup=wup, wup_t=wup.T, wdown=wdown, wdown_t=wdown.T,
                        cw=w["ffn_conv_w"][l], cb=row(w["ffn_conv_b"][l])))
    h1 = _attn_out_fwd(x, o, wo, tm)
    f0 = ffn[0]
    h2, *saved0 = _ffn_fwd(h1, f0["g"], f0["wup"], f0["cw"], f0["cb"], f0["wdown"], tm_ff)
    bb, apow = _s5_prep(lr, li, ldt, braw, seg)
    u = _s5_in_fwd(h2, gmix1, win, tm)
    y, xin, xs = _s5_scan_fwd(u, bb, apow, cblk, dskip, t_chunk)
    h3, z_glu = _s5_out_fwd(h2, y, wglu, tm)
    f1 = ffn[1]
    h4, *saved1 = _ffn_fwd(h3, f1["g"], f1["wup"], f1["cw"], f1["cb"], f1["wdown"], tm_ff)
    (dh4,), (sq, d_gfinal) = _final_loss(h4, tgt, row(w["g_final"]), tm)
    loss = 0.5 * jnp.sum(sq) / D

    grads = {}

    def ffn_back(hin, dout, saved, f):
        act, upv, upg, cv, cg = saved
        din, dupv, dupg, hn, cacc, dg = _ffn_bwd(hin, dout, upv, upg, cv, cg, f["g"], f["cw"], f["wdown_t"], f["wup_t"], tm_fb)
        d_wup = jnp.concatenate([_matmul_tn(hn, dupv, DFF, min(512, s_len), "ffn_dwup_v"),
                                 _matmul_tn(hn, dupg, DFF, min(512, s_len), "ffn_dwup_g")], axis=1)
        d_wdown = _matmul_tn(act, dout, D, min(512, s_len), "ffn_dwdown")
        cflat = cacc.transpose(1, 0, 2).reshape(8, 2 * DFF)
        return din, d_wup, d_wdown, cflat[:3], cflat[3], dg[0]

    dh3, d_wup1, d_wdown1, d_cw1, d_cb1, d_gffn1 = ffn_back(h3, dh4, saved1, f1)
    (dy,), (d_wglu,) = _s5_out_bwd(dh3, y, z_glu, wglu.T, tm)
    du, d_bb, d_cblk, d_a, d_dsk = _s5_scan_bwd(u, dy, xin, xs, bb.transpose(0, 2, 1), apow, cblk.transpose(0, 2, 1), dskip, t_chunk)
    d_braw, d_lr, d_li, d_ldt = _s5_prep_bwd(lr, li, ldt, braw, d_bb, d_a)
    (dh2,), (d_win, d_gmix1) = _s5_in_bwd(h2, du, dh3, gmix1, win.T, tm)
    dh1, d_wup0, d_wdown0, d_cw0, d_cb0, d_gffn0 = ffn_back(h1, dh2, saved0, f0)
    (do, delta), (d_wo,) = _attn_out_bwd(dh1, o, wo.T, tm)

    grads["mla_w_o"] = d_wo[None]
    grads["ssm_w_in"] = d_win[None]
    grads["ssm_lambda_re"] = d_lr.reshape(1, G, P)
    grads["ssm_lambda_im"] = d_li.reshape(1, G, P)
    grads["ssm_log_dt"] = jnp.sum(d_ldt.reshape(G, P), axis=1)[None]
    grads["ssm_b_re"] = _s5_unblock_b(d_braw[:, :, :SL])[None]
    grads["ssm_b_im"] = _s5_unblock_b(d_braw[:, :, SL:])[None]
    grads["ssm_c_re"] = _s5_unblock_c(d_cblk[:, :SL, :])[None]
    grads["ssm_c_im"] = -_s5_unblock_c(d_cblk[:, SL:, :])[None]
    grads["ssm_d"] = jnp.sum(d_dsk, axis=1).reshape(1, D)
    grads["ssm_w_glu"] = d_wglu[None]
    grads["ffn_w_up"] = jnp.stack([d_wup0, d_wup1])
    grads["ffn_conv_w"] = jnp.stack([d_cw0, d_cw1])
    grads["ffn_conv_b"] = jnp.stack([d_cb0, d_cb1])
    grads["ffn_w_down"] = jnp.stack([d_wdown0, d_wdown1])
    grads["g_ffn"] = jnp.stack([d_gffn0, d_gffn1])
    grads["g_final"] = d_gfinal[0]

    sends = [_split_shards(grads[n], axis).astype(MXU) for n, axis, _ in late]
    casts = [jnp.concatenate([grads[n].reshape(-1) for n in late_small])] if late_small else []
    dqc, dkc, dv, landed, landed_small = _flash_bwd(qc, kc, v, do, lse, delta, *t_bwd, sends, casts)
    (dx,), (d_wa, d_wuq, d_wukv, d_gq, d_gkv, d_gmix0) = _mla_front_bwd(
        x, pos, dqc, dkc, dv, dh1, gmix0, wa, gq, gkv, wuq, wukv, invf, wa.T, wuq.T, wukv.T, tm_w)
    grads["mla_w_a"] = d_wa[None, :, :QL + KVL + ROPE]
    grads["mla_g_q"] = d_gq
    grads["mla_g_kv"] = d_gkv
    grads["mla_w_uq"] = d_wuq.reshape(QL, HEADS, HC)[:, :, :NOPE + ROPE].reshape(1, QL, HEADS * (NOPE + ROPE))
    grads["mla_w_ukv"] = d_wukv[None]
    grads["g_mix"] = jnp.concatenate([d_gmix0, d_gmix1], axis=0)
    return loss, dx, grads, landed, (landed_small[0] if late_small else None)


MESH = pl.DeviceIdType.MESH
ANY = pl.BlockSpec(memory_space=pl.ANY)


def _gather_many(blocks, name):
    n = len(blocks)

    def body(*refs):
        _gather_issue(refs[:n], refs[n:2 * n], refs[2 * n:])
        _gather_finish(refs[:n], refs[n:2 * n], refs[2 * n:])

    return pl.pallas_call(body, out_shape=_gathered_shapes(blocks), in_specs=[ANY] * n, out_specs=[ANY] * n,
                          scratch_shapes=_comm_scratch(n), name=name)(*blocks)


def _comm_scratch(n):
    if n == 0:
        return []
    return [pltpu.SemaphoreType.DMA((7 * n,)), pltpu.SemaphoreType.DMA((7 * n,)), pltpu.SemaphoreType.DMA((n,))]


def _gathered_shapes(blocks):
    return [jax.ShapeDtypeStruct((NDEV,) + b.shape, b.dtype) for b in blocks]


def _gather_copies(ins, outs, sems):
    send_sems, recv_sems, local_sems = sems
    x, y, c = lax.axis_index("x"), lax.axis_index("y"), lax.axis_index("c")
    me, sibling = (x, y, c), (x, y, 1 - c)
    chips = [(1 - x, y), (x, 1 - y), (1 - x, 1 - y)]

    def copy(a, k, block, to, own=False):
        px, py, pc = block
        slot = outs[a].at[4 * px + 2 * py + pc]
        return pltpu.make_async_remote_copy(src_ref=ins[a] if own else slot, dst_ref=slot,
                                            send_sem=send_sems.at[7 * a + k], recv_sem=recv_sems.at[7 * a + k],
                                            device_id=to, device_id_type=MESH)

    mine = [pltpu.make_async_copy(ins[a], outs[a].at[4 * x + 2 * y + c], local_sems.at[a]) for a in range(len(ins))]
    first = []
    for a in range(len(ins)):
        first.append(copy(a, 0, me, sibling, own=True))
        first += [copy(a, 1 + j, me, (*chip, c), own=True) for j, chip in enumerate(chips)]
    return copy, mine, first, me, sibling, chips, c


def _gather_issue(ins, outs, sems):
    _, mine, first, *_ = _gather_copies(ins, outs, sems)
    for cp in mine + first:
        cp.start()


def _gather_finish(ins, outs, sems):
    copy, mine, first, me, sibling, chips, c = _gather_copies(ins, outs, sems)
    passed = []
    for j, chip in enumerate(chips):
        for a in range(len(ins)):
            copy(a, 1 + j, (*chip, c), me).wait_recv()
            passed.append(copy(a, 4 + j, (*chip, c), sibling))
            passed[-1].start()
    for a in range(len(ins)):
        copy(a, 0, sibling, me).wait_recv()
        for j, chip in enumerate(chips):
            copy(a, 4 + j, (*chip, 1 - c), me).wait_recv()
    for cp in first + passed:
        cp.wait_send()
    for cp in mine:
        cp.wait()


def _exchange_many(sends, casts, name):
    n, tot = len(sends), len(sends) + len(casts)

    def body(*refs):
        _exchange_issue(refs[:tot], refs[tot:2 * tot], refs[2 * tot:], n)
        _exchange_finish(refs[:tot], refs[tot:2 * tot], refs[2 * tot:], n)

    out = pl.pallas_call(body, out_shape=_exchanged_shapes(sends, casts), in_specs=[ANY] * tot, out_specs=[ANY] * tot,
                         scratch_shapes=_comm_scratch(tot), name=name)(*sends, *casts)
    return out[:n], out[n:]


def _exchanged_shapes(sends, casts):
    return [jax.ShapeDtypeStruct(s.shape, s.dtype) for s in sends] + _gathered_shapes(casts)


def _exchange_copies(ins, outs, sems, n, with_arrivals):
    send_sems, recv_sems, local_sems = sems
    x, y, c = lax.axis_index("x"), lax.axis_index("y"), lax.axis_index("c")
    me = 4 * x + 2 * y + c
    local, sent, arrivals = [], [], []
    for a in range(len(ins)):
        own = ins[a].at[me] if a < n else ins[a]
        local.append(pltpu.make_async_copy(own, outs[a].at[me], local_sems.at[a]))
        for m in range(1, NDEV):
            px = 1 - x if m & 4 else x
            py = 1 - y if m & 2 else y
            pc = 1 - c if m & 1 else c
            peer = 4 * px + 2 * py + pc
            to = dict(send_sem=send_sems.at[7 * a + m - 1], recv_sem=recv_sems.at[7 * a + m - 1],
                      device_id=(px, py, pc), device_id_type=MESH)
            sent.append(pltpu.make_async_remote_copy(src_ref=ins[a].at[peer] if a < n else ins[a], dst_ref=outs[a].at[me], **to))
            if with_arrivals:
                arrivals.append(pltpu.make_async_remote_copy(src_ref=own, dst_ref=outs[a].at[peer], **to))
    return local, sent, arrivals


def _exchange_issue(ins, outs, sems, n):
    local, sent, _ = _exchange_copies(ins, outs, sems, n, False)
    for cp in local + sent:
        cp.start()


def _exchange_finish(ins, outs, sems, n):
    local, sent, arrivals = _exchange_copies(ins, outs, sems, n, True)
    for cp in arrivals:
        cp.wait_recv()
    for cp in sent:
        cp.wait_send()
    for cp in local:
        cp.wait()


def _adam_math(w, parts, m, v):
    g = parts[0].astype(F32)
    for k in range(1, NDEV):
        g = g + parts[k].astype(F32)
    m2 = ADAM_B1 * m + (1.0 - ADAM_B1) * g
    v2 = ADAM_B2 * v + (1.0 - ADAM_B2) * jnp.square(g)
    m_hat = m2 / (1.0 - ADAM_B1 ** ADAM_STEP)
    v_hat = v2 / (1.0 - ADAM_B2 ** ADAM_STEP)
    return g, -ADAM_LR * (m_hat / (jnp.sqrt(v_hat) + ADAM_EPS) + ADAM_WD * w), m2, v2


def _adamw_many(ws, parts, ms, vs, name):
    n = len(ws)

    def body(*refs):
        w_refs, p_refs, m_refs, v_refs, outs = refs[:n], refs[n:2 * n], refs[2 * n:3 * n], refs[3 * n:4 * n], refs[4 * n:]
        for a in range(n):
            res = _adam_math(w_refs[a][...], [p_refs[a][k] for k in range(NDEV)], m_refs[a][...], v_refs[a][...])
            for o, val in zip(outs[4 * a:4 * a + 4], res):
                o[...] = val

    out = pl.pallas_call(body, out_shape=[jax.ShapeDtypeStruct(w.shape, F32) for w in ws for _ in range(4)],
                         compiler_params=pltpu.CompilerParams(vmem_limit_bytes=VMEM_LIMIT), name=name)(*ws, *parts, *ms, *vs)
    return [out[4 * a:4 * a + 4] for a in range(n)]


def _adamw_rows(w, parts, m, v, tr, name):
    rows, cols = w.shape

    def body(w_ref, p_ref, m_ref, v_ref, g_ref, d_ref, m2_ref, v2_ref):
        res = _adam_math(w_ref[...], [p_ref[k] for k in range(NDEV)], m_ref[...], v_ref[...])
        for o, val in zip((g_ref, d_ref, m2_ref, v2_ref), res):
            o[...] = val

    flat = pl.BlockSpec((tr, cols), lambda i: (i, 0))
    return pl.pallas_call(body, grid=(rows // tr,),
                          in_specs=[flat, pl.BlockSpec((NDEV, tr, cols), lambda i: (0, i, 0)), flat, flat],
                          out_specs=[flat] * 4, out_shape=[jax.ShapeDtypeStruct((rows, cols), F32)] * 4,
                          compiler_params=_cparams(1), name=name)(w, parts, m, v)


SHARDED = (("mla_w_a", 1), ("mla_w_uq", 2), ("mla_w_ukv", 2), ("mla_w_o", 1), ("ssm_w_in", 1), ("ssm_d", 1),
           ("ssm_w_glu", 2), ("ffn_w_up", 2), ("ffn_conv_w", 2), ("ffn_w_down", 1))
WIRE_EXACT = ("ssm_d", "ffn_conv_w")
GATHERED_FIRST = ("mla_w_a", "mla_w_uq", "mla_w_ukv")
REPLICATED_LAST = ("mla_g_q", "mla_g_kv", "g_mix")
REPLICATED = ("mla_g_q", "mla_g_kv", "ssm_lambda_re", "ssm_lambda_im", "ssm_log_dt", "ssm_b_re", "ssm_b_im",
              "ssm_c_re", "ssm_c_im", "ffn_conv_b", "g_mix", "g_ffn", "g_final")
WEIGHTS = ("mla_w_a", "mla_g_q", "mla_g_kv", "mla_w_uq", "mla_w_ukv", "mla_w_o", "ssm_w_in", "ssm_lambda_re",
           "ssm_lambda_im", "ssm_log_dt", "ssm_b_re", "ssm_b_im", "ssm_c_re", "ssm_c_im", "ssm_d", "ssm_w_glu",
           "ffn_w_up", "ffn_conv_w", "ffn_conv_b", "ffn_w_down", "g_mix", "g_ffn", "g_final")


ADAM_ROW_TILED = (("ffn_w_up", 256), ("ffn_w_down", 176))


def _two_d(shape):
    if len(shape) == 1:
        return (1, shape[0])
    if len(shape) > 2 and shape[-1] < LANES:
        return (math.prod(shape[:-2]), shape[-2] * shape[-1])
    return (math.prod(shape[:-1]), shape[-1])


def kernel(x, positions, mla_w_a, mla_g_q, mla_g_kv, mla_w_uq, mla_w_ukv, mla_w_o, ssm_w_in, ssm_lambda_re, ssm_lambda_im, ssm_log_dt, ssm_b_re, ssm_b_im, ssm_c_re, ssm_c_im, ssm_d, ssm_w_glu, ffn_w_up, ffn_conv_w, ffn_conv_b, ffn_w_down, g_mix, g_ffn, g_final, loss_target, m_mla_w_a, m_mla_g_q, m_mla_g_kv, m_mla_w_uq, m_mla_w_ukv, m_mla_w_o, m_ssm_w_in, m_ssm_lambda_re, m_ssm_lambda_im, m_ssm_log_dt, m_ssm_b_re, m_ssm_b_im, m_ssm_c_re, m_ssm_c_im, m_ssm_d, m_ssm_w_glu, m_ffn_w_up, m_ffn_conv_w, m_ffn_conv_b, m_ffn_w_down, m_g_mix, m_g_ffn, m_g_final, v_mla_w_a, v_mla_g_q, v_mla_g_kv, v_mla_w_uq, v_mla_w_ukv, v_mla_w_o, v_ssm_w_in, v_ssm_lambda_re, v_ssm_lambda_im, v_ssm_log_dt, v_ssm_b_re, v_ssm_b_im, v_ssm_c_re, v_ssm_c_im, v_ssm_d, v_ssm_w_glu, v_ffn_w_up, v_ffn_conv_w, v_ffn_conv_b, v_ffn_w_down, v_g_mix, v_g_ffn, v_g_final):
    a = dict(locals())
    s_len = x.shape[1]
    sh_names = [n for n, _ in SHARDED]

    def wire(n):
        return a[n] if n in WIRE_EXACT else a[n].astype(MXU)

    early = [(n, axis) for n, axis in SHARDED if n in GATHERED_FIRST]
    late = [(n, axis, wire(n)) for n, axis in SHARDED if n not in GATHERED_FIRST]
    w = {n: a[n] for n in REPLICATED}
    for (n, axis), g in zip(early, _gather_many([wire(n) for n, _ in early], "gather_weights")):
        w[n] = _merge_shards(g, axis)

    rep_late = [n for n in REPLICATED if n not in REPLICATED_LAST]
    loss, dx, grads, landed_late, rep_late_all = _sequence_step(
        x[0], positions.reshape(s_len, 1).astype(F32), loss_target[0], w, late, rep_late)
    loss = lax.psum(loss, ("x", "y", "c"))

    packed = jnp.concatenate([grads[n].reshape(-1) for n in REPLICATED_LAST])
    landed_early, (rep_last_all,) = _exchange_many([_split_shards(grads[n], axis).astype(MXU) for n, axis in early],
                                                   [packed], "exchange_grads")
    parts_of = dict(zip([n for n, _, _ in late], landed_late)) | dict(zip([n for n, _ in early], landed_early))
    for names, pack in ((rep_late, rep_late_all), (REPLICATED_LAST, rep_last_all)):
        off = 0
        for n in names:
            size = math.prod(a[n].shape)
            parts_of[n] = pack[:, off:off + size]
            off += size

    def view(n, arr, lead=()):
        return arr.reshape(lead + _two_d(a[n].shape))

    out = {}

    def finish(n, res):
        for kind, val in zip(("grad_", "delta_", "new_m_", "new_v_"), res):
            out[kind + n] = val.reshape(a[n].shape)

    for n, tr in ADAM_ROW_TILED:
        finish(n, _adamw_rows(view(n, a[n]), view(n, parts_of[n], (NDEV,)), view(n, a["m_" + n]), view(n, a["v_" + n]),
                              tr, "adamw_" + n))
    tiled = [n for n, _ in ADAM_ROW_TILED]
    for names, tag in (([n for n in sh_names if n not in tiled], "adamw_sharded"), (list(REPLICATED), "adamw_replicated")):
        res = _adamw_many([view(n, a[n]) for n in names], [view(n, parts_of[n], (NDEV,)) for n in names],
                          [view(n, a["m_" + n]) for n in names], [view(n, a["v_" + n]) for n in names], tag)
        for n, r in zip(names, res):
            finish(n, r)
    return (loss, dx[None], *[out[kind + n] for kind in ("grad_", "delta_", "new_m_", "new_v_") for n in WEIGHTS])
```

```python
import functools
import math

import jax
import jax.numpy as jnp
from jax import lax
from jax.experimental import pallas as pl
from jax.experimental.pallas import tpu as pltpu

F32 = jnp.float32
MXU = jnp.bfloat16

D = 1024
HEADS = 8
NOPE = 128
ROPE = 64
VH = 128
QL = 384
KVL = 256
CHUNK = 64
ROPE_THETA = 10000.0
EPS = 1e-6
G, P, C = 64, 64, 16
DFF = 2816
ADAM_LR, ADAM_B1, ADAM_B2, ADAM_EPS, ADAM_WD, ADAM_STEP = 0.001, 0.9, 0.999, 1e-08, 0.01, 10

LANES = 128
AW = 768
HC = 256
GB = 8
SL = (G // GB) * P
NCOL = 2 * SL // LANES
TC = DFF
HALO = 16
NDEV = 8
VMEM_LIMIT = 56 * 1024 * 1024


def _mm(a, b):
    return jnp.dot(a.astype(MXU), b.astype(MXU), preferred_element_type=F32)


def _mm_tn(a, b):
    return lax.dot_general(a.astype(MXU), b.astype(MXU), (((0,), (0,)), ((), ())), preferred_element_type=F32)


def _mm_nt(a, b):
    return lax.dot_general(a.astype(MXU), b.astype(MXU), (((1,), (1,)), ((), ())), preferred_element_type=F32)


def _rms_fwd(x, g):
    r = lax.rsqrt(jnp.mean(x * x, axis=-1, keepdims=True) + EPS)
    xh = x * r
    return xh * g, xh, r


def _rms_bwd(dy, xh, r, g):
    dxh = dy * g
    dx = r * (dxh - xh * jnp.mean(dxh * xh, axis=-1, keepdims=True))
    return dx, jnp.sum(dy * xh, axis=0, keepdims=True)


def _rot_partner(b):
    lane = lax.broadcasted_iota(jnp.int32, b.shape, 1)
    return jnp.where(lane < ROPE // 2, -pltpu.roll(b, LANES - ROPE // 2, 1), pltpu.roll(b, ROPE // 2, 1))


def _rope_blk(b, cos2, sin2):
    return b * cos2 + _rot_partner(b) * sin2


def _unrope_blk(db, cos2, sin2):
    return db * cos2 - _rot_partner(db * sin2)


def _cparams(n_axes, vmem=VMEM_LIMIT):
    return pltpu.CompilerParams(dimension_semantics=("arbitrary",) * n_axes, vmem_limit_bytes=vmem)


def _rowcall(name, fn, tm, row_ins, consts, row_outs, acc_outs):
    n = row_ins[0].shape[0]
    n_in = len(row_ins) + len(consts)
    n_ro = len(row_outs)

    def body(*refs):
        ins, ro_refs, acc_refs = refs[:n_in], refs[n_in:n_in + n_ro], refs[n_in + n_ro:]
        ro, ao = fn(*[r[...] for r in ins])

        @pl.when(pl.program_id(0) == 0)
        def _():
            for r in acc_refs:
                r[...] = jnp.zeros(r.shape, r.dtype)

        for r, val in zip(ro_refs, ro):
            r[...] = val.astype(r.dtype)
        for r, val in zip(acc_refs, ao):
            r[...] += val

    in_specs = [pl.BlockSpec((tm, a.shape[1]), lambda i: (i, 0)) for a in row_ins]
    in_specs += [pl.BlockSpec(c.shape, lambda i, nd=c.ndim: (0,) * nd) for c in consts]
    out_specs = [pl.BlockSpec((tm, w), lambda i: (i, 0)) for w, _ in row_outs]
    out_specs += [pl.BlockSpec(s, lambda i, nd=len(s): (0,) * nd) for s in acc_outs]
    out_shape = [jax.ShapeDtypeStruct((n, w), dt) for w, dt in row_outs]
    out_shape += [jax.ShapeDtypeStruct(s, F32) for s in acc_outs]
    out = pl.pallas_call(body, grid=(n // tm,), in_specs=in_specs, out_specs=out_specs, out_shape=out_shape,
                         compiler_params=_cparams(1), name=name)(*row_ins, *consts)
    return list(out[:n_ro]), list(out[n_ro:])


def _mla_front_tile(x, pos, gmix, wa, gq, gkv, wuq, wukv, invf):
    hn, xh, r = _rms_fwd(x, gmix)
    a = _mm(hn, wa)
    cq, ckv, krb = a[:, :QL], a[:, QL:QL + KVL], a[:, QL + KVL:]
    cqn, cqh, rq = _rms_fwd(cq, gq)
    ckvn, ckvh, rkv = _rms_fwd(ckv, gkv)
    q = _mm(cqn, wuq)
    kv = _mm(ckvn, wukv)
    ang = pos * invf
    cos2, sin2 = jnp.cos(ang), jnp.sin(ang)
    krr = _rope_blk(krb, cos2, sin2)
    qp, kp, vp = [], [], []
    for h in range(HEADS):
        qp += [q[:, h * HC:h * HC + NOPE], _rope_blk(q[:, h * HC + NOPE:(h + 1) * HC], cos2, sin2)]
        kp += [kv[:, h * HC:h * HC + NOPE], krr]
        vp += [kv[:, h * HC + NOPE:(h + 1) * HC]]
    res = (hn, xh, r, cqn, cqh, rq, ckvn, ckvh, rkv, cos2, sin2)
    return jnp.concatenate(qp, axis=1), jnp.concatenate(kp, axis=1), jnp.concatenate(vp, axis=1), res


def _mla_front_fwd(x, pos, gmix, wa, gq, gkv, wuq, wukv, invf, tm):
    def fn(*args):
        qc, kc, v, _ = _mla_front_tile(*args)
        return (qc * Q_PRESCALE, kc, v), ()

    return _rowcall("mla_front_fwd", fn, tm, [x, pos], [gmix, wa, gq, gkv, wuq, wukv, invf],
                    [(HEADS * HC, MXU), (HEADS * HC, MXU), (HEADS * VH, MXU)], [])[0]


def _mla_front_bwd(x, pos, dqc, dkc, dv, dh, gmix, wa, gq, gkv, wuq, wukv, invf, wa_t, wuq_t, wukv_t, tm):
    def fn(x, pos, dqc, dkc, dv, dh, gmix, wa, gq, gkv, wuq, wukv, invf, wa_t, wuq_t, wukv_t):
        _, _, _, (hn, xh, r, cqn, cqh, rq, ckvn, ckvh, rkv, cos2, sin2) = _mla_front_tile(
            x, pos, gmix, wa, gq, gkv, wuq, wukv, invf)
        dqc, dkc, dv = dqc * SM_SCALE, dkc.astype(F32), dv.astype(F32)
        dqp, dkvp = [], []
        dkr = jnp.zeros((x.shape[0], LANES), F32)
        for h in range(HEADS):
            dqp += [dqc[:, h * HC:h * HC + NOPE], _unrope_blk(dqc[:, h * HC + NOPE:(h + 1) * HC], cos2, sin2)]
            dkvp += [dkc[:, h * HC:h * HC + NOPE], dv[:, h * VH:(h + 1) * VH]]
            dkr = dkr + dkc[:, h * HC + NOPE:(h + 1) * HC]
        dq = jnp.concatenate(dqp, axis=1)
        dkv = jnp.concatenate(dkvp, axis=1)
        dkrb = _unrope_blk(dkr, cos2, sin2)
        dcqn = _mm(dq, wuq_t)
        dckvn = _mm(dkv, wukv_t)
        d_wuq = _mm_tn(cqn, dq)
        d_wukv = _mm_tn(ckvn, dkv)
        dcq, d_gq = _rms_bwd(dcqn, cqh, rq, gq)
        dckv, d_gkv = _rms_bwd(dckvn, ckvh, rkv, gkv)
        da = jnp.concatenate([dcq, dckv, dkrb], axis=1)
        d_wa = _mm_tn(hn, da)
        dhn = _mm(da, wa_t)
        dx, d_gmix = _rms_bwd(dhn, xh, r, gmix)
        return (dh + dx,), (d_wa, d_wuq, d_wukv, d_gq, d_gkv, d_gmix)

    return _rowcall("mla_front_bwd", fn, tm, [x, pos, dqc, dkc, dv, dh],
                    [gmix, wa, gq, gkv, wuq, wukv, invf, wa_t, wuq_t, wukv_t], [(D, F32)],
                    [(D, AW), (QL, HEADS * HC), (KVL, HEADS * HC), (1, QL), (1, KVL), (1, D)])


SM_SCALE = (NOPE + ROPE) ** -0.5
LOG2E = 1.0 / math.log(2.0)
Q_PRESCALE = SM_SCALE * LOG2E


def _pair_tables(s_len, tq, tk, q_major):
    pairs = [(qi, ki) for qi in range(s_len // tq) for ki in range(s_len // tk) if ki * tk < (qi + 1) * tq]
    if not q_major:
        pairs.sort(key=lambda p: (p[1], p[0]))
    return (jnp.asarray([p[0] for p in pairs], jnp.int32), jnp.asarray([p[1] for p in pairs], jnp.int32))


def _last_key_tile(qi, tq, tk):
    return ((qi + 1) * tq - 1) // tk


def _visible(qi, ki, tq, tk, width=None):
    shape = (tq, tk if width is None else width)
    row = qi * (tq // CHUNK) + lax.broadcasted_iota(jnp.int32, shape, 0) // CHUNK
    col = ki * (tk // CHUNK) + lax.broadcasted_iota(jnp.int32, shape, 1) // CHUNK
    return col <= row


def _masked_and_not(qi, ki, tq, tk, fn):
    needs_mask = (ki + 1) * tk > qi * tq
    pl.when(needs_mask)(lambda: fn(True))
    pl.when(jnp.logical_not(needs_mask))(lambda: fn(False))


def _by_visible_width(qi, ki, tq, tk, fn):
    if tk % tq or tk == tq:
        _masked_and_not(qi, ki, tq, tk, lambda masked: fn(tk, masked))
        return
    blocks = tk // tq
    seen = qi + 1 - ki * blocks
    pl.when(seen > blocks)(lambda: fn(tk, False))
    for j in range(1, blocks + 1):
        pl.when(seen == j)(functools.partial(fn, j * tq, True))


def _first_and_last_step(n_steps):
    first = jnp.logical_and(pl.program_id(0) == 0, pl.program_id(1) == 0)
    last = jnp.logical_and(pl.program_id(0) == HEADS - 1, pl.program_id(1) == n_steps - 1)
    return first, last


def _flash_fwd(qc, kc, v, tq, tk, riders):
    s_len = qc.shape[0]
    qt, kt = _pair_tables(s_len, tq, tk, True)
    nr = len(riders)

    def body(qt_ref, kt_ref, q_ref, k_ref, v_ref, *rest):
        r_in, (o_ref, lse_ref), r_out = rest[:nr], rest[nr:nr + 2], rest[nr + 2:2 * nr + 2]
        m_sc, l_sc, acc_sc = rest[2 * nr + 2:2 * nr + 5]
        sems = rest[2 * nr + 5:]
        p_id = pl.program_id(1)
        qi, ki = qt_ref[p_id], kt_ref[p_id]
        first, last = _first_and_last_step(qt.shape[0])
        if nr:
            pl.when(first)(lambda: _gather_issue(r_in, r_out, sems))

        @pl.when(ki == 0)
        def _():
            m_sc[...] = jnp.full(m_sc.shape, -jnp.inf, F32)
            l_sc[...] = jnp.zeros(l_sc.shape, F32)
            acc_sc[...] = jnp.zeros(acc_sc.shape, F32)

        def update(width, masked):
            s = _mm_nt(q_ref[...], k_ref[:width, :])
            if masked:
                s = jnp.where(_visible(qi, ki, tq, tk, width), s, -jnp.inf)
            m_old = m_sc[...]
            m_new = jnp.maximum(m_old, jnp.max(s, axis=1, keepdims=True))
            alpha = jnp.exp2(m_old - m_new)
            p = jnp.exp2(s - m_new)
            l_sc[...] = alpha * l_sc[...] + jnp.sum(p, axis=1, keepdims=True)
            acc_sc[...] = alpha * acc_sc[...] + _mm(p, v_ref[:width, :])
            m_sc[...] = m_new

        _by_visible_width(qi, ki, tq, tk, update)

        @pl.when(ki == _last_key_tile(qi, tq, tk))
        def _():
            l = l_sc[...]
            o_ref[...] = (acc_sc[...] / l).astype(o_ref.dtype)
            lse_ref[...] = jnp.broadcast_to(m_sc[...] + jnp.log2(l), lse_ref.shape)

        if nr:
            pl.when(last)(lambda: _gather_finish(r_in, r_out, sems))

    qmap = lambda h, p, qt, kt: (qt[p], h)
    kmap = lambda h, p, qt, kt: (kt[p], h)
    grid_spec = pltpu.PrefetchScalarGridSpec(
        num_scalar_prefetch=2, grid=(HEADS, qt.shape[0]),
        in_specs=[pl.BlockSpec((tq, HC), qmap), pl.BlockSpec((tk, HC), kmap), pl.BlockSpec((tk, VH), kmap)] + [ANY] * nr,
        out_specs=[pl.BlockSpec((tq, VH), qmap), pl.BlockSpec((tq, LANES), qmap)] + [ANY] * nr,
        scratch_shapes=[pltpu.VMEM((tq, 1), F32), pltpu.VMEM((tq, 1), F32), pltpu.VMEM((tq, VH), F32)] + _comm_scratch(nr))
    out = pl.pallas_call(body, grid_spec=grid_spec,
                         out_shape=[jax.ShapeDtypeStruct((s_len, HEADS * VH), MXU),
                                    jax.ShapeDtypeStruct((s_len, HEADS * LANES), F32)] + _gathered_shapes(riders),
                         compiler_params=_cparams(2), name="flash_fwd")(qt, kt, qc, kc, v, *riders)
    return out[0], out[1], out[2:]


def _tile_dscores(q, k, v, do, lse, delta, qi, ki, tq, tk, width, masked):
    p = jnp.exp2(_mm_nt(q, k) - lse[:, :1])
    if masked:
        p = jnp.where(_visible(qi, ki, tq, tk, width), p, 0.0)
    return p, p * (_mm_nt(do, v) - delta[:, :1])


def _flash_bwd(qc, kc, v, do, lse, delta, tq, tk, riders, casts=()):
    s_len = qc.shape[0]
    qt, kt = _pair_tables(s_len, tq, tk, False)
    nq = s_len // tq
    n_rows, nr = len(riders), len(riders) + len(casts)

    def body(qt_ref, kt_ref, q_ref, k_ref, v_ref, do_ref, lse_ref, dl_ref, *rest):
        r_in, (dq_ref, dk_ref, dv_ref), r_out = rest[:nr], rest[nr:nr + 3], rest[nr + 3:2 * nr + 3]
        dk_sc, dv_sc = rest[2 * nr + 3:2 * nr + 5]
        sems = rest[2 * nr + 5:]
        p_id = pl.program_id(1)
        qi, ki = qt_ref[p_id], kt_ref[p_id]
        rows = pl.ds(pl.multiple_of(qi * tq, tq), tq)
        first, last = _first_and_last_step(qt.shape[0])
        if nr:
            pl.when(first)(lambda: _exchange_issue(r_in, r_out, sems, n_rows))

        @pl.when(qi == (ki * tk) // tq)
        def _():
            dk_sc[...] = jnp.zeros(dk_sc.shape, F32)
            dv_sc[...] = jnp.zeros(dv_sc.shape, F32)

        @pl.when(ki == 0)
        def _():
            dq_ref[rows, :] = jnp.zeros((tq, HC), F32)

        def update(width, masked):
            q, k, do = q_ref[...], k_ref[:width, :], do_ref[...]
            p, ds = _tile_dscores(q, k, v_ref[:width, :], do, lse_ref[...], dl_ref[...], qi, ki, tq, tk, width, masked)
            ds = ds.astype(MXU)
            dv_sc[:width, :] += _mm_tn(p, do)
            dk_sc[:width, :] += _mm_tn(ds, q)
            dq_ref[rows, :] += _mm(ds, k)

        _by_visible_width(qi, ki, tq, tk, update)

        @pl.when(qi == nq - 1)
        def _():
            dk_ref[...] = (dk_sc[...] * (1.0 / LOG2E)).astype(dk_ref.dtype)
            dv_ref[...] = dv_sc[...].astype(dv_ref.dtype)

        if nr:
            pl.when(last)(lambda: _exchange_finish(r_in, r_out, sems, n_rows))

    qmap = lambda h, p, qt, kt: (qt[p], h)
    kmap = lambda h, p, qt, kt: (kt[p], h)
    grid_spec = pltpu.PrefetchScalarGridSpec(
        num_scalar_prefetch=2, grid=(HEADS, qt.shape[0]),
        in_specs=[pl.BlockSpec((tq, HC), qmap), pl.BlockSpec((tk, HC), kmap), pl.BlockSpec((tk, VH), kmap),
                  pl.BlockSpec((tq, VH), qmap), pl.BlockSpec((tq, LANES), qmap), pl.BlockSpec((tq, LANES), qmap)] + [ANY] * nr,
        out_specs=[pl.BlockSpec((s_len, HC), lambda h, p, qt, kt: (0, h), pipeline_mode=pl.Buffered(1)),
                   pl.BlockSpec((tk, HC), kmap), pl.BlockSpec((tk, VH), kmap)] + [ANY] * nr,
        scratch_shapes=[pltpu.VMEM((tk, HC), F32), pltpu.VMEM((tk, VH), F32)] + _comm_scratch(nr))
    out = pl.pallas_call(body, grid_spec=grid_spec,
                         out_shape=[jax.ShapeDtypeStruct((s_len, HEADS * HC), F32),
                                    jax.ShapeDtypeStruct((s_len, HEADS * HC), MXU),
                                    jax.ShapeDtypeStruct((s_len, HEADS * VH), MXU)] + _exchanged_shapes(riders, casts),
                         compiler_params=_cparams(2), name="flash_bwd")(qt, kt, qc, kc, v, do, lse, delta, *riders, *casts)
    return out[0], out[1], out[2], out[3:3 + n_rows], out[3 + n_rows:]


def _attn_out_fwd(x, o, wo, tm):
    def fn(x, o, wo):
        return (x + _mm(o, wo),), ()

    return _rowcall("attn_out_fwd", fn, tm, [x, o], [wo], [(D, F32)], [])[0][0]


def _attn_out_bwd(dh, o, wo_t, tm):
    def fn(dh, o, wo_t):
        do = _mm(dh, wo_t)
        of = o.astype(F32)
        dl = [jnp.broadcast_to(jnp.sum(do[:, h * VH:(h + 1) * VH] * of[:, h * VH:(h + 1) * VH], axis=1, keepdims=True),
                               (dh.shape[0], LANES)) for h in range(HEADS)]
        return (do, jnp.concatenate(dl, axis=1)), (_mm_tn(o, dh),)

    return _rowcall("attn_out_bwd", fn, tm, [dh, o], [wo_t], [(HEADS * VH, MXU), (HEADS * LANES, F32)], [(HEADS * VH, D)])


def _shift_rows(a, k):
    return a if k == 0 else pltpu.roll(a, k % a.shape[0], 0)


def _stack_rows(rows):
    idx = lax.broadcasted_iota(jnp.int32, (8, rows[0].shape[1]), 0)
    out = jnp.zeros((8, rows[0].shape[1]), F32)
    for k, r in enumerate(rows):
        out = jnp.where(idx == k, r, out)
    return out


def _ffn_fwd(h, g, wup, cw, cb, wdown, tm):
    s_len = h.shape[0]
    nj = DFF // TC
    hb = tm // HALO

    def body(h_ref, hp_ref, g_ref, wv_ref, wg_ref, cwv_ref, cwg_ref, cbv_ref, cbg_ref, wd_ref, out_ref, act_ref,
             upv_ref, upg_ref, cv_ref, cg_ref, hn_sc, acc_sc):
        i, j = pl.program_id(0), pl.program_id(1)

        @pl.when(j == 0)
        def _():
            gg = g_ref[...]
            hp = _rms_fwd(hp_ref[...], gg)[0]
            hn_sc[:HALO, :] = jnp.where(i > 0, hp, 0.0).astype(MXU)
            hn_sc[HALO:, :] = _rms_fwd(h_ref[...], gg)[0].astype(MXU)
            acc_sc[...] = jnp.zeros(acc_sc.shape, F32)

        hn = hn_sc[...]

        def conv(w_ref, cw_ref, cb_ref, up_ref):
            up = jnp.dot(hn, w_ref[...], preferred_element_type=F32)
            up_ref[...] = up[HALO:].astype(up_ref.dtype)
            cwv = cw_ref[...]
            c = cwv[2:3] * up + cwv[1:2] * _shift_rows(up, 1) + cwv[0:1] * _shift_rows(up, 2)
            return c[HALO:] + cb_ref[...]

        cv = conv(wv_ref, cwv_ref, cbv_ref, upv_ref)
        cg = conv(wg_ref, cwg_ref, cbg_ref, upg_ref)
        cv_ref[...] = cv.astype(cv_ref.dtype)
        cg_ref[...] = cg.astype(cg_ref.dtype)
        act = cg * jax.nn.sigmoid(cg) * cv
        act_ref[...] = act.astype(act_ref.dtype)
        acc_sc[...] += _mm(act, wd_ref[...])

        @pl.when(j == nj - 1)
        def _():
            out_ref[...] = h_ref[...] + acc_sc[...]

    in_specs = [pl.BlockSpec((tm, D), lambda i, j: (i, 0)),
                pl.BlockSpec((HALO, D), lambda i, j: (jnp.maximum(i * hb - 1, 0), 0)),
                pl.BlockSpec((1, D), lambda i, j: (0, 0)),
                pl.BlockSpec((D, TC), lambda i, j: (0, j)), pl.BlockSpec((D, TC), lambda i, j: (0, j + nj)),
                pl.BlockSpec((3, TC), lambda i, j: (0, j)), pl.BlockSpec((3, TC), lambda i, j: (0, j + nj)),
                pl.BlockSpec((1, TC), lambda i, j: (0, j)), pl.BlockSpec((1, TC), lambda i, j: (0, j + nj)),
                pl.BlockSpec((TC, D), lambda i, j: (j, 0))]
    out_specs = [pl.BlockSpec((tm, D), lambda i, j: (i, 0))] + [pl.BlockSpec((tm, TC), lambda i, j: (i, j))] * 5
    return pl.pallas_call(body, grid=(s_len // tm, nj), in_specs=in_specs, out_specs=out_specs,
                          out_shape=[jax.ShapeDtypeStruct((s_len, D), F32)] + [jax.ShapeDtypeStruct((s_len, DFF), MXU)] * 5,
                          scratch_shapes=[pltpu.VMEM((tm + HALO, D), MXU), pltpu.VMEM((tm, D), F32)],
                          compiler_params=_cparams(2), name="ffn_fwd")(h, h, g, wup, wup, cw, cw, cb, cb, wdown)


def _ffn_bwd(h, dout, upv, upg, cv, cg, g, cw, wdown_t, wup_t, tm):
    s_len = h.shape[0]
    ni = s_len // tm
    hb = tm // HALO

    def body(h_ref, d_ref, dx_ref, uv_ref, ug_ref, cv_ref, cvx_ref, cg_ref, cgx_ref, g_ref, cwv_ref, cwg_ref,
             wdt_ref, wutv_ref, wutg_ref, din_ref, dupv_ref, dupg_ref, hn_ref, cacc_ref, dg_ref):
        i = pl.program_id(0)

        @pl.when(i == 0)
        def _():
            cacc_ref[...] = jnp.zeros(cacc_ref.shape, F32)
            dg_ref[...] = jnp.zeros(dg_ref.shape, F32)

        gg = g_ref[...]
        hn, xh, r = _rms_fwd(h_ref[...], gg)
        hn_ref[...] = hn.astype(hn_ref.dtype)
        dd = jnp.concatenate([d_ref[...], jnp.where(i < ni - 1, dx_ref[...], 0.0)], axis=0).astype(MXU)
        dact = jnp.dot(dd, wdt_ref[...], preferred_element_type=F32)
        cvv = jnp.concatenate([cv_ref[...], cvx_ref[...]], axis=0).astype(F32)
        cgg = jnp.concatenate([cg_ref[...], cgx_ref[...]], axis=0).astype(F32)
        sg = jax.nn.sigmoid(cgg)
        dcv = dact * (cgg * sg)
        dcg = dact * cvv * (sg * (1.0 + cgg * (1.0 - sg)))

        def back(dc, up_ref, cw_ref, slot, dup_ref, wut_ref):
            cwx = cw_ref[...]
            d1, d2 = _shift_rows(dc, -1), _shift_rows(dc, -2)
            dup_ref[...] = (cwx[2:3] * dc + cwx[1:2] * d1 + cwx[0:1] * d2)[:tm].astype(dup_ref.dtype)
            up = up_ref[...].astype(F32)
            cacc_ref[slot] += _stack_rows([jnp.sum(d2[:tm] * up, axis=0, keepdims=True),
                                           jnp.sum(d1[:tm] * up, axis=0, keepdims=True),
                                           jnp.sum(dc[:tm] * up, axis=0, keepdims=True),
                                           jnp.sum(dc[:tm], axis=0, keepdims=True)])
            return _mm(dup_ref[...], wut_ref[...])

        dhn = back(dcv, uv_ref, cwv_ref, 0, dupv_ref, wutv_ref) + back(dcg, ug_ref, cwg_ref, 1, dupg_ref, wutg_ref)
        dx, dgp = _rms_bwd(dhn, xh, r, gg)
        din_ref[...] = d_ref[...] + dx
        dg_ref[...] += dgp

    last_blk = s_len // HALO - 1
    tile = lambda w: pl.BlockSpec((tm, w), lambda i: (i, 0))
    nxt = lambda w: pl.BlockSpec((HALO, w), lambda i: (jnp.minimum((i + 1) * hb, last_blk), 0))
    const = lambda s, j=0: pl.BlockSpec(s, lambda i: (0, j))
    in_specs = [tile(D), tile(D), nxt(D), tile(DFF), tile(DFF), tile(DFF), nxt(DFF), tile(DFF), nxt(DFF),
                const((1, D)), const((3, DFF)), const((3, DFF), 1),
                const((D, DFF)), pl.BlockSpec((DFF, D), lambda i: (0, 0)), pl.BlockSpec((DFF, D), lambda i: (1, 0))]
    out_specs = [tile(D), tile(DFF), tile(DFF), tile(D), pl.BlockSpec((2, 8, DFF), lambda i: (0, 0, 0)), const((1, D))]
    out_shape = [jax.ShapeDtypeStruct((s_len, D), F32), jax.ShapeDtypeStruct((s_len, DFF), MXU),
                 jax.ShapeDtypeStruct((s_len, DFF), MXU), jax.ShapeDtypeStruct((s_len, D), MXU),
                 jax.ShapeDtypeStruct((2, 8, DFF), F32), jax.ShapeDtypeStruct((1, D), F32)]
    return pl.pallas_call(body, grid=(ni,), in_specs=in_specs, out_specs=out_specs, out_shape=out_shape,
                          compiler_params=_cparams(1), name="ffn_bwd")(
        h, dout, dout, upv, upg, cv, cv, cg, cg, g, cw, cw, wdown_t, wup_t, wup_t)


def _matmul_tn(a, b, tn, ts, name):
    s_len, m = a.shape
    n = b.shape[1]

    def body(a_ref, b_ref, o_ref):
        @pl.when(pl.program_id(1) == 0)
        def _():
            o_ref[...] = jnp.zeros(o_ref.shape, F32)

        o_ref[...] += _mm_tn(a_ref[...], b_ref[...])

    return pl.pallas_call(body, grid=(n // tn, s_len // ts),
                          in_specs=[pl.BlockSpec((ts, m), lambda jn, k: (k, 0)), pl.BlockSpec((ts, tn), lambda jn, k: (k, jn))],
                          out_specs=pl.BlockSpec((m, tn), lambda jn, k: (0, jn)),
                          out_shape=jax.ShapeDtypeStruct((m, n), F32), compiler_params=_cparams(2), name=name)(a, b)


def _s5_coefs(lr, li, ldt):
    dt = jnp.exp(ldt)
    mag = jnp.exp(lr * dt)
    th = li * dt
    ar, ai = mag * jnp.cos(th), mag * jnp.sin(th)
    den = lr * lr + li * li
    nr = ar - 1.0
    cr = (nr * lr + ai * li) / den
    ci = (ai * lr - nr * li) / den
    return dt, mag, th, ar, ai, den, nr, cr, ci


def _s5_prep(lr, li, ldt, braw, seg):
    def body(lr_ref, li_ref, ldt_ref, b_ref, bb_ref, ap_ref):
        lr_, li_, ldt_ = lr_ref[0], li_ref[0], ldt_ref[0]
        dt, mag, th, ar, ai, den, nr, cr, ci = _s5_coefs(lr_, li_, ldt_)
        br, bi = b_ref[0, :, :SL], b_ref[0, :, SL:]
        bb_ref[0, :, :SL] = (cr * br - ci * bi).astype(bb_ref.dtype)
        bb_ref[0, :, SL:] = (cr * bi + ci * br).astype(bb_ref.dtype)
        for i in range(seg):
            m = jnp.exp((i + 1.0) * (lr_ * dt))
            ap_ref[0, i * 8:(i + 1) * 8, :SL] = jnp.broadcast_to(m * jnp.cos((i + 1.0) * th), (8, SL))
            ap_ref[0, i * 8:(i + 1) * 8, SL:] = jnp.broadcast_to(m * jnp.sin((i + 1.0) * th), (8, SL))

    vec = pl.BlockSpec((1, 1, SL), lambda k: (k, 0, 0))
    return pl.pallas_call(
        body, grid=(GB,), in_specs=[vec, vec, vec, pl.BlockSpec((1, LANES, 2 * SL), lambda k: (k, 0, 0))],
        out_specs=[pl.BlockSpec((1, LANES, 2 * SL), lambda k: (k, 0, 0)),
                   pl.BlockSpec((1, 8 * seg, 2 * SL), lambda k: (k, 0, 0))],
        out_shape=[jax.ShapeDtypeStruct((GB, LANES, 2 * SL), MXU), jax.ShapeDtypeStruct((GB, 8 * seg, 2 * SL), F32)],
        compiler_params=_cparams(1), name="s5_prep")(lr, li, ldt, braw)


def _s5_prep_bwd(lr, li, ldt, braw, dbb, da):
    def body(lr_ref, li_ref, ldt_ref, b_ref, dbb_ref, da_ref, dbraw_ref, dlr_ref, dli_ref, dldt_ref):
        lr_, li_, ldt_ = lr_ref[0], li_ref[0], ldt_ref[0]
        dt, mag, th, ar, ai, den, nr, cr, ci = _s5_coefs(lr_, li_, ldt_)
        br, bi = b_ref[0, :, :SL], b_ref[0, :, SL:]
        gbr, gbi = dbb_ref[0, :, :SL], dbb_ref[0, :, SL:]
        dbraw_ref[0, :, :SL] = cr * gbr + ci * gbi
        dbraw_ref[0, :, SL:] = cr * gbi - ci * gbr
        dcr = jnp.sum(gbr * br + gbi * bi, axis=0, keepdims=True)
        dci = jnp.sum(gbi * br - gbr * bi, axis=0, keepdims=True)
        dar = jnp.sum(da_ref[0, :, :SL], axis=0, keepdims=True)
        dai = jnp.sum(da_ref[0, :, SL:], axis=0, keepdims=True)
        g1, g2 = dcr / den, dci / den
        gden = -(dcr * cr + dci * ci) / den
        gar = dar + g1 * lr_ - g2 * li_
        gai = dai + g1 * li_ + g2 * lr_
        glr = g1 * nr + g2 * ai + 2.0 * lr_ * gden
        gli = g1 * ai - g2 * nr + 2.0 * li_ * gden
        gmag = gar * jnp.cos(th) + gai * jnp.sin(th)
        gth = gai * ar - gar * ai
        dlr_ref[0] = glr + gmag * mag * dt
        dli_ref[0] = gli + gth * dt
        dldt_ref[0] = (gmag * mag * lr_ + gth * li_) * dt

    vec = pl.BlockSpec((1, 1, SL), lambda k: (k, 0, 0))
    mat = pl.BlockSpec((1, LANES, 2 * SL), lambda k: (k, 0, 0))
    return pl.pallas_call(
        body, grid=(GB,), in_specs=[vec, vec, vec, mat, mat, pl.BlockSpec((1, 8, 2 * SL), lambda k: (k, 0, 0))],
        out_specs=[mat, vec, vec, vec],
        out_shape=[jax.ShapeDtypeStruct((GB, LANES, 2 * SL), F32)] + [jax.ShapeDtypeStruct((GB, 1, SL), F32)] * 3,
        compiler_params=_cparams(1), name="s5_prep_bwd")(lr, li, ldt, braw, dbb, da)


def _bcast_row(tile, j):
    return jnp.broadcast_to(tile[j:j + 1, :], tile.shape)


def _tile_rows(i):
    return pl.ds(pl.multiple_of(i * 8, 8), 8)


def _col(c):
    return slice(c * LANES, (c + 1) * LANES)


def _permute_rows(ref, seg):
    return jnp.concatenate([ref[pl.ds(i, 8, stride=seg), :] for i in range(seg)], axis=0)


def _unpermute_rows(src, dst, seg):
    for j in range(8):
        dst[j * seg:(j + 1) * seg, :] = src[pl.ds(j, seg, stride=8), :]


SCAN_UNROLL = 2


def _segment_scan(src, dst, ap, seg, reverse):
    half = NCOL // 2
    sign = -1.0 if reverse else 1.0
    a_r = [ap[0:8, _col(c)] for c in range(half)]
    a_i = [ap[0:8, _col(half + c)] for c in range(half)]

    def step(n, carry):
        i = seg - 1 - n if reverse else n
        rows = _tile_rows(i)
        out_r, out_i = [], []
        for c in range(half):
            xr, xi = carry[c], carry[half + c]
            nr = a_r[c] * xr - sign * a_i[c] * xi + src[rows, _col(c)]
            ni = a_r[c] * xi + sign * a_i[c] * xr + src[rows, _col(half + c)]
            dst[rows, _col(c)] = nr
            dst[rows, _col(half + c)] = ni
            out_r.append(nr)
            out_i.append(ni)
        return tuple(out_r + out_i)

    zero = jnp.zeros((8, LANES), F32)
    return lax.fori_loop(0, seg, step, (zero,) * NCOL, unroll=SCAN_UNROLL)


def _segment_entries(ends, cin, ap, seg, reverse):
    half = NCOL // 2
    sign = -1.0 if reverse else 1.0
    al_r = [ap[(seg - 1) * 8:seg * 8, _col(c)] for c in range(half)]
    al_i = [ap[(seg - 1) * 8:seg * 8, _col(half + c)] for c in range(half)]
    row = lax.broadcasted_iota(jnp.int32, (8, LANES), 0)
    ent, out = [None] * NCOL, [None] * NCOL
    for c in range(half):
        zr, zi = cin[c], cin[half + c]
        er, ei = jnp.zeros((8, LANES), F32), jnp.zeros((8, LANES), F32)
        for j in (range(7, -1, -1) if reverse else range(8)):
            er, ei = jnp.where(row == j, zr, er), jnp.where(row == j, zi, ei)
            fr, fi = _bcast_row(ends[c], j), _bcast_row(ends[half + c], j)
            zr, zi = (al_r[c] * zr - sign * al_i[c] * zi + fr, al_r[c] * zi + sign * al_i[c] * zr + fi)
        ent[c], ent[half + c] = er, ei
        out[c], out[half + c] = zr, zi
    return ent, out


def _chunk_states(u, bb_ref, ap, cin, bu_sc, x_sc, seg):
    half = NCOL // 2
    bu_sc[...] = _mm(u, bb_ref[0])
    ends = _segment_scan(bu_sc, x_sc, ap, seg, False)
    ent, out = _segment_entries(ends, cin, ap, seg, False)

    def fix(i, _):
        rows = _tile_rows(i)
        for c in range(half):
            pr, pi = ap[rows, _col(c)], ap[rows, _col(half + c)]
            x_sc[rows, _col(c)] += pr * ent[c] - pi * ent[half + c]
            x_sc[rows, _col(half + c)] += pr * ent[half + c] + pi * ent[c]
        return 0

    lax.fori_loop(0, seg, fix, 0, unroll=SCAN_UNROLL)
    return out


def _s5_scan_fwd(u, bb, apow, cblk, dskip, t_chunk):
    s_len = u.shape[0]
    nc = s_len // t_chunk
    seg = t_chunk // 8

    def body(u_ref, bb_ref, ap_ref, c_ref, d_ref, y_ref, xin_ref, xs_ref, bu_sc, x_sc, y_sc, carry_sc):
        @pl.when(pl.program_id(1) == 0)
        def _():
            carry_sc[...] = jnp.zeros(carry_sc.shape, F32)

        uu = _permute_rows(u_ref, seg)
        cin = [carry_sc[:, _col(c)] for c in range(NCOL)]
        xin_ref[0, 0] = carry_sc[...]
        out = _chunk_states(uu, bb_ref, ap_ref.at[0], cin, bu_sc, x_sc, seg)
        for c in range(NCOL):
            carry_sc[:, _col(c)] = out[c]
        xs = x_sc[...].astype(MXU)
        xs_ref[0] = xs
        y_sc[...] = _mm(xs, c_ref[0]) + d_ref[...] * uu
        _unpermute_rows(y_sc, y_ref, seg)

    in_specs = [pl.BlockSpec((t_chunk, LANES), lambda k, c: (c, k)),
                pl.BlockSpec((1, LANES, 2 * SL), lambda k, c: (k, 0, 0)),
                pl.BlockSpec((1, 8 * seg, 2 * SL), lambda k, c: (k, 0, 0)),
                pl.BlockSpec((1, 2 * SL, LANES), lambda k, c: (k, 0, 0)),
                pl.BlockSpec((1, LANES), lambda k, c: (0, k))]
    out_specs = [pl.BlockSpec((t_chunk, LANES), lambda k, c: (c, k)),
                 pl.BlockSpec((1, 1, 8, 2 * SL), lambda k, c: (k, c, 0, 0)),
                 pl.BlockSpec((1, t_chunk, 2 * SL), lambda k, c: (k, c, 0))]
    return pl.pallas_call(body, grid=(GB, nc), in_specs=in_specs, out_specs=out_specs,
                          out_shape=[jax.ShapeDtypeStruct((s_len, D), F32), jax.ShapeDtypeStruct((GB, nc, 8, 2 * SL), F32),
                                     jax.ShapeDtypeStruct((GB, s_len, 2 * SL), MXU)],
                          scratch_shapes=[pltpu.VMEM((t_chunk, 2 * SL), F32), pltpu.VMEM((t_chunk, 2 * SL), F32),
                                          pltpu.VMEM((t_chunk, LANES), F32), pltpu.VMEM((8, 2 * SL), F32)],
                          compiler_params=_cparams(2), name="s5_scan_fwd")(u, bb, apow, cblk, dskip)


def _s5_scan_bwd(u, dy, xin, xs, bb_t, apow, cblk_t, dskip, t_chunk):
    s_len = u.shape[0]
    nc = s_len // t_chunk
    seg = t_chunk // 8
    half = NCOL // 2

    def body(u_ref, dy_ref, xin_ref, xs_ref, bbt_ref, ap_ref, ct_ref, d_ref, du_ref, dbb_ref, dc_ref, da_ref, dd_ref,
             x_sc, g_sc, y_sc, carry_sc):
        @pl.when(pl.program_id(1) == 0)
        def _():
            carry_sc[...] = jnp.zeros(carry_sc.shape, F32)
            dbb_ref[...] = jnp.zeros(dbb_ref.shape, F32)
            dc_ref[...] = jnp.zeros(dc_ref.shape, F32)
            da_ref[...] = jnp.zeros(da_ref.shape, F32)
            dd_ref[...] = jnp.zeros(dd_ref.shape, F32)

        ap = ap_ref.at[0]
        uu, dyy = _permute_rows(u_ref, seg), _permute_rows(dy_ref, seg)
        cin = [xin_ref[0, 0, :, _col(c)] for c in range(NCOL)]
        x_sc[...] = xs_ref[0].astype(F32)
        g_sc[...] = _mm(dyy, ct_ref[0])
        ends = _segment_scan(g_sc, g_sc, ap, seg, True)
        lam_in = [carry_sc[:, _col(c)] for c in range(NCOL)]
        ent, out = _segment_entries(ends, lam_in, ap, seg, True)
        for c in range(NCOL):
            carry_sc[:, _col(c)] = out[c]
        row = lax.broadcasted_iota(jnp.int32, (8, LANES), 0)
        xp0 = [jnp.where(row == 0, cin[c], pltpu.roll(x_sc[(seg - 1) * 8:seg * 8, _col(c)], 1, 0)) for c in range(NCOL)]

        def fix(i, acc):
            rows, prev, tab = _tile_rows(i), _tile_rows(jnp.maximum(i - 1, 0)), _tile_rows(seg - 1 - i)
            new = list(acc)
            for c in range(half):
                pr, pi = ap[tab, _col(c)], ap[tab, _col(half + c)]
                lr_ = g_sc[rows, _col(c)] + pr * ent[c] + pi * ent[half + c]
                li_ = g_sc[rows, _col(half + c)] + pr * ent[half + c] - pi * ent[c]
                g_sc[rows, _col(c)] = lr_
                g_sc[rows, _col(half + c)] = li_
                xr = jnp.where(i == 0, xp0[c], x_sc[prev, _col(c)])
                xi = jnp.where(i == 0, xp0[half + c], x_sc[prev, _col(half + c)])
                new[c] = acc[c] + lr_ * xr + li_ * xi
                new[half + c] = acc[half + c] + li_ * xr - lr_ * xi
            return tuple(new)

        zero = jnp.zeros((8, LANES), F32)
        dacc = lax.fori_loop(0, seg, fix, (zero,) * NCOL, unroll=SCAN_UNROLL)
        for c in range(NCOL):
            da_ref[0, :, _col(c)] += dacc[c]
        lam = g_sc[...]
        y_sc[...] = _mm(lam, bbt_ref[0]) + d_ref[...] * dyy
        _unpermute_rows(y_sc, du_ref, seg)
        dbb_ref[0] += _mm_tn(uu, lam)
        dc_ref[0] += _mm_tn(xs_ref[0], dyy)
        dd_ref[0] += _stack_rows([jnp.sum(dyy * uu, axis=0, keepdims=True)])

    rev = lambda k, c: (nc - 1 - c, k)
    in_specs = [pl.BlockSpec((t_chunk, LANES), rev), pl.BlockSpec((t_chunk, LANES), rev),
                pl.BlockSpec((1, 1, 8, 2 * SL), lambda k, c: (k, nc - 1 - c, 0, 0)),
                pl.BlockSpec((1, t_chunk, 2 * SL), lambda k, c: (k, nc - 1 - c, 0)),
                pl.BlockSpec((1, 2 * SL, LANES), lambda k, c: (k, 0, 0)),
                pl.BlockSpec((1, 8 * seg, 2 * SL), lambda k, c: (k, 0, 0)),
                pl.BlockSpec((1, LANES, 2 * SL), lambda k, c: (k, 0, 0)),
                pl.BlockSpec((1, LANES), lambda k, c: (0, k))]
    out_specs = [pl.BlockSpec((t_chunk, LANES), rev),
                 pl.BlockSpec((1, LANES, 2 * SL), lambda k, c: (k, 0, 0)),
                 pl.BlockSpec((1, 2 * SL, LANES), lambda k, c: (k, 0, 0)),
                 pl.BlockSpec((1, 8, 2 * SL), lambda k, c: (k, 0, 0)),
                 pl.BlockSpec((1, 8, LANES), lambda k, c: (k, 0, 0))]
    out_shape = [jax.ShapeDtypeStruct((s_len, D), F32), jax.ShapeDtypeStruct((GB, LANES, 2 * SL), F32),
                 jax.ShapeDtypeStruct((GB, 2 * SL, LANES), F32), jax.ShapeDtypeStruct((GB, 8, 2 * SL), F32),
                 jax.ShapeDtypeStruct((GB, 8, LANES), F32)]
    return pl.pallas_call(body, grid=(GB, nc), in_specs=in_specs, out_specs=out_specs, out_shape=out_shape,
                          scratch_shapes=[pltpu.VMEM((t_chunk, 2 * SL), F32)] * 2 + [pltpu.VMEM((t_chunk, LANES), F32),
                                                                                        pltpu.VMEM((8, 2 * SL), F32)],
                          compiler_params=_cparams(2), name="s5_scan_bwd")(u, dy, xin, xs, bb_t, apow, cblk_t, dskip)


_GELU_K = math.sqrt(2.0 / math.pi)


def _gelu(y):
    t = jnp.tanh(_GELU_K * (y + 0.044715 * (y * y * y)))
    return 0.5 * y * (1.0 + t), 0.5 * (1.0 + t) + 0.5 * y * (1.0 - t * t) * (_GELU_K * (1.0 + 3 * 0.044715 * (y * y)))


def _s5_in_fwd(h, gmix, win, tm):
    def fn(h, gmix, win):
        return (_mm(_rms_fwd(h, gmix)[0], win),), ()

    return _rowcall("s5_in_fwd", fn, tm, [h], [gmix, win], [(D, F32)], [])[0][0]


def _s5_in_bwd(h, du, dh, gmix, win_t, tm):
    def fn(h, du, dh, gmix, win_t):
        hn, xh, r = _rms_fwd(h, gmix)
        dx, dg = _rms_bwd(_mm(du, win_t), xh, r, gmix)
        return (dh + dx,), (_mm_tn(hn, du), dg)

    return _rowcall("s5_in_bwd", fn, tm, [h, du, dh], [gmix, win_t], [(D, F32)], [(D, D), (1, D)])


def _s5_out_fwd(h, y, wglu, tm):
    def fn(h, y, wglu):
        z = _mm(_gelu(y)[0], wglu)
        return (h + z[:, :D] * jax.nn.sigmoid(z[:, D:]), z), ()

    return _rowcall("s5_out_fwd", fn, tm, [h, y], [wglu], [(D, F32), (2 * D, MXU)], [])[0]


def _s5_out_bwd(dh, y, z, wglu_t, tm):
    def fn(dh, y, z, wglu_t):
        yg, dgelu = _gelu(y)
        z = z.astype(F32)
        val, sg = z[:, :D], jax.nn.sigmoid(z[:, D:])
        dz = jnp.concatenate([dh * sg, dh * val * sg * (1.0 - sg)], axis=1)
        return (_mm(dz, wglu_t) * dgelu,), (_mm_tn(yg, dz),)

    return _rowcall("s5_out_bwd", fn, tm, [dh, y, z], [wglu_t], [(D, F32)], [(D, 2 * D)])


def _final_loss(h, tgt, gfin, tm):
    def fn(h, tgt, gfin):
        y, xh, r = _rms_fwd(h, gfin)
        err = y - tgt
        dx, dg = _rms_bwd(err * (1.0 / D), xh, r, gfin)
        return (dx,), (jnp.sum(err * err, axis=0, keepdims=True), dg)

    return _rowcall("final_loss", fn, tm, [h, tgt], [gfin], [(D, F32)], [(1, D), (1, D)])


def _bf(a):
    return a.astype(MXU)


def _s5_block_mats(b_re, b_im, c_re, c_im):
    gl = G // GB
    eye = jnp.eye(gl, dtype=F32)

    def b_blk(b):
        bt = b.reshape(GB, gl, P, C).transpose(0, 1, 3, 2)
        return (bt[:, :, :, None, :] * eye[None, :, None, :, None]).reshape(GB, gl * C, gl * P)

    def c_blk(cm):
        ct = cm.reshape(GB, gl, C, P).transpose(0, 1, 3, 2)
        return (ct[:, :, :, None, :] * eye[None, :, None, :, None]).reshape(GB, gl * P, gl * C)

    braw = jnp.concatenate([b_blk(b_re), b_blk(b_im)], axis=2)
    cblk = jnp.concatenate([c_blk(c_re), -c_blk(c_im)], axis=1)
    return braw, cblk


def _s5_unblock_b(d):
    gl = G // GB
    eye = jnp.eye(gl, dtype=F32)
    d5 = d.reshape(GB, gl, C, gl, P)
    return jnp.sum(d5 * eye[None, :, None, :, None], axis=3).transpose(0, 1, 3, 2).reshape(G, P, C)


def _s5_unblock_c(d):
    gl = G // GB
    eye = jnp.eye(gl, dtype=F32)
    d5 = d.reshape(GB, gl, P, gl, C)
    return jnp.sum(d5 * eye[None, :, None, :, None], axis=3).transpose(0, 1, 3, 2).reshape(G, C, P)


def _tiles(s_len):
    return (min(512, s_len), min(1024, s_len), min(2048, s_len),
            (min(1024, s_len), min(4096, s_len)), (min(1024, s_len), min(2048, s_len)),
            min(256, s_len), min(256, s_len))


def _merge_shards(g, axis):
    moved = jnp.moveaxis(g, 0, axis)
    return moved.reshape(moved.shape[:axis] + (-1,) + moved.shape[axis + 2:])


def _split_shards(full, axis):
    cut = full.reshape(full.shape[:axis] + (NDEV, -1) + full.shape[axis + 1:])
    return jnp.moveaxis(cut, axis, 0)


def _sequence_step(x, pos, tgt, w, late=(), late_small=()):
    s_len = x.shape[0]
    tm_w, tm, t_chunk, t_fwd, t_bwd, tm_ff, tm_fb = _tiles(s_len)
    seg = t_chunk // 8
    row = lambda v: v.reshape(1, -1)

    wa = _bf(jnp.pad(w["mla_w_a"][0], ((0, 0), (0, AW - (QL + KVL + ROPE)))))
    wuq = _bf(jnp.pad(w["mla_w_uq"][0].reshape(QL, HEADS, NOPE + ROPE), ((0, 0), (0, 0), (0, HC - NOPE - ROPE))).reshape(QL, HEADS * HC))
    wukv = _bf(w["mla_w_ukv"][0])
    inv = 1.0 / (ROPE_THETA ** (jnp.arange(0, ROPE, 2, dtype=F32) / ROPE))
    invf = jnp.concatenate([inv, inv, jnp.zeros((LANES - ROPE,), F32)]).reshape(1, LANES)
    gmix0, gmix1 = row(w["g_mix"][0]), row(w["g_mix"][1])
    gq, gkv = row(w["mla_g_q"][0]), row(w["mla_g_kv"][0])
    lr = w["ssm_lambda_re"][0].reshape(GB, 1, SL)
    li = w["ssm_lambda_im"][0].reshape(GB, 1, SL)
    ldt = jnp.broadcast_to(w["ssm_log_dt"][0][:, None], (G, P)).reshape(GB, 1, SL)
    braw, cblk = _s5_block_mats(w["ssm_b_re"][0], w["ssm_b_im"][0], w["ssm_c_re"][0], w["ssm_c_im"][0])
    cblk = _bf(cblk)

    qc, kc, v = _mla_front_fwd(x, pos, gmix0, wa, gq, gkv, wuq, wukv, invf, tm_w)
    o, lse, gathered = _flash_fwd(qc, kc, v, *t_fwd, [shard for _, _, shard in late])
    w = dict(w)
    for (n, axis, _), g in zip(late, gathered):
        w[n] = _merge_shards(g, axis)
    wo = _bf(w["mla_w_o"][0])
    win, wglu = _bf(w["ssm_w_in"][0]), _bf(w["ssm_w_glu"][0])
    dskip = row(w["ssm_d"][0])
    ffn = []
    for l in range(2):
        wup, wdown = _bf(w["ffn_w_up"][l]), _bf(w["ffn_w_down"][l])
        ffn.append(dict(g=row(w["g_ffn"][l]), wup=wup, wup_t=wup.T, wdown=wdown, wdown_t=wdown.T,
                        cw=w["ffn_conv_w"][l], cb=row(w["ffn_conv_b"][l])))
    h1 = _attn_out_fwd(x, o, wo, tm)
    f0 = ffn[0]
    h2, *saved0 = _ffn_fwd(h1, f0["g"], f0["wup"], f0["cw"], f0["cb"], f0["wdown"], tm_ff)
    bb, apow = _s5_prep(lr, li, ldt, braw, seg)
    u = _s5_in_fwd(h2, gmix1, win, tm)
    y, xin, xs = _s5_scan_fwd(u, bb, apow, cblk, dskip, t_chunk)
    h3, z_glu = _s5_out_fwd(h2, y, wglu, tm)
    f1 = ffn[1]
    h4, *saved1 = _ffn_fwd(h3, f1["g"], f1["wup"], f1["cw"], f1["cb"], f1["wdown"], tm_ff)
    (dh4,), (sq, d_gfinal) = _final_loss(h4, tgt, row(w["g_final"]), tm)
    loss = 0.5 * jnp.sum(sq) / D

    grads = {}

    def ffn_back(hin, dout, saved, f):
        act, upv, upg, cv, cg = saved
        din, dupv, dupg, hn, cacc, dg = _ffn_bwd(hin, dout, upv, upg, cv, cg, f["g"], f["cw"], f["wdown_t"], f["wup_t"], tm_fb)
        d_wup = jnp.concatenate([_matmul_tn(hn, dupv, DFF, min(512, s_len), "ffn_dwup_v"),
                                 _matmul_tn(hn, dupg, DFF, min(512, s_len), "ffn_dwup_g")], axis=1)
        d_wdown = _matmul_tn(act, dout, D, min(512, s_len), "ffn_dwdown")
        cflat = cacc.transpose(1, 0, 2).reshape(8, 2 * DFF)
        return din, d_wup, d_wdown, cflat[:3], cflat[3], dg[0]

    dh3, d_wup1, d_wdown1, d_cw1, d_cb1, d_gffn1 = ffn_back(h3, dh4, saved1, f1)
    (dy,), (d_wglu,) = _s5_out_bwd(dh3, y, z_glu, wglu.T, tm_w)
    du, d_bb, d_cblk, d_a, d_dsk = _s5_scan_bwd(u, dy, xin, xs, bb.transpose(0, 2, 1), apow, cblk.transpose(0, 2, 1), dskip, t_chunk)
    d_braw, d_lr, d_li, d_ldt = _s5_prep_bwd(lr, li, ldt, braw, d_bb, d_a)
    (dh2,), (d_win, d_gmix1) = _s5_in_bwd(h2, du, dh3, gmix1, win.T, tm)
    dh1, d_wup0, d_wdown0, d_cw0, d_cb0, d_gffn0 = ffn_back(h1, dh2, saved0, f0)
    (do, delta), (d_wo,) = _attn_out_bwd(dh1, o, wo.T, tm)

    grads["mla_w_o"] = d_wo[None]
    grads["ssm_w_in"] = d_win[None]
    grads["ssm_lambda_re"] = d_lr.reshape(1, G, P)
    grads["ssm_lambda_im"] = d_li.reshape(1, G, P)
    grads["ssm_log_dt"] = jnp.sum(d_ldt.reshape(G, P), axis=1)[None]
    grads["ssm_b_re"] = _s5_unblock_b(d_braw[:, :, :SL])[None]
    grads["ssm_b_im"] = _s5_unblock_b(d_braw[:, :, SL:])[None]
    grads["ssm_c_re"] = _s5_unblock_c(d_cblk[:, :SL, :])[None]
    grads["ssm_c_im"] = -_s5_unblock_c(d_cblk[:, SL:, :])[None]
    grads["ssm_d"] = jnp.sum(d_dsk, axis=1).reshape(1, D)
    grads["ssm_w_glu"] = d_wglu[None]
    grads["ffn_w_up"] = jnp.stack([d_wup0, d_wup1])
    grads["ffn_conv_w"] = jnp.stack([d_cw0, d_cw1])
    grads["ffn_conv_b"] = jnp.stack([d_cb0, d_cb1])
    grads["ffn_w_down"] = jnp.stack([d_wdown0, d_wdown1])
    grads["g_ffn"] = jnp.stack([d_gffn0, d_gffn1])
    grads["g_final"] = d_gfinal[0]

    sends = [_split_shards(grads[n], axis).astype(MXU) for n, axis, _ in late]
    casts = [jnp.concatenate([grads[n].reshape(-1) for n in late_small])] if late_small else []
    dqc, dkc, dv, landed, landed_small = _flash_bwd(qc, kc, v, do, lse, delta, *t_bwd, sends, casts)
    (dx,), (d_wa, d_wuq, d_wukv, d_gq, d_gkv, d_gmix0) = _mla_front_bwd(
        x, pos, dqc, dkc, dv, dh1, gmix0, wa, gq, gkv, wuq, wukv, invf, wa.T, wuq.T, wukv.T, tm_w)
    grads["mla_w_a"] = d_wa[None, :, :QL + KVL + ROPE]
    grads["mla_g_q"] = d_gq
    grads["mla_g_kv"] = d_gkv
    grads["mla_w_uq"] = d_wuq.reshape(QL, HEADS, HC)[:, :, :NOPE + ROPE].reshape(1, QL, HEADS * (NOPE + ROPE))
    grads["mla_w_ukv"] = d_wukv[None]
    grads["g_mix"] = jnp.concatenate([d_gmix0, d_gmix1], axis=0)
    return loss, dx, grads, landed, (landed_small[0] if late_small else None)


MESH = pl.DeviceIdType.MESH
ANY = pl.BlockSpec(memory_space=pl.ANY)


def _gather_many(blocks, name):
    n = len(blocks)

    def body(*refs):
        _gather_issue(refs[:n], refs[n:2 * n], refs[2 * n:])
        _gather_finish(refs[:n], refs[n:2 * n], refs[2 * n:])

    return pl.pallas_call(body, out_shape=_gathered_shapes(blocks), in_specs=[ANY] * n, out_specs=[ANY] * n,
                          scratch_shapes=_comm_scratch(n), name=name)(*blocks)


def _comm_scratch(n):
    if n == 0:
        return []
    return [pltpu.SemaphoreType.DMA((7 * n,)), pltpu.SemaphoreType.DMA((7 * n,)), pltpu.SemaphoreType.DMA((n,))]


def _gathered_shapes(blocks):
    return [jax.ShapeDtypeStruct((NDEV,) + b.shape, b.dtype) for b in blocks]


def _gather_copies(ins, outs, sems):
    send_sems, recv_sems, local_sems = sems
    x, y, c = lax.axis_index("x"), lax.axis_index("y"), lax.axis_index("c")
    me, sibling = (x, y, c), (x, y, 1 - c)
    chips = [(1 - x, y), (x, 1 - y), (1 - x, 1 - y)]

    def copy(a, k, block, to, own=False):
        px, py, pc = block
        slot = outs[a].at[4 * px + 2 * py + pc]
        return pltpu.make_async_remote_copy(src_ref=ins[a] if own else slot, dst_ref=slot,
                                            send_sem=send_sems.at[7 * a + k], recv_sem=recv_sems.at[7 * a + k],
                                            device_id=to, device_id_type=MESH)

    mine = [pltpu.make_async_copy(ins[a], outs[a].at[4 * x + 2 * y + c], local_sems.at[a]) for a in range(len(ins))]
    first = []
    for a in range(len(ins)):
        first.append(copy(a, 0, me, sibling, own=True))
        first += [copy(a, 1 + j, me, (*chip, c), own=True) for j, chip in enumerate(chips)]
    return copy, mine, first, me, sibling, chips, c


def _gather_issue(ins, outs, sems):
    _, mine, first, *_ = _gather_copies(ins, outs, sems)
    for cp in mine + first:
        cp.start()


def _gather_finish(ins, outs, sems):
    copy, mine, first, me, sibling, chips, c = _gather_copies(ins, outs, sems)
    passed = []
    for j, chip in enumerate(chips):
        for a in range(len(ins)):
            copy(a, 1 + j, (*chip, c), me).wait_recv()
            passed.append(copy(a, 4 + j, (*chip, c), sibling))
            passed[-1].start()
    for a in range(len(ins)):
        copy(a, 0, sibling, me).wait_recv()
        for j, chip in enumerate(chips):
            copy(a, 4 + j, (*chip, 1 - c), me).wait_recv()
    for cp in first + passed:
        cp.wait_send()
    for cp in mine:
        cp.wait()


def _exchange_many(sends, casts, name):
    n, tot = len(sends), len(sends) + len(casts)

    def body(*refs):
        _exchange_issue(refs[:tot], refs[tot:2 * tot], refs[2 * tot:], n)
        _exchange_finish(refs[:tot], refs[tot:2 * tot], refs[2 * tot:], n)

    out = pl.pallas_call(body, out_shape=_exchanged_shapes(sends, casts), in_specs=[ANY] * tot, out_specs=[ANY] * tot,
                         scratch_shapes=_comm_scratch(tot), name=name)(*sends, *casts)
    return out[:n], out[n:]


def _exchanged_shapes(sends, casts):
    return [jax.ShapeDtypeStruct(s.shape, s.dtype) for s in sends] + _gathered_shapes(casts)


def _exchange_copies(ins, outs, sems, n, with_arrivals):
    send_sems, recv_sems, local_sems = sems
    x, y, c = lax.axis_index("x"), lax.axis_index("y"), lax.axis_index("c")
    me = 4 * x + 2 * y + c
    local, sent, arrivals = [], [], []
    for a in range(len(ins)):
        own = ins[a].at[me] if a < n else ins[a]
        local.append(pltpu.make_async_copy(own, outs[a].at[me], local_sems.at[a]))
        for m in range(1, NDEV):
            px = 1 - x if m & 4 else x
            py = 1 - y if m & 2 else y
            pc = 1 - c if m & 1 else c
            peer = 4 * px + 2 * py + pc
            to = dict(send_sem=send_sems.at[7 * a + m - 1], recv_sem=recv_sems.at[7 * a + m - 1],
                      device_id=(px, py, pc), device_id_type=MESH)
            sent.append(pltpu.make_async_remote_copy(src_ref=ins[a].at[peer] if a < n else ins[a], dst_ref=outs[a].at[me], **to))
            if with_arrivals:
                arrivals.append(pltpu.make_async_remote_copy(src_ref=own, dst_ref=outs[a].at[peer], **to))
    return local, sent, arrivals


def _exchange_issue(ins, outs, sems, n):
    local, sent, _ = _exchange_copies(ins, outs, sems, n, False)
    for cp in local + sent:
        cp.start()


def _exchange_finish(ins, outs, sems, n):
    local, sent, arrivals = _exchange_copies(ins, outs, sems, n, True)
    for cp in arrivals:
        cp.wait_recv()
    for cp in sent:
        cp.wait_send()
    for cp in local:
        cp.wait()


def _adam_math(w, parts, m, v):
    g = parts[0].astype(F32)
    for k in range(1, NDEV):
        g = g + parts[k].astype(F32)
    m2 = ADAM_B1 * m + (1.0 - ADAM_B1) * g
    v2 = ADAM_B2 * v + (1.0 - ADAM_B2) * jnp.square(g)
    m_hat = m2 / (1.0 - ADAM_B1 ** ADAM_STEP)
    v_hat = v2 / (1.0 - ADAM_B2 ** ADAM_STEP)
    return g, -ADAM_LR * (m_hat / (jnp.sqrt(v_hat) + ADAM_EPS) + ADAM_WD * w), m2, v2


def _adamw_many(ws, parts, ms, vs, name):
    n = len(ws)

    def body(*refs):
        w_refs, p_refs, m_refs, v_refs, outs = refs[:n], refs[n:2 * n], refs[2 * n:3 * n], refs[3 * n:4 * n], refs[4 * n:]
        for a in range(n):
            res = _adam_math(w_refs[a][...], [p_refs[a][k] for k in range(NDEV)], m_refs[a][...], v_refs[a][...])
            for o, val in zip(outs[4 * a:4 * a + 4], res):
                o[...] = val

    out = pl.pallas_call(body, out_shape=[jax.ShapeDtypeStruct(w.shape, F32) for w in ws for _ in range(4)],
                         compiler_params=pltpu.CompilerParams(vmem_limit_bytes=VMEM_LIMIT), name=name)(*ws, *parts, *ms, *vs)
    return [out[4 * a:4 * a + 4] for a in range(n)]


def _adamw_rows(w, parts, m, v, tr, name):
    rows, cols = w.shape

    def body(w_ref, p_ref, m_ref, v_ref, g_ref, d_ref, m2_ref, v2_ref):
        res = _adam_math(w_ref[...], [p_ref[k] for k in range(NDEV)], m_ref[...], v_ref[...])
        for o, val in zip((g_ref, d_ref, m2_ref, v2_ref), res):
            o[...] = val

    flat = pl.BlockSpec((tr, cols), lambda i: (i, 0))
    return pl.pallas_call(body, grid=(rows // tr,),
                          in_specs=[flat, pl.BlockSpec((NDEV, tr, cols), lambda i: (0, i, 0)), flat, flat],
                          out_specs=[flat] * 4, out_shape=[jax.ShapeDtypeStruct((rows, cols), F32)] * 4,
                          compiler_params=_cparams(1), name=name)(w, parts, m, v)


SHARDED = (("mla_w_a", 1), ("mla_w_uq", 2), ("mla_w_ukv", 2), ("mla_w_o", 1), ("ssm_w_in", 1), ("ssm_d", 1),
           ("ssm_w_glu", 2), ("ffn_w_up", 2), ("ffn_conv_w", 2), ("ffn_w_down", 1))
WIRE_EXACT = ("ssm_d", "ffn_conv_w")
GATHERED_FIRST = ("mla_w_a", "mla_w_uq", "mla_w_ukv")
REPLICATED_LAST = ("mla_g_q", "mla_g_kv", "g_mix")
REPLICATED = ("mla_g_q", "mla_g_kv", "ssm_lambda_re", "ssm_lambda_im", "ssm_log_dt", "ssm_b_re", "ssm_b_im",
              "ssm_c_re", "ssm_c_im", "ffn_conv_b", "g_mix", "g_ffn", "g_final")
WEIGHTS = ("mla_w_a", "mla_g_q", "mla_g_kv", "mla_w_uq", "mla_w_ukv", "mla_w_o", "ssm_w_in", "ssm_lambda_re",
           "ssm_lambda_im", "ssm_log_dt", "ssm_b_re", "ssm_b_im", "ssm_c_re", "ssm_c_im", "ssm_d", "ssm_w_glu",
           "ffn_w_up", "ffn_conv_w", "ffn_conv_b", "ffn_w_down", "g_mix", "g_ffn", "g_final")


ADAM_ROW_TILED = (("ffn_w_up", 256), ("ffn_w_down", 176))


def _two_d(shape):
    if len(shape) == 1:
        return (1, shape[0])
    if len(shape) > 2 and shape[-1] < LANES:
        return (math.prod(shape[:-2]), shape[-2] * shape[-1])
    return (math.prod(shape[:-1]), shape[-1])


def kernel(x, positions, mla_w_a, mla_g_q, mla_g_kv, mla_w_uq, mla_w_ukv, mla_w_o, ssm_w_in, ssm_lambda_re, ssm_lambda_im, ssm_log_dt, ssm_b_re, ssm_b_im, ssm_c_re, ssm_c_im, ssm_d, ssm_w_glu, ffn_w_up, ffn_conv_w, ffn_conv_b, ffn_w_down, g_mix, g_ffn, g_final, loss_target, m_mla_w_a, m_mla_g_q, m_mla_g_kv, m_mla_w_uq, m_mla_w_ukv, m_mla_w_o, m_ssm_w_in, m_ssm_lambda_re, m_ssm_lambda_im, m_ssm_log_dt, m_ssm_b_re, m_ssm_b_im, m_ssm_c_re, m_ssm_c_im, m_ssm_d, m_ssm_w_glu, m_ffn_w_up, m_ffn_conv_w, m_ffn_conv_b, m_ffn_w_down, m_g_mix, m_g_ffn, m_g_final, v_mla_w_a, v_mla_g_q, v_mla_g_kv, v_mla_w_uq, v_mla_w_ukv, v_mla_w_o, v_ssm_w_in, v_ssm_lambda_re, v_ssm_lambda_im, v_ssm_log_dt, v_ssm_b_re, v_ssm_b_im, v_ssm_c_re, v_ssm_c_im, v_ssm_d, v_ssm_w_glu, v_ffn_w_up, v_ffn_conv_w, v_ffn_conv_b, v_ffn_w_down, v_g_mix, v_g_ffn, v_g_final):
    a = dict(locals())
    s_len = x.shape[1]
    sh_names = [n for n, _ in SHARDED]

    def wire(n):
        return a[n] if n in WIRE_EXACT else a[n].astype(MXU)

    early = [(n, axis) for n, axis in SHARDED if n in GATHERED_FIRST]
    late = [(n, axis, wire(n)) for n, axis in SHARDED if n not in GATHERED_FIRST]
    w = {n: a[n] for n in REPLICATED}
    for (n, axis), g in zip(early, _gather_many([wire(n) for n, _ in early], "gather_weights")):
        w[n] = _merge_shards(g, axis)

    rep_late = [n for n in REPLICATED if n not in REPLICATED_LAST]
    loss, dx, grads, landed_late, rep_late_all = _sequence_step(
        x[0], positions.reshape(s_len, 1).astype(F32), loss_target[0], w, late, rep_late)
    loss = lax.psum(loss, ("x", "y", "c"))

    packed = jnp.concatenate([grads[n].reshape(-1) for n in REPLICATED_LAST])
    landed_early, (rep_last_all,) = _exchange_many([_split_shards(grads[n], axis).astype(MXU) for n, axis in early],
                                                   [packed], "exchange_grads")
    parts_of = dict(zip([n for n, _, _ in late], landed_late)) | dict(zip([n for n, _ in early], landed_early))
    for names, pack in ((rep_late, rep_late_all), (REPLICATED_LAST, rep_last_all)):
        off = 0
        for n in names:
            size = math.prod(a[n].shape)
            parts_of[n] = pack[:, off:off + size]
            off += size

    def view(n, arr, lead=()):
        return arr.reshape(lead + _two_d(a[n].shape))

    out = {}

    def finish(n, res):
        for kind, val in zip(("grad_", "delta_", "new_m_", "new_v_"), res):
            out[kind + n] = val.reshape(a[n].shape)

    for n, tr in ADAM_ROW_TILED:
        finish(n, _adamw_rows(view(n, a[n]), view(n, parts_of[n], (NDEV,)), view(n, a["m_" + n]), view(n, a["v_" + n]),
                              tr, "adamw_" + n))
    tiled = [n for n, _ in ADAM_ROW_TILED]
    for names, tag in (([n for n in sh_names if n not in tiled], "adamw_sharded"), (list(REPLICATED), "adamw_replicated")):
        res = _adamw_many([view(n, a[n]) for n in names], [view(n, parts_of[n], (NDEV,)) for n in names],
                          [view(n, a["m_" + n]) for n in names], [view(n, a["v_" + n]) for n in names], tag)
        for n, r in zip(names, res):
            finish(n, r)
    return (loss, dx[None], *[out[kind + n] for kind in ("grad_", "delta_", "new_m_", "new_v_") for n in WEIGHTS])
```

```python
import functools
import math

import jax
import jax.numpy as jnp
from jax import lax
from jax.experimental import pallas as pl
from jax.experimental.pallas import tpu as pltpu

F32 = jnp.float32
MXU = jnp.bfloat16

D = 1024
HEADS = 8
NOPE = 128
ROPE = 64
VH = 128
QL = 384
KVL = 256
CHUNK = 64
ROPE_THETA = 10000.0
EPS = 1e-6
G, P, C = 64, 64, 16
DFF = 2816
ADAM_LR, ADAM_B1, ADAM_B2, ADAM_EPS, ADAM_WD, ADAM_STEP = 0.001, 0.9, 0.999, 1e-08, 0.01, 10

LANES = 128
AW = 768
HC = 256
GB = 8
SL = (G // GB) * P
NCOL = 2 * SL // LANES
TC = DFF
HALO = 16
NDEV = 8
VMEM_LIMIT = 56 * 1024 * 1024


def _mm(a, b):
    return jnp.dot(a.astype(MXU), b.astype(MXU), preferred_element_type=F32)


def _mm_tn(a, b):
    return lax.dot_general(a.astype(MXU), b.astype(MXU), (((0,), (0,)), ((), ())), preferred_element_type=F32)


def _mm_nt(a, b):
    return lax.dot_general(a.astype(MXU), b.astype(MXU), (((1,), (1,)), ((), ())), preferred_element_type=F32)


def _rms_fwd(x, g):
    r = lax.rsqrt(jnp.mean(x * x, axis=-1, keepdims=True) + EPS)
    xh = x * r
    return xh * g, xh, r


def _rms_bwd(dy, xh, r, g):
    dxh = dy * g
    dx = r * (dxh - xh * jnp.mean(dxh * xh, axis=-1, keepdims=True))
    return dx, jnp.sum(dy * xh, axis=0, keepdims=True)


def _rot_partner(b):
    lane = lax.broadcasted_iota(jnp.int32, b.shape, 1)
    return jnp.where(lane < ROPE // 2, -pltpu.roll(b, LANES - ROPE // 2, 1), pltpu.roll(b, ROPE // 2, 1))


def _rope_blk(b, cos2, sin2):
    return b * cos2 + _rot_partner(b) * sin2


def _unrope_blk(db, cos2, sin2):
    return db * cos2 - _rot_partner(db * sin2)


def _cparams(n_axes, vmem=VMEM_LIMIT):
    return pltpu.CompilerParams(dimension_semantics=("arbitrary",) * n_axes, vmem_limit_bytes=vmem)


def _rowcall(name, fn, tm, row_ins, consts, row_outs, acc_outs):
    n = row_ins[0].shape[0]
    n_in = len(row_ins) + len(consts)
    n_ro = len(row_outs)

    def body(*refs):
        ins, ro_refs, acc_refs = refs[:n_in], refs[n_in:n_in + n_ro], refs[n_in + n_ro:]
        ro, ao = fn(*[r[...] for r in ins])

        @pl.when(pl.program_id(0) == 0)
        def _():
            for r in acc_refs:
                r[...] = jnp.zeros(r.shape, r.dtype)

        for r, val in zip(ro_refs, ro):
            r[...] = val.astype(r.dtype)
        for r, val in zip(acc_refs, ao):
            r[...] += val

    in_specs = [pl.BlockSpec((tm, a.shape[1]), lambda i: (i, 0)) for a in row_ins]
    in_specs += [pl.BlockSpec(c.shape, lambda i, nd=c.ndim: (0,) * nd) for c in consts]
    out_specs = [pl.BlockSpec((tm, w), lambda i: (i, 0)) for w, _ in row_outs]
    out_specs += [pl.BlockSpec(s, lambda i, nd=len(s): (0,) * nd) for s in acc_outs]
    out_shape = [jax.ShapeDtypeStruct((n, w), dt) for w, dt in row_outs]
    out_shape += [jax.ShapeDtypeStruct(s, F32) for s in acc_outs]
    out = pl.pallas_call(body, grid=(n // tm,), in_specs=in_specs, out_specs=out_specs, out_shape=out_shape,
                         compiler_params=_cparams(1), name=name)(*row_ins, *consts)
    return list(out[:n_ro]), list(out[n_ro:])


def _mla_front_tile(x, pos, gmix, wa, gq, gkv, wuq, wukv, invf):
    hn, xh, r = _rms_fwd(x, gmix)
    a = _mm(hn, wa)
    cq, ckv, krb = a[:, :QL], a[:, QL:QL + KVL], a[:, QL + KVL:]
    cqn, cqh, rq = _rms_fwd(cq, gq)
    ckvn, ckvh, rkv = _rms_fwd(ckv, gkv)
    q = _mm(cqn, wuq)
    kv = _mm(ckvn, wukv)
    ang = pos * invf
    cos2, sin2 = jnp.cos(ang), jnp.sin(ang)
    krr = _rope_blk(krb, cos2, sin2)
    qp, kp, vp = [], [], []
    for h in range(HEADS):
        qp += [q[:, h * HC:h * HC + NOPE], _rope_blk(q[:, h * HC + NOPE:(h + 1) * HC], cos2, sin2)]
        kp += [kv[:, h * HC:h * HC + NOPE], krr]
        vp += [kv[:, h * HC + NOPE:(h + 1) * HC]]
    res = (hn, xh, r, cqn, cqh, rq, ckvn, ckvh, rkv, cos2, sin2)
    return jnp.concatenate(qp, axis=1), jnp.concatenate(kp, axis=1), jnp.concatenate(vp, axis=1), res


def _mla_front_fwd(x, pos, gmix, wa, gq, gkv, wuq, wukv, invf, tm):
    def fn(*args):
        qc, kc, v, _ = _mla_front_tile(*args)
        return (qc * Q_PRESCALE, kc, v), ()

    return _rowcall("mla_front_fwd", fn, tm, [x, pos], [gmix, wa, gq, gkv, wuq, wukv, invf],
                    [(HEADS * HC, MXU), (HEADS * HC, MXU), (HEADS * VH, MXU)], [])[0]


def _mla_front_bwd(x, pos, dqc, dkc, dv, dh, gmix, wa, gq, gkv, wuq, wukv, invf, wa_t, wuq_t, wukv_t, tm):
    def fn(x, pos, dqc, dkc, dv, dh, gmix, wa, gq, gkv, wuq, wukv, invf, wa_t, wuq_t, wukv_t):
        _, _, _, (hn, xh, r, cqn, cqh, rq, ckvn, ckvh, rkv, cos2, sin2) = _mla_front_tile(
            x, pos, gmix, wa, gq, gkv, wuq, wukv, invf)
        dqc, dkc, dv = dqc * SM_SCALE, dkc.astype(F32), dv.astype(F32)
        dqp, dkvp = [], []
        dkr = jnp.zeros((x.shape[0], LANES), F32)
        for h in range(HEADS):
            dqp += [dqc[:, h * HC:h * HC + NOPE], _unrope_blk(dqc[:, h * HC + NOPE:(h + 1) * HC], cos2, sin2)]
            dkvp += [dkc[:, h * HC:h * HC + NOPE], dv[:, h * VH:(h + 1) * VH]]
            dkr = dkr + dkc[:, h * HC + NOPE:(h + 1) * HC]
        dq = jnp.concatenate(dqp, axis=1)
        dkv = jnp.concatenate(dkvp, axis=1)
        dkrb = _unrope_blk(dkr, cos2, sin2)
        dcqn = _mm(dq, wuq_t)
        dckvn = _mm(dkv, wukv_t)
        d_wuq = _mm_tn(cqn, dq)
        d_wukv = _mm_tn(ckvn, dkv)
        dcq, d_gq = _rms_bwd(dcqn, cqh, rq, gq)
        dckv, d_gkv = _rms_bwd(dckvn, ckvh, rkv, gkv)
        da = jnp.concatenate([dcq, dckv, dkrb], axis=1)
        d_wa = _mm_tn(hn, da)
        dhn = _mm(da, wa_t)
        dx, d_gmix = _rms_bwd(dhn, xh, r, gmix)
        return (dh + dx,), (d_wa, d_wuq, d_wukv, d_gq, d_gkv, d_gmix)

    return _rowcall("mla_front_bwd", fn, tm, [x, pos, dqc, dkc, dv, dh],
                    [gmix, wa, gq, gkv, wuq, wukv, invf, wa_t, wuq_t, wukv_t], [(D, F32)],
                    [(D, AW), (QL, HEADS * HC), (KVL, HEADS * HC), (1, QL), (1, KVL), (1, D)])


SM_SCALE = (NOPE + ROPE) ** -0.5
LOG2E = 1.0 / math.log(2.0)
Q_PRESCALE = SM_SCALE * LOG2E


def _pair_tables(s_len, tq, tk, q_major):
    pairs = [(qi, ki) for qi in range(s_len // tq) for ki in range(s_len // tk) if ki * tk < (qi + 1) * tq]
    if not q_major:
        pairs.sort(key=lambda p: (p[1], p[0]))
    return (jnp.asarray([p[0] for p in pairs], jnp.int32), jnp.asarray([p[1] for p in pairs], jnp.int32))


def _last_key_tile(qi, tq, tk):
    return ((qi + 1) * tq - 1) // tk


def _visible(qi, ki, tq, tk, width=None):
    shape = (tq, tk if width is None else width)
    row = qi * (tq // CHUNK) + lax.broadcasted_iota(jnp.int32, shape, 0) // CHUNK
    col = ki * (tk // CHUNK) + lax.broadcasted_iota(jnp.int32, shape, 1) // CHUNK
    return col <= row


def _masked_and_not(qi, ki, tq, tk, fn):
    needs_mask = (ki + 1) * tk > qi * tq
    pl.when(needs_mask)(lambda: fn(True))
    pl.when(jnp.logical_not(needs_mask))(lambda: fn(False))


def _by_visible_width(qi, ki, tq, tk, fn):
    if tk % tq or tk == tq:
        _masked_and_not(qi, ki, tq, tk, lambda masked: fn(tk, masked))
        return
    blocks = tk // tq
    seen = qi + 1 - ki * blocks
    pl.when(seen > blocks)(lambda: fn(tk, False))
    for j in range(1, blocks + 1):
        pl.when(seen == j)(functools.partial(fn, j * tq, True))


def _first_and_last_step(n_steps):
    first = jnp.logical_and(pl.program_id(0) == 0, pl.program_id(1) == 0)
    last = jnp.logical_and(pl.program_id(0) == HEADS - 1, pl.program_id(1) == n_steps - 1)
    return first, last


def _flash_fwd(qc, kc, v, tq, tk, riders):
    s_len = qc.shape[0]
    qt, kt = _pair_tables(s_len, tq, tk, True)
    nr = len(riders)

    def body(qt_ref, kt_ref, q_ref, k_ref, v_ref, *rest):
        r_in, (o_ref, lse_ref), r_out = rest[:nr], rest[nr:nr + 2], rest[nr + 2:2 * nr + 2]
        m_sc, l_sc, acc_sc = rest[2 * nr + 2:2 * nr + 5]
        sems = rest[2 * nr + 5:]
        p_id = pl.program_id(1)
        qi, ki = qt_ref[p_id], kt_ref[p_id]
        first, last = _first_and_last_step(qt.shape[0])
        if nr:
            pl.when(first)(lambda: _gather_issue(r_in, r_out, sems))

        @pl.when(ki == 0)
        def _():
            m_sc[...] = jnp.full(m_sc.shape, -jnp.inf, F32)
            l_sc[...] = jnp.zeros(l_sc.shape, F32)
            acc_sc[...] = jnp.zeros(acc_sc.shape, F32)

        def update(width, masked):
            s = _mm_nt(q_ref[...], k_ref[:width, :])
            if masked:
                s = jnp.where(_visible(qi, ki, tq, tk, width), s, -jnp.inf)
            m_old = m_sc[...]
            m_new = jnp.maximum(m_old, jnp.max(s, axis=1, keepdims=True))
            alpha = jnp.exp2(m_old - m_new)
            p = jnp.exp2(s - m_new)
            l_sc[...] = alpha * l_sc[...] + jnp.sum(p, axis=1, keepdims=True)
            acc_sc[...] = alpha * acc_sc[...] + _mm(p, v_ref[:width, :])
            m_sc[...] = m_new

        _by_visible_width(qi, ki, tq, tk, update)

        @pl.when(ki == _last_key_tile(qi, tq, tk))
        def _():
            l = l_sc[...]
            o_ref[...] = (acc_sc[...] / l).astype(o_ref.dtype)
            lse_ref[...] = jnp.broadcast_to(m_sc[...] + jnp.log2(l), lse_ref.shape)

        if nr:
            pl.when(last)(lambda: _gather_finish(r_in, r_out, sems))

    qmap = lambda h, p, qt, kt: (qt[p], h)
    kmap = lambda h, p, qt, kt: (kt[p], h)
    grid_spec = pltpu.PrefetchScalarGridSpec(
        num_scalar_prefetch=2, grid=(HEADS, qt.shape[0]),
        in_specs=[pl.BlockSpec((tq, HC), qmap), pl.BlockSpec((tk, HC), kmap), pl.BlockSpec((tk, VH), kmap)] + [ANY] * nr,
        out_specs=[pl.BlockSpec((tq, VH), qmap), pl.BlockSpec((tq, LANES), qmap)] + [ANY] * nr,
        scratch_shapes=[pltpu.VMEM((tq, 1), F32), pltpu.VMEM((tq, 1), F32), pltpu.VMEM((tq, VH), F32)] + _comm_scratch(nr))
    out = pl.pallas_call(body, grid_spec=grid_spec,
                         out_shape=[jax.ShapeDtypeStruct((s_len, HEADS * VH), MXU),
                                    jax.ShapeDtypeStruct((s_len, HEADS * LANES), F32)] + _gathered_shapes(riders),
                         compiler_params=_cparams(2), name="flash_fwd")(qt, kt, qc, kc, v, *riders)
    return out[0], out[1], out[2:]


def _tile_dscores(q, k, v, do, lse, delta, qi, ki, tq, tk, width, masked):
    p = jnp.exp2(_mm_nt(q, k) - lse[:, :1])
    if masked:
        p = jnp.where(_visible(qi, ki, tq, tk, width), p, 0.0)
    return p, p * (_mm_nt(do, v) - delta[:, :1])


def _flash_bwd(qc, kc, v, do, lse, delta, tq, tk, riders, casts=()):
    s_len = qc.shape[0]
    qt, kt = _pair_tables(s_len, tq, tk, False)
    nq = s_len // tq
    n_rows, nr = len(riders), len(riders) + len(casts)

    def body(qt_ref, kt_ref, q_ref, k_ref, v_ref, do_ref, lse_ref, dl_ref, *rest):
        r_in, (dq_ref, dk_ref, dv_ref), r_out = rest[:nr], rest[nr:nr + 3], rest[nr + 3:2 * nr + 3]
        dk_sc, dv_sc = rest[2 * nr + 3:2 * nr + 5]
        sems = rest[2 * nr + 5:]
        p_id = pl.program_id(1)
        qi, ki = qt_ref[p_id], kt_ref[p_id]
        rows = pl.ds(pl.multiple_of(qi * tq, tq), tq)
        first, last = _first_and_last_step(qt.shape[0])
        if nr:
            pl.when(first)(lambda: _exchange_issue(r_in, r_out, sems, n_rows))

        @pl.when(qi == (ki * tk) // tq)
        def _():
            dk_sc[...] = jnp.zeros(dk_sc.shape, F32)
            dv_sc[...] = jnp.zeros(dv_sc.shape, F32)

        @pl.when(ki == 0)
        def _():
            dq_ref[rows, :] = jnp.zeros((tq, HC), F32)

        def update(width, masked):
            q, k, do = q_ref[...], k_ref[:width, :], do_ref[...]
            p, ds = _tile_dscores(q, k, v_ref[:width, :], do, lse_ref[...], dl_ref[...], qi, ki, tq, tk, width, masked)
            ds = ds.astype(MXU)
            dv_sc[:width, :] += _mm_tn(p, do)
            dk_sc[:width, :] += _mm_tn(ds, q)
            dq_ref[rows, :] += _mm(ds, k)

        _by_visible_width(qi, ki, tq, tk, update)

        @pl.when(qi == nq - 1)
        def _():
            dk_ref[...] = (dk_sc[...] * (1.0 / LOG2E)).astype(dk_ref.dtype)
            dv_ref[...] = dv_sc[...].astype(dv_ref.dtype)

        if nr:
            pl.when(last)(lambda: _exchange_finish(r_in, r_out, sems, n_rows))

    qmap = lambda h, p, qt, kt: (qt[p], h)
    kmap = lambda h, p, qt, kt: (kt[p], h)
    grid_spec = pltpu.PrefetchScalarGridSpec(
        num_scalar_prefetch=2, grid=(HEADS, qt.shape[0]),
        in_specs=[pl.BlockSpec((tq, HC), qmap), pl.BlockSpec((tk, HC), kmap), pl.BlockSpec((tk, VH), kmap),
                  pl.BlockSpec((tq, VH), qmap), pl.BlockSpec((tq, LANES), qmap), pl.BlockSpec((tq, LANES), qmap)] + [ANY] * nr,
        out_specs=[pl.BlockSpec((s_len, HC), lambda h, p, qt, kt: (0, h), pipeline_mode=pl.Buffered(1)),
                   pl.BlockSpec((tk, HC), kmap), pl.BlockSpec((tk, VH), kmap)] + [ANY] * nr,
        scratch_shapes=[pltpu.VMEM((tk, HC), F32), pltpu.VMEM((tk, VH), F32)] + _comm_scratch(nr))
    out = pl.pallas_call(body, grid_spec=grid_spec,
                         out_shape=[jax.ShapeDtypeStruct((s_len, HEADS * HC), F32),
                                    jax.ShapeDtypeStruct((s_len, HEADS * HC), MXU),
                                    jax.ShapeDtypeStruct((s_len, HEADS * VH), MXU)] + _exchanged_shapes(riders, casts),
                         compiler_params=_cparams(2), name="flash_bwd")(qt, kt, qc, kc, v, do, lse, delta, *riders, *casts)
    return out[0], out[1], out[2], out[3:3 + n_rows], out[3 + n_rows:]


def _attn_out_fwd(x, o, wo, tm):
    def fn(x, o, wo):
        return (x + _mm(o, wo),), ()

    return _rowcall("attn_out_fwd", fn, tm, [x, o], [wo], [(D, F32)], [])[0][0]


def _attn_out_bwd(dh, o, wo_t, tm):
    def fn(dh, o, wo_t):
        do = _mm(dh, wo_t)
        of = o.astype(F32)
        dl = [jnp.broadcast_to(jnp.sum(do[:, h * VH:(h + 1) * VH] * of[:, h * VH:(h + 1) * VH], axis=1, keepdims=True),
                               (dh.shape[0], LANES)) for h in range(HEADS)]
        return (do, jnp.concatenate(dl, axis=1)), (_mm_tn(o, dh),)

    return _rowcall("attn_out_bwd", fn, tm, [dh, o], [wo_t], [(HEADS * VH, MXU), (HEADS * LANES, F32)], [(HEADS * VH, D)])


def _shift_rows(a, k):
    return a if k == 0 else pltpu.roll(a, k % a.shape[0], 0)


def _stack_rows(rows):
    idx = lax.broadcasted_iota(jnp.int32, (8, rows[0].shape[1]), 0)
    out = jnp.zeros((8, rows[0].shape[1]), F32)
    for k, r in enumerate(rows):
        out = jnp.where(idx == k, r, out)
    return out


def _ffn_fwd(h, g, wup, cw, cb, wdown, tm):
    s_len = h.shape[0]
    nj = DFF // TC
    hb = tm // HALO

    def body(h_ref, hp_ref, g_ref, wv_ref, wg_ref, cwv_ref, cwg_ref, cbv_ref, cbg_ref, wd_ref, out_ref, act_ref,
             upv_ref, upg_ref, cv_ref, cg_ref, hn_sc, acc_sc):
        i, j = pl.program_id(0), pl.program_id(1)

        @pl.when(j == 0)
        def _():
            gg = g_ref[...]
            hp = _rms_fwd(hp_ref[...], gg)[0]
            hn_sc[:HALO, :] = jnp.where(i > 0, hp, 0.0).astype(MXU)
            hn_sc[HALO:, :] = _rms_fwd(h_ref[...], gg)[0].astype(MXU)
            acc_sc[...] = jnp.zeros(acc_sc.shape, F32)

        hn = hn_sc[...]

        def conv(w_ref, cw_ref, cb_ref, up_ref):
            up = jnp.dot(hn, w_ref[...], preferred_element_type=F32)
            up_ref[...] = up[HALO:].astype(up_ref.dtype)
            cwv = cw_ref[...]
            c = cwv[2:3] * up + cwv[1:2] * _shift_rows(up, 1) + cwv[0:1] * _shift_rows(up, 2)
            return c[HALO:] + cb_ref[...]

        cv = conv(wv_ref, cwv_ref, cbv_ref, upv_ref)
        cg = conv(wg_ref, cwg_ref, cbg_ref, upg_ref)
        cv_ref[...] = cv.astype(cv_ref.dtype)
        cg_ref[...] = cg.astype(cg_ref.dtype)
        act = cg * jax.nn.sigmoid(cg) * cv
        act_ref[...] = act.astype(act_ref.dtype)
        acc_sc[...] += _mm(act, wd_ref[...])

        @pl.when(j == nj - 1)
        def _():
            out_ref[...] = h_ref[...] + acc_sc[...]

    in_specs = [pl.BlockSpec((tm, D), lambda i, j: (i, 0)),
                pl.BlockSpec((HALO, D), lambda i, j: (jnp.maximum(i * hb - 1, 0), 0)),
                pl.BlockSpec((1, D), lambda i, j: (0, 0)),
                pl.BlockSpec((D, TC), lambda i, j: (0, j)), pl.BlockSpec((D, TC), lambda i, j: (0, j + nj)),
                pl.BlockSpec((3, TC), lambda i, j: (0, j)), pl.BlockSpec((3, TC), lambda i, j: (0, j + nj)),
                pl.BlockSpec((1, TC), lambda i, j: (0, j)), pl.BlockSpec((1, TC), lambda i, j: (0, j + nj)),
                pl.BlockSpec((TC, D), lambda i, j: (j, 0))]
    out_specs = [pl.BlockSpec((tm, D), lambda i, j: (i, 0))] + [pl.BlockSpec((tm, TC), lambda i, j: (i, j))] * 5
    return pl.pallas_call(body, grid=(s_len // tm, nj), in_specs=in_specs, out_specs=out_specs,
                          out_shape=[jax.ShapeDtypeStruct((s_len, D), F32)] + [jax.ShapeDtypeStruct((s_len, DFF), MXU)] * 5,
                          scratch_shapes=[pltpu.VMEM((tm + HALO, D), MXU), pltpu.VMEM((tm, D), F32)],
                          compiler_params=_cparams(2), name="ffn_fwd")(h, h, g, wup, wup, cw, cw, cb, cb, wdown)


def _ffn_bwd(h, dout, upv, upg, cv, cg, g, cw, wdown_t, wup_t, tm):
    s_len = h.shape[0]
    ni = s_len // tm
    hb = tm // HALO

    def body(h_ref, d_ref, dx_ref, uv_ref, ug_ref, cv_ref, cvx_ref, cg_ref, cgx_ref, g_ref, cwv_ref, cwg_ref,
             wdt_ref, wutv_ref, wutg_ref, din_ref, dupv_ref, dupg_ref, hn_ref, cacc_ref, dg_ref):
        i = pl.program_id(0)

        @pl.when(i == 0)
        def _():
            cacc_ref[...] = jnp.zeros(cacc_ref.shape, F32)
            dg_ref[...] = jnp.zeros(dg_ref.shape, F32)

        gg = g_ref[...]
        hn, xh, r = _rms_fwd(h_ref[...], gg)
        hn_ref[...] = hn.astype(hn_ref.dtype)
        dd = jnp.concatenate([d_ref[...], jnp.where(i < ni - 1, dx_ref[...], 0.0)], axis=0).astype(MXU)
        dact = jnp.dot(dd, wdt_ref[...], preferred_element_type=F32)
        cvv = jnp.concatenate([cv_ref[...], cvx_ref[...]], axis=0).astype(F32)
        cgg = jnp.concatenate([cg_ref[...], cgx_ref[...]], axis=0).astype(F32)
        sg = jax.nn.sigmoid(cgg)
        dcv = dact * (cgg * sg)
        dcg = dact * cvv * (sg * (1.0 + cgg * (1.0 - sg)))

        def back(dc, up_ref, cw_ref, slot, dup_ref, wut_ref):
            cwx = cw_ref[...]
            d1, d2 = _shift_rows(dc, -1), _shift_rows(dc, -2)
            dup_ref[...] = (cwx[2:3] * dc + cwx[1:2] * d1 + cwx[0:1] * d2)[:tm].astype(dup_ref.dtype)
            up = up_ref[...].astype(F32)
            ones = jnp.ones((8, tm), MXU)
            colsum = lambda a: jnp.dot(ones, a.astype(MXU), preferred_element_type=F32)[0:1]
            cacc_ref[slot] += _stack_rows([colsum(d2[:tm] * up), colsum(d1[:tm] * up), colsum(dc[:tm] * up),
                                           colsum(dc[:tm])])
            return _mm(dup_ref[...], wut_ref[...])

        dhn = back(dcv, uv_ref, cwv_ref, 0, dupv_ref, wutv_ref) + back(dcg, ug_ref, cwg_ref, 1, dupg_ref, wutg_ref)
        dx, dgp = _rms_bwd(dhn, xh, r, gg)
        din_ref[...] = d_ref[...] + dx
        dg_ref[...] += dgp

    last_blk = s_len // HALO - 1
    tile = lambda w: pl.BlockSpec((tm, w), lambda i: (i, 0))
    nxt = lambda w: pl.BlockSpec((HALO, w), lambda i: (jnp.minimum((i + 1) * hb, last_blk), 0))
    const = lambda s, j=0: pl.BlockSpec(s, lambda i: (0, j))
    in_specs = [tile(D), tile(D), nxt(D), tile(DFF), tile(DFF), tile(DFF), nxt(DFF), tile(DFF), nxt(DFF),
                const((1, D)), const((3, DFF)), const((3, DFF), 1),
                const((D, DFF)), pl.BlockSpec((DFF, D), lambda i: (0, 0)), pl.BlockSpec((DFF, D), lambda i: (1, 0))]
    out_specs = [tile(D), tile(DFF), tile(DFF), tile(D), pl.BlockSpec((2, 8, DFF), lambda i: (0, 0, 0)), const((1, D))]
    out_shape = [jax.ShapeDtypeStruct((s_len, D), F32), jax.ShapeDtypeStruct((s_len, DFF), MXU),
                 jax.ShapeDtypeStruct((s_len, DFF), MXU), jax.ShapeDtypeStruct((s_len, D), MXU),
                 jax.ShapeDtypeStruct((2, 8, DFF), F32), jax.ShapeDtypeStruct((1, D), F32)]
    return pl.pallas_call(body, grid=(ni,), in_specs=in_specs, out_specs=out_specs, out_shape=out_shape,
                          compiler_params=_cparams(1), name="ffn_bwd")(
        h, dout, dout, upv, upg, cv, cv, cg, cg, g, cw, cw, wdown_t, wup_t, wup_t)


def _matmul_tn(a, b, tn, ts, name):
    s_len, m = a.shape
    n = b.shape[1]

    def body(a_ref, b_ref, o_ref):
        @pl.when(pl.program_id(1) == 0)
        def _():
            o_ref[...] = jnp.zeros(o_ref.shape, F32)

        o_ref[...] += _mm_tn(a_ref[...], b_ref[...])

    return pl.pallas_call(body, grid=(n // tn, s_len // ts),
                          in_specs=[pl.BlockSpec((ts, m), lambda jn, k: (k, 0)), pl.BlockSpec((ts, tn), lambda jn, k: (k, jn))],
                          out_specs=pl.BlockSpec((m, tn), lambda jn, k: (0, jn)),
                          out_shape=jax.ShapeDtypeStruct((m, n), F32), compiler_params=_cparams(2), name=name)(a, b)


def _s5_coefs(lr, li, ldt):
    dt = jnp.exp(ldt)
    mag = jnp.exp(lr * dt)
    th = li * dt
    ar, ai = mag * jnp.cos(th), mag * jnp.sin(th)
    den = lr * lr + li * li
    nr = ar - 1.0
    cr = (nr * lr + ai * li) / den
    ci = (ai * lr - nr * li) / den
    return dt, mag, th, ar, ai, den, nr, cr, ci


def _s5_prep(lr, li, ldt, braw, seg):
    def body(lr_ref, li_ref, ldt_ref, b_ref, bb_ref, ap_ref):
        lr_, li_, ldt_ = lr_ref[0], li_ref[0], ldt_ref[0]
        dt, mag, th, ar, ai, den, nr, cr, ci = _s5_coefs(lr_, li_, ldt_)
        br, bi = b_ref[0, :, :SL], b_ref[0, :, SL:]
        bb_ref[0, :, :SL] = (cr * br - ci * bi).astype(bb_ref.dtype)
        bb_ref[0, :, SL:] = (cr * bi + ci * br).astype(bb_ref.dtype)
        for i in range(seg):
            m = jnp.exp((i + 1.0) * (lr_ * dt))
            ap_ref[0, i * 8:(i + 1) * 8, :SL] = jnp.broadcast_to(m * jnp.cos((i + 1.0) * th), (8, SL))
            ap_ref[0, i * 8:(i + 1) * 8, SL:] = jnp.broadcast_to(m * jnp.sin((i + 1.0) * th), (8, SL))

    vec = pl.BlockSpec((1, 1, SL), lambda k: (k, 0, 0))
    return pl.pallas_call(
        body, grid=(GB,), in_specs=[vec, vec, vec, pl.BlockSpec((1, LANES, 2 * SL), lambda k: (k, 0, 0))],
        out_specs=[pl.BlockSpec((1, LANES, 2 * SL), lambda k: (k, 0, 0)),
                   pl.BlockSpec((1, 8 * seg, 2 * SL), lambda k: (k, 0, 0))],
        out_shape=[jax.ShapeDtypeStruct((GB, LANES, 2 * SL), MXU), jax.ShapeDtypeStruct((GB, 8 * seg, 2 * SL), F32)],
        compiler_params=_cparams(1), name="s5_prep")(lr, li, ldt, braw)


def _s5_prep_bwd(lr, li, ldt, braw, dbb, da):
    def body(lr_ref, li_ref, ldt_ref, b_ref, dbb_ref, da_ref, dbraw_ref, dlr_ref, dli_ref, dldt_ref):
        lr_, li_, ldt_ = lr_ref[0], li_ref[0], ldt_ref[0]
        dt, mag, th, ar, ai, den, nr, cr, ci = _s5_coefs(lr_, li_, ldt_)
        br, bi = b_ref[0, :, :SL], b_ref[0, :, SL:]
        gbr, gbi = dbb_ref[0, :, :SL], dbb_ref[0, :, SL:]
        dbraw_ref[0, :, :SL] = cr * gbr + ci * gbi
        dbraw_ref[0, :, SL:] = cr * gbi - ci * gbr
        dcr = jnp.sum(gbr * br + gbi * bi, axis=0, keepdims=True)
        dci = jnp.sum(gbi * br - gbr * bi, axis=0, keepdims=True)
        dar = jnp.sum(da_ref[0, :, :SL], axis=0, keepdims=True)
        dai = jnp.sum(da_ref[0, :, SL:], axis=0, keepdims=True)
        g1, g2 = dcr / den, dci / den
        gden = -(dcr * cr + dci * ci) / den
        gar = dar + g1 * lr_ - g2 * li_
        gai = dai + g1 * li_ + g2 * lr_
        glr = g1 * nr + g2 * ai + 2.0 * lr_ * gden
        gli = g1 * ai - g2 * nr + 2.0 * li_ * gden
        gmag = gar * jnp.cos(th) + gai * jnp.sin(th)
        gth = gai * ar - gar * ai
        dlr_ref[0] = glr + gmag * mag * dt
        dli_ref[0] = gli + gth * dt
        dldt_ref[0] = (gmag * mag * lr_ + gth * li_) * dt

    vec = pl.BlockSpec((1, 1, SL), lambda k: (k, 0, 0))
    mat = pl.BlockSpec((1, LANES, 2 * SL), lambda k: (k, 0, 0))
    return pl.pallas_call(
        body, grid=(GB,), in_specs=[vec, vec, vec, mat, mat, pl.BlockSpec((1, 8, 2 * SL), lambda k: (k, 0, 0))],
        out_specs=[mat, vec, vec, vec],
        out_shape=[jax.ShapeDtypeStruct((GB, LANES, 2 * SL), F32)] + [jax.ShapeDtypeStruct((GB, 1, SL), F32)] * 3,
        compiler_params=_cparams(1), name="s5_prep_bwd")(lr, li, ldt, braw, dbb, da)


def _bcast_row(tile, j):
    return jnp.broadcast_to(tile[j:j + 1, :], tile.shape)


def _tile_rows(i):
    return pl.ds(pl.multiple_of(i * 8, 8), 8)


def _col(c):
    return slice(c * LANES, (c + 1) * LANES)


def _permute_rows(ref, seg):
    return jnp.concatenate([ref[pl.ds(i, 8, stride=seg), :] for i in range(seg)], axis=0)


def _unpermute_rows(src, dst, seg):
    for j in range(8):
        dst[j * seg:(j + 1) * seg, :] = src[pl.ds(j, seg, stride=8), :]


SCAN_UNROLL = 2


def _segment_scan(src, dst, ap, seg, reverse):
    half = NCOL // 2
    sign = -1.0 if reverse else 1.0
    a_r = [ap[0:8, _col(c)] for c in range(half)]
    a_i = [ap[0:8, _col(half + c)] for c in range(half)]

    def step(n, carry):
        i = seg - 1 - n if reverse else n
        rows = _tile_rows(i)
        out_r, out_i = [], []
        for c in range(half):
            xr, xi = carry[c], carry[half + c]
            nr = a_r[c] * xr - sign * a_i[c] * xi + src[rows, _col(c)]
            ni = a_r[c] * xi + sign * a_i[c] * xr + src[rows, _col(half + c)]
            dst[rows, _col(c)] = nr
            dst[rows, _col(half + c)] = ni
            out_r.append(nr)
            out_i.append(ni)
        return tuple(out_r + out_i)

    zero = jnp.zeros((8, LANES), F32)
    return lax.fori_loop(0, seg, step, (zero,) * NCOL, unroll=SCAN_UNROLL)


def _segment_entries(ends, cin, ap, seg, reverse):
    half = NCOL // 2
    sign = -1.0 if reverse else 1.0
    al_r = [ap[(seg - 1) * 8:seg * 8, _col(c)] for c in range(half)]
    al_i = [ap[(seg - 1) * 8:seg * 8, _col(half + c)] for c in range(half)]
    row = lax.broadcasted_iota(jnp.int32, (8, LANES), 0)
    ent, out = [None] * NCOL, [None] * NCOL
    for c in range(half):
        zr, zi = cin[c], cin[half + c]
        er, ei = jnp.zeros((8, LANES), F32), jnp.zeros((8, LANES), F32)
        for j in (range(7, -1, -1) if reverse else range(8)):
            er, ei = jnp.where(row == j, zr, er), jnp.where(row == j, zi, ei)
            fr, fi = _bcast_row(ends[c], j), _bcast_row(ends[half + c], j)
            zr, zi = (al_r[c] * zr - sign * al_i[c] * zi + fr, al_r[c] * zi + sign * al_i[c] * zr + fi)
        ent[c], ent[half + c] = er, ei
        out[c], out[half + c] = zr, zi
    return ent, out


def _chunk_states(u, bb_ref, ap, cin, bu_sc, x_sc, seg):
    half = NCOL // 2
    bu_sc[...] = _mm(u, bb_ref[0])
    ends = _segment_scan(bu_sc, x_sc, ap, seg, False)
    ent, out = _segment_entries(ends, cin, ap, seg, False)

    def fix(i, _):
        rows = _tile_rows(i)
        for c in range(half):
            pr, pi = ap[rows, _col(c)], ap[rows, _col(half + c)]
            x_sc[rows, _col(c)] += pr * ent[c] - pi * ent[half + c]
            x_sc[rows, _col(half + c)] += pr * ent[half + c] + pi * ent[c]
        return 0

    lax.fori_loop(0, seg, fix, 0, unroll=SCAN_UNROLL)
    return out


def _s5_scan_fwd(u, bb, apow, cblk, dskip, t_chunk):
    s_len = u.shape[0]
    nc = s_len // t_chunk
    seg = t_chunk // 8

    def body(u_ref, bb_ref, ap_ref, c_ref, d_ref, y_ref, xin_ref, xs_ref, bu_sc, x_sc, y_sc, carry_sc):
        @pl.when(pl.program_id(1) == 0)
        def _():
            carry_sc[...] = jnp.zeros(carry_sc.shape, F32)

        uu = _permute_rows(u_ref, seg)
        cin = [carry_sc[:, _col(c)] for c in range(NCOL)]
        xin_ref[0, 0] = carry_sc[...]
        out = _chunk_states(uu, bb_ref, ap_ref.at[0], cin, bu_sc, x_sc, seg)
        for c in range(NCOL):
            carry_sc[:, _col(c)] = out[c]
        xs = x_sc[...].astype(MXU)
        xs_ref[0] = xs
        y_sc[...] = _mm(xs, c_ref[0]) + d_ref[...] * uu
        _unpermute_rows(y_sc, y_ref, seg)

    in_specs = [pl.BlockSpec((t_chunk, LANES), lambda k, c: (c, k)),
                pl.BlockSpec((1, LANES, 2 * SL), lambda k, c: (k, 0, 0)),
                pl.BlockSpec((1, 8 * seg, 2 * SL), lambda k, c: (k, 0, 0)),
                pl.BlockSpec((1, 2 * SL, LANES), lambda k, c: (k, 0, 0)),
                pl.BlockSpec((1, LANES), lambda k, c: (0, k))]
    out_specs = [pl.BlockSpec((t_chunk, LANES), lambda k, c: (c, k)),
                 pl.BlockSpec((1, 1, 8, 2 * SL), lambda k, c: (k, c, 0, 0)),
                 pl.BlockSpec((1, t_chunk, 2 * SL), lambda k, c: (k, c, 0))]
    return pl.pallas_call(body, grid=(GB, nc), in_specs=in_specs, out_specs=out_specs,
                          out_shape=[jax.ShapeDtypeStruct((s_len, D), F32), jax.ShapeDtypeStruct((GB, nc, 8, 2 * SL), F32),
                                     jax.ShapeDtypeStruct((GB, s_len, 2 * SL), MXU)],
                          scratch_shapes=[pltpu.VMEM((t_chunk, 2 * SL), F32), pltpu.VMEM((t_chunk, 2 * SL), F32),
                                          pltpu.VMEM((t_chunk, LANES), F32), pltpu.VMEM((8, 2 * SL), F32)],
                          compiler_params=_cparams(2), name="s5_scan_fwd")(u, bb, apow, cblk, dskip)


def _s5_scan_bwd(u, dy, xin, xs, bb_t, apow, cblk_t, dskip, t_chunk):
    s_len = u.shape[0]
    nc = s_len // t_chunk
    seg = t_chunk // 8
    half = NCOL // 2

    def body(u_ref, dy_ref, xin_ref, xs_ref, bbt_ref, ap_ref, ct_ref, d_ref, du_ref, dbb_ref, dc_ref, da_ref, dd_ref,
             x_sc, g_sc, y_sc, carry_sc):
        @pl.when(pl.program_id(1) == 0)
        def _():
            carry_sc[...] = jnp.zeros(carry_sc.shape, F32)
            dbb_ref[...] = jnp.zeros(dbb_ref.shape, F32)
            dc_ref[...] = jnp.zeros(dc_ref.shape, F32)
            da_ref[...] = jnp.zeros(da_ref.shape, F32)
            dd_ref[...] = jnp.zeros(dd_ref.shape, F32)

        ap = ap_ref.at[0]
        uu, dyy = _permute_rows(u_ref, seg), _permute_rows(dy_ref, seg)
        cin = [xin_ref[0, 0, :, _col(c)] for c in range(NCOL)]
        x_sc[...] = xs_ref[0].astype(F32)
        g_sc[...] = _mm(dyy, ct_ref[0])
        ends = _segment_scan(g_sc, g_sc, ap, seg, True)
        lam_in = [carry_sc[:, _col(c)] for c in range(NCOL)]
        ent, out = _segment_entries(ends, lam_in, ap, seg, True)
        for c in range(NCOL):
            carry_sc[:, _col(c)] = out[c]
        row = lax.broadcasted_iota(jnp.int32, (8, LANES), 0)
        xp0 = [jnp.where(row == 0, cin[c], pltpu.roll(x_sc[(seg - 1) * 8:seg * 8, _col(c)], 1, 0)) for c in range(NCOL)]

        def fix(i, acc):
            rows, prev, tab = _tile_rows(i), _tile_rows(jnp.maximum(i - 1, 0)), _tile_rows(seg - 1 - i)
            new = list(acc)
            for c in range(half):
                pr, pi = ap[tab, _col(c)], ap[tab, _col(half + c)]
                lr_ = g_sc[rows, _col(c)] + pr * ent[c] + pi * ent[half + c]
                li_ = g_sc[rows, _col(half + c)] + pr * ent[half + c] - pi * ent[c]
                g_sc[rows, _col(c)] = lr_
                g_sc[rows, _col(half + c)] = li_
                xr = jnp.where(i == 0, xp0[c], x_sc[prev, _col(c)])
                xi = jnp.where(i == 0, xp0[half + c], x_sc[prev, _col(half + c)])
                new[c] = acc[c] + lr_ * xr + li_ * xi
                new[half + c] = acc[half + c] + li_ * xr - lr_ * xi
            return tuple(new)

        zero = jnp.zeros((8, LANES), F32)
        dacc = lax.fori_loop(0, seg, fix, (zero,) * NCOL, unroll=SCAN_UNROLL)
        for c in range(NCOL):
            da_ref[0, :, _col(c)] += dacc[c]
        lam = g_sc[...]
        y_sc[...] = _mm(lam, bbt_ref[0]) + d_ref[...] * dyy
        _unpermute_rows(y_sc, du_ref, seg)
        dbb_ref[0] += _mm_tn(uu, lam)
        dc_ref[0] += _mm_tn(xs_ref[0], dyy)
        dd_ref[0] += _stack_rows([jnp.sum(dyy * uu, axis=0, keepdims=True)])

    rev = lambda k, c: (nc - 1 - c, k)
    in_specs = [pl.BlockSpec((t_chunk, LANES), rev), pl.BlockSpec((t_chunk, LANES), rev),
                pl.BlockSpec((1, 1, 8, 2 * SL), lambda k, c: (k, nc - 1 - c, 0, 0)),
                pl.BlockSpec((1, t_chunk, 2 * SL), lambda k, c: (k, nc - 1 - c, 0)),
                pl.BlockSpec((1, 2 * SL, LANES), lambda k, c: (k, 0, 0)),
                pl.BlockSpec((1, 8 * seg, 2 * SL), lambda k, c: (k, 0, 0)),
                pl.BlockSpec((1, LANES, 2 * SL), lambda k, c: (k, 0, 0)),
                pl.BlockSpec((1, LANES), lambda k, c: (0, k))]
    out_specs = [pl.BlockSpec((t_chunk, LANES), rev),
                 pl.BlockSpec((1, LANES, 2 * SL), lambda k, c: (k, 0, 0)),
                 pl.BlockSpec((1, 2 * SL, LANES), lambda k, c: (k, 0, 0)),
                 pl.BlockSpec((1, 8, 2 * SL), lambda k, c: (k, 0, 0)),
                 pl.BlockSpec((1, 8, LANES), lambda k, c: (k, 0, 0))]
    out_shape = [jax.ShapeDtypeStruct((s_len, D), F32), jax.ShapeDtypeStruct((GB, LANES, 2 * SL), F32),
                 jax.ShapeDtypeStruct((GB, 2 * SL, LANES), F32), jax.ShapeDtypeStruct((GB, 8, 2 * SL), F32),
                 jax.ShapeDtypeStruct((GB, 8, LANES), F32)]
    return pl.pallas_call(body, grid=(GB, nc), in_specs=in_specs, out_specs=out_specs, out_shape=out_shape,
                          scratch_shapes=[pltpu.VMEM((t_chunk, 2 * SL), F32)] * 2 + [pltpu.VMEM((t_chunk, LANES), F32),
                                                                                        pltpu.VMEM((8, 2 * SL), F32)],
                          compiler_params=_cparams(2), name="s5_scan_bwd")(u, dy, xin, xs, bb_t, apow, cblk_t, dskip)


_GELU_K = math.sqrt(2.0 / math.pi)


def _gelu(y):
    t = jnp.tanh(_GELU_K * (y + 0.044715 * (y * y * y)))
    return 0.5 * y * (1.0 + t), 0.5 * (1.0 + t) + 0.5 * y * (1.0 - t * t) * (_GELU_K * (1.0 + 3 * 0.044715 * (y * y)))


def _s5_in_fwd(h, gmix, win, tm):
    def fn(h, gmix, win):
        return (_mm(_rms_fwd(h, gmix)[0], win),), ()

    return _rowcall("s5_in_fwd", fn, tm, [h], [gmix, win], [(D, F32)], [])[0][0]


def _s5_in_bwd(h, du, dh, gmix, win_t, tm):
    def fn(h, du, dh, gmix, win_t):
        hn, xh, r = _rms_fwd(h, gmix)
        dx, dg = _rms_bwd(_mm(du, win_t), xh, r, gmix)
        return (dh + dx,), (_mm_tn(hn, du), dg)

    return _rowcall("s5_in_bwd", fn, tm, [h, du, dh], [gmix, win_t], [(D, F32)], [(D, D), (1, D)])


def _s5_out_fwd(h, y, wglu, tm):
    def fn(h, y, wglu):
        z = _mm(_gelu(y)[0], wglu)
        return (h + z[:, :D] * jax.nn.sigmoid(z[:, D:]), z), ()

    return _rowcall("s5_out_fwd", fn, tm, [h, y], [wglu], [(D, F32), (2 * D, MXU)], [])[0]


def _s5_out_bwd(dh, y, z, wglu_t, tm):
    def fn(dh, y, z, wglu_t):
        yg, dgelu = _gelu(y)
        z = z.astype(F32)
        val, sg = z[:, :D], jax.nn.sigmoid(z[:, D:])
        dz = jnp.concatenate([dh * sg, dh * val * sg * (1.0 - sg)], axis=1)
        return (_mm(dz, wglu_t) * dgelu,), (_mm_tn(yg, dz),)

    return _rowcall("s5_out_bwd", fn, tm, [dh, y, z], [wglu_t], [(D, F32)], [(D, 2 * D)])


def _final_loss(h, tgt, gfin, tm):
    def fn(h, tgt, gfin):
        y, xh, r = _rms_fwd(h, gfin)
        err = y - tgt
        dx, dg = _rms_bwd(err * (1.0 / D), xh, r, gfin)
        return (dx,), (jnp.sum(err * err, axis=0, keepdims=True), dg)

    return _rowcall("final_loss", fn, tm, [h, tgt], [gfin], [(D, F32)], [(1, D), (1, D)])


def _bf(a):
    return a.astype(MXU)


def _s5_block_mats(b_re, b_im, c_re, c_im):
    gl = G // GB
    eye = jnp.eye(gl, dtype=F32)

    def b_blk(b):
        bt = b.reshape(GB, gl, P, C).transpose(0, 1, 3, 2)
        return (bt[:, :, :, None, :] * eye[None, :, None, :, None]).reshape(GB, gl * C, gl * P)

    def c_blk(cm):
        ct = cm.reshape(GB, gl, C, P).transpose(0, 1, 3, 2)
        return (ct[:, :, :, None, :] * eye[None, :, None, :, None]).reshape(GB, gl * P, gl * C)

    braw = jnp.concatenate([b_blk(b_re), b_blk(b_im)], axis=2)
    cblk = jnp.concatenate([c_blk(c_re), -c_blk(c_im)], axis=1)
    return braw, cblk


def _s5_unblock_b(d):
    gl = G // GB
    eye = jnp.eye(gl, dtype=F32)
    d5 = d.reshape(GB, gl, C, gl, P)
    return jnp.sum(d5 * eye[None, :, None, :, None], axis=3).transpose(0, 1, 3, 2).reshape(G, P, C)


def _s5_unblock_c(d):
    gl = G // GB
    eye = jnp.eye(gl, dtype=F32)
    d5 = d.reshape(GB, gl, P, gl, C)
    return jnp.sum(d5 * eye[None, :, None, :, None], axis=3).transpose(0, 1, 3, 2).reshape(G, C, P)


def _tiles(s_len):
    return (min(512, s_len), min(1024, s_len), min(2048, s_len),
            (min(1024, s_len), min(4096, s_len)), (min(1024, s_len), min(2048, s_len)),
            min(256, s_len), min(256, s_len))


def _merge_shards(g, axis):
    moved = jnp.moveaxis(g, 0, axis)
    return moved.reshape(moved.shape[:axis] + (-1,) + moved.shape[axis + 2:])


def _split_shards(full, axis):
    cut = full.reshape(full.shape[:axis] + (NDEV, -1) + full.shape[axis + 1:])
    return jnp.moveaxis(cut, axis, 0)


def _sequence_step(x, pos, tgt, w, late=(), late_small=()):
    s_len = x.shape[0]
    tm_w, tm, t_chunk, t_fwd, t_bwd, tm_ff, tm_fb = _tiles(s_len)
    seg = t_chunk // 8
    row = lambda v: v.reshape(1, -1)

    wa = _bf(jnp.pad(w["mla_w_a"][0], ((0, 0), (0, AW - (QL + KVL + ROPE)))))
    wuq = _bf(jnp.pad(w["mla_w_uq"][0].reshape(QL, HEADS, NOPE + ROPE), ((0, 0), (0, 0), (0, HC - NOPE - ROPE))).reshape(QL, HEADS * HC))
    wukv = _bf(w["mla_w_ukv"][0])
    inv = 1.0 / (ROPE_THETA ** (jnp.arange(0, ROPE, 2, dtype=F32) / ROPE))
    invf = jnp.concatenate([inv, inv, jnp.zeros((LANES - ROPE,), F32)]).reshape(1, LANES)
    gmix0, gmix1 = row(w["g_mix"][0]), row(w["g_mix"][1])
    gq, gkv = row(w["mla_g_q"][0]), row(w["mla_g_kv"][0])
    lr = w["ssm_lambda_re"][0].reshape(GB, 1, SL)
    li = w["ssm_lambda_im"][0].reshape(GB, 1, SL)
    ldt = jnp.broadcast_to(w["ssm_log_dt"][0][:, None], (G, P)).reshape(GB, 1, SL)
    braw, cblk = _s5_block_mats(w["ssm_b_re"][0], w["ssm_b_im"][0], w["ssm_c_re"][0], w["ssm_c_im"][0])
    cblk = _bf(cblk)

    qc, kc, v = _mla_front_fwd(x, pos, gmix0, wa, gq, gkv, wuq, wukv, invf, tm_w // 2)
    o, lse, gathered = _flash_fwd(qc, kc, v, *t_fwd, [shard for _, _, shard in late])
    w = dict(w)
    for (n, axis, _), g in zip(late, gathered):
        w[n] = _merge_shards(g, axis)
    wo = _bf(w["mla_w_o"][0])
    win, wglu = _bf(w["ssm_w_in"][0]), _bf(w["ssm_w_glu"][0])
    dskip = row(w["ssm_d"][0])
    ffn = []
    for l in range(2):
        wup, wdown = _bf(w["ffn_w_up"][l]), _bf(w["ffn_w_down"][l])
        ffn.append(dict(g=row(w["g_ffn"][l]), wup=wup, wup_t=wup.T, wdown=wdown, wdown_t=wdown.T,
                        cw=w["ffn_conv_w"][l], cb=row(w["ffn_conv_b"][l])))
    h1 = _attn_out_fwd(x, o, wo, tm)
    f0 = ffn[0]
    h2, *saved0 = _ffn_fwd(h1, f0["g"], f0["wup"], f0["cw"], f0["cb"], f0["wdown"], tm_ff)
    bb, apow = _s5_prep(lr, li, ldt, braw, seg)
    u = _s5_in_fwd(h2, gmix1, win, tm)
    y, xin, xs = _s5_scan_fwd(u, bb, apow, cblk, dskip, t_chunk)
    h3, z_glu = _s5_out_fwd(h2, y, wglu, tm)
    f1 = ffn[1]
    h4, *saved1 = _ffn_fwd(h3, f1["g"], f1["wup"], f1["cw"], f1["cb"], f1["wdown"], tm_ff)
    (dh4,), (sq, d_gfinal) = _final_loss(h4, tgt, row(w["g_final"]), tm)
    loss = 0.5 * jnp.sum(sq) / D

    grads = {}

    def ffn_back(hin, dout, saved, f):
        act, upv, upg, cv, cg = saved
        din, dupv, dupg, hn, cacc, dg = _ffn_bwd(hin, dout, upv, upg, cv, cg, f["g"], f["cw"], f["wdown_t"], f["wup_t"], tm_fb)
        d_wup = jnp.concatenate([_matmul_tn(hn, dupv, DFF, min(512, s_len), "ffn_dwup_v"),
                                 _matmul_tn(hn, dupg, DFF, min(512, s_len), "ffn_dwup_g")], axis=1)
        d_wdown = _matmul_tn(act, dout, D, min(512, s_len), "ffn_dwdown")
        cflat = cacc.transpose(1, 0, 2).reshape(8, 2 * DFF)
        return din, d_wup, d_wdown, cflat[:3], cflat[3], dg[0]

    dh3, d_wup1, d_wdown1, d_cw1, d_cb1, d_gffn1 = ffn_back(h3, dh4, saved1, f1)
    (dy,), (d_wglu,) = _s5_out_bwd(dh3, y, z_glu, wglu.T, tm_w)
    du, d_bb, d_cblk, d_a, d_dsk = _s5_scan_bwd(u, dy, xin, xs, bb.transpose(0, 2, 1), apow, cblk.transpose(0, 2, 1), dskip, t_chunk)
    d_braw, d_lr, d_li, d_ldt = _s5_prep_bwd(lr, li, ldt, braw, d_bb, d_a)
    (dh2,), (d_win, d_gmix1) = _s5_in_bwd(h2, du, dh3, gmix1, win.T, tm)
    dh1, d_wup0, d_wdown0, d_cw0, d_cb0, d_gffn0 = ffn_back(h1, dh2, saved0, f0)
    (do, delta), (d_wo,) = _attn_out_bwd(dh1, o, wo.T, tm)

    grads["mla_w_o"] = d_wo[None]
    grads["ssm_w_in"] = d_win[None]
    grads["ssm_lambda_re"] = d_lr.reshape(1, G, P)
    grads["ssm_lambda_im"] = d_li.reshape(1, G, P)
    grads["ssm_log_dt"] = jnp.sum(d_ldt.reshape(G, P), axis=1)[None]
    grads["ssm_b_re"] = _s5_unblock_b(d_braw[:, :, :SL])[None]
    grads["ssm_b_im"] = _s5_unblock_b(d_braw[:, :, SL:])[None]
    grads["ssm_c_re"] = _s5_unblock_c(d_cblk[:, :SL, :])[None]
    grads["ssm_c_im"] = -_s5_unblock_c(d_cblk[:, SL:, :])[None]
    grads["ssm_d"] = jnp.sum(d_dsk, axis=1).reshape(1, D)
    grads["ssm_w_glu"] = d_wglu[None]
    grads["ffn_w_up"] = jnp.stack([d_wup0, d_wup1])
    grads["ffn_conv_w"] = jnp.stack([d_cw0, d_cw1])
    grads["ffn_conv_b"] = jnp.stack([d_cb0, d_cb1])
    grads["ffn_w_down"] = jnp.stack([d_wdown0, d_wdown1])
    grads["g_ffn"] = jnp.stack([d_gffn0, d_gffn1])
    grads["g_final"] = d_gfinal[0]

    sends = [_split_shards(grads[n], axis).astype(MXU) for n, axis, _ in late]
    casts = [jnp.concatenate([grads[n].reshape(-1) for n in late_small])] if late_small else []
    dqc, dkc, dv, landed, landed_small = _flash_bwd(qc, kc, v, do, lse, delta, *t_bwd, sends, casts)
    (dx,), (d_wa, d_wuq, d_wukv, d_gq, d_gkv, d_gmix0) = _mla_front_bwd(
        x, pos, dqc, dkc, dv, dh1, gmix0, wa, gq, gkv, wuq, wukv, invf, wa.T, wuq.T, wukv.T, tm_w)
    grads["mla_w_a"] = d_wa[None, :, :QL + KVL + ROPE]
    grads["mla_g_q"] = d_gq
    grads["mla_g_kv"] = d_gkv
    grads["mla_w_uq"] = d_wuq.reshape(QL, HEADS, HC)[:, :, :NOPE + ROPE].reshape(1, QL, HEADS * (NOPE + ROPE))
    grads["mla_w_ukv"] = d_wukv[None]
    grads["g_mix"] = jnp.concatenate([d_gmix0, d_gmix1], axis=0)
    return loss, dx, grads, landed, (landed_small[0] if late_small else None)


MESH = pl.DeviceIdType.MESH
ANY = pl.BlockSpec(memory_space=pl.ANY)


def _gather_many(blocks, name):
    n = len(blocks)

    def body(*refs):
        _gather_issue(refs[:n], refs[n:2 * n], refs[2 * n:])
        _gather_finish(refs[:n], refs[n:2 * n], refs[2 * n:])

    return pl.pallas_call(body, out_shape=_gathered_shapes(blocks), in_specs=[ANY] * n, out_specs=[ANY] * n,
                          scratch_shapes=_comm_scratch(n), name=name)(*blocks)


def _comm_scratch(n):
    if n == 0:
        return []
    return [pltpu.SemaphoreType.DMA((7 * n,)), pltpu.SemaphoreType.DMA((7 * n,)), pltpu.SemaphoreType.DMA((n,))]


def _gathered_shapes(blocks):
    return [jax.ShapeDtypeStruct((NDEV,) + b.shape, b.dtype) for b in blocks]


def _gather_copies(ins, outs, sems):
    send_sems, recv_sems, local_sems = sems
    x, y, c = lax.axis_index("x"), lax.axis_index("y"), lax.axis_index("c")
    me, sibling = (x, y, c), (x, y, 1 - c)
    chips = [(1 - x, y), (x, 1 - y), (1 - x, 1 - y)]

    def copy(a, k, block, to, own=False):
        px, py, pc = block
        slot = outs[a].at[4 * px + 2 * py + pc]
        return pltpu.make_async_remote_copy(src_ref=ins[a] if own else slot, dst_ref=slot,
                                            send_sem=send_sems.at[7 * a + k], recv_sem=recv_sems.at[7 * a + k],
                                            device_id=to, device_id_type=MESH)

    mine = [pltpu.make_async_copy(ins[a], outs[a].at[4 * x + 2 * y + c], local_sems.at[a]) for a in range(len(ins))]
    first = []
    for a in range(len(ins)):
        first.append(copy(a, 0, me, sibling, own=True))
        first += [copy(a, 1 + j, me, (*chip, c), own=True) for j, chip in enumerate(chips)]
    return copy, mine, first, me, sibling, chips, c


def _gather_issue(ins, outs, sems):
    _, mine, first, *_ = _gather_copies(ins, outs, sems)
    for cp in mine + first:
        cp.start()


def _gather_finish(ins, outs, sems):
    copy, mine, first, me, sibling, chips, c = _gather_copies(ins, outs, sems)
    passed = []
    for j, chip in enumerate(chips):
        for a in range(len(ins)):
            copy(a, 1 + j, (*chip, c), me).wait_recv()
            passed.append(copy(a, 4 + j, (*chip, c), sibling))
            passed[-1].start()
    for a in range(len(ins)):
        copy(a, 0, sibling, me).wait_recv()
        for j, chip in enumerate(chips):
            copy(a, 4 + j, (*chip, 1 - c), me).wait_recv()
    for cp in first + passed:
        cp.wait_send()
    for cp in mine:
        cp.wait()


def _exchange_many(sends, casts, name):
    n, tot = len(sends), len(sends) + len(casts)

    def body(*refs):
        _exchange_issue(refs[:tot], refs[tot:2 * tot], refs[2 * tot:], n)
        _exchange_finish(refs[:tot], refs[tot:2 * tot], refs[2 * tot:], n)

    out = pl.pallas_call(body, out_shape=_exchanged_shapes(sends, casts), in_specs=[ANY] * tot, out_specs=[ANY] * tot,
                         scratch_shapes=_comm_scratch(tot), name=name)(*sends, *casts)
    return out[:n], out[n:]


def _exchanged_shapes(sends, casts):
    return [jax.ShapeDtypeStruct(s.shape, s.dtype) for s in sends] + _gathered_shapes(casts)


def _exchange_copies(ins, outs, sems, n, with_arrivals):
    send_sems, recv_sems, local_sems = sems
    x, y, c = lax.axis_index("x"), lax.axis_index("y"), lax.axis_index("c")
    me = 4 * x + 2 * y + c
    local, sent, arrivals = [], [], []
    for a in range(len(ins)):
        own = ins[a].at[me] if a < n else ins[a]
        local.append(pltpu.make_async_copy(own, outs[a].at[me], local_sems.at[a]))
        for m in range(1, NDEV):
            px = 1 - x if m & 4 else x
            py = 1 - y if m & 2 else y
            pc = 1 - c if m & 1 else c
            peer = 4 * px + 2 * py + pc
            to = dict(send_sem=send_sems.at[7 * a + m - 1], recv_sem=recv_sems.at[7 * a + m - 1],
                      device_id=(px, py, pc), device_id_type=MESH)
            sent.append(pltpu.make_async_remote_copy(src_ref=ins[a].at[peer] if a < n else ins[a], dst_ref=outs[a].at[me], **to))
            if with_arrivals:
                arrivals.append(pltpu.make_async_remote_copy(src_ref=own, dst_ref=outs[a].at[peer], **to))
    return local, sent, arrivals


def _exchange_issue(ins, outs, sems, n):
    local, sent, _ = _exchange_copies(ins, outs, sems, n, False)
    for cp in local + sent:
        cp.start()


def _exchange_finish(ins, outs, sems, n):
    local, sent, arrivals = _exchange_copies(ins, outs, sems, n, True)
    for cp in arrivals:
        cp.wait_recv()
    for cp in sent:
        cp.wait_send()
    for cp in local:
        cp.wait()


def _adam_math(w, parts, m, v):
    g = parts[0].astype(F32)
    for k in range(1, NDEV):
        g = g + parts[k].astype(F32)
    m2 = ADAM_B1 * m + (1.0 - ADAM_B1) * g
    v2 = ADAM_B2 * v + (1.0 - ADAM_B2) * jnp.square(g)
    m_hat = m2 / (1.0 - ADAM_B1 ** ADAM_STEP)
    v_hat = v2 / (1.0 - ADAM_B2 ** ADAM_STEP)
    return g, -ADAM_LR * (m_hat / (jnp.sqrt(v_hat) + ADAM_EPS) + ADAM_WD * w), m2, v2


def _adamw_many(ws, parts, ms, vs, name):
    n = len(ws)

    def body(*refs):
        w_refs, p_refs, m_refs, v_refs, outs = refs[:n], refs[n:2 * n], refs[2 * n:3 * n], refs[3 * n:4 * n], refs[4 * n:]
        for a in range(n):
            res = _adam_math(w_refs[a][...], [p_refs[a][k] for k in range(NDEV)], m_refs[a][...], v_refs[a][...])
            for o, val in zip(outs[4 * a:4 * a + 4], res):
                o[...] = val

    out = pl.pallas_call(body, out_shape=[jax.ShapeDtypeStruct(w.shape, F32) for w in ws for _ in range(4)],
                         compiler_params=pltpu.CompilerParams(vmem_limit_bytes=VMEM_LIMIT), name=name)(*ws, *parts, *ms, *vs)
    return [out[4 * a:4 * a + 4] for a in range(n)]


def _adamw_rows(w, parts, m, v, tr, name):
    rows, cols = w.shape

    def body(w_ref, p_ref, m_ref, v_ref, g_ref, d_ref, m2_ref, v2_ref):
        res = _adam_math(w_ref[...], [p_ref[k] for k in range(NDEV)], m_ref[...], v_ref[...])
        for o, val in zip((g_ref, d_ref, m2_ref, v2_ref), res):
            o[...] = val

    flat = pl.BlockSpec((tr, cols), lambda i: (i, 0))
    return pl.pallas_call(body, grid=(rows // tr,),
                          in_specs=[flat, pl.BlockSpec((NDEV, tr, cols), lambda i: (0, i, 0)), flat, flat],
                          out_specs=[flat] * 4, out_shape=[jax.ShapeDtypeStruct((rows, cols), F32)] * 4,
                          compiler_params=_cparams(1), name=name)(w, parts, m, v)


SHARDED = (("mla_w_a", 1), ("mla_w_uq", 2), ("mla_w_ukv", 2), ("mla_w_o", 1), ("ssm_w_in", 1), ("ssm_d", 1),
           ("ssm_w_glu", 2), ("ffn_w_up", 2), ("ffn_conv_w", 2), ("ffn_w_down", 1))
WIRE_EXACT = ("ssm_d", "ffn_conv_w")
GATHERED_FIRST = ("mla_w_a", "mla_w_uq", "mla_w_ukv")
REPLICATED_LAST = ("mla_g_q", "mla_g_kv", "g_mix")
REPLICATED = ("mla_g_q", "mla_g_kv", "ssm_lambda_re", "ssm_lambda_im", "ssm_log_dt", "ssm_b_re", "ssm_b_im",
              "ssm_c_re", "ssm_c_im", "ffn_conv_b", "g_mix", "g_ffn", "g_final")
WEIGHTS = ("mla_w_a", "mla_g_q", "mla_g_kv", "mla_w_uq", "mla_w_ukv", "mla_w_o", "ssm_w_in", "ssm_lambda_re",
           "ssm_lambda_im", "ssm_log_dt", "ssm_b_re", "ssm_b_im", "ssm_c_re", "ssm_c_im", "ssm_d", "ssm_w_glu",
           "ffn_w_up", "ffn_conv_w", "ffn_conv_b", "ffn_w_down", "g_mix", "g_ffn", "g_final")


ADAM_ROW_TILED = (("ffn_w_up", 256), ("ffn_w_down", 176))


def _two_d(shape):
    if len(shape) == 1:
        return (1, shape[0])
    if len(shape) > 2 and shape[-1] < LANES:
        return (math.prod(shape[:-2]), shape[-2] * shape[-1])
    return (math.prod(shape[:-1]), shape[-1])


def kernel(x, positions, mla_w_a, mla_g_q, mla_g_kv, mla_w_uq, mla_w_ukv, mla_w_o, ssm_w_in, ssm_lambda_re, ssm_lambda_im, ssm_log_dt, ssm_b_re, ssm_b_im, ssm_c_re, ssm_c_im, ssm_d, ssm_w_glu, ffn_w_up, ffn_conv_w, ffn_conv_b, ffn_w_down, g_mix, g_ffn, g_final, loss_target, m_mla_w_a, m_mla_g_q, m_mla_g_kv, m_mla_w_uq, m_mla_w_ukv, m_mla_w_o, m_ssm_w_in, m_ssm_lambda_re, m_ssm_lambda_im, m_ssm_log_dt, m_ssm_b_re, m_ssm_b_im, m_ssm_c_re, m_ssm_c_im, m_ssm_d, m_ssm_w_glu, m_ffn_w_up, m_ffn_conv_w, m_ffn_conv_b, m_ffn_w_down, m_g_mix, m_g_ffn, m_g_final, v_mla_w_a, v_mla_g_q, v_mla_g_kv, v_mla_w_uq, v_mla_w_ukv, v_mla_w_o, v_ssm_w_in, v_ssm_lambda_re, v_ssm_lambda_im, v_ssm_log_dt, v_ssm_b_re, v_ssm_b_im, v_ssm_c_re, v_ssm_c_im, v_ssm_d, v_ssm_w_glu, v_ffn_w_up, v_ffn_conv_w, v_ffn_conv_b, v_ffn_w_down, v_g_mix, v_g_ffn, v_g_final):
    a = dict(locals())
    s_len = x.shape[1]
    sh_names = [n for n, _ in SHARDED]

    def wire(n):
        return a[n] if n in WIRE_EXACT else a[n].astype(MXU)

    early = [(n, axis) for n, axis in SHARDED if n in GATHERED_FIRST]
    late = [(n, axis, wire(n)) for n, axis in SHARDED if n not in GATHERED_FIRST]
    w = {n: a[n] for n in REPLICATED}
    for (n, axis), g in zip(early, _gather_many([wire(n) for n, _ in early], "gather_weights")):
        w[n] = _merge_shards(g, axis)

    rep_late = [n for n in REPLICATED if n not in REPLICATED_LAST]
    loss, dx, grads, landed_late, rep_late_all = _sequence_step(
        x[0], positions.reshape(s_len, 1).astype(F32), loss_target[0], w, late, rep_late)
    loss = lax.psum(loss, ("x", "y", "c"))

    packed = jnp.concatenate([grads[n].reshape(-1) for n in REPLICATED_LAST])
    landed_early, (rep_last_all,) = _exchange_many([_split_shards(grads[n], axis).astype(MXU) for n, axis in early],
                                                   [packed], "exchange_grads")
    parts_of = dict(zip([n for n, _, _ in late], landed_late)) | dict(zip([n for n, _ in early], landed_early))
    for names, pack in ((rep_late, rep_late_all), (REPLICATED_LAST, rep_last_all)):
        off = 0
        for n in names:
            size = math.prod(a[n].shape)
            parts_of[n] = pack[:, off:off + size]
            off += size

    def view(n, arr, lead=()):
        return arr.reshape(lead + _two_d(a[n].shape))

    out = {}

    def finish(n, res):
        for kind, val in zip(("grad_", "delta_", "new_m_", "new_v_"), res):
            out[kind + n] = val.reshape(a[n].shape)

    for n, tr in ADAM_ROW_TILED:
        finish(n, _adamw_rows(view(n, a[n]), view(n, parts_of[n], (NDEV,)), view(n, a["m_" + n]), view(n, a["v_" + n]),
                              tr, "adamw_" + n))
    tiled = [n for n, _ in ADAM_ROW_TILED]
    for names, tag in (([n for n in sh_names if n not in tiled], "adamw_sharded"), (list(REPLICATED), "adamw_replicated")):
        res = _adamw_many([view(n, a[n]) for n in names], [view(n, parts_of[n], (NDEV,)) for n in names],
                          [view(n, a["m_" + n]) for n in names], [view(n, a["v_" + n]) for n in names], tag)
        for n, r in zip(names, res):
            finish(n, r)
    return (loss, dx[None], *[out[kind + n] for kind in ("grad_", "delta_", "new_m_", "new_v_") for n in WEIGHTS])
```
